```python
import math
import jax, jax.numpy as jnp
from jax import lax
import numpy as np

D_MODEL = 1024
BATCH = 8
SEQ = 2048
DEPTH = 1
DEC_BATCH = 128
DEC_SEQ = 8
PAST_LEN = 16384
PAGE_SIZE = 128

N_GLA_HEADS = 4
GLA_VAL_W = D_MODEL // 2
GLA_KEY_W = GLA_VAL_W // 2
GLA_DK = GLA_KEY_W // N_GLA_HEADS
GLA_DV = GLA_VAL_W // N_GLA_HEADS
GATE_RANK = 16
GATE_NORMALIZER = 16.0
GLA_CHUNK = 64
POOL_WINDOWS = (2, 4, 8, 16)
POOL_W = D_MODEL - GLA_VAL_W
POOL_GROUP = POOL_W // len(POOL_WINDOWS)
POOL_BUF = max(POOL_WINDOWS) - 1
MIX_W = GLA_VAL_W + POOL_W
IN_COLS = 2 * GLA_KEY_W + 2 * GLA_VAL_W + GATE_RANK + POOL_W
N_EXPERTS = 32
TOP_K = 4
D_FF = D_MODEL
SWIGLU_LIMIT = 7.0
SWIGLU_ALPHA = 1.702
EPS = 1e-5

kernel_name = "hybrid_gla_pool_moe_decode_step"


def rmsnorm(x, w):
    xf = x.astype(jnp.float32)
    y = xf * lax.rsqrt(jnp.mean(xf * xf, axis=-1, keepdims=True) + EPS)
    return (y * w.astype(jnp.float32)).astype(x.dtype)


def gla_scan(q, k, v, g, s0):
    B, L, H, _ = q.shape
    C = math.gcd(L, GLA_CHUNK)
    n = L // C

    def to_chunks(t):
        return t.astype(jnp.float32).reshape(B, n, C, H, t.shape[-1]).transpose(1, 0, 3, 2, 4)

    causal = jnp.tril(jnp.ones((C, C), dtype=bool))

    def step(S, inp):
        qc, kc, vc, gc = inp
        G = jnp.cumsum(gc, axis=2)
        o_inter = jnp.einsum('bhck,bhkv->bhcv', qc * jnp.exp(G), S)
        diff = G[:, :, :, None, :] - G[:, :, None, :, :]
        decay = jnp.exp(jnp.where(causal[:, :, None], diff, -jnp.inf))
        A = jnp.einsum('bhik,bhjk,bhijk->bhij', qc, kc, decay)
        o = o_inter + jnp.einsum('bhij,bhjv->bhiv', A, vc)
        G_last = G[:, :, -1:, :]
        S_new = jnp.exp(G_last[:, :, 0, :])[..., None] * S + jnp.einsum(
            'bhck,bhcv->bhkv', kc * jnp.exp(G_last - G), vc)
        return S_new, o

    S_fin, o = lax.scan(step, s0.astype(jnp.float32), (to_chunks(q), to_chunks(k), to_chunks(v), to_chunks(g)))
    o = o.transpose(1, 0, 3, 2, 4).reshape(B, L, H, v.shape[-1])
    return o, S_fin


def pool_mix(u, buf, start_pos, w_pool, pool_scale):
    B, L, _ = u.shape
    ext = jnp.concatenate([buf.astype(jnp.float32), u.astype(jnp.float32)], axis=1)
    cs = jnp.concatenate([jnp.zeros((B, 1, POOL_W), jnp.float32), jnp.cumsum(ext, axis=1)], axis=1)
    end = cs[:, POOL_BUF + 1:]
    pos = start_pos + jnp.arange(L)
    means = []
    for gi, w in enumerate(POOL_WINDOWS):
        sl = slice(gi * POOL_GROUP, (gi + 1) * POOL_GROUP)
        s = end[..., sl] - cs[:, POOL_BUF + 1 - w:POOL_BUF + 1 - w + L, sl]
        cnt = jnp.minimum(w, pos + 1).astype(jnp.float32)[None, :, None]
        means.append(s / cnt)
    d = (jnp.concatenate(means, axis=-1) - ext[:, POOL_BUF:]).reshape(B, L, len(POOL_WINDOWS), POOL_GROUP)
    z = jnp.einsum('blgc,gcd->blgd', d, w_pool.astype(jnp.float32)).reshape(B, L, POOL_W)
    z = z * pool_scale.astype(jnp.float32)
    return z.astype(u.dtype), ext[:, -POOL_BUF:].astype(u.dtype)


def mixer(xn, s0, buf, start_pos, w_in, w_gk2, b_gk, gla_norm, w_pool, pool_scale, w_o):
    B, L, _ = xn.shape
    p = xn @ w_in
    o0 = 0
    q = p[..., o0:o0 + GLA_KEY_W]; o0 += GLA_KEY_W
    k = p[..., o0:o0 + GLA_KEY_W]; o0 += GLA_KEY_W
    v = p[..., o0:o0 + GLA_VAL_W]; o0 += GLA_VAL_W
    og = p[..., o0:o0 + GLA_VAL_W]; o0 += GLA_VAL_W
    glr = p[..., o0:o0 + GATE_RANK]; o0 += GATE_RANK
    u = p[..., o0:o0 + POOL_W]
    gk = jax.nn.log_sigmoid((glr @ w_gk2).astype(jnp.float32) + b_gk.astype(jnp.float32)) / GATE_NORMALIZER
    q = q.reshape(B, L, N_GLA_HEADS, GLA_DK) * (GLA_DK ** -0.5)
    k = k.reshape(B, L, N_GLA_HEADS, GLA_DK)
    v = v.reshape(B, L, N_GLA_HEADS, GLA_DV)
    gk = gk.reshape(B, L, N_GLA_HEADS, GLA_DK)
    o, s_new = gla_scan(q, k, v, gk, s0)
    o = rmsnorm(o, gla_norm) * jax.nn.silu(og.astype(jnp.float32)).reshape(B, L, N_GLA_HEADS, GLA_DV)
    o = o.reshape(B, L, GLA_VAL_W).astype(xn.dtype)
    z, buf_new = pool_mix(u, buf, start_pos, w_pool, pool_scale)
    mix = jnp.concatenate([o, z], axis=-1) @ w_o
    return mix, s_new.astype(s0.dtype), buf_new


def moe(h, w_r, b_r, w_gu, b_gu, w_down, b_down):
    B, L, D = h.shape
    T = B * L
    x = h.reshape(T, D)
    logits = (x @ w_r).astype(jnp.float32) + b_r.astype(jnp.float32)
    top_v, top_i = lax.top_k(logits, TOP_K)
    gates = jax.nn.softmax(top_v, axis=-1)
    flat_e = top_i.reshape(-1)
    order = jnp.argsort(flat_e)
    e_sorted = flat_e[order]
    tok = order // TOP_K
    sizes = jnp.bincount(flat_e, length=N_EXPERTS).astype(jnp.int32)
    xs = x[tok]
    gu = lax.ragged_dot(xs, w_gu, sizes) + b_gu[e_sorted]
    gate = jnp.minimum(gu[:, :D_FF].astype(jnp.float32), SWIGLU_LIMIT)
    up = jnp.clip(gu[:, D_FF:].astype(jnp.float32), -SWIGLU_LIMIT, SWIGLU_LIMIT)
    act = ((up + 1.0) * gate * jax.nn.sigmoid(SWIGLU_ALPHA * gate)).astype(x.dtype)
    out = lax.ragged_dot(act, w_down, sizes) + b_down[e_sorted]
    out = (out.astype(jnp.float32) * gates.reshape(-1)[order][:, None]).astype(x.dtype)
    y = jax.ops.segment_sum(out, tok, num_segments=T)
    return y.reshape(B, L, D)


def setup_inputs(seed: int = 0) -> dict:
    key = jax.random.key(seed)
    ks = jax.random.split(key, 20)
    f = jnp.float32
    nrm = lambda k, shape, s: jax.random.normal(k, shape, f) * s
    return {
        "x_prompt": nrm(ks[0], (BATCH, SEQ, D_MODEL), 1.0),
        "x_sample": nrm(ks[1], (DEC_BATCH, DEC_SEQ, D_MODEL), 1.0),
        "state_gla": nrm(ks[2], (DEPTH, DEC_BATCH, N_GLA_HEADS, GLA_DK, GLA_DV), 0.5),
        "state_pool": nrm(ks[3], (DEPTH, DEC_BATCH, POOL_BUF, POOL_W), 1.0),
        "norm1": 1.0 + nrm(ks[4], (DEPTH, D_MODEL), 0.02),
        "w_in": nrm(ks[5], (DEPTH, D_MODEL, IN_COLS), D_MODEL ** -0.5),
        "w_gk2": nrm(ks[6], (DEPTH, GATE_RANK, GLA_KEY_W), GATE_RANK ** -0.5),
        "b_gk": nrm(ks[7], (DEPTH, GLA_KEY_W), 0.1),
        "gla_norm": 1.0 + nrm(ks[8], (DEPTH, GLA_DV), 0.02),
        "w_pool": nrm(ks[9], (DEPTH, len(POOL_WINDOWS), POOL_GROUP, POOL_GROUP), POOL_GROUP ** -0.5),
        "pool_scale": 1.0 + nrm(ks[10], (DEPTH, POOL_W), 0.02),
        "w_o": nrm(ks[11], (DEPTH, MIX_W, D_MODEL), MIX_W ** -0.5),
        "norm2": 1.0 + nrm(ks[12], (DEPTH, D_MODEL), 0.02),
        "w_router": nrm(ks[13], (DEPTH, D_MODEL, N_EXPERTS), D_MODEL ** -0.5),
        "b_router": nrm(ks[14], (DEPTH, N_EXPERTS), 0.01),
        "w_gate_up": nrm(ks[15], (DEPTH, N_EXPERTS, D_MODEL, 2 * D_FF), D_MODEL ** -0.5),
        "b_gate_up": nrm(ks[16], (DEPTH, N_EXPERTS, 2 * D_FF), 0.02),
        "w_down": nrm(ks[17], (DEPTH, N_EXPERTS, D_FF, D_MODEL), D_FF ** -0.5),
        "b_down": nrm(ks[18], (DEPTH, N_EXPERTS, D_MODEL), 0.02),
        "final_norm": 1.0 + nrm(ks[19], (D_MODEL,), 0.02),
    }


def reference(x_prompt, x_sample, state_gla, state_pool, norm1, w_in, w_gk2, b_gk, gla_norm,
              w_pool, pool_scale, w_o, norm2, w_router, b_router, w_gate_up, b_gate_up,
              w_down, b_down, final_norm):
    xp, xs = x_prompt, x_sample
    gla_p, pool_p, gla_s, pool_s = [], [], [], []
    for l in range(DEPTH):
        lw = (w_in[l], w_gk2[l], b_gk[l], gla_norm[l], w_pool[l], pool_scale[l], w_o[l])
        mw = (w_router[l], b_router[l], w_gate_up[l], b_gate_up[l], w_down[l], b_down[l])
        s0_p = jnp.zeros((BATCH, N_GLA_HEADS, GLA_DK, GLA_DV), state_gla.dtype)
        buf0_p = jnp.zeros((BATCH, POOL_BUF, POOL_W), xp.dtype)
        mix_p, sp, bp = mixer(rmsnorm(xp, norm1[l]), s0_p, buf0_p, 0, *lw)
        xp = xp + mix_p
        xp = xp + moe(rmsnorm(xp, norm2[l]), *mw)
        mix_s, ss, bs = mixer(rmsnorm(xs, norm1[l]), state_gla[l], state_pool[l], PAST_LEN, *lw)
        xs = xs + mix_s
        xs = xs + moe(rmsnorm(xs, norm2[l]), *mw)
        gla_p.append(sp); pool_p.append(bp); gla_s.append(ss); pool_s.append(bs)
    y_prompt = rmsnorm(xp, final_norm)
    y_sample = rmsnorm(xs, final_norm)
    new_gla_prompt = jnp.stack(gla_p, axis=0)
    new_pool_prompt = jnp.stack(pool_p, axis=0)
    new_gla_sample = jnp.stack(gla_s, axis=0)
    new_pool_sample = jnp.stack(pool_s, axis=0)
    return (y_prompt, y_sample, new_gla_prompt, new_pool_prompt, new_gla_sample, new_pool_sample)
```

```python
import functools
from typing import NamedTuple

import jax
import jax.numpy as jnp
from jax import lax
from jax.experimental import pallas as pl
from jax.experimental.pallas import tpu as pltpu

F32 = jnp.float32
BF16 = jnp.bfloat16

D_MODEL = 1024
N_HEADS = 4
DK = 64
DV = 128
KEY_W = N_HEADS * DK
VAL_W = N_HEADS * DV
GATE_RANK = 16
GATE_NORMALIZER = 16.0
GLA_CHUNK = 64
POOL_WINDOWS = (2, 4, 8, 16)
POOL_W = 512
POOL_GROUP = 128
POOL_BUF = 15
N_EXPERTS = 32
TOP_K = 4
D_FF = 1024
SWIGLU_LIMIT = 7.0
SWIGLU_ALPHA = 1.702
EPS = 1e-5
PAST_LEN = 16384

LANES = 128
HALO = 16
DIAG = 16
MAIN_COLS = 2 * KEY_W + 2 * VAL_W + POOL_W + LANES
ROW_TILE = 256
TOK_TILE = 256
ROUTE_TILE = 512
INV_BLOCK = 2048
VMEM_LIMIT = 56 * 1024 * 1024
NEG_BIG = -1e30


def _dot(a, b):
    return jnp.dot(a, b, preferred_element_type=F32)


def _dot_nt(a, b):
    return lax.dot_general(a, b, (((1,), (1,)), ((), ())), preferred_element_type=F32)


def _dot_tn(a, b):
    return lax.dot_general(a, b, (((0,), (0,)), ((), ())), preferred_element_type=F32)


def _split2(a):
    hi = a.astype(BF16)
    lo = (a - hi.astype(F32)).astype(BF16)
    return hi, lo


def _dot3(a, b_hi, b_lo):
    a_hi, a_lo = _split2(a)
    return _dot(a_hi, b_hi) + _dot(a_lo, b_hi) + _dot(a_hi, b_lo)


def _rms(x, w):
    return x * lax.rsqrt(jnp.mean(x * x, axis=-1, keepdims=True) + EPS) * w


class MixerCfg(NamedTuple):
    ns: int
    tl: int
    chunk: int
    start_pos: int
    n_alias: int


def _mixer_kernel(cfg, x_ref, s0_ref, buf0_ref, n1_ref, wmain_ref, wgkh_ref, wgkl_ref, bgk_ref,
                  gn_ref, wpool_ref, pscale_ref, wo_ref, n2_ref, wrh_ref, wrl_ref, br_ref, *rest):
    rest = rest[cfg.n_alias:]
    x1_ref, h_ref, topi_ref, gates_ref, st_out_ref, bufo_ref = rest[:6]
    st_s, ext_s, kh_s, gh_s, qg_s, kd_s, egl_s, v_s, a_s, o_s, qt_s, kt_s = rest[6:]

    ns, tl, C = cfg.ns, cfg.tl, cfg.chunk
    R = ns * tl
    S = min(C, DIAG)
    n_off = C // S - 1
    W = C if n_off else R
    l = pl.program_id(1)
    n_l = pl.num_programs(1)

    x = x_ref[...].reshape(R, D_MODEL)
    xn = _rms(x, n1_ref[...]).astype(BF16)
    p = _dot(xn, wmain_ref[...])
    q = p[:, 0:KEY_W] * (DK ** -0.5)
    k = p[:, KEY_W:2 * KEY_W]
    v = p[:, 2 * KEY_W:2 * KEY_W + VAL_W]
    og = p[:, 2 * KEY_W + VAL_W:2 * KEY_W + 2 * VAL_W]
    u = p[:, 2 * KEY_W + 2 * VAL_W:2 * KEY_W + 2 * VAL_W + POOL_W]
    glr = p[:, MAIN_COLS - LANES:MAIN_COLS]
    ext_s[:, HALO:HALO + tl, :] = u.reshape(ns, tl, POOL_W)

    z = _dot3(glr, wgkh_ref[...], wgkl_ref[...]) + bgk_ref[...]
    g = -(jnp.maximum(-z, 0.0) + jnp.log1p(jnp.exp(-jnp.abs(z)))) / GATE_NORMALIZER
    ri = lax.broadcasted_iota(jnp.int32, (R, R), 0)
    ci = lax.broadcasted_iota(jnp.int32, (R, R), 1)
    tri = ((ri // C == ci // C) & (ci <= ri)).astype(BF16)
    g_hi = g.astype(BF16)
    g_r = g - g_hi.astype(F32)
    g_mid = g_r.astype(BF16)
    g_lo = (g_r - g_mid.astype(F32)).astype(BF16)
    G = _dot(tri, g_hi) + _dot(tri, g_mid) + _dot(tri, g_lo)

    G3 = G.reshape(R // C, C, KEY_W)
    glast = jnp.broadcast_to(G3[:, C - 1:C, :], (R // C, C, KEY_W)).reshape(R, KEY_W)
    qg_s[...] = q * jnp.exp(G)
    kd_s[...] = k * jnp.exp(glast - G)
    egl_s[...] = jnp.exp(glast)
    v_s[...] = v

    kh_s[0:HALO, :] = jnp.zeros((HALO, KEY_W), F32)
    gh_s[0:HALO, :] = jnp.zeros((HALO, KEY_W), F32)
    kh_s[HALO:HALO + R, :] = k
    gh_s[HALO:HALO + R, :] = G
    row = lax.broadcasted_iota(jnp.int32, (R, 1), 0)
    row_s = row % S
    row_w = row % W
    hb_r = lax.broadcasted_iota(jnp.int32, (KEY_W, N_HEADS * W), 0)
    hb_c = lax.broadcasted_iota(jnp.int32, (KEY_W, N_HEADS * W), 1)
    head_bcast = (hb_r // DK == hb_c // W).astype(BF16)
    col_w = lax.broadcasted_iota(jnp.int32, (1, N_HEADS * W), 1) % W
    a_all = jnp.zeros((R, N_HEADS * W), F32)
    for d in range(S):
        k_sh = kh_s[HALO - d:HALO - d + R, :]
        g_sh = gh_s[HALO - d:HALO - d + R, :]
        valid = row_s >= d
        e = jnp.exp(jnp.where(valid, G - g_sh, NEG_BIG))
        term = (q * k_sh * e).astype(BF16)
        spread = _dot(term, head_bcast)
        a_all = jnp.where(col_w == row_w - d, spread, a_all)
    a_s[...] = a_all

    row_c = row % C
    for a in range(1, n_off + 1):
        ra = jnp.broadcast_to(G3[:, a * S - 1:a * S, :], (R // C, C, KEY_W)).reshape(R, KEY_W)
        qt_s[a - 1] = q * jnp.exp(jnp.minimum(G - ra, 0.0))
        kt_s[a - 1] = jnp.where(row_c < a * S, k * jnp.exp(jnp.minimum(ra - G, 0.0)), 0.0)

    @pl.when(l == 0)
    def _():
        st_s[...] = s0_ref[...]

    if not n_off:
        a_bf = a_s[...].astype(BF16)
        v_bf = v.astype(BF16)
        for h in range(N_HEADS):
            o_s[:, h * DV:(h + 1) * DV] = _dot(a_bf[:, h * W:(h + 1) * W], v_bf[:, h * DV:(h + 1) * DV])

    chunks_per_seq = tl // C
    kpad = max(C, 16)

    def chunk_body(c, carry):
        cs = pl.multiple_of(c * C, C)
        seq = c // chunks_per_seq
        rows = pl.ds(cs, C)
        for h in range(N_HEADS):
            kc = slice(h * DK, (h + 1) * DK)
            vc = slice(h * DV, (h + 1) * DV)
            st = st_s[seq, h]
            v_h = v_s[rows, vc]
            o_h = _dot_nt(qg_s[rows, kc].astype(BF16), st.astype(BF16))
            if n_off:
                blocks = [jnp.zeros((S, C), F32)]
                for a in range(1, n_off + 1):
                    qa = qt_s[a - 1, pl.ds(cs + a * S, S), kc].astype(BF16)
                    ka = kt_s[a - 1, rows, kc].astype(BF16)
                    blocks.append(_dot_nt(qa, ka))
                a_h = a_s[rows, h * W:(h + 1) * W] + jnp.concatenate(blocks, axis=0)
                o_s[rows, vc] = o_h + _dot(a_h.astype(BF16), v_h.astype(BF16))
            else:
                o_s[rows, vc] = o_s[rows, vc] + o_h
            kd_h = kd_s[rows, kc]
            if kpad > C:
                v_h = jnp.concatenate([v_h, jnp.zeros((kpad - C, DV), F32)], axis=0)
                kd_h = jnp.concatenate([kd_h, jnp.zeros((kpad - C, DK), F32)], axis=0)
            st_s[seq, h] = st * egl_s[pl.ds(cs, 1), kc] + _dot_tn(v_h.astype(BF16), kd_h.astype(BF16))
        return carry

    lax.fori_loop(0, R // C, chunk_body, 0)

    @pl.when(l == n_l - 1)
    def _():
        st_out_ref[...] = st_s[...]

    o = o_s[...]
    gn = gn_ref[...]
    o_heads = []
    for h in range(N_HEADS):
        vc = slice(h * DV, (h + 1) * DV)
        og_h = og[:, vc]
        o_heads.append(_rms(o[:, vc], gn) * (og_h * jax.nn.sigmoid(og_h)))

    @pl.when(l == 0)
    def _():
        ext_s[:, 0:HALO - POOL_BUF, :] = jnp.zeros((ns, HALO - POOL_BUF, POOL_W), F32)
        ext_s[:, HALO - POOL_BUF:HALO, :] = buf0_ref[...]

    pos = cfg.start_pos + l * tl + lax.broadcasted_iota(jnp.int32, (1, tl, 1), 1)
    z_groups = []
    for gi, w in enumerate(POOL_WINDOWS):
        gc = slice(gi * POOL_GROUP, (gi + 1) * POOL_GROUP)
        s = ext_s[:, HALO:HALO + tl, gc]
        for dd in range(1, w):
            s = s + ext_s[:, HALO - dd:HALO - dd + tl, gc]
        cnt = jnp.minimum(w, pos + 1).astype(F32)
        dmean = (s / cnt - ext_s[:, HALO:HALO + tl, gc]).reshape(R, POOL_GROUP)
        z_groups.append(_dot(dmean.astype(BF16), wpool_ref[gi]))
    zp = jnp.concatenate(z_groups, axis=1) * pscale_ref[...]

    @pl.when(l == n_l - 1)
    def _():
        bufo_ref[...] = ext_s[:, tl + HALO - POOL_BUF:tl + HALO, :]

    @pl.when(l < n_l - 1)
    def _():
        ext_s[:, 0:HALO, :] = ext_s[:, tl:tl + HALO, :]

    cat = jnp.concatenate(o_heads + [zp], axis=1).astype(BF16)
    x1 = x + _dot(cat, wo_ref[...])
    x1_ref[...] = x1
    hn = _rms(x1, n2_ref[...])
    h_ref[...] = hn
    logits = _dot3(hn, wrh_ref[...], wrl_ref[...]) + br_ref[...]
    lane = lax.broadcasted_iota(jnp.int32, (R, LANES), 1)
    lg = jnp.where(lane < N_EXPERTS, logits, -jnp.inf)
    vals, idxs = [], []
    for _ in range(TOP_K):
        m = jnp.max(lg, axis=1, keepdims=True)
        idx = jnp.min(jnp.where(lg == m, lane, LANES), axis=1, keepdims=True)
        vals.append(m)
        idxs.append(idx)
        lg = jnp.where(lane == idx, -jnp.inf, lg)
    exps = [jnp.exp(vv - vals[0]) for vv in vals]
    den = exps[0] + exps[1] + exps[2] + exps[3]
    ti = jnp.zeros((R, LANES), jnp.int32)
    gt = jnp.zeros((R, LANES), F32)
    for kk in range(TOP_K):
        ti = jnp.where(lane == kk, idxs[kk], ti)
        gt = jnp.where(lane == kk, exps[kk] / den, gt)
    topi_ref[...] = ti[:, 0:TOP_K]
    gates_ref[...] = gt[:, 0:TOP_K]


def _mixer_call(cfg, n_tok_all, row_block0, x, s0t, buf0, weights, aliased):
    B, L, _ = x.shape
    ns, tl, C = cfg.ns, cfg.tl, cfg.chunk
    R = ns * tl
    S = min(C, DIAG)
    n_off = C // S - 1
    W = C if n_off else R
    n_b, n_l = B // ns, L // tl
    assert B % ns == 0 and L % tl == 0 and tl % C == 0 and R % 8 == 0
    assert n_l == 1 or tl >= HALO

    def const(shape):
        return pl.BlockSpec(shape, lambda b, l: (0,) * len(shape))

    (n1, wmain, wgkh, wgkl, bgk, gn, wpool, pscale, wo, n2, wrh, wrl, br) = weights
    in_specs = [
        pl.BlockSpec((ns, tl, D_MODEL), lambda b, l: (b, l, 0)),
        pl.BlockSpec((ns, N_HEADS, DV, DK), lambda b, l: (b, 0, 0, 0)),
        pl.BlockSpec((ns, POOL_BUF, POOL_W), lambda b, l: (b, 0, 0)),
        const(n1.shape), const(wmain.shape), const(wgkh.shape), const(wgkl.shape), const(bgk.shape),
        const(gn.shape), const(wpool.shape), const(pscale.shape), const(wo.shape), const(n2.shape),
        const(wrh.shape), const(wrl.shape), const(br.shape),
    ] + [pl.BlockSpec(memory_space=pl.ANY)] * len(aliased)

    def tok_block(width):
        return pl.BlockSpec((R, width), lambda b, l: (row_block0 + b * n_l + l, 0))

    out_specs = [
        tok_block(D_MODEL), tok_block(D_MODEL), tok_block(TOP_K), tok_block(TOP_K),
        pl.BlockSpec((ns, N_HEADS, DV, DK), lambda b, l: (b, 0, 0, 0)),
        pl.BlockSpec((ns, POOL_BUF, POOL_W), lambda b, l: (b, 0, 0)),
    ]
    out_shape = [
        jax.ShapeDtypeStruct((n_tok_all, D_MODEL), F32),
        jax.ShapeDtypeStruct((n_tok_all, D_MODEL), F32),
        jax.ShapeDtypeStruct((n_tok_all, TOP_K), jnp.int32),
        jax.ShapeDtypeStruct((n_tok_all, TOP_K), F32),
        jax.ShapeDtypeStruct((B, N_HEADS, DV, DK), F32),
        jax.ShapeDtypeStruct((B, POOL_BUF, POOL_W), F32),
    ]
    scratch = [
        pltpu.VMEM((ns, N_HEADS, DV, DK), F32),
        pltpu.VMEM((ns, tl + HALO, POOL_W), F32),
        pltpu.VMEM((R + HALO, KEY_W), F32),
        pltpu.VMEM((R + HALO, KEY_W), F32),
        pltpu.VMEM((R, KEY_W), F32),
        pltpu.VMEM((R, KEY_W), F32),
        pltpu.VMEM((R, KEY_W), F32),
        pltpu.VMEM((R, VAL_W), F32),
        pltpu.VMEM((R, N_HEADS * W), F32),
        pltpu.VMEM((R, VAL_W), F32),
        pltpu.VMEM((max(n_off, 1), R, KEY_W), F32),
        pltpu.VMEM((max(n_off, 1), R, KEY_W), F32),
    ]
    n_fixed = 16
    aliases = {n_fixed + i: i for i in range(len(aliased))}
    return pl.pallas_call(
        functools.partial(_mixer_kernel, cfg),
        grid=(n_b, n_l),
        in_specs=in_specs,
        out_specs=out_specs,
        out_shape=out_shape,
        scratch_shapes=scratch,
        input_output_aliases=aliases,
        compiler_params=pltpu.CompilerParams(
            dimension_semantics=("arbitrary", "arbitrary"), vmem_limit_bytes=VMEM_LIMIT),
        name="mixer",
    )(x, s0t, buf0, n1, wmain, wgkh, wgkl, bgk, gn, wpool, pscale, wo, n2, wrh, wrl, br, *aliased)


def _route_kernel(topi_ref, pos_ref, counts_ref, cnt_s, carry_s, gstart_s):
    ph = pl.program_id(0)
    i = pl.program_id(1)
    TT = topi_ref.shape[0]
    topi = topi_ref[...]
    lane = lax.broadcasted_iota(jnp.int32, (TT, LANES), 1)
    hot = jnp.zeros((TT, LANES), F32)
    for kk in range(TOP_K):
        hot = hot + (lane == topi[:, kk:kk + 1]).astype(F32)
    colsum = jnp.sum(hot, axis=0, keepdims=True)

    @pl.when((ph == 0) & (i == 0))
    def _():
        cnt_s[...] = jnp.zeros_like(cnt_s)

    @pl.when(ph == 0)
    def _():
        cnt_s[...] = cnt_s[...] + colsum

    @pl.when((ph == 1) & (i == 0))
    def _():
        cnt = cnt_s[...]
        counts_ref[...] = cnt.astype(jnp.int32)
        tiles = jnp.floor((cnt + (ROW_TILE - 1)) * (1.0 / ROW_TILE))
        tiles8 = jnp.broadcast_to(tiles, (8, LANES))
        ur = lax.broadcasted_iota(jnp.int32, (LANES, LANES), 0)
        uc = lax.broadcasted_iota(jnp.int32, (LANES, LANES), 1)
        upper = (ur < uc).astype(BF16)
        t_hi, t_lo = _split2(tiles8)
        excl = _dot(t_hi, upper) + _dot(t_lo, upper)
        gstart_s[...] = excl[0:1, :] * float(ROW_TILE)
        carry_s[...] = jnp.zeros_like(carry_s)

    @pl.when(ph == 1)
    def _():
        lr = lax.broadcasted_iota(jnp.int32, (TT, TT), 0)
        lc = lax.broadcasted_iota(jnp.int32, (TT, TT), 1)
        lower = (lc < lr).astype(BF16)
        rank = _dot(lower, hot.astype(BF16)) + carry_s[...] + gstart_s[...]
        out = jnp.zeros((TT, LANES), F32)
        for kk in range(TOP_K):
            pk = jnp.sum(jnp.where(lane == topi[:, kk:kk + 1], rank, 0.0), axis=1, keepdims=True)
            out = jnp.where(lane == kk, pk, out)
        pos_ref[...] = out[:, 0:TOP_K].astype(jnp.int32)
        carry_s[...] = carry_s[...] + colsum


def _route_call(topi):
    T = topi.shape[0]
    assert T % ROUTE_TILE == 0
    return pl.pallas_call(
        _route_kernel,
        grid=(2, T // ROUTE_TILE),
        in_specs=[pl.BlockSpec((ROUTE_TILE, TOP_K), lambda ph, i: (i, 0))],
        out_specs=[pl.BlockSpec((ROUTE_TILE, TOP_K), lambda ph, i: (i * ph, 0)),
                   pl.BlockSpec((1, LANES), lambda ph, i: (0, 0))],
        out_shape=[jax.ShapeDtypeStruct((T, TOP_K), jnp.int32),
                   jax.ShapeDtypeStruct((1, LANES), jnp.int32)],
        scratch_shapes=[pltpu.VMEM((1, LANES), F32)] * 3,
        compiler_params=pltpu.CompilerParams(dimension_semantics=("arbitrary", "arbitrary")),
        name="route",
    )(topi)


def _invperm_kernel(n_zero_steps, pos_ref, tok_ref):
    i = pl.program_id(0)

    @pl.when(i < n_zero_steps)
    def _():
        def zero(r, c):
            tok_ref[i * INV_BLOCK + r] = 0
            return c

        lax.fori_loop(0, INV_BLOCK, zero, 0, unroll=8)

    @pl.when(i >= n_zero_steps)
    def _():
        base = (i - n_zero_steps) * INV_BLOCK

        def put(n, c):
            tok_ref[pos_ref[n]] = (base + n) // TOP_K
            return c

        lax.fori_loop(0, INV_BLOCK, put, 0, unroll=8)


def _invperm_call(pos_flat, n_rows):
    n_pairs = pos_flat.shape[0]
    assert n_rows % INV_BLOCK == 0 and n_pairs % INV_BLOCK == 0
    n_zero = n_rows // INV_BLOCK
    return pl.pallas_call(
        functools.partial(_invperm_kernel, n_zero),
        grid=(n_zero + n_pairs // INV_BLOCK,),
        in_specs=[pl.BlockSpec((INV_BLOCK,), lambda i: (jnp.maximum(i - n_zero, 0),),
                               memory_space=pltpu.SMEM)],
        out_specs=pl.BlockSpec(memory_space=pltpu.SMEM),
        out_shape=jax.ShapeDtypeStruct((n_rows,), jnp.int32),
        compiler_params=pltpu.CompilerParams(dimension_semantics=("arbitrary",)),
        name="invperm",
    )(pos_flat)


def _gather_rows(tok_ref, h_hbm, xbuf, sem, tile, slot):
    base = tile * ROW_TILE

    def issue(r, c):
        t = tok_ref[base + r]
        pltpu.make_async_copy(h_hbm.at[pl.ds(t, 1), :], xbuf.at[slot, pl.ds(r, 1), :], sem.at[slot]).start()
        return c

    lax.fori_loop(0, ROW_TILE, issue, 0, unroll=8)


def _experts_kernel(te_ref, nt_ref, tok_ref, h_hbm, wgu_ref, bgu_ref, wd_ref, bd_ref, out_ref,
                    xbuf, sem, wgu_s, wd_s):
    i = pl.program_id(0)
    n_used = nt_ref[0]
    slot = i % 2

    @pl.when(i == 0)
    def _():
        _gather_rows(tok_ref, h_hbm, xbuf, sem, 0, 0)

    @pl.when(i + 1 < n_used)
    def _():
        _gather_rows(tok_ref, h_hbm, xbuf, sem, i + 1, 1 - slot)

    changed = (i == 0) | (te_ref[i] != te_ref[jnp.maximum(i - 1, 0)])

    @pl.when(changed & (i < n_used))
    def _():
        wgu_s[...] = wgu_ref[0].astype(BF16)
        wd_s[...] = wd_ref[0].astype(BF16)

    @pl.when(i < n_used)
    def _():
        pltpu.make_async_copy(xbuf.at[slot], xbuf.at[slot], sem.at[slot]).wait()
        xs = xbuf[slot].astype(BF16)
        gu = _dot(xs, wgu_s[...]) + bgu_ref[0]
        gate = jnp.minimum(gu[:, 0:D_FF], SWIGLU_LIMIT)
        up = jnp.clip(gu[:, D_FF:2 * D_FF], -SWIGLU_LIMIT, SWIGLU_LIMIT)
        act = ((up + 1.0) * gate * jax.nn.sigmoid(SWIGLU_ALPHA * gate)).astype(BF16)
        out_ref[...] = _dot(act, wd_s[...]) + bd_ref[0]

    @pl.when(i >= n_used)
    def _():
        out_ref[...] = jnp.zeros_like(out_ref)


def _experts_call(tile_e, n_used, tok_of_row, h_all, w_gu, b_gu, w_down, b_down):
    n_tiles = tile_e.shape[0]
    grid_spec = pltpu.PrefetchScalarGridSpec(
        num_scalar_prefetch=3,
        grid=(n_tiles,),
        in_specs=[
            pl.BlockSpec(memory_space=pl.ANY),
            pl.BlockSpec((1, D_MODEL, 2 * D_FF), lambda i, te, nt, tok: (te[i], 0, 0)),
            pl.BlockSpec((1, 1, 2 * D_FF), lambda i, te, nt, tok: (te[i], 0, 0)),
            pl.BlockSpec((1, D_FF, D_MODEL), lambda i, te, nt, tok: (te[i], 0, 0)),
            pl.BlockSpec((1, 1, D_MODEL), lambda i, te, nt, tok: (te[i], 0, 0)),
        ],
        out_specs=pl.BlockSpec((ROW_TILE, D_MODEL), lambda i, te, nt, tok: (i, 0)),
        scratch_shapes=[
            pltpu.VMEM((2, ROW_TILE, D_MODEL), F32),
            pltpu.SemaphoreType.DMA((2,)),
            pltpu.VMEM((D_MODEL, 2 * D_FF), BF16),
            pltpu.VMEM((D_FF, D_MODEL), BF16),
        ],
    )
    return pl.pallas_call(
        _experts_kernel,
        grid_spec=grid_spec,
        out_shape=jax.ShapeDtypeStruct((n_tiles * ROW_TILE, D_MODEL), F32),
        compiler_params=pltpu.CompilerParams(
            dimension_semantics=("arbitrary",), vmem_limit_bytes=VMEM_LIMIT),
        name="experts",
    )(tile_e, n_used, tok_of_row, h_all, w_gu, b_gu.reshape(N_EXPERTS, 1, 2 * D_FF),
      w_down, b_down.reshape(N_EXPERTS, 1, D_MODEL))


def _gather_expert_rows(pos_ref, y_hbm, gbuf, sem, tile, slot):
    base = tile * (TOK_TILE * TOP_K)

    def issue(t, c):
        for kk in range(TOP_K):
            r = pos_ref[base + t * TOP_K + kk]
            pltpu.make_async_copy(y_hbm.at[pl.ds(r, 1), :], gbuf.at[slot, kk, pl.ds(t, 1), :],
                                  sem.at[slot]).start()
        return c

    lax.fori_loop(0, TOK_TILE, issue, 0, unroll=2)


def _final_kernel(tile0, pos_ref, y_hbm, x1_ref, gates_ref, fn_ref, out_ref, gbuf, sem):
    i = pl.program_id(0)
    n = pl.num_programs(0)
    slot = i % 2

    @pl.when(i == 0)
    def _():
        _gather_expert_rows(pos_ref, y_hbm, gbuf, sem, tile0, 0)

    @pl.when(i + 1 < n)
    def _():
        _gather_expert_rows(pos_ref, y_hbm, gbuf, sem, tile0 + i + 1, 1 - slot)

    pltpu.make_async_copy(gbuf.at[slot], gbuf.at[slot], sem.at[slot]).wait()
    gates = gates_ref[...]
    acc = x1_ref[...]
    for kk in range(TOP_K):
        acc = acc + gbuf[slot, kk] * gates[:, kk:kk + 1]
    out_ref[...] = _rms(acc, fn_ref[...])


def _final_call(tok0, n_tok, pos_flat, y_sorted, x1_all, gates_all, final_norm):
    assert tok0 % TOK_TILE == 0 and n_tok % TOK_TILE == 0
    tile0 = tok0 // TOK_TILE
    grid_spec = pltpu.PrefetchScalarGridSpec(
        num_scalar_prefetch=1,
        grid=(n_tok // TOK_TILE,),
        in_specs=[
            pl.BlockSpec(memory_space=pl.ANY),
            pl.BlockSpec((TOK_TILE, D_MODEL), lambda i, pos: (tile0 + i, 0)),
            pl.BlockSpec((TOK_TILE, TOP_K), lambda i, pos: (tile0 + i, 0)),
            pl.BlockSpec((1, D_MODEL), lambda i, pos: (0, 0)),
        ],
        out_specs=pl.BlockSpec((TOK_TILE, D_MODEL), lambda i, pos: (i, 0)),
        scratch_shapes=[
            pltpu.VMEM((2, TOP_K, TOK_TILE, D_MODEL), F32),
            pltpu.SemaphoreType.DMA((2,)),
        ],
    )
    return pl.pallas_call(
        functools.partial(_final_kernel, tile0),
        grid_spec=grid_spec,
        out_shape=jax.ShapeDtypeStruct((n_tok, D_MODEL), F32),
        compiler_params=pltpu.CompilerParams(
            dimension_semantics=("arbitrary",), vmem_limit_bytes=VMEM_LIMIT),
        name="final",
    )(pos_flat, y_sorted, x1_all, gates_all, final_norm)


def _pick_tile(n, target):
    t = min(n, target)
    while n % t:
        t -= 1
    return t


def kernel(x_prompt, x_sample, state_gla, state_pool, norm1, w_in, w_gk2, b_gk, gla_norm, w_pool,
           pool_scale, w_o, norm2, w_router, b_router, w_gate_up, b_gate_up, w_down, b_down, final_norm):
    depth = w_in.shape[0]
    assert depth == 1
    B, L, _ = x_prompt.shape
    BS, LS, _ = x_sample.shape
    n_p, n_s = B * L, BS * LS
    n_all = n_p + n_s

    wi = w_in[0]
    o_glr = 2 * KEY_W + 2 * VAL_W
    wmain = jnp.concatenate(
        [wi[:, 0:o_glr], wi[:, o_glr + GATE_RANK:], wi[:, o_glr:o_glr + GATE_RANK],
         jnp.zeros((D_MODEL, LANES - GATE_RANK), F32)], axis=1).astype(BF16)
    wgk = jnp.concatenate([w_gk2[0], jnp.zeros((LANES - GATE_RANK, KEY_W), F32)], axis=0)
    wgkh, wgkl = _split2(wgk)
    wr = jnp.concatenate([w_router[0], jnp.zeros((D_MODEL, LANES - N_EXPERTS), F32)], axis=1)
    wrh, wrl = _split2(wr)
    br = jnp.concatenate([b_router[0], jnp.zeros((LANES - N_EXPERTS,), F32)]).reshape(1, LANES)
    weights = (norm1[0].reshape(1, D_MODEL), wmain, wgkh, wgkl, b_gk[0].reshape(1, KEY_W),
               gla_norm[0].reshape(1, DV), w_pool[0].astype(BF16), pool_scale[0].reshape(1, POOL_W),
               w_o[0].astype(BF16), norm2[0].reshape(1, D_MODEL), wrh, wrl, br)

    assert L % GLA_CHUNK == 0 and LS in (8, 16)
    cfg_p = MixerCfg(ns=1, tl=_pick_tile(L, 256), chunk=GLA_CHUNK, start_pos=0, n_alias=0)
    s0_p = jnp.zeros((B, N_HEADS, DV, DK), F32)
    buf0_p = jnp.zeros((B, POOL_BUF, POOL_W), F32)
    x1_all, h_all, topi_all, gates_all, st_p, buf_p = _mixer_call(
        cfg_p, n_all, 0, x_prompt, s0_p, buf0_p, weights, ())

    cfg_s = MixerCfg(ns=_pick_tile(BS, 128 // LS), tl=LS, chunk=LS, start_pos=PAST_LEN, n_alias=4)
    r_s = cfg_s.ns * cfg_s.tl
    assert n_p % r_s == 0
    x1_all, h_all, topi_all, gates_all, st_s, buf_s = _mixer_call(
        cfg_s, n_all, n_p // r_s, x_sample, jnp.swapaxes(state_gla[0], -1, -2), state_pool[0], weights,
        (x1_all, h_all, topi_all, gates_all))

    pos, counts = _route_call(topi_all)
    n_tiles = (n_all * TOP_K + N_EXPERTS * (ROW_TILE - 1)) // ROW_TILE
    tiles_per_e = (counts[0, :N_EXPERTS] + (ROW_TILE - 1)) // ROW_TILE
    ends = jnp.cumsum(tiles_per_e)
    n_used = ends[-1].astype(jnp.int32)
    tile_ids = jnp.minimum(jnp.arange(n_tiles, dtype=jnp.int32), n_used - 1)
    tile_e = jnp.sum(tile_ids[:, None] >= ends[None, :], axis=1).astype(jnp.int32)
    pos_flat = pos.reshape(n_all * TOP_K)
    tok_of_row = _invperm_call(pos_flat, -(-n_tiles * ROW_TILE // INV_BLOCK) * INV_BLOCK)

    y_sorted = _experts_call(tile_e, n_used.reshape(1), tok_of_row, h_all,
                             w_gate_up[0], b_gate_up[0], w_down[0], b_down[0])

    fn = final_norm.reshape(1, D_MODEL)
    y_p = _final_call(0, n_p, pos_flat, y_sorted, x1_all, gates_all, fn)
    y_s = _final_call(n_p, n_s, pos_flat, y_sorted, x1_all, gates_all, fn)

    return (y_p.reshape(B, L, D_MODEL), y_s.reshape(BS, LS, D_MODEL),
            jnp.swapaxes(st_p, -1, -2)[None], buf_p[None],
            jnp.swapaxes(st_s, -1, -2)[None], buf_s[None])
```

```python
import functools
from typing import NamedTuple

import jax
import jax.numpy as jnp
from jax import lax
from jax.experimental import pallas as pl
from jax.experimental.pallas import tpu as pltpu

F32 = jnp.float32
BF16 = jnp.bfloat16

D_MODEL = 1024
N_HEADS = 4
DK = 64
DV = 128
KEY_W = N_HEADS * DK
VAL_W = N_HEADS * DV
GATE_RANK = 16
GATE_NORMALIZER = 16.0
GLA_CHUNK = 64
POOL_WINDOWS = (2, 4, 8, 16)
POOL_W = 512
POOL_GROUP = 128
POOL_BUF = 15
N_EXPERTS = 32
TOP_K = 4
D_FF = 1024
SWIGLU_LIMIT = 7.0
SWIGLU_ALPHA = 1.702
EPS = 1e-5
PAST_LEN = 16384

LANES = 128
CHUNKS = D_MODEL // LANES
HALO = 16
DIAG = 16
MAIN_COLS = 2 * KEY_W + 2 * VAL_W + POOL_W + LANES
ROW_TILE = 256
TOK_TILE = 256
ROUTE_TILE = 512
INV_BLOCK = 2048
VMEM_LIMIT = 56 * 1024 * 1024
NEG_BIG = -1e30


def _dot(a, b):
    return jnp.dot(a, b, preferred_element_type=F32)


def _dot_nt(a, b):
    return lax.dot_general(a, b, (((1,), (1,)), ((), ())), preferred_element_type=F32)


def _dot_tn(a, b):
    return lax.dot_general(a, b, (((0,), (0,)), ((), ())), preferred_element_type=F32)


def _split2(a):
    hi = a.astype(BF16)
    lo = (a - hi.astype(F32)).astype(BF16)
    return hi, lo


def _dot3(a, b_hi, b_lo):
    a_hi, a_lo = _split2(a)
    return _dot(a_hi, b_hi) + _dot(a_lo, b_hi) + _dot(a_hi, b_lo)


def _rms(x, w):
    return x * lax.rsqrt(jnp.mean(x * x, axis=-1, keepdims=True) + EPS) * w


class MixerCfg(NamedTuple):
    ns: int
    tl: int
    chunk: int
    start_pos: int
    n_alias: int


def _mixer_kernel(cfg, x_ref, s0_ref, buf0_ref, n1_ref, wmain_ref, wgkh_ref, wgkl_ref, bgk_ref,
                  gn_ref, wpool_ref, pscale_ref, wo_ref, n2_ref, wrh_ref, wrl_ref, br_ref, *rest):
    rest = rest[cfg.n_alias:]
    x1_ref, h_ref, topi_ref, gates_ref, st_out_ref, bufo_ref = rest[:6]
    st_s, ext_s, kh_s, gh_s, qg_s, kd_s, egl_s, v_s, a_s, o_s, qt_s, kt_s = rest[6:]

    ns, tl, C = cfg.ns, cfg.tl, cfg.chunk
    R = ns * tl
    S = min(C, DIAG)
    n_off = C // S - 1
    W = C if n_off else R
    l = pl.program_id(1)
    n_l = pl.num_programs(1)

    x = x_ref[...].reshape(R, D_MODEL)
    xn = _rms(x, n1_ref[...]).astype(BF16)
    p = _dot(xn, wmain_ref[...])
    q = p[:, 0:KEY_W] * (DK ** -0.5)
    k = p[:, KEY_W:2 * KEY_W]
    v = p[:, 2 * KEY_W:2 * KEY_W + VAL_W]
    og = p[:, 2 * KEY_W + VAL_W:2 * KEY_W + 2 * VAL_W]
    u = p[:, 2 * KEY_W + 2 * VAL_W:2 * KEY_W + 2 * VAL_W + POOL_W]
    glr = p[:, MAIN_COLS - LANES:MAIN_COLS]
    ext_s[:, HALO:HALO + tl, :] = u.reshape(ns, tl, POOL_W)

    z = _dot3(glr, wgkh_ref[...], wgkl_ref[...]) + bgk_ref[...]
    g = -(jnp.maximum(-z, 0.0) + jnp.log1p(jnp.exp(-jnp.abs(z)))) / GATE_NORMALIZER
    ri = lax.broadcasted_iota(jnp.int32, (R, R), 0)
    ci = lax.broadcasted_iota(jnp.int32, (R, R), 1)
    tri = ((ri // C == ci // C) & (ci <= ri)).astype(BF16)
    g_hi = g.astype(BF16)
    g_r = g - g_hi.astype(F32)
    g_mid = g_r.astype(BF16)
    g_lo = (g_r - g_mid.astype(F32)).astype(BF16)
    G = _dot(tri, g_hi) + _dot(tri, g_mid) + _dot(tri, g_lo)

    G3 = G.reshape(R // C, C, KEY_W)
    glast = jnp.broadcast_to(G3[:, C - 1:C, :], (R // C, C, KEY_W)).reshape(R, KEY_W)
    qg_s[...] = q * jnp.exp(G)
    kd_s[...] = k * jnp.exp(glast - G)
    egl_s[...] = jnp.exp(glast)
    v_s[...] = v

    kh_s[0:HALO, :] = jnp.zeros((HALO, KEY_W), F32)
    gh_s[0:HALO, :] = jnp.zeros((HALO, KEY_W), F32)
    kh_s[HALO:HALO + R, :] = k
    gh_s[HALO:HALO + R, :] = G
    row = lax.broadcasted_iota(jnp.int32, (R, 1), 0)
    row_s = row % S
    row_w = row % W
    hb_r = lax.broadcasted_iota(jnp.int32, (KEY_W, N_HEADS * W), 0)
    hb_c = lax.broadcasted_iota(jnp.int32, (KEY_W, N_HEADS * W), 1)
    head_bcast = (hb_r // DK == hb_c // W).astype(BF16)
    col_w = lax.broadcasted_iota(jnp.int32, (1, N_HEADS * W), 1) % W
    a_all = jnp.zeros((R, N_HEADS * W), F32)
    for d in range(S):
        k_sh = kh_s[HALO - d:HALO - d + R, :]
        g_sh = gh_s[HALO - d:HALO - d + R, :]
        valid = row_s >= d
        e = jnp.exp(jnp.where(valid, G - g_sh, NEG_BIG))
        term = (q * k_sh * e).astype(BF16)
        spread = _dot(term, head_bcast)
        a_all = jnp.where(col_w == row_w - d, spread, a_all)
    a_s[...] = a_all

    row_c = row % C
    for a in range(1, n_off + 1):
        ra = jnp.broadcast_to(G3[:, a * S - 1:a * S, :], (R // C, C, KEY_W)).reshape(R, KEY_W)
        qt_s[a - 1] = q * jnp.exp(jnp.minimum(G - ra, 0.0))
        kt_s[a - 1] = jnp.where(row_c < a * S, k * jnp.exp(jnp.minimum(ra - G, 0.0)), 0.0)

    @pl.when(l == 0)
    def _():
        st_s[...] = s0_ref[...]

    if not n_off:
        a_bf = a_s[...].astype(BF16)
        v_bf = v.astype(BF16)
        for h in range(N_HEADS):
            o_s[:, h * DV:(h + 1) * DV] = _dot(a_bf[:, h * W:(h + 1) * W], v_bf[:, h * DV:(h + 1) * DV])

    chunks_per_seq = tl // C
    kpad = max(C, 16)

    def chunk_body(c, carry):
        cs = pl.multiple_of(c * C, C)
        seq = c // chunks_per_seq
        rows = pl.ds(cs, C)
        for h in range(N_HEADS):
            kc = slice(h * DK, (h + 1) * DK)
            vc = slice(h * DV, (h + 1) * DV)
            st = st_s[seq, h]
            v_h = v_s[rows, vc]
            o_h = _dot_nt(qg_s[rows, kc].astype(BF16), st.astype(BF16))
            if n_off:
                blocks = [jnp.zeros((S, C), F32)]
                for a in range(1, n_off + 1):
                    qa = qt_s[a - 1, pl.ds(cs + a * S, S), kc].astype(BF16)
                    ka = kt_s[a - 1, rows, kc].astype(BF16)
                    blocks.append(_dot_nt(qa, ka))
                a_h = a_s[rows, h * W:(h + 1) * W] + jnp.concatenate(blocks, axis=0)
                o_s[rows, vc] = o_h + _dot(a_h.astype(BF16), v_h.astype(BF16))
            else:
                o_s[rows, vc] = o_s[rows, vc] + o_h
            kd_h = kd_s[rows, kc]
            if kpad > C:
                v_h = jnp.concatenate([v_h, jnp.zeros((kpad - C, DV), F32)], axis=0)
                kd_h = jnp.concatenate([kd_h, jnp.zeros((kpad - C, DK), F32)], axis=0)
            st_s[seq, h] = st * egl_s[pl.ds(cs, 1), kc] + _dot_tn(v_h.astype(BF16), kd_h.astype(BF16))
        return carry

    lax.fori_loop(0, R // C, chunk_body, 0)

    @pl.when(l == n_l - 1)
    def _():
        st_out_ref[...] = st_s[...]

    o = o_s[...]
    gn = gn_ref[...]
    o_heads = []
    for h in range(N_HEADS):
        vc = slice(h * DV, (h + 1) * DV)
        og_h = og[:, vc]
        o_heads.append(_rms(o[:, vc], gn) * (og_h * jax.nn.sigmoid(og_h)))

    @pl.when(l == 0)
    def _():
        ext_s[:, 0:HALO - POOL_BUF, :] = jnp.zeros((ns, HALO - POOL_BUF, POOL_W), F32)
        ext_s[:, HALO - POOL_BUF:HALO, :] = buf0_ref[...]

    pos = cfg.start_pos + l * tl + lax.broadcasted_iota(jnp.int32, (1, tl, 1), 1)
    z_groups = []
    for gi, w in enumerate(POOL_WINDOWS):
        gc = slice(gi * POOL_GROUP, (gi + 1) * POOL_GROUP)
        s = ext_s[:, HALO:HALO + tl, gc]
        for dd in range(1, w):
            s = s + ext_s[:, HALO - dd:HALO - dd + tl, gc]
        cnt = jnp.minimum(w, pos + 1).astype(F32)
        dmean = (s / cnt - ext_s[:, HALO:HALO + tl, gc]).reshape(R, POOL_GROUP)
        z_groups.append(_dot(dmean.astype(BF16), wpool_ref[gi]))
    zp = jnp.concatenate(z_groups, axis=1) * pscale_ref[...]

    @pl.when(l == n_l - 1)
    def _():
        bufo_ref[...] = ext_s[:, tl + HALO - POOL_BUF:tl + HALO, :]

    @pl.when(l < n_l - 1)
    def _():
        ext_s[:, 0:HALO, :] = ext_s[:, tl:tl + HALO, :]

    cat = jnp.concatenate(o_heads + [zp], axis=1).astype(BF16)
    x1 = x + _dot(cat, wo_ref[...])
    x1_ref[...] = x1
    hn = _rms(x1, n2_ref[...])
    for c in range(CHUNKS):
        h_ref[pl.ds(c, R, stride=CHUNKS), :] = hn[:, c * LANES:(c + 1) * LANES]
    logits = _dot3(hn, wrh_ref[...], wrl_ref[...]) + br_ref[...]
    lane = lax.broadcasted_iota(jnp.int32, (R, LANES), 1)
    lg = jnp.where(lane < N_EXPERTS, logits, -jnp.inf)
    vals, idxs = [], []
    for _ in range(TOP_K):
        m = jnp.max(lg, axis=1, keepdims=True)
        idx = jnp.min(jnp.where(lg == m, lane, LANES), axis=1, keepdims=True)
        vals.append(m)
        idxs.append(idx)
        lg = jnp.where(lane == idx, -jnp.inf, lg)
    exps = [jnp.exp(vv - vals[0]) for vv in vals]
    den = exps[0] + exps[1] + exps[2] + exps[3]
    ti = jnp.zeros((R, LANES), jnp.int32)
    gt = jnp.zeros((R, LANES), F32)
    for kk in range(TOP_K):
        ti = jnp.where(lane == kk, idxs[kk], ti)
        gt = jnp.where(lane == kk, exps[kk] / den, gt)
    topi_ref[...] = ti[:, 0:TOP_K]
    gates_ref[...] = gt[:, 0:TOP_K]


def _mixer_call(cfg, n_tok_all, row_block0, x, s0t, buf0, weights, aliased):
    B, L, _ = x.shape
    ns, tl, C = cfg.ns, cfg.tl, cfg.chunk
    R = ns * tl
    S = min(C, DIAG)
    n_off = C // S - 1
    W = C if n_off else R
    n_b, n_l = B // ns, L // tl
    assert B % ns == 0 and L % tl == 0 and tl % C == 0 and R % 8 == 0
    assert n_l == 1 or tl >= HALO

    def const(shape):
        return pl.BlockSpec(shape, lambda b, l: (0,) * len(shape))

    (n1, wmain, wgkh, wgkl, bgk, gn, wpool, pscale, wo, n2, wrh, wrl, br) = weights
    in_specs = [
        pl.BlockSpec((ns, tl, D_MODEL), lambda b, l: (b, l, 0)),
        pl.BlockSpec((ns, N_HEADS, DV, DK), lambda b, l: (b, 0, 0, 0)),
        pl.BlockSpec((ns, POOL_BUF, POOL_W), lambda b, l: (b, 0, 0)),
        const(n1.shape), const(wmain.shape), const(wgkh.shape), const(wgkl.shape), const(bgk.shape),
        const(gn.shape), const(wpool.shape), const(pscale.shape), const(wo.shape), const(n2.shape),
        const(wrh.shape), const(wrl.shape), const(br.shape),
    ] + [pl.BlockSpec(memory_space=pl.ANY)] * len(aliased)

    def tok_block(width):
        return pl.BlockSpec((R, width), lambda b, l: (row_block0 + b * n_l + l, 0))

    out_specs = [
        tok_block(D_MODEL),
        pl.BlockSpec((R * CHUNKS, LANES), lambda b, l: (row_block0 + b * n_l + l, 0)),
        tok_block(TOP_K), tok_block(TOP_K),
        pl.BlockSpec((ns, N_HEADS, DV, DK), lambda b, l: (b, 0, 0, 0)),
        pl.BlockSpec((ns, POOL_BUF, POOL_W), lambda b, l: (b, 0, 0)),
    ]
    out_shape = [
        jax.ShapeDtypeStruct((n_tok_all, D_MODEL), F32),
        jax.ShapeDtypeStruct((n_tok_all * CHUNKS, LANES), F32),
        jax.ShapeDtypeStruct((n_tok_all, TOP_K), jnp.int32),
        jax.ShapeDtypeStruct((n_tok_all, TOP_K), F32),
        jax.ShapeDtypeStruct((B, N_HEADS, DV, DK), F32),
        jax.ShapeDtypeStruct((B, POOL_BUF, POOL_W), F32),
    ]
    scratch = [
        pltpu.VMEM((ns, N_HEADS, DV, DK), F32),
        pltpu.VMEM((ns, tl + HALO, POOL_W), F32),
        pltpu.VMEM((R + HALO, KEY_W), F32),
        pltpu.VMEM((R + HALO, KEY_W), F32),
        pltpu.VMEM((R, KEY_W), F32),
        pltpu.VMEM((R, KEY_W), F32),
        pltpu.VMEM((R, KEY_W), F32),
        pltpu.VMEM((R, VAL_W), F32),
        pltpu.VMEM((R, N_HEADS * W), F32),
        pltpu.VMEM((R, VAL_W), F32),
        pltpu.VMEM((max(n_off, 1), R, KEY_W), F32),
        pltpu.VMEM((max(n_off, 1), R, KEY_W), F32),
    ]
    n_fixed = 16
    aliases = {n_fixed + i: i for i in range(len(aliased))}
    return pl.pallas_call(
        functools.partial(_mixer_kernel, cfg),
        grid=(n_b, n_l),
        in_specs=in_specs,
        out_specs=out_specs,
        out_shape=out_shape,
        scratch_shapes=scratch,
        input_output_aliases=aliases,
        compiler_params=pltpu.CompilerParams(
            dimension_semantics=("arbitrary", "arbitrary"), vmem_limit_bytes=VMEM_LIMIT),
        name="mixer",
    )(x, s0t, buf0, n1, wmain, wgkh, wgkl, bgk, gn, wpool, pscale, wo, n2, wrh, wrl, br, *aliased)


def _route_kernel(topi_ref, pos_ref, counts_ref, cnt_s, carry_s, gstart_s):
    ph = pl.program_id(0)
    i = pl.program_id(1)
    TT = topi_ref.shape[0]
    topi = topi_ref[...]
    lane = lax.broadcasted_iota(jnp.int32, (TT, LANES), 1)
    hot = jnp.zeros((TT, LANES), F32)
    for kk in range(TOP_K):
        hot = hot + (lane == topi[:, kk:kk + 1]).astype(F32)
    colsum = jnp.sum(hot, axis=0, keepdims=True)

    @pl.when((ph == 0) & (i == 0))
    def _():
        cnt_s[...] = jnp.zeros_like(cnt_s)

    @pl.when(ph == 0)
    def _():
        cnt_s[...] = cnt_s[...] + colsum

    @pl.when((ph == 1) & (i == 0))
    def _():
        cnt = cnt_s[...]
        counts_ref[...] = cnt.astype(jnp.int32)
        tiles = jnp.floor((cnt + (ROW_TILE - 1)) * (1.0 / ROW_TILE))
        tiles8 = jnp.broadcast_to(tiles, (8, LANES))
        ur = lax.broadcasted_iota(jnp.int32, (LANES, LANES), 0)
        uc = lax.broadcasted_iota(jnp.int32, (LANES, LANES), 1)
        upper = (ur < uc).astype(BF16)
        t_hi, t_lo = _split2(tiles8)
        excl = _dot(t_hi, upper) + _dot(t_lo, upper)
        gstart_s[...] = excl[0:1, :] * float(ROW_TILE)
        carry_s[...] = jnp.zeros_like(carry_s)

    @pl.when(ph == 1)
    def _():
        lr = lax.broadcasted_iota(jnp.int32, (TT, TT), 0)
        lc = lax.broadcasted_iota(jnp.int32, (TT, TT), 1)
        lower = (lc < lr).astype(BF16)
        rank = _dot(lower, hot.astype(BF16)) + carry_s[...] + gstart_s[...]
        out = jnp.zeros((TT, LANES), F32)
        for kk in range(TOP_K):
            pk = jnp.sum(jnp.where(lane == topi[:, kk:kk + 1], rank, 0.0), axis=1, keepdims=True)
            out = jnp.where(lane == kk, pk, out)
        pos_ref[...] = out[:, 0:TOP_K].astype(jnp.int32)
        carry_s[...] = carry_s[...] + colsum


def _route_call(topi):
    T = topi.shape[0]
    assert T % ROUTE_TILE == 0
    return pl.pallas_call(
        _route_kernel,
        grid=(2, T // ROUTE_TILE),
        in_specs=[pl.BlockSpec((ROUTE_TILE, TOP_K), lambda ph, i: (i, 0))],
        out_specs=[pl.BlockSpec((ROUTE_TILE, TOP_K), lambda ph, i: (i * ph, 0)),
                   pl.BlockSpec((1, LANES), lambda ph, i: (0, 0))],
        out_shape=[jax.ShapeDtypeStruct((T, TOP_K), jnp.int32),
                   jax.ShapeDtypeStruct((1, LANES), jnp.int32)],
        scratch_shapes=[pltpu.VMEM((1, LANES), F32)] * 3,
        compiler_params=pltpu.CompilerParams(dimension_semantics=("arbitrary", "arbitrary")),
        name="route",
    )(topi)


def _invperm_kernel(n_zero_steps, pos_ref, tok_ref):
    i = pl.program_id(0)

    @pl.when(i < n_zero_steps)
    def _():
        def clear(r, c):
            tok_ref[i * INV_BLOCK + r] = -1
            return c

        lax.fori_loop(0, INV_BLOCK, clear, 0, unroll=16)

    @pl.when(i >= n_zero_steps)
    def _():
        base = (i - n_zero_steps) * INV_BLOCK

        def put(n, c):
            tok_ref[pos_ref[n]] = base + n
            return c

        lax.fori_loop(0, INV_BLOCK, put, 0, unroll=16)


def _invperm_call(pos_flat, n_rows):
    n_pairs = pos_flat.shape[0]
    assert n_rows % INV_BLOCK == 0 and n_pairs % INV_BLOCK == 0
    n_zero = n_rows // INV_BLOCK
    return pl.pallas_call(
        functools.partial(_invperm_kernel, n_zero),
        grid=(n_zero + n_pairs // INV_BLOCK,),
        in_specs=[pl.BlockSpec((INV_BLOCK,), lambda i: (jnp.maximum(i - n_zero, 0),),
                               memory_space=pltpu.SMEM)],
        out_specs=pl.BlockSpec(memory_space=pltpu.SMEM),
        out_shape=jax.ShapeDtypeStruct((n_rows,), jnp.int32),
        compiler_params=pltpu.CompilerParams(dimension_semantics=("arbitrary",)),
        name="invperm",
    )(pos_flat)


def _row_copy_in(h_hbm, xbuf, sem, slot, r, pair):
    tok = jnp.maximum(pair, 0) >> 2
    return pltpu.make_async_copy(h_hbm.at[pl.ds(pl.multiple_of(tok * CHUNKS, CHUNKS), CHUNKS), :],
                                 xbuf.at[slot, pl.ds(r * CHUNKS, CHUNKS), :], sem.at[slot])


def _row_copy_out(ybuf, y_hbm, sem, slot, r, dst_row):
    return pltpu.make_async_copy(ybuf.at[slot, pl.ds(r * CHUNKS, CHUNKS), :],
                                 y_hbm.at[pl.ds(pl.multiple_of(dst_row * CHUNKS, CHUNKS), CHUNKS), :],
                                 sem.at[slot])


def _gather_tile(tab_ref, h_hbm, xbuf, sem, tile, slot):
    base = tile * ROW_TILE
    for r in range(ROW_TILE):
        _row_copy_in(h_hbm, xbuf, sem, slot, r, tab_ref[base + r]).start()


def _scatter_tile(n_pairs, tab_ref, te_ref, ybuf, y_hbm, sem, tile, slot):
    base = tile * ROW_TILE
    spare0 = n_pairs + te_ref[tile] * ROW_TILE
    for r in range(ROW_TILE):
        pair = tab_ref[base + r]
        _row_copy_out(ybuf, y_hbm, sem, slot, r, jnp.where(pair >= 0, pair, spare0 + r)).start()


def _wait_all_rows(buf, sem, slot):
    pltpu.make_async_copy(buf.at[slot], buf.at[slot], sem.at[slot]).wait()


def _experts_kernel(n_pairs, te_ref, nt_ref, tab_ref, h_hbm, wgu_ref, bgu_ref, wd_ref, bd_ref, y_hbm,
                    xbuf, ybuf, gsem, ssem, wgu_s, wd_s):
    i = pl.program_id(0)
    n_used = nt_ref[0]
    slot = i % 2

    @pl.when(i == 0)
    def _():
        _gather_tile(tab_ref, h_hbm, xbuf, gsem, 0, 0)

    changed = (i == 0) | (te_ref[i] != te_ref[jnp.maximum(i - 1, 0)])

    @pl.when(changed & (i < n_used))
    def _():
        wgu_s[...] = wgu_ref[0].astype(BF16)
        wd_s[...] = wd_ref[0].astype(BF16)

    @pl.when((i >= 2) & (i < n_used))
    def _():
        _wait_all_rows(ybuf, ssem, slot)

    @pl.when(i < n_used)
    def _():
        _gather_tile(tab_ref, h_hbm, xbuf, gsem, jnp.minimum(i + 1, n_used - 1), 1 - slot)
        _wait_all_rows(xbuf, gsem, slot)
        xs = jnp.concatenate([xbuf[slot, pl.ds(c, ROW_TILE, stride=CHUNKS), :] for c in range(CHUNKS)],
                             axis=1).astype(BF16)
        gu = _dot(xs, wgu_s[...]) + bgu_ref[0]
        gate = jnp.minimum(gu[:, 0:D_FF], SWIGLU_LIMIT)
        up = jnp.clip(gu[:, D_FF:2 * D_FF], -SWIGLU_LIMIT, SWIGLU_LIMIT)
        act = ((up + 1.0) * gate * jax.nn.sigmoid(SWIGLU_ALPHA * gate)).astype(BF16)
        y = _dot(act, wd_s[...]) + bd_ref[0]
        for c in range(CHUNKS):
            ybuf[slot, pl.ds(c, ROW_TILE, stride=CHUNKS), :] = y[:, c * LANES:(c + 1) * LANES]
        _scatter_tile(n_pairs, tab_ref, te_ref, ybuf, y_hbm, ssem, i, slot)

    @pl.when(i == n_used - 1)
    def _():
        _wait_all_rows(xbuf, gsem, 1 - slot)
        _wait_all_rows(ybuf, ssem, slot)

    @pl.when((i == n_used - 1) & (i >= 1))
    def _():
        _wait_all_rows(ybuf, ssem, 1 - slot)


def _experts_call(n_pairs, tile_e, n_used, pair_of_row, h_rows, w_gu, b_gu, w_down, b_down):
    n_tiles = tile_e.shape[0]
    grid_spec = pltpu.PrefetchScalarGridSpec(
        num_scalar_prefetch=3,
        grid=(n_tiles,),
        in_specs=[
            pl.BlockSpec(memory_space=pl.ANY),
            pl.BlockSpec((1, D_MODEL, 2 * D_FF), lambda i, te, nt, tab: (te[i], 0, 0)),
            pl.BlockSpec((1, 1, 2 * D_FF), lambda i, te, nt, tab: (te[i], 0, 0)),
            pl.BlockSpec((1, D_FF, D_MODEL), lambda i, te, nt, tab: (te[i], 0, 0)),
            pl.BlockSpec((1, 1, D_MODEL), lambda i, te, nt, tab: (te[i], 0, 0)),
        ],
        out_specs=pl.BlockSpec(memory_space=pl.ANY),
        scratch_shapes=[
            pltpu.VMEM((2, ROW_TILE * CHUNKS, LANES), F32),
            pltpu.VMEM((2, ROW_TILE * CHUNKS, LANES), F32),
            pltpu.SemaphoreType.DMA((2,)),
            pltpu.SemaphoreType.DMA((2,)),
            pltpu.VMEM((D_MODEL, 2 * D_FF), BF16),
            pltpu.VMEM((D_FF, D_MODEL), BF16),
        ],
    )
    return pl.pallas_call(
        functools.partial(_experts_kernel, n_pairs),
        grid_spec=grid_spec,
        out_shape=jax.ShapeDtypeStruct(((n_pairs + N_EXPERTS * ROW_TILE) * CHUNKS, LANES), F32),
        compiler_params=pltpu.CompilerParams(
            dimension_semantics=("arbitrary",), vmem_limit_bytes=VMEM_LIMIT),
        name="experts",
    )(tile_e, n_used, pair_of_row, h_rows, w_gu, b_gu.reshape(N_EXPERTS, 1, 2 * D_FF),
      w_down, b_down.reshape(N_EXPERTS, 1, D_MODEL))


def _final_kernel(y_ref, x1_ref, gates_ref, fn_ref, out_ref):
    gates = gates_ref[...]
    cols = []
    for c in range(CHUNKS):
        acc = y_ref[pl.ds(c, TOK_TILE, stride=TOP_K * CHUNKS), :] * gates[:, 0:1]
        for kk in range(1, TOP_K):
            acc = acc + y_ref[pl.ds(kk * CHUNKS + c, TOK_TILE, stride=TOP_K * CHUNKS), :] * gates[:, kk:kk + 1]
        cols.append(acc)
    out_ref[...] = _rms(x1_ref[...] + jnp.concatenate(cols, axis=1), fn_ref[...])


def _final_call(tok0, n_tok, y_rows, x1_all, gates_all, final_norm):
    assert tok0 % TOK_TILE == 0 and n_tok % TOK_TILE == 0
    tile0 = tok0 // TOK_TILE
    return pl.pallas_call(
        _final_kernel,
        grid=(n_tok // TOK_TILE,),
        in_specs=[
            pl.BlockSpec((TOK_TILE * TOP_K * CHUNKS, LANES), lambda i: (tile0 + i, 0)),
            pl.BlockSpec((TOK_TILE, D_MODEL), lambda i: (tile0 + i, 0)),
            pl.BlockSpec((TOK_TILE, TOP_K), lambda i: (tile0 + i, 0)),
            pl.BlockSpec((1, D_MODEL), lambda i: (0, 0)),
        ],
        out_specs=pl.BlockSpec((TOK_TILE, D_MODEL), lambda i: (i, 0)),
        out_shape=jax.ShapeDtypeStruct((n_tok, D_MODEL), F32),
        compiler_params=pltpu.CompilerParams(
            dimension_semantics=("arbitrary",), vmem_limit_bytes=VMEM_LIMIT),
        name="final",
    )(y_rows, x1_all, gates_all, final_norm)


def _pick_tile(n, target):
    t = min(n, target)
    while n % t:
        t -= 1
    return t


def kernel(x_prompt, x_sample, state_gla, state_pool, norm1, w_in, w_gk2, b_gk, gla_norm, w_pool,
           pool_scale, w_o, norm2, w_router, b_router, w_gate_up, b_gate_up, w_down, b_down, final_norm):
    depth = w_in.shape[0]
    assert depth == 1
    B, L, _ = x_prompt.shape
    BS, LS, _ = x_sample.shape
    n_p, n_s = B * L, BS * LS
    n_all = n_p + n_s

    wi = w_in[0]
    o_glr = 2 * KEY_W + 2 * VAL_W
    wmain = jnp.concatenate(
        [wi[:, 0:o_glr], wi[:, o_glr + GATE_RANK:], wi[:, o_glr:o_glr + GATE_RANK],
         jnp.zeros((D_MODEL, LANES - GATE_RANK), F32)], axis=1).astype(BF16)
    wgk = jnp.concatenate([w_gk2[0], jnp.zeros((LANES - GATE_RANK, KEY_W), F32)], axis=0)
    wgkh, wgkl = _split2(wgk)
    wr = jnp.concatenate([w_router[0], jnp.zeros((D_MODEL, LANES - N_EXPERTS), F32)], axis=1)
    wrh, wrl = _split2(wr)
    br = jnp.concatenate([b_router[0], jnp.zeros((LANES - N_EXPERTS,), F32)]).reshape(1, LANES)
    weights = (norm1[0].reshape(1, D_MODEL), wmain, wgkh, wgkl, b_gk[0].reshape(1, KEY_W),
               gla_norm[0].reshape(1, DV), w_pool[0].astype(BF16), pool_scale[0].reshape(1, POOL_W),
               w_o[0].astype(BF16), norm2[0].reshape(1, D_MODEL), wrh, wrl, br)

    assert L % GLA_CHUNK == 0 and LS in (8, 16)
    cfg_p = MixerCfg(ns=1, tl=_pick_tile(L, 256), chunk=GLA_CHUNK, start_pos=0, n_alias=0)
    s0_p = jnp.zeros((B, N_HEADS, DV, DK), F32)
    buf0_p = jnp.zeros((B, POOL_BUF, POOL_W), F32)
    x1_all, h_all, topi_all, gates_all, st_p, buf_p = _mixer_call(
        cfg_p, n_all, 0, x_prompt, s0_p, buf0_p, weights, ())

    cfg_s = MixerCfg(ns=_pick_tile(BS, 128 // LS), tl=LS, chunk=LS, start_pos=PAST_LEN, n_alias=4)
    r_s = cfg_s.ns * cfg_s.tl
    assert n_p % r_s == 0
    x1_all, h_all, topi_all, gates_all, st_s, buf_s = _mixer_call(
        cfg_s, n_all, n_p // r_s, x_sample, jnp.swapaxes(state_gla[0], -1, -2), state_pool[0], weights,
        (x1_all, h_all, topi_all, gates_all))

    pos, counts = _route_call(topi_all)
    n_tiles = (n_all * TOP_K + N_EXPERTS * (ROW_TILE - 1)) // ROW_TILE
    tiles_per_e = (counts[0, :N_EXPERTS] + (ROW_TILE - 1)) // ROW_TILE
    ends = jnp.cumsum(tiles_per_e)
    n_used = ends[-1].astype(jnp.int32)
    tile_ids = jnp.minimum(jnp.arange(n_tiles, dtype=jnp.int32), n_used - 1)
    tile_e = jnp.sum(tile_ids[:, None] >= ends[None, :], axis=1).astype(jnp.int32)
    n_pairs = n_all * TOP_K
    pair_of_row = _invperm_call(pos.reshape(n_pairs), -(-n_tiles * ROW_TILE // INV_BLOCK) * INV_BLOCK)

    y_rows = _experts_call(n_pairs, tile_e, n_used.reshape(1), pair_of_row, h_all,
                           w_gate_up[0], b_gate_up[0], w_down[0], b_down[0])

    fn = final_norm.reshape(1, D_MODEL)
    y_p = _final_call(0, n_p, y_rows, x1_all, gates_all, fn)
    y_s = _final_call(n_p, n_s, y_rows, x1_all, gates_all, fn)

    return (y_p.reshape(B, L, D_MODEL), y_s.reshape(BS, LS, D_MODEL),
            jnp.swapaxes(st_p, -1, -2)[None], buf_p[None],
            jnp.swapaxes(st_s, -1, -2)[None], buf_s[None])
```

```python
import functools
from typing import NamedTuple

import jax
import jax.numpy as jnp
from jax import lax
from jax.experimental import pallas as pl
from jax.experimental.pallas import tpu as pltpu

F32 = jnp.float32
BF16 = jnp.bfloat16

D_MODEL = 1024
N_HEADS = 4
DK = 64
DV = 128
KEY_W = N_HEADS * DK
VAL_W = N_HEADS * DV
GATE_RANK = 16
GATE_NORMALIZER = 16.0
GLA_CHUNK = 64
POOL_WINDOWS = (2, 4, 8, 16)
POOL_W = 512
POOL_GROUP = 128
POOL_BUF = 15
N_EXPERTS = 32
TOP_K = 4
D_FF = 1024
SWIGLU_LIMIT = 7.0
SWIGLU_ALPHA = 1.702
EPS = 1e-5
PAST_LEN = 16384

LANES = 128
CHUNKS = D_MODEL // LANES
HALO = 16
DIAG = 16
MAIN_COLS = 2 * KEY_W + 2 * VAL_W + POOL_W + LANES
ROW_TILE = 256
TOK_TILE = 256
ROUTE_TILE = 512
INV_BLOCK = 2048
VMEM_LIMIT = 56 * 1024 * 1024
NEG_BIG = -1e30


def _dot(a, b):
    return jnp.dot(a, b, preferred_element_type=F32)


def _dot_nt(a, b):
    return lax.dot_general(a, b, (((1,), (1,)), ((), ())), preferred_element_type=F32)


def _dot_tn(a, b):
    return lax.dot_general(a, b, (((0,), (0,)), ((), ())), preferred_element_type=F32)


def _split2(a):
    hi = a.astype(BF16)
    lo = (a - hi.astype(F32)).astype(BF16)
    return hi, lo


def _dot3(a, b_hi, b_lo):
    a_hi, a_lo = _split2(a)
    return _dot(a_hi, b_hi) + _dot(a_lo, b_hi) + _dot(a_hi, b_lo)


def _rms(x, w):
    return x * lax.rsqrt(jnp.mean(x * x, axis=-1, keepdims=True) + EPS) * w


class MixerCfg(NamedTuple):
    ns: int
    tl: int
    chunk: int
    start_pos: int
    n_alias: int


def _mixer_kernel(cfg, x_ref, s0_ref, buf0_ref, n1_ref, wmain_ref, wgkh_ref, wgkl_ref, bgk_ref,
                  gn_ref, wpool_ref, pscale_ref, wo_ref, n2_ref, wrh_ref, wrl_ref, br_ref, *rest):
    rest = rest[cfg.n_alias:]
    x1_ref, h_ref, topi_ref, gates_ref, st_out_ref, bufo_ref = rest[:6]
    st_s, ext_s, kh_s, gh_s, qg_s, kd_s, egl_s, v_s, a_s, o_s, qt_s, kt_s = rest[6:]

    ns, tl, C = cfg.ns, cfg.tl, cfg.chunk
    R = ns * tl
    S = min(C, DIAG)
    n_off = C // S - 1
    W = C if n_off else R
    l = pl.program_id(1)
    n_l = pl.num_programs(1)

    x = x_ref[...].reshape(R, D_MODEL)
    xn = _rms(x, n1_ref[...]).astype(BF16)
    p = _dot(xn, wmain_ref[...])
    q = p[:, 0:KEY_W] * (DK ** -0.5)
    k = p[:, KEY_W:2 * KEY_W]
    v = p[:, 2 * KEY_W:2 * KEY_W + VAL_W]
    og = p[:, 2 * KEY_W + VAL_W:2 * KEY_W + 2 * VAL_W]
    u = p[:, 2 * KEY_W + 2 * VAL_W:2 * KEY_W + 2 * VAL_W + POOL_W]
    glr = p[:, MAIN_COLS - LANES:MAIN_COLS]
    ext_s[:, HALO:HALO + tl, :] = u.reshape(ns, tl, POOL_W)

    z = _dot3(glr, wgkh_ref[...], wgkl_ref[...]) + bgk_ref[...]
    g = -(jnp.maximum(-z, 0.0) + jnp.log1p(jnp.exp(-jnp.abs(z)))) / GATE_NORMALIZER
    ri = lax.broadcasted_iota(jnp.int32, (R, R), 0)
    ci = lax.broadcasted_iota(jnp.int32, (R, R), 1)
    tri = ((ri // C == ci // C) & (ci <= ri)).astype(BF16)
    g_hi = g.astype(BF16)
    g_r = g - g_hi.astype(F32)
    g_mid = g_r.astype(BF16)
    g_lo = (g_r - g_mid.astype(F32)).astype(BF16)
    G = _dot(tri, g_hi) + _dot(tri, g_mid) + _dot(tri, g_lo)

    G3 = G.reshape(R // C, C, KEY_W)
    glast = jnp.broadcast_to(G3[:, C - 1:C, :], (R // C, C, KEY_W)).reshape(R, KEY_W)
    qg_s[...] = q * jnp.exp(G)
    kd_s[...] = k * jnp.exp(glast - G)
    egl_s[...] = jnp.exp(glast)
    v_s[...] = v

    kh_s[0:HALO, :] = jnp.zeros((HALO, KEY_W), F32)
    gh_s[0:HALO, :] = jnp.zeros((HALO, KEY_W), F32)
    kh_s[HALO:HALO + R, :] = k
    gh_s[HALO:HALO + R, :] = G
    row = lax.broadcasted_iota(jnp.int32, (R, 1), 0)
    row_s = row % S
    row_w = row % W
    hb_r = lax.broadcasted_iota(jnp.int32, (KEY_W, N_HEADS * W), 0)
    hb_c = lax.broadcasted_iota(jnp.int32, (KEY_W, N_HEADS * W), 1)
    head_bcast = (hb_r // DK == hb_c // W).astype(BF16)
    col_w = lax.broadcasted_iota(jnp.int32, (1, N_HEADS * W), 1) % W
    a_all = jnp.zeros((R, N_HEADS * W), F32)
    for d in range(S):
        k_sh = kh_s[HALO - d:HALO - d + R, :]
        g_sh = gh_s[HALO - d:HALO - d + R, :]
        valid = row_s >= d
        e = jnp.exp(jnp.where(valid, G - g_sh, NEG_BIG))
        term = (q * k_sh * e).astype(BF16)
        spread = _dot(term, head_bcast)
        a_all = jnp.where(col_w == row_w - d, spread, a_all)
    a_s[...] = a_all

    row_c = row % C
    for a in range(1, n_off + 1):
        ra = jnp.broadcast_to(G3[:, a * S - 1:a * S, :], (R // C, C, KEY_W)).reshape(R, KEY_W)
        qt_s[a - 1] = q * jnp.exp(jnp.minimum(G - ra, 0.0))
        kt_s[a - 1] = jnp.where(row_c < a * S, k * jnp.exp(jnp.minimum(ra - G, 0.0)), 0.0)

    @pl.when(l == 0)
    def _():
        st_s[...] = s0_ref[...]

    if not n_off:
        a_bf = a_s[...].astype(BF16)
        v_bf = v.astype(BF16)
        for h in range(N_HEADS):
            o_s[:, h * DV:(h + 1) * DV] = _dot(a_bf[:, h * W:(h + 1) * W], v_bf[:, h * DV:(h + 1) * DV])

    chunks_per_seq = tl // C
    kpad = max(C, 16)

    def chunk_body(c, carry):
        cs = pl.multiple_of(c * C, C)
        seq = c // chunks_per_seq
        rows = pl.ds(cs, C)
        for h in range(N_HEADS):
            kc = slice(h * DK, (h + 1) * DK)
            vc = slice(h * DV, (h + 1) * DV)
            st = st_s[seq, h]
            v_h = v_s[rows, vc]
            o_h = _dot_nt(qg_s[rows, kc].astype(BF16), st.astype(BF16))
            if n_off:
                blocks = [jnp.zeros((S, C), F32)]
                for a in range(1, n_off + 1):
                    qa = qt_s[a - 1, pl.ds(cs + a * S, S), kc].astype(BF16)
                    ka = kt_s[a - 1, rows, kc].astype(BF16)
                    blocks.append(_dot_nt(qa, ka))
                a_h = a_s[rows, h * W:(h + 1) * W] + jnp.concatenate(blocks, axis=0)
                o_s[rows, vc] = o_h + _dot(a_h.astype(BF16), v_h.astype(BF16))
            else:
                o_s[rows, vc] = o_s[rows, vc] + o_h
            kd_h = kd_s[rows, kc]
            if kpad > C:
                v_h = jnp.concatenate([v_h, jnp.zeros((kpad - C, DV), F32)], axis=0)
                kd_h = jnp.concatenate([kd_h, jnp.zeros((kpad - C, DK), F32)], axis=0)
            st_s[seq, h] = st * egl_s[pl.ds(cs, 1), kc] + _dot_tn(v_h.astype(BF16), kd_h.astype(BF16))
        return carry

    lax.fori_loop(0, R // C, chunk_body, 0)

    @pl.when(l == n_l - 1)
    def _():
        st_out_ref[...] = st_s[...]

    o = o_s[...]
    gn = gn_ref[...]
    o_heads = []
    for h in range(N_HEADS):
        vc = slice(h * DV, (h + 1) * DV)
        og_h = og[:, vc]
        o_heads.append(_rms(o[:, vc], gn) * (og_h * jax.nn.sigmoid(og_h)))

    @pl.when(l == 0)
    def _():
        ext_s[:, 0:HALO - POOL_BUF, :] = jnp.zeros((ns, HALO - POOL_BUF, POOL_W), F32)
        ext_s[:, HALO - POOL_BUF:HALO, :] = buf0_ref[...]

    pos = cfg.start_pos + l * tl + lax.broadcasted_iota(jnp.int32, (1, tl, 1), 1)
    z_groups = []
    for gi, w in enumerate(POOL_WINDOWS):
        gc = slice(gi * POOL_GROUP, (gi + 1) * POOL_GROUP)
        s = ext_s[:, HALO:HALO + tl, gc]
        for dd in range(1, w):
            s = s + ext_s[:, HALO - dd:HALO - dd + tl, gc]
        cnt = jnp.minimum(w, pos + 1).astype(F32)
        dmean = (s / cnt - ext_s[:, HALO:HALO + tl, gc]).reshape(R, POOL_GROUP)
        z_groups.append(_dot(dmean.astype(BF16), wpool_ref[gi]))
    zp = jnp.concatenate(z_groups, axis=1) * pscale_ref[...]

    @pl.when(l == n_l - 1)
    def _():
        bufo_ref[...] = ext_s[:, tl + HALO - POOL_BUF:tl + HALO, :]

    @pl.when(l < n_l - 1)
    def _():
        ext_s[:, 0:HALO, :] = ext_s[:, tl:tl + HALO, :]

    cat = jnp.concatenate(o_heads + [zp], axis=1).astype(BF16)
    x1 = x + _dot(cat, wo_ref[...])
    x1_ref[...] = x1
    hn = _rms(x1, n2_ref[...])
    for c in range(CHUNKS):
        h_ref[pl.ds(c, R, stride=CHUNKS), :] = hn[:, c * LANES:(c + 1) * LANES]
    logits = _dot3(hn, wrh_ref[...], wrl_ref[...]) + br_ref[...]
    lane = lax.broadcasted_iota(jnp.int32, (R, LANES), 1)
    lg = jnp.where(lane < N_EXPERTS, logits, -jnp.inf)
    vals, idxs = [], []
    for _ in range(TOP_K):
        m = jnp.max(lg, axis=1, keepdims=True)
        idx = jnp.min(jnp.where(lg == m, lane, LANES), axis=1, keepdims=True)
        vals.append(m)
        idxs.append(idx)
        lg = jnp.where(lane == idx, -jnp.inf, lg)
    exps = [jnp.exp(vv - vals[0]) for vv in vals]
    den = exps[0] + exps[1] + exps[2] + exps[3]
    ti = jnp.zeros((R, LANES), jnp.int32)
    gt = jnp.zeros((R, LANES), F32)
    for kk in range(TOP_K):
        ti = jnp.where(lane == kk, idxs[kk], ti)
        gt = jnp.where(lane == kk, exps[kk] / den, gt)
    topi_ref[...] = ti[:, 0:TOP_K]
    gates_ref[...] = gt[:, 0:TOP_K]


def _mixer_call(cfg, n_tok_all, row_block0, x, s0t, buf0, weights, aliased):
    B, L, _ = x.shape
    ns, tl, C = cfg.ns, cfg.tl, cfg.chunk
    R = ns * tl
    S = min(C, DIAG)
    n_off = C // S - 1
    W = C if n_off else R
    n_b, n_l = B // ns, L // tl
    assert B % ns == 0 and L % tl == 0 and tl % C == 0 and R % 8 == 0
    assert n_l == 1 or tl >= HALO

    def const(shape):
        return pl.BlockSpec(shape, lambda b, l: (0,) * len(shape))

    (n1, wmain, wgkh, wgkl, bgk, gn, wpool, pscale, wo, n2, wrh, wrl, br) = weights
    in_specs = [
        pl.BlockSpec((ns, tl, D_MODEL), lambda b, l: (b, l, 0)),
        pl.BlockSpec((ns, N_HEADS, DV, DK), lambda b, l: (b, 0, 0, 0)),
        pl.BlockSpec((ns, POOL_BUF, POOL_W), lambda b, l: (b, 0, 0)),
        const(n1.shape), const(wmain.shape), const(wgkh.shape), const(wgkl.shape), const(bgk.shape),
        const(gn.shape), const(wpool.shape), const(pscale.shape), const(wo.shape), const(n2.shape),
        const(wrh.shape), const(wrl.shape), const(br.shape),
    ] + [pl.BlockSpec(memory_space=pl.ANY)] * len(aliased)

    def tok_block(width):
        return pl.BlockSpec((R, width), lambda b, l: (row_block0 + b * n_l + l, 0))

    out_specs = [
        tok_block(D_MODEL),
        pl.BlockSpec((R * CHUNKS, LANES), lambda b, l: (row_block0 + b * n_l + l, 0)),
        tok_block(TOP_K), tok_block(TOP_K),
        pl.BlockSpec((ns, N_HEADS, DV, DK), lambda b, l: (b, 0, 0, 0)),
        pl.BlockSpec((ns, POOL_BUF, POOL_W), lambda b, l: (b, 0, 0)),
    ]
    out_shape = [
        jax.ShapeDtypeStruct((n_tok_all, D_MODEL), F32),
        jax.ShapeDtypeStruct((n_tok_all * CHUNKS, LANES), F32),
        jax.ShapeDtypeStruct((n_tok_all, TOP_K), jnp.int32),
        jax.ShapeDtypeStruct((n_tok_all, TOP_K), F32),
        jax.ShapeDtypeStruct((B, N_HEADS, DV, DK), F32),
        jax.ShapeDtypeStruct((B, POOL_BUF, POOL_W), F32),
    ]
    scratch = [
        pltpu.VMEM((ns, N_HEADS, DV, DK), F32),
        pltpu.VMEM((ns, tl + HALO, POOL_W), F32),
        pltpu.VMEM((R + HALO, KEY_W), F32),
        pltpu.VMEM((R + HALO, KEY_W), F32),
        pltpu.VMEM((R, KEY_W), F32),
        pltpu.VMEM((R, KEY_W), F32),
        pltpu.VMEM((R, KEY_W), F32),
        pltpu.VMEM((R, VAL_W), F32),
        pltpu.VMEM((R, N_HEADS * W), F32),
        pltpu.VMEM((R, VAL_W), F32),
        pltpu.VMEM((max(n_off, 1), R, KEY_W), F32),
        pltpu.VMEM((max(n_off, 1), R, KEY_W), F32),
    ]
    n_fixed = 16
    aliases = {n_fixed + i: i for i in range(len(aliased))}
    return pl.pallas_call(
        functools.partial(_mixer_kernel, cfg),
        grid=(n_b, n_l),
        in_specs=in_specs,
        out_specs=out_specs,
        out_shape=out_shape,
        scratch_shapes=scratch,
        input_output_aliases=aliases,
        compiler_params=pltpu.CompilerParams(
            dimension_semantics=("arbitrary", "arbitrary"), vmem_limit_bytes=VMEM_LIMIT),
        name="mixer",
    )(x, s0t, buf0, n1, wmain, wgkh, wgkl, bgk, gn, wpool, pscale, wo, n2, wrh, wrl, br, *aliased)


def _route_kernel(topi_ref, pos_ref, counts_ref, cnt_s, carry_s, gstart_s):
    ph = pl.program_id(0)
    i = pl.program_id(1)
    TT = topi_ref.shape[0]
    topi = topi_ref[...]
    lane = lax.broadcasted_iota(jnp.int32, (TT, LANES), 1)
    hot = jnp.zeros((TT, LANES), F32)
    for kk in range(TOP_K):
        hot = hot + (lane == topi[:, kk:kk + 1]).astype(F32)
    colsum = jnp.sum(hot, axis=0, keepdims=True)

    @pl.when((ph == 0) & (i == 0))
    def _():
        cnt_s[...] = jnp.zeros_like(cnt_s)

    @pl.when(ph == 0)
    def _():
        cnt_s[...] = cnt_s[...] + colsum

    @pl.when((ph == 1) & (i == 0))
    def _():
        cnt = cnt_s[...]
        counts_ref[...] = cnt.astype(jnp.int32)
        tiles = jnp.floor((cnt + (ROW_TILE - 1)) * (1.0 / ROW_TILE))
        tiles8 = jnp.broadcast_to(tiles, (8, LANES))
        ur = lax.broadcasted_iota(jnp.int32, (LANES, LANES), 0)
        uc = lax.broadcasted_iota(jnp.int32, (LANES, LANES), 1)
        upper = (ur < uc).astype(BF16)
        t_hi, t_lo = _split2(tiles8)
        excl = _dot(t_hi, upper) + _dot(t_lo, upper)
        gstart_s[...] = excl[0:1, :] * float(ROW_TILE)
        carry_s[...] = jnp.zeros_like(carry_s)

    @pl.when(ph == 1)
    def _():
        lr = lax.broadcasted_iota(jnp.int32, (TT, TT), 0)
        lc = lax.broadcasted_iota(jnp.int32, (TT, TT), 1)
        lower = (lc < lr).astype(BF16)
        rank = _dot(lower, hot.astype(BF16)) + carry_s[...] + gstart_s[...]
        out = jnp.zeros((TT, LANES), F32)
        for kk in range(TOP_K):
            pk = jnp.sum(jnp.where(lane == topi[:, kk:kk + 1], rank, 0.0), axis=1, keepdims=True)
            out = jnp.where(lane == kk, pk, out)
        pos_ref[...] = out[:, 0:TOP_K].astype(jnp.int32)
        carry_s[...] = carry_s[...] + colsum


def _route_call(topi):
    T = topi.shape[0]
    assert T % ROUTE_TILE == 0
    return pl.pallas_call(
        _route_kernel,
        grid=(2, T // ROUTE_TILE),
        in_specs=[pl.BlockSpec((ROUTE_TILE, TOP_K), lambda ph, i: (i, 0))],
        out_specs=[pl.BlockSpec((ROUTE_TILE, TOP_K), lambda ph, i: (i * ph, 0)),
                   pl.BlockSpec((1, LANES), lambda ph, i: (0, 0))],
        out_shape=[jax.ShapeDtypeStruct((T, TOP_K), jnp.int32),
                   jax.ShapeDtypeStruct((1, LANES), jnp.int32)],
        scratch_shapes=[pltpu.VMEM((1, LANES), F32)] * 3,
        compiler_params=pltpu.CompilerParams(dimension_semantics=("arbitrary", "arbitrary")),
        name="route",
    )(topi)


def _invperm_kernel(n_zero_steps, pos_ref, tok_ref):
    i = pl.program_id(0)

    @pl.when(i < n_zero_steps)
    def _():
        def clear(r, c):
            tok_ref[i * INV_BLOCK + r] = -1
            return c

        lax.fori_loop(0, INV_BLOCK, clear, 0, unroll=16)

    @pl.when(i >= n_zero_steps)
    def _():
        base = (i - n_zero_steps) * INV_BLOCK

        def put(n, c):
            tok_ref[pos_ref[n]] = base + n
            return c

        lax.fori_loop(0, INV_BLOCK, put, 0, unroll=16)


def _invperm_call(pos_flat, n_rows):
    n_pairs = pos_flat.shape[0]
    assert n_rows % INV_BLOCK == 0 and n_pairs % INV_BLOCK == 0
    n_zero = n_rows // INV_BLOCK
    return pl.pallas_call(
        functools.partial(_invperm_kernel, n_zero),
        grid=(n_zero + n_pairs // INV_BLOCK,),
        in_specs=[pl.BlockSpec((INV_BLOCK,), lambda i: (jnp.maximum(i - n_zero, 0),),
                               memory_space=pltpu.SMEM)],
        out_specs=pl.BlockSpec(memory_space=pltpu.SMEM),
        out_shape=jax.ShapeDtypeStruct((n_rows,), jnp.int32),
        compiler_params=pltpu.CompilerParams(dimension_semantics=("arbitrary",)),
        name="invperm",
    )(pos_flat)


def _row_copy_in(h_hbm, xb, sem, r, pair):
    tok = jnp.maximum(pair, 0) >> 2
    return pltpu.make_async_copy(h_hbm.at[pl.ds(pl.multiple_of(tok * CHUNKS, CHUNKS), CHUNKS), :],
                                 xb.at[pl.ds(r * CHUNKS, CHUNKS), :], sem)


def _row_copy_out(yb, y_hbm, sem, r, dst_row):
    return pltpu.make_async_copy(yb.at[pl.ds(r * CHUNKS, CHUNKS), :],
                                 y_hbm.at[pl.ds(pl.multiple_of(dst_row * CHUNKS, CHUNKS), CHUNKS), :], sem)


def _gather_tile(tab_ref, h_hbm, xb, sem, tile, unrolled):
    base = tile * ROW_TILE
    if unrolled:
        for r in range(ROW_TILE):
            _row_copy_in(h_hbm, xb, sem, r, tab_ref[base + r]).start()
    else:
        def issue(r, c):
            _row_copy_in(h_hbm, xb, sem, r, tab_ref[base + r]).start()
            return c

        lax.fori_loop(0, ROW_TILE, issue, 0)


def _scatter_tile(n_pairs, tab_ref, te_ref, yb, y_hbm, sem, tile, unrolled):
    base = tile * ROW_TILE
    spare0 = n_pairs + te_ref[tile] * ROW_TILE

    def dst_row(r):
        pair = tab_ref[base + r]
        return jnp.where(pair >= 0, pair, spare0 + r)

    if unrolled:
        for r in range(ROW_TILE):
            _row_copy_out(yb, y_hbm, sem, r, dst_row(r)).start()
    else:
        def issue(r, c):
            _row_copy_out(yb, y_hbm, sem, r, dst_row(r)).start()
            return c

        lax.fori_loop(0, ROW_TILE, issue, 0)


def _wait_all_rows(buf, sem):
    pltpu.make_async_copy(buf, buf, sem).wait()


def _expert_mlp(xb, yb, wgu_s, wd_s, bgu_ref, bd_ref):
    xs = jnp.concatenate([xb[pl.ds(c, ROW_TILE, stride=CHUNKS), :] for c in range(CHUNKS)],
                         axis=1).astype(BF16)
    gu = _dot(xs, wgu_s[...]) + bgu_ref[0]
    gate = jnp.minimum(gu[:, 0:D_FF], SWIGLU_LIMIT)
    up = jnp.clip(gu[:, D_FF:2 * D_FF], -SWIGLU_LIMIT, SWIGLU_LIMIT)
    act = ((up + 1.0) * gate * jax.nn.sigmoid(SWIGLU_ALPHA * gate)).astype(BF16)
    y = _dot(act, wd_s[...]) + bd_ref[0]
    for c in range(CHUNKS):
        yb[pl.ds(c, ROW_TILE, stride=CHUNKS), :] = y[:, c * LANES:(c + 1) * LANES]


def _experts_kernel(n_pairs, te_ref, nt_ref, tab_ref, h_hbm, wgu_ref, bgu_ref, wd_ref, bd_ref, y_hbm,
                    x0, x1, y0, y1, gsem, ssem, wgu_s, wd_s):
    s = pl.program_id(0)
    n_used = nt_ref[0]
    par = s % 2
    tc = jnp.clip(s - 1, 0, te_ref.shape[0] - 1)
    xs, ys = (x0, x1), (y0, y1)

    @pl.when((s >= 1) & (s <= n_used) & ((s == 1) | (te_ref[tc] != te_ref[jnp.maximum(tc - 1, 0)])))
    def _():
        wgu_s[...] = wgu_ref[0].astype(BF16)
        wd_s[...] = wd_ref[0].astype(BF16)

    steady = (s >= 2) & (s < n_used)
    edge = jnp.logical_not(steady) & (s <= n_used + 1)
    for p in (0, 1):
        q = 1 - p
        mine = par == p

        @pl.when(mine & (s >= 3) & (s <= n_used + 1))
        def _(q=q):
            _wait_all_rows(ys[q], ssem.at[q])

        @pl.when(mine & steady)
        def _(p=p, q=q):
            _wait_all_rows(xs[q], gsem.at[q])
            _gather_tile(tab_ref, h_hbm, xs[p], gsem.at[p], s, True)
            _scatter_tile(n_pairs, tab_ref, te_ref, ys[p], y_hbm, ssem.at[p], s - 2, True)
            _expert_mlp(xs[q], ys[q], wgu_s, wd_s, bgu_ref, bd_ref)

        @pl.when(mine & edge & (s >= 1) & (s <= n_used))
        def _(q=q):
            _wait_all_rows(xs[q], gsem.at[q])

        @pl.when(mine & edge & (s < n_used))
        def _(p=p):
            _gather_tile(tab_ref, h_hbm, xs[p], gsem.at[p], s, False)

        @pl.when(mine & edge & (s >= 2))
        def _(p=p):
            _scatter_tile(n_pairs, tab_ref, te_ref, ys[p], y_hbm, ssem.at[p], s - 2, False)

        @pl.when(mine & edge & (s >= 1) & (s <= n_used))
        def _(q=q):
            _expert_mlp(xs[q], ys[q], wgu_s, wd_s, bgu_ref, bd_ref)

        @pl.when(mine & (s == n_used + 1))
        def _(p=p):
            _wait_all_rows(ys[p], ssem.at[p])


def _experts_call(n_pairs, tile_e, n_used, pair_of_row, h_rows, w_gu, b_gu, w_down, b_down):
    n_tiles = tile_e.shape[0]

    def expert_block(s, te, nt, tab):
        return (te[jnp.clip(s - 1, 0, n_tiles - 1)], 0, 0)

    grid_spec = pltpu.PrefetchScalarGridSpec(
        num_scalar_prefetch=3,
        grid=(n_tiles + 2,),
        in_specs=[
            pl.BlockSpec(memory_space=pl.ANY),
            pl.BlockSpec((1, D_MODEL, 2 * D_FF), expert_block),
            pl.BlockSpec((1, 1, 2 * D_FF), expert_block),
            pl.BlockSpec((1, D_FF, D_MODEL), expert_block),
            pl.BlockSpec((1, 1, D_MODEL), expert_block),
        ],
        out_specs=pl.BlockSpec(memory_space=pl.ANY),
        scratch_shapes=[
            pltpu.VMEM((ROW_TILE * CHUNKS, LANES), F32),
            pltpu.VMEM((ROW_TILE * CHUNKS, LANES), F32),
            pltpu.VMEM((ROW_TILE * CHUNKS, LANES), F32),
            pltpu.VMEM((ROW_TILE * CHUNKS, LANES), F32),
            pltpu.SemaphoreType.DMA((2,)),
            pltpu.SemaphoreType.DMA((2,)),
            pltpu.VMEM((D_MODEL, 2 * D_FF), BF16),
            pltpu.VMEM((D_FF, D_MODEL), BF16),
        ],
    )
    return pl.pallas_call(
        functools.partial(_experts_kernel, n_pairs),
        grid_spec=grid_spec,
        out_shape=jax.ShapeDtypeStruct(((n_pairs + N_EXPERTS * ROW_TILE) * CHUNKS, LANES), F32),
        compiler_params=pltpu.CompilerParams(
            dimension_semantics=("arbitrary",), vmem_limit_bytes=VMEM_LIMIT),
        name="experts",
    )(tile_e, n_used, pair_of_row, h_rows, w_gu, b_gu.reshape(N_EXPERTS, 1, 2 * D_FF),
      w_down, b_down.reshape(N_EXPERTS, 1, D_MODEL))


def _final_kernel(y_ref, x1_ref, gates_ref, fn_ref, out_ref):
    gates = gates_ref[...]
    cols = []
    for c in range(CHUNKS):
        acc = y_ref[pl.ds(c, TOK_TILE, stride=TOP_K * CHUNKS), :] * gates[:, 0:1]
        for kk in range(1, TOP_K):
            acc = acc + y_ref[pl.ds(kk * CHUNKS + c, TOK_TILE, stride=TOP_K * CHUNKS), :] * gates[:, kk:kk + 1]
        cols.append(acc)
    out_ref[...] = _rms(x1_ref[...] + jnp.concatenate(cols, axis=1), fn_ref[...])


def _final_call(tok0, n_tok, y_rows, x1_all, gates_all, final_norm):
    assert tok0 % TOK_TILE == 0 and n_tok % TOK_TILE == 0
    tile0 = tok0 // TOK_TILE
    return pl.pallas_call(
        _final_kernel,
        grid=(n_tok // TOK_TILE,),
        in_specs=[
            pl.BlockSpec((TOK_TILE * TOP_K * CHUNKS, LANES), lambda i: (tile0 + i, 0)),
            pl.BlockSpec((TOK_TILE, D_MODEL), lambda i: (tile0 + i, 0)),
            pl.BlockSpec((TOK_TILE, TOP_K), lambda i: (tile0 + i, 0)),
            pl.BlockSpec((1, D_MODEL), lambda i: (0, 0)),
        ],
        out_specs=pl.BlockSpec((TOK_TILE, D_MODEL), lambda i: (i, 0)),
        out_shape=jax.ShapeDtypeStruct((n_tok, D_MODEL), F32),
        compiler_params=pltpu.CompilerParams(
            dimension_semantics=("arbitrary",), vmem_limit_bytes=VMEM_LIMIT),
        name="final",
    )(y_rows, x1_all, gates_all, final_norm)


def _pick_tile(n, target):
    t = min(n, target)
    while n % t:
        t -= 1
    return t


def kernel(x_prompt, x_sample, state_gla, state_pool, norm1, w_in, w_gk2, b_gk, gla_norm, w_pool,
           pool_scale, w_o, norm2, w_router, b_router, w_gate_up, b_gate_up, w_down, b_down, final_norm):
    depth = w_in.shape[0]
    assert depth == 1
    B, L, _ = x_prompt.shape
    BS, LS, _ = x_sample.shape
    n_p, n_s = B * L, BS * LS
    n_all = n_p + n_s

    wi = w_in[0]
    o_glr = 2 * KEY_W + 2 * VAL_W
    wmain = jnp.concatenate(
        [wi[:, 0:o_glr], wi[:, o_glr + GATE_RANK:], wi[:, o_glr:o_glr + GATE_RANK],
         jnp.zeros((D_MODEL, LANES - GATE_RANK), F32)], axis=1).astype(BF16)
    wgk = jnp.concatenate([w_gk2[0], jnp.zeros((LANES - GATE_RANK, KEY_W), F32)], axis=0)
    wgkh, wgkl = _split2(wgk)
    wr = jnp.concatenate([w_router[0], jnp.zeros((D_MODEL, LANES - N_EXPERTS), F32)], axis=1)
    wrh, wrl = _split2(wr)
    br = jnp.concatenate([b_router[0], jnp.zeros((LANES - N_EXPERTS,), F32)]).reshape(1, LANES)
    weights = (norm1[0].reshape(1, D_MODEL), wmain, wgkh, wgkl, b_gk[0].reshape(1, KEY_W),
               gla_norm[0].reshape(1, DV), w_pool[0].astype(BF16), pool_scale[0].reshape(1, POOL_W),
               w_o[0].astype(BF16), norm2[0].reshape(1, D_MODEL), wrh, wrl, br)

    assert L % GLA_CHUNK == 0 and LS in (8, 16)
    cfg_p = MixerCfg(ns=1, tl=_pick_tile(L, 256), chunk=GLA_CHUNK, start_pos=0, n_alias=0)
    s0_p = jnp.zeros((B, N_HEADS, DV, DK), F32)
    buf0_p = jnp.zeros((B, POOL_BUF, POOL_W), F32)
    x1_all, h_all, topi_all, gates_all, st_p, buf_p = _mixer_call(
        cfg_p, n_all, 0, x_prompt, s0_p, buf0_p, weights, ())

    cfg_s = MixerCfg(ns=_pick_tile(BS, 128 // LS), tl=LS, chunk=LS, start_pos=PAST_LEN, n_alias=4)
    r_s = cfg_s.ns * cfg_s.tl
    assert n_p % r_s == 0
    x1_all, h_all, topi_all, gates_all, st_s, buf_s = _mixer_call(
        cfg_s, n_all, n_p // r_s, x_sample, jnp.swapaxes(state_gla[0], -1, -2), state_pool[0], weights,
        (x1_all, h_all, topi_all, gates_all))

    pos, counts = _route_call(topi_all)
    n_tiles = (n_all * TOP_K + N_EXPERTS * (ROW_TILE - 1)) // ROW_TILE
    tiles_per_e = (counts[0, :N_EXPERTS] + (ROW_TILE - 1)) // ROW_TILE
    ends = jnp.cumsum(tiles_per_e)
    n_used = ends[-1].astype(jnp.int32)
    tile_ids = jnp.minimum(jnp.arange(n_tiles, dtype=jnp.int32), n_used - 1)
    tile_e = jnp.sum(tile_ids[:, None] >= ends[None, :], axis=1).astype(jnp.int32)
    n_pairs = n_all * TOP_K
    pair_of_row = _invperm_call(pos.reshape(n_pairs), -(-n_tiles * ROW_TILE // INV_BLOCK) * INV_BLOCK)

    y_rows = _experts_call(n_pairs, tile_e, n_used.reshape(1), pair_of_row, h_all,
                           w_gate_up[0], b_gate_up[0], w_down[0], b_down[0])

    fn = final_norm.reshape(1, D_MODEL)
    y_p = _final_call(0, n_p, y_rows, x1_all, gates_all, fn)
    y_s = _final_call(n_p, n_s, y_rows, x1_all, gates_all, fn)

    return (y_p.reshape(B, L, D_MODEL), y_s.reshape(BS, LS, D_MODEL),
            jnp.swapaxes(st_p, -1, -2)[None], buf_p[None],
            jnp.swapaxes(st_s, -1, -2)[None], buf_s[None])
```

```python
import functools
from typing import NamedTuple

import jax
import jax.numpy as jnp
from jax import lax
from jax.experimental import pallas as pl
from jax.experimental.pallas import tpu as pltpu

F32 = jnp.float32
BF16 = jnp.bfloat16

D_MODEL = 1024
N_HEADS = 4
DK = 64
DV = 128
KEY_W = N_HEADS * DK
VAL_W = N_HEADS * DV
GATE_RANK = 16
GATE_NORMALIZER = 16.0
GLA_CHUNK = 64
POOL_WINDOWS = (2, 4, 8, 16)
POOL_W = 512
POOL_GROUP = 128
POOL_BUF = 15
N_EXPERTS = 32
TOP_K = 4
D_FF = 1024
SWIGLU_LIMIT = 7.0
SWIGLU_ALPHA = 1.702
EPS = 1e-5
PAST_LEN = 16384

LANES = 128
CHUNKS = D_MODEL // LANES
HALO = 16
DIAG = 16
MAIN_COLS = 2 * KEY_W + 2 * VAL_W + POOL_W + LANES
ROW_TILE = 256
TOK_TILE = 256
ROUTE_TILE = 512
INV_BLOCK = 2048
VMEM_LIMIT = 56 * 1024 * 1024
NEG_BIG = -1e30


def _dot(a, b):
    return jnp.dot(a, b, preferred_element_type=F32)


def _dot_nt(a, b):
    return lax.dot_general(a, b, (((1,), (1,)), ((), ())), preferred_element_type=F32)


def _dot_tn(a, b):
    return lax.dot_general(a, b, (((0,), (0,)), ((), ())), preferred_element_type=F32)


def _split2(a):
    hi = a.astype(BF16)
    lo = (a - hi.astype(F32)).astype(BF16)
    return hi, lo


def _dot3(a, b_hi, b_lo):
    a_hi, a_lo = _split2(a)
    return _dot(a_hi, b_hi) + _dot(a_lo, b_hi) + _dot(a_hi, b_lo)


def _rms(x, w):
    return x * lax.rsqrt(jnp.mean(x * x, axis=-1, keepdims=True) + EPS) * w


class MixerCfg(NamedTuple):
    ns: int
    tl: int
    chunk: int
    start_pos: int
    n_alias: int


def _mixer_dims(cfg):
    rows = cfg.ns * cfg.tl
    diag = min(cfg.chunk, DIAG)
    n_off = cfg.chunk // diag - 1
    width = cfg.chunk if n_off else rows
    return rows, diag, n_off, width


def _scratch_spec(cfg):
    R, S, n_off, W = _mixer_dims(cfg)
    C, ns, tl = cfg.chunk, cfg.ns, cfg.tl
    spec = [
        ("ext", (ns, tl + HALO, POOL_W), F32),
        ("kh", (R + HALO, KEY_W), F32),
        ("gh", (R + HALO, KEY_W), F32),
        ("egl", (R, KEY_W), F32),
        ("a", (R, N_HEADS * W), F32),
        ("o", (R, VAL_W), F32),
    ]
    if n_off:
        nc = R // C
        spec += [
            ("st", (VAL_W, KEY_W), F32),
            ("qg", (R, KEY_W), BF16),
            ("kd", (R, KEY_W), BF16),
            ("v", (R, VAL_W), BF16),
            ("qcat", (R, n_off * KEY_W), BF16),
            ("kbd", (nc, N_HEADS * C, n_off * KEY_W), BF16),
            ("vbd", (nc, N_HEADS * C, VAL_W), BF16),
        ]
    else:
        spec += [
            ("st", (ns, N_HEADS, DV, DK), F32),
            ("qg", (R, KEY_W), F32),
            ("kd", (R, KEY_W), F32),
            ("v", (R, VAL_W), F32),
        ]
    return spec


def _gla_chunked(cfg, l, q, k, v, G, s0_ref, st_out_ref, bdm_ref, sc):
    R, S, n_off, W = _mixer_dims(cfg)
    C = cfg.chunk
    nc = R // C
    n_l = pl.num_programs(1)
    G3 = G.reshape(nc, C, KEY_W)
    glast = jnp.broadcast_to(G3[:, C - 1:C, :], (nc, C, KEY_W)).reshape(R, KEY_W)
    sc["qg"][...] = (q * jnp.exp(G)).astype(BF16)
    sc["kd"][...] = (k * jnp.exp(glast - G)).astype(BF16)
    sc["egl"][...] = jnp.exp(glast)
    v_bf = v.astype(BF16)
    sc["v"][...] = v_bf

    row_c = lax.broadcasted_iota(jnp.int32, (R, 1), 0) % C
    q_parts, k_parts = [], []
    for a in range(1, n_off + 1):
        ra = jnp.broadcast_to(G3[:, a * S - 1:a * S, :], (nc, C, KEY_W)).reshape(R, KEY_W)
        in_block = (row_c >= a * S) & (row_c < (a + 1) * S)
        q_parts.append(jnp.where(in_block, q * jnp.exp(jnp.minimum(G - ra, 0.0)), 0.0))
        k_parts.append(jnp.where(row_c < a * S, k * jnp.exp(jnp.minimum(ra - G, 0.0)), 0.0))
    sc["qcat"][...] = jnp.concatenate(q_parts, axis=1).astype(BF16)
    kcat = jnp.concatenate(k_parts, axis=1).astype(BF16)
    head_of_k = (lax.broadcasted_iota(jnp.int32, (1, n_off * KEY_W), 1) % KEY_W) // DK
    head_of_v = lax.broadcasted_iota(jnp.int32, (1, VAL_W), 1) // DV
    for h in range(N_HEADS):
        sc["kbd"][:, h * C:(h + 1) * C, :] = jnp.where(head_of_k == h, kcat, 0.0).reshape(nc, C, n_off * KEY_W)
        sc["vbd"][:, h * C:(h + 1) * C, :] = jnp.where(head_of_v == h, v_bf, 0.0).reshape(nc, C, VAL_W)

    @pl.when(l == 0)
    def _():
        sc["st"][...] = s0_ref[0]

    def chunk_body(c, carry):
        rows = pl.ds(pl.multiple_of(c * C, C), C)
        st = sc["st"][...]
        a_all = sc["a"][rows, :] + _dot_nt(sc["qcat"][rows, :], sc["kbd"][c])
        sc["o"][rows, :] = (_dot(a_all.astype(BF16), sc["vbd"][c])
                            + _dot_nt(sc["qg"][rows, :], st.astype(BF16)))
        upd = _dot_tn(sc["v"][rows, :], sc["kd"][rows, :])
        sc["st"][...] = st * sc["egl"][pl.ds(c * C, 1), :] + upd * bdm_ref[...]
        return carry

    lax.fori_loop(0, nc, chunk_body, 0)

    @pl.when(l == n_l - 1)
    def _():
        st_out_ref[0] = sc["st"][...]


def _gla_single_chunk(cfg, q, k, v, G, s0_ref, st_out_ref, sc):
    R, S, n_off, W = _mixer_dims(cfg)
    C = cfg.chunk
    G3 = G.reshape(R // C, C, KEY_W)
    glast = jnp.broadcast_to(G3[:, C - 1:C, :], (R // C, C, KEY_W)).reshape(R, KEY_W)
    sc["qg"][...] = q * jnp.exp(G)
    sc["kd"][...] = k * jnp.exp(glast - G)
    sc["egl"][...] = jnp.exp(glast)
    sc["v"][...] = v
    sc["st"][...] = s0_ref[...]

    a_bf = sc["a"][...].astype(BF16)
    v_bf = v.astype(BF16)
    for h in range(N_HEADS):
        sc["o"][:, h * DV:(h + 1) * DV] = _dot(a_bf[:, h * W:(h + 1) * W], v_bf[:, h * DV:(h + 1) * DV])

    kpad = max(C, 16)

    def seq_body(c, carry):
        cs = pl.multiple_of(c * C, C)
        rows = pl.ds(cs, C)
        for h in range(N_HEADS):
            kc = slice(h * DK, (h + 1) * DK)
            vc = slice(h * DV, (h + 1) * DV)
            st = sc["st"][c, h]
            sc["o"][rows, vc] = sc["o"][rows, vc] + _dot_nt(sc["qg"][rows, kc].astype(BF16), st.astype(BF16))
            v_h = sc["v"][rows, vc]
            kd_h = sc["kd"][rows, kc]
            if kpad > C:
                v_h = jnp.concatenate([v_h, jnp.zeros((kpad - C, DV), F32)], axis=0)
                kd_h = jnp.concatenate([kd_h, jnp.zeros((kpad - C, DK), F32)], axis=0)
            sc["st"][c, h] = st * sc["egl"][pl.ds(cs, 1), kc] + _dot_tn(v_h.astype(BF16), kd_h.astype(BF16))
        return carry

    lax.fori_loop(0, R // C, seq_body, 0)
    st_out_ref[...] = sc["st"][...]


def _mixer_kernel(cfg, x_ref, s0_ref, buf0_ref, n1_ref, wmain_ref, wgkh_ref, wgkl_ref, bgk_ref,
                  gn_ref, wpool_ref, pscale_ref, wo_ref, n2_ref, wrh_ref, wrl_ref, br_ref,
                  tri_ref, hb_ref, bdm_ref, *rest):
    rest = rest[cfg.n_alias:]
    x1_ref, h_ref, topi_ref, gates_ref, st_out_ref, bufo_ref = rest[:6]
    sc = dict(zip([name for name, _, _ in _scratch_spec(cfg)], rest[6:]))

    ns, tl, C = cfg.ns, cfg.tl, cfg.chunk
    R, S, n_off, W = _mixer_dims(cfg)
    l = pl.program_id(1)
    n_l = pl.num_programs(1)
    ext_s = sc["ext"]

    x = x_ref[...].reshape(R, D_MODEL)
    xn = _rms(x, n1_ref[...]).astype(BF16)
    p = _dot(xn, wmain_ref[...])
    q = p[:, 0:KEY_W] * (DK ** -0.5)
    k = p[:, KEY_W:2 * KEY_W]
    v = p[:, 2 * KEY_W:2 * KEY_W + VAL_W]
    og = p[:, 2 * KEY_W + VAL_W:2 * KEY_W + 2 * VAL_W]
    u = p[:, 2 * KEY_W + 2 * VAL_W:2 * KEY_W + 2 * VAL_W + POOL_W]
    glr = p[:, MAIN_COLS - LANES:MAIN_COLS]
    ext_s[:, HALO:HALO + tl, :] = u.reshape(ns, tl, POOL_W)

    z = _dot3(glr, wgkh_ref[...], wgkl_ref[...]) + bgk_ref[...]
    g = -(jnp.maximum(-z, 0.0) + jnp.log1p(jnp.exp(-jnp.abs(z)))) / GATE_NORMALIZER
    tri = tri_ref[...]
    g_hi = g.astype(BF16)
    g_r = g - g_hi.astype(F32)
    g_mid = g_r.astype(BF16)
    g_lo = (g_r - g_mid.astype(F32)).astype(BF16)
    G = _dot(tri, g_hi) + _dot(tri, g_mid) + _dot(tri, g_lo)

    sc["kh"][0:HALO, :] = jnp.zeros((HALO, KEY_W), F32)
    sc["gh"][0:HALO, :] = jnp.zeros((HALO, KEY_W), F32)
    sc["kh"][HALO:HALO + R, :] = k
    sc["gh"][HALO:HALO + R, :] = G
    row = lax.broadcasted_iota(jnp.int32, (R, 1), 0)
    row_s = row % S
    row_w = row % W
    col_w = lax.broadcasted_iota(jnp.int32, (1, N_HEADS * W), 1) % W
    head_bcast = hb_ref[...]
    a_all = jnp.zeros((R, N_HEADS * W), F32)
    for d in range(S):
        k_sh = sc["kh"][HALO - d:HALO - d + R, :]
        g_sh = sc["gh"][HALO - d:HALO - d + R, :]
        e = jnp.exp(jnp.where(row_s >= d, G - g_sh, NEG_BIG))
        term = (q * k_sh * e).astype(BF16)
        spread = _dot(term, head_bcast)
        a_all = jnp.where(col_w == row_w - d, spread, a_all)
    sc["a"][...] = a_all

    if n_off:
        _gla_chunked(cfg, l, q, k, v, G, s0_ref, st_out_ref, bdm_ref, sc)
    else:
        _gla_single_chunk(cfg, q, k, v, G, s0_ref, st_out_ref, sc)

    o = sc["o"][...]
    gn = gn_ref[...]
    o_heads = []
    for h in range(N_HEADS):
        vc = slice(h * DV, (h + 1) * DV)
        og_h = og[:, vc]
        o_heads.append(_rms(o[:, vc], gn) * (og_h * jax.nn.sigmoid(og_h)))

    @pl.when(l == 0)
    def _():
        ext_s[:, 0:HALO - POOL_BUF, :] = jnp.zeros((ns, HALO - POOL_BUF, POOL_W), F32)
        ext_s[:, HALO - POOL_BUF:HALO, :] = buf0_ref[...]

    pos = cfg.start_pos + l * tl + lax.broadcasted_iota(jnp.int32, (1, tl, 1), 1)
    z_groups = []
    for gi, w in enumerate(POOL_WINDOWS):
        gc = slice(gi * POOL_GROUP, (gi + 1) * POOL_GROUP)
        s = ext_s[:, HALO:HALO + tl, gc]
        for dd in range(1, w):
            s = s + ext_s[:, HALO - dd:HALO - dd + tl, gc]
        cnt = jnp.minimum(w, pos + 1).astype(F32)
        dmean = (s / cnt - ext_s[:, HALO:HALO + tl, gc]).reshape(R, POOL_GROUP)
        z_groups.append(_dot(dmean.astype(BF16), wpool_ref[gi]))
    zp = jnp.concatenate(z_groups, axis=1) * pscale_ref[...]

    @pl.when(l == n_l - 1)
    def _():
        bufo_ref[...] = ext_s[:, tl + HALO - POOL_BUF:tl + HALO, :]

    @pl.when(l < n_l - 1)
    def _():
        ext_s[:, 0:HALO, :] = ext_s[:, tl:tl + HALO, :]

    cat = jnp.concatenate(o_heads + [zp], axis=1).astype(BF16)
    x1 = x + _dot(cat, wo_ref[...])
    x1_ref[...] = x1
    hn = _rms(x1, n2_ref[...])
    for c in range(CHUNKS):
        h_ref[pl.ds(c, R, stride=CHUNKS), :] = hn[:, c * LANES:(c + 1) * LANES]
    logits = _dot3(hn, wrh_ref[...], wrl_ref[...]) + br_ref[...]
    lane = lax.broadcasted_iota(jnp.int32, (R, LANES), 1)
    lg = jnp.where(lane < N_EXPERTS, logits, -jnp.inf)
    vals, idxs = [], []
    for _ in range(TOP_K):
        m = jnp.max(lg, axis=1, keepdims=True)
        idx = jnp.min(jnp.where(lg == m, lane, LANES), axis=1, keepdims=True)
        vals.append(m)
        idxs.append(idx)
        lg = jnp.where(lane == idx, -jnp.inf, lg)
    exps = [jnp.exp(vv - vals[0]) for vv in vals]
    den = exps[0] + exps[1] + exps[2] + exps[3]
    ti = jnp.zeros((R, LANES), jnp.int32)
    gt = jnp.zeros((R, LANES), F32)
    for kk in range(TOP_K):
        ti = jnp.where(lane == kk, idxs[kk], ti)
        gt = jnp.where(lane == kk, exps[kk] / den, gt)
    topi_ref[...] = ti[:, 0:TOP_K]
    gates_ref[...] = gt[:, 0:TOP_K]


def _mixer_constants(cfg):
    R, S, n_off, W = _mixer_dims(cfg)
    C = cfg.chunk
    r = jnp.arange(R)
    tri = ((r[:, None] // C == r[None, :] // C) & (r[None, :] <= r[:, None])).astype(BF16)
    head_bcast = (jnp.arange(KEY_W)[:, None] // DK == jnp.arange(N_HEADS * W)[None, :] // W).astype(BF16)
    block_diag = (jnp.arange(VAL_W)[:, None] // DV == jnp.arange(KEY_W)[None, :] // DK).astype(F32)
    return tri, head_bcast, block_diag


def _mixer_call(cfg, n_tok_all, row_block0, x, s0, buf0, weights, aliased):
    B, L, _ = x.shape
    ns, tl, C = cfg.ns, cfg.tl, cfg.chunk
    R, S, n_off, W = _mixer_dims(cfg)
    n_b, n_l = B // ns, L // tl
    assert B % ns == 0 and L % tl == 0 and tl % C == 0 and R % 8 == 0
    assert (n_l == 1 or tl >= HALO) and (n_off == 0 or ns == 1) and (n_off > 0 or tl == C)

    def const(shape):
        return pl.BlockSpec(shape, lambda b, l: (0,) * len(shape))

    state_block = (1, VAL_W, KEY_W) if n_off else (ns, N_HEADS, DV, DK)
    state_spec = pl.BlockSpec(state_block, lambda b, l: (b,) + (0,) * (len(state_block) - 1))
    consts = _mixer_constants(cfg)
    operands = (x, s0, buf0) + tuple(weights) + consts
    in_specs = [
        pl.BlockSpec((ns, tl, D_MODEL), lambda b, l: (b, l, 0)),
        state_spec,
        pl.BlockSpec((ns, POOL_BUF, POOL_W), lambda b, l: (b, 0, 0)),
    ] + [const(w.shape) for w in tuple(weights) + consts] + [pl.BlockSpec(memory_space=pl.ANY)] * len(aliased)

    def tok_block(rows, width):
        return pl.BlockSpec((rows, width), lambda b, l: (row_block0 + b * n_l + l, 0))

    out_specs = [
        tok_block(R, D_MODEL), tok_block(R * CHUNKS, LANES), tok_block(R, TOP_K), tok_block(R, TOP_K),
        state_spec,
        pl.BlockSpec((ns, POOL_BUF, POOL_W), lambda b, l: (b, 0, 0)),
    ]
    out_shape = [
        jax.ShapeDtypeStruct((n_tok_all, D_MODEL), F32),
        jax.ShapeDtypeStruct((n_tok_all * CHUNKS, LANES), F32),
        jax.ShapeDtypeStruct((n_tok_all, TOP_K), jnp.int32),
        jax.ShapeDtypeStruct((n_tok_all, TOP_K), F32),
        jax.ShapeDtypeStruct((B,) + state_block[1:], F32),
        jax.ShapeDtypeStruct((B, POOL_BUF, POOL_W), F32),
    ]
    aliases = {len(operands) + i: i for i in range(len(aliased))}
    return pl.pallas_call(
        functools.partial(_mixer_kernel, cfg),
        grid=(n_b, n_l),
        in_specs=in_specs,
        out_specs=out_specs,
        out_shape=out_shape,
        scratch_shapes=[pltpu.VMEM(shape, dtype) for _, shape, dtype in _scratch_spec(cfg)],
        input_output_aliases=aliases,
        compiler_params=pltpu.CompilerParams(
            dimension_semantics=("arbitrary", "arbitrary"), vmem_limit_bytes=VMEM_LIMIT),
        name="mixer",
    )(*operands, *aliased)


def _route_kernel(topi_ref, pos_ref, counts_ref, cnt_s, carry_s, gstart_s):
    ph = pl.program_id(0)
    i = pl.program_id(1)
    TT = topi_ref.shape[0]
    topi = topi_ref[...]
    lane = lax.broadcasted_iota(jnp.int32, (TT, LANES), 1)
    hot = jnp.zeros((TT, LANES), F32)
    for kk in range(TOP_K):
        hot = hot + (lane == topi[:, kk:kk + 1]).astype(F32)
    colsum = jnp.sum(hot, axis=0, keepdims=True)

    @pl.when((ph == 0) & (i == 0))
    def _():
        cnt_s[...] = jnp.zeros_like(cnt_s)

    @pl.when(ph == 0)
    def _():
        cnt_s[...] = cnt_s[...] + colsum

    @pl.when((ph == 1) & (i == 0))
    def _():
        cnt = cnt_s[...]
        counts_ref[...] = cnt.astype(jnp.int32)
        tiles = jnp.floor((cnt + (ROW_TILE - 1)) * (1.0 / ROW_TILE))
        tiles8 = jnp.broadcast_to(tiles, (8, LANES))
        ur = lax.broadcasted_iota(jnp.int32, (LANES, LANES), 0)
        uc = lax.broadcasted_iota(jnp.int32, (LANES, LANES), 1)
        upper = (ur < uc).astype(BF16)
        t_hi, t_lo = _split2(tiles8)
        excl = _dot(t_hi, upper) + _dot(t_lo, upper)
        gstart_s[...] = excl[0:1, :] * float(ROW_TILE)
        carry_s[...] = jnp.zeros_like(carry_s)

    @pl.when(ph == 1)
    def _():
        lr = lax.broadcasted_iota(jnp.int32, (TT, TT), 0)
        lc = lax.broadcasted_iota(jnp.int32, (TT, TT), 1)
        lower = (lc < lr).astype(BF16)
        rank = _dot(lower, hot.astype(BF16)) + carry_s[...] + gstart_s[...]
        out = jnp.zeros((TT, LANES), F32)
        for kk in range(TOP_K):
            pk = jnp.sum(jnp.where(lane == topi[:, kk:kk + 1], rank, 0.0), axis=1, keepdims=True)
            out = jnp.where(lane == kk, pk, out)
        pos_ref[...] = out[:, 0:TOP_K].astype(jnp.int32)
        carry_s[...] = carry_s[...] + colsum


def _route_call(topi):
    T = topi.shape[0]
    assert T % ROUTE_TILE == 0
    return pl.pallas_call(
        _route_kernel,
        grid=(2, T // ROUTE_TILE),
        in_specs=[pl.BlockSpec((ROUTE_TILE, TOP_K), lambda ph, i: (i, 0))],
        out_specs=[pl.BlockSpec((ROUTE_TILE, TOP_K), lambda ph, i: (i * ph, 0)),
                   pl.BlockSpec((1, LANES), lambda ph, i: (0, 0))],
        out_shape=[jax.ShapeDtypeStruct((T, TOP_K), jnp.int32),
                   jax.ShapeDtypeStruct((1, LANES), jnp.int32)],
        scratch_shapes=[pltpu.VMEM((1, LANES), F32)] * 3,
        compiler_params=pltpu.CompilerParams(dimension_semantics=("arbitrary", "arbitrary")),
        name="route",
    )(topi)


def _invperm_kernel(n_zero_steps, pos_ref, tok_ref):
    i = pl.program_id(0)

    @pl.when(i < n_zero_steps)
    def _():
        def clear(r, c):
            tok_ref[i * INV_BLOCK + r] = -1
            return c

        lax.fori_loop(0, INV_BLOCK, clear, 0, unroll=16)

    @pl.when(i >= n_zero_steps)
    def _():
        base = (i - n_zero_steps) * INV_BLOCK

        def put(n, c):
            tok_ref[pos_ref[n]] = base + n
            return c

        lax.fori_loop(0, INV_BLOCK, put, 0, unroll=16)


def _invperm_call(pos_flat, n_rows):
    n_pairs = pos_flat.shape[0]
    assert n_rows % INV_BLOCK == 0 and n_pairs % INV_BLOCK == 0
    n_zero = n_rows // INV_BLOCK
    return pl.pallas_call(
        functools.partial(_invperm_kernel, n_zero),
        grid=(n_zero + n_pairs // INV_BLOCK,),
        in_specs=[pl.BlockSpec((INV_BLOCK,), lambda i: (jnp.maximum(i - n_zero, 0),),
                               memory_space=pltpu.SMEM)],
        out_specs=pl.BlockSpec(memory_space=pltpu.SMEM),
        out_shape=jax.ShapeDtypeStruct((n_rows,), jnp.int32),
        compiler_params=pltpu.CompilerParams(dimension_semantics=("arbitrary",)),
        name="invperm",
    )(pos_flat)


def _row_copy_in(h_hbm, xb, sem, r, pair):
    tok = jnp.maximum(pair, 0) >> 2
    return pltpu.make_async_copy(h_hbm.at[pl.ds(pl.multiple_of(tok * CHUNKS, CHUNKS), CHUNKS), :],
                                 xb.at[pl.ds(r * CHUNKS, CHUNKS), :], sem)


def _row_copy_out(yb, y_hbm, sem, r, dst_row):
    return pltpu.make_async_copy(yb.at[pl.ds(r * CHUNKS, CHUNKS), :],
                                 y_hbm.at[pl.ds(pl.multiple_of(dst_row * CHUNKS, CHUNKS), CHUNKS), :], sem)


def _gather_tile(tab_ref, h_hbm, xb, sem, tile, unrolled):
    base = tile * ROW_TILE
    if unrolled:
        for r in range(ROW_TILE):
            _row_copy_in(h_hbm, xb, sem, r, tab_ref[base + r]).start()
    else:
        def issue(r, c):
            _row_copy_in(h_hbm, xb, sem, r, tab_ref[base + r]).start()
            return c

        lax.fori_loop(0, ROW_TILE, issue, 0)


def _scatter_tile(n_pairs, tab_ref, te_ref, yb, y_hbm, sem, tile, unrolled):
    base = tile * ROW_TILE
    spare0 = n_pairs + te_ref[tile] * ROW_TILE

    def dst_row(r):
        pair = tab_ref[base + r]
        return jnp.where(pair >= 0, pair, spare0 + r)

    if unrolled:
        for r in range(ROW_TILE):
            _row_copy_out(yb, y_hbm, sem, r, dst_row(r)).start()
    else:
        def issue(r, c):
            _row_copy_out(yb, y_hbm, sem, r, dst_row(r)).start()
            return c

        lax.fori_loop(0, ROW_TILE, issue, 0)


def _wait_all_rows(buf, sem):
    pltpu.make_async_copy(buf, buf, sem).wait()


def _expert_mlp(xb, yb, wgu_s, wd_s, bgu_ref, bd_ref):
    xs = jnp.concatenate([xb[pl.ds(c, ROW_TILE, stride=CHUNKS), :] for c in range(CHUNKS)],
                         axis=1).astype(BF16)
    gu = _dot(xs, wgu_s[...]) + bgu_ref[0]
    gate = jnp.minimum(gu[:, 0:D_FF], SWIGLU_LIMIT)
    up = jnp.clip(gu[:, D_FF:2 * D_FF], -SWIGLU_LIMIT, SWIGLU_LIMIT)
    act = ((up + 1.0) * gate * jax.nn.sigmoid(SWIGLU_ALPHA * gate)).astype(BF16)
    y = _dot(act, wd_s[...]) + bd_ref[0]
    for c in range(CHUNKS):
        yb[pl.ds(c, ROW_TILE, stride=CHUNKS), :] = y[:, c * LANES:(c + 1) * LANES]


def _experts_kernel(n_pairs, te_ref, nt_ref, tab_ref, h_hbm, wgu_ref, bgu_ref, wd_ref, bd_ref, y_hbm,
                    x0, x1, y0, y1, gsem, ssem, wgu_s, wd_s):
    s = pl.program_id(0)
    n_used = nt_ref[0]
    par = s % 2
    tc = jnp.clip(s - 1, 0, te_ref.shape[0] - 1)
    xs, ys = (x0, x1), (y0, y1)

    @pl.when((s >= 1) & (s <= n_used) & ((s == 1) | (te_ref[tc] != te_ref[jnp.maximum(tc - 1, 0)])))
    def _():
        wgu_s[...] = wgu_ref[0].astype(BF16)
        wd_s[...] = wd_ref[0].astype(BF16)

    steady = (s >= 2) & (s < n_used)
    edge = jnp.logical_not(steady) & (s <= n_used + 1)
    for p in (0, 1):
        q = 1 - p
        mine = par == p

        @pl.when(mine & (s >= 3) & (s <= n_used + 1))
        def _(q=q):
            _wait_all_rows(ys[q], ssem.at[q])

        @pl.when(mine & steady)
        def _(p=p, q=q):
            _wait_all_rows(xs[q], gsem.at[q])
            _gather_tile(tab_ref, h_hbm, xs[p], gsem.at[p], s, True)
            _scatter_tile(n_pairs, tab_ref, te_ref, ys[p], y_hbm, ssem.at[p], s - 2, True)
            _expert_mlp(xs[q], ys[q], wgu_s, wd_s, bgu_ref, bd_ref)

        @pl.when(mine & edge & (s >= 1) & (s <= n_used))
        def _(q=q):
            _wait_all_rows(xs[q], gsem.at[q])

        @pl.when(mine & edge & (s < n_used))
        def _(p=p):
            _gather_tile(tab_ref, h_hbm, xs[p], gsem.at[p], s, False)

        @pl.when(mine & edge & (s >= 2))
        def _(p=p):
            _scatter_tile(n_pairs, tab_ref, te_ref, ys[p], y_hbm, ssem.at[p], s - 2, False)

        @pl.when(mine & edge & (s >= 1) & (s <= n_used))
        def _(q=q):
            _expert_mlp(xs[q], ys[q], wgu_s, wd_s, bgu_ref, bd_ref)

        @pl.when(mine & (s == n_used + 1))
        def _(p=p):
            _wait_all_rows(ys[p], ssem.at[p])


def _experts_call(n_pairs, tile_e, n_used, pair_of_row, h_rows, w_gu, b_gu, w_down, b_down):
    n_tiles = tile_e.shape[0]

    def expert_block(s, te, nt, tab):
        return (te[jnp.clip(s - 1, 0, n_tiles - 1)], 0, 0)

    grid_spec = pltpu.PrefetchScalarGridSpec(
        num_scalar_prefetch=3,
        grid=(n_tiles + 2,),
        in_specs=[
            pl.BlockSpec(memory_space=pl.ANY),
            pl.BlockSpec((1, D_MODEL, 2 * D_FF), expert_block),
            pl.BlockSpec((1, 1, 2 * D_FF), expert_block),
            pl.BlockSpec((1, D_FF, D_MODEL), expert_block),
            pl.BlockSpec((1, 1, D_MODEL), expert_block),
        ],
        out_specs=pl.BlockSpec(memory_space=pl.ANY),
        scratch_shapes=[
            pltpu.VMEM((ROW_TILE * CHUNKS, LANES), F32),
            pltpu.VMEM((ROW_TILE * CHUNKS, LANES), F32),
            pltpu.VMEM((ROW_TILE * CHUNKS, LANES), F32),
            pltpu.VMEM((ROW_TILE * CHUNKS, LANES), F32),
            pltpu.SemaphoreType.DMA((2,)),
            pltpu.SemaphoreType.DMA((2,)),
            pltpu.VMEM((D_MODEL, 2 * D_FF), BF16),
            pltpu.VMEM((D_FF, D_MODEL), BF16),
        ],
    )
    return pl.pallas_call(
        functools.partial(_experts_kernel, n_pairs),
        grid_spec=grid_spec,
        out_shape=jax.ShapeDtypeStruct(((n_pairs + N_EXPERTS * ROW_TILE) * CHUNKS, LANES), F32),
        compiler_params=pltpu.CompilerParams(
            dimension_semantics=("arbitrary",), vmem_limit_bytes=VMEM_LIMIT),
        name="experts",
    )(tile_e, n_used, pair_of_row, h_rows, w_gu, b_gu.reshape(N_EXPERTS, 1, 2 * D_FF),
      w_down, b_down.reshape(N_EXPERTS, 1, D_MODEL))


def _final_kernel(y_ref, x1_ref, gates_ref, fn_ref, out_ref):
    gates = gates_ref[...]
    cols = []
    for c in range(CHUNKS):
        acc = y_ref[pl.ds(c, TOK_TILE, stride=TOP_K * CHUNKS), :] * gates[:, 0:1]
        for kk in range(1, TOP_K):
            acc = acc + y_ref[pl.ds(kk * CHUNKS + c, TOK_TILE, stride=TOP_K * CHUNKS), :] * gates[:, kk:kk + 1]
        cols.append(acc)
    out_ref[...] = _rms(x1_ref[...] + jnp.concatenate(cols, axis=1), fn_ref[...])


def _final_call(tok0, n_tok, y_rows, x1_all, gates_all, final_norm):
    assert tok0 % TOK_TILE == 0 and n_tok % TOK_TILE == 0
    tile0 = tok0 // TOK_TILE
    return pl.pallas_call(
        _final_kernel,
        grid=(n_tok // TOK_TILE,),
        in_specs=[
            pl.BlockSpec((TOK_TILE * TOP_K * CHUNKS, LANES), lambda i: (tile0 + i, 0)),
            pl.BlockSpec((TOK_TILE, D_MODEL), lambda i: (tile0 + i, 0)),
            pl.BlockSpec((TOK_TILE, TOP_K), lambda i: (tile0 + i, 0)),
            pl.BlockSpec((1, D_MODEL), lambda i: (0, 0)),
        ],
        out_specs=pl.BlockSpec((TOK_TILE, D_MODEL), lambda i: (i, 0)),
        out_shape=jax.ShapeDtypeStruct((n_tok, D_MODEL), F32),
        compiler_params=pltpu.CompilerParams(
            dimension_semantics=("arbitrary",), vmem_limit_bytes=VMEM_LIMIT),
        name="final",
    )(y_rows, x1_all, gates_all, final_norm)


def _pick_tile(n, target):
    t = min(n, target)
    while n % t:
        t -= 1
    return t


def kernel(x_prompt, x_sample, state_gla, state_pool, norm1, w_in, w_gk2, b_gk, gla_norm, w_pool,
           pool_scale, w_o, norm2, w_router, b_router, w_gate_up, b_gate_up, w_down, b_down, final_norm):
    depth = w_in.shape[0]
    assert depth == 1
    B, L, _ = x_prompt.shape
    BS, LS, _ = x_sample.shape
    n_p, n_s = B * L, BS * LS
    n_all = n_p + n_s

    wi = w_in[0]
    o_glr = 2 * KEY_W + 2 * VAL_W
    wmain = jnp.concatenate(
        [wi[:, 0:o_glr], wi[:, o_glr + GATE_RANK:], wi[:, o_glr:o_glr + GATE_RANK],
         jnp.zeros((D_MODEL, LANES - GATE_RANK), F32)], axis=1).astype(BF16)
    wgk = jnp.concatenate([w_gk2[0], jnp.zeros((LANES - GATE_RANK, KEY_W), F32)], axis=0)
    wgkh, wgkl = _split2(wgk)
    wr = jnp.concatenate([w_router[0], jnp.zeros((D_MODEL, LANES - N_EXPERTS), F32)], axis=1)
    wrh, wrl = _split2(wr)
    br = jnp.concatenate([b_router[0], jnp.zeros((LANES - N_EXPERTS,), F32)]).reshape(1, LANES)
    weights = (norm1[0].reshape(1, D_MODEL), wmain, wgkh, wgkl, b_gk[0].reshape(1, KEY_W),
               gla_norm[0].reshape(1, DV), w_pool[0].astype(BF16), pool_scale[0].reshape(1, POOL_W),
               w_o[0].astype(BF16), norm2[0].reshape(1, D_MODEL), wrh, wrl, br)

    assert L % GLA_CHUNK == 0 and LS in (8, 16)
    cfg_p = MixerCfg(ns=1, tl=_pick_tile(L, 256), chunk=GLA_CHUNK, start_pos=0, n_alias=0)
    s0_p = jnp.zeros((B, VAL_W, KEY_W), F32)
    buf0_p = jnp.zeros((B, POOL_BUF, POOL_W), F32)
    x1_all, h_all, topi_all, gates_all, st_p, buf_p = _mixer_call(
        cfg_p, n_all, 0, x_prompt, s0_p, buf0_p, weights, ())

    cfg_s = MixerCfg(ns=_pick_tile(BS, 128 // LS), tl=LS, chunk=LS, start_pos=PAST_LEN, n_alias=4)
    r_s = cfg_s.ns * cfg_s.tl
    assert n_p % r_s == 0
    x1_all, h_all, topi_all, gates_all, st_s, buf_s = _mixer_call(
        cfg_s, n_all, n_p // r_s, x_sample, jnp.swapaxes(state_gla[0], -1, -2), state_pool[0], weights,
        (x1_all, h_all, topi_all, gates_all))

    pos, counts = _route_call(topi_all)
    n_tiles = (n_all * TOP_K + N_EXPERTS * (ROW_TILE - 1)) // ROW_TILE
    tiles_per_e = (counts[0, :N_EXPERTS] + (ROW_TILE - 1)) // ROW_TILE
    ends = jnp.cumsum(tiles_per_e)
    n_used = ends[-1].astype(jnp.int32)
    tile_ids = jnp.minimum(jnp.arange(n_tiles, dtype=jnp.int32), n_used - 1)
    tile_e = jnp.sum(tile_ids[:, None] >= ends[None, :], axis=1).astype(jnp.int32)
    n_pairs = n_all * TOP_K
    pair_of_row = _invperm_call(pos.reshape(n_pairs), -(-n_tiles * ROW_TILE // INV_BLOCK) * INV_BLOCK)

    y_rows = _experts_call(n_pairs, tile_e, n_used.reshape(1), pair_of_row, h_all,
                           w_gate_up[0], b_gate_up[0], w_down[0], b_down[0])

    fn = final_norm.reshape(1, D_MODEL)
    y_p = _final_call(0, n_p, y_rows, x1_all, gates_all, fn)
    y_s = _final_call(n_p, n_s, y_rows, x1_all, gates_all, fn)

    st_p = jnp.stack([st_p[:, h * DV:(h + 1) * DV, h * DK:(h + 1) * DK] for h in range(N_HEADS)], axis=1)
    return (y_p.reshape(B, L, D_MODEL), y_s.reshape(BS, LS, D_MODEL),
            jnp.swapaxes(st_p, -1, -2)[None], buf_p[None],
            jnp.swapaxes(st_s, -1, -2)[None], buf_s[None])
```

```python
import functools
from typing import NamedTuple

import jax
import jax.numpy as jnp
from jax import lax
from jax.experimental import pallas as pl
from jax.experimental.pallas import tpu as pltpu

F32 = jnp.float32
BF16 = jnp.bfloat16

D_MODEL = 1024
N_HEADS = 4
DK = 64
DV = 128
KEY_W = N_HEADS * DK
VAL_W = N_HEADS * DV
GATE_RANK = 16
GATE_NORMALIZER = 16.0
GLA_CHUNK = 64
POOL_WINDOWS = (2, 4, 8, 16)
POOL_W = 512
POOL_GROUP = 128
POOL_BUF = 15
N_EXPERTS = 32
TOP_K = 4
D_FF = 1024
SWIGLU_LIMIT = 7.0
SWIGLU_ALPHA = 1.702
EPS = 1e-5
PAST_LEN = 16384

LANES = 128
CHUNKS = D_MODEL // LANES
HALO = 16
DIAG = 16
MAIN_COLS = 2 * KEY_W + 2 * VAL_W + POOL_W + LANES
ROW_TILE = 256
TOK_TILE = 256
ROUTE_TILE = 512
INV_BLOCK = 2048
SCATTER_DMA_QUEUE = 1
VMEM_LIMIT = 56 * 1024 * 1024
NEG_BIG = -1e30


def _dot(a, b):
    return jnp.dot(a, b, preferred_element_type=F32)


def _dot_nt(a, b):
    return lax.dot_general(a, b, (((1,), (1,)), ((), ())), preferred_element_type=F32)


def _dot_tn(a, b):
    return lax.dot_general(a, b, (((0,), (0,)), ((), ())), preferred_element_type=F32)


def _split2(a):
    hi = a.astype(BF16)
    lo = (a - hi.astype(F32)).astype(BF16)
    return hi, lo


def _dot3(a, b_hi, b_lo):
    a_hi, a_lo = _split2(a)
    return _dot(a_hi, b_hi) + _dot(a_lo, b_hi) + _dot(a_hi, b_lo)


def _rms(x, w):
    return x * lax.rsqrt(jnp.mean(x * x, axis=-1, keepdims=True) + EPS) * w


class MixerCfg(NamedTuple):
    ns: int
    tl: int
    chunk: int
    start_pos: int
    n_alias: int


def _mixer_dims(cfg):
    rows = cfg.ns * cfg.tl
    diag = min(cfg.chunk, DIAG)
    n_off = cfg.chunk // diag - 1
    width = cfg.chunk if n_off else rows
    return rows, diag, n_off, width


def _scratch_spec(cfg):
    R, S, n_off, W = _mixer_dims(cfg)
    C, ns, tl = cfg.chunk, cfg.ns, cfg.tl
    spec = [
        ("ext", (ns, tl + HALO, POOL_W), F32),
        ("kh", (R + HALO, KEY_W), F32),
        ("gh", (R + HALO, KEY_W), F32),
        ("egl", (R, KEY_W), F32),
        ("a", (R, N_HEADS * W), F32),
        ("o", (R, VAL_W), F32),
    ]
    if n_off:
        nc = R // C
        spec += [
            ("st", (VAL_W, KEY_W), F32),
            ("qg", (R, KEY_W), BF16),
            ("kd", (R, KEY_W), BF16),
            ("v", (R, VAL_W), BF16),
            ("qcat", (R, n_off * KEY_W), BF16),
            ("kbd", (nc, N_HEADS * C, n_off * KEY_W), BF16),
            ("vbd", (nc, N_HEADS * C, VAL_W), BF16),
        ]
    else:
        spec += [
            ("st", (ns, N_HEADS, DV, DK), F32),
            ("qg", (R, KEY_W), F32),
            ("kd", (R, KEY_W), F32),
            ("v", (R, VAL_W), F32),
        ]
    return spec


def _gla_chunked(cfg, l, q, k, v, G, s0_ref, st_out_ref, bdm_ref, sc):
    R, S, n_off, W = _mixer_dims(cfg)
    C = cfg.chunk
    nc = R // C
    n_l = pl.num_programs(1)
    G3 = G.reshape(nc, C, KEY_W)
    glast = jnp.broadcast_to(G3[:, C - 1:C, :], (nc, C, KEY_W)).reshape(R, KEY_W)
    sc["qg"][...] = (q * jnp.exp(G)).astype(BF16)
    sc["kd"][...] = (k * jnp.exp(glast - G)).astype(BF16)
    sc["egl"][...] = jnp.exp(glast)
    v_bf = v.astype(BF16)
    sc["v"][...] = v_bf

    row_c = lax.broadcasted_iota(jnp.int32, (R, 1), 0) % C
    q_parts, k_parts = [], []
    for a in range(1, n_off + 1):
        ra = jnp.broadcast_to(G3[:, a * S - 1:a * S, :], (nc, C, KEY_W)).reshape(R, KEY_W)
        in_block = (row_c >= a * S) & (row_c < (a + 1) * S)
        q_parts.append(jnp.where(in_block, q * jnp.exp(jnp.minimum(G - ra, 0.0)), 0.0))
        k_parts.append(jnp.where(row_c < a * S, k * jnp.exp(jnp.minimum(ra - G, 0.0)), 0.0))
    sc["qcat"][...] = jnp.concatenate(q_parts, axis=1).astype(BF16)
    kcat = jnp.concatenate(k_parts, axis=1).astype(BF16)
    head_of_k = (lax.broadcasted_iota(jnp.int32, (1, n_off * KEY_W), 1) % KEY_W) // DK
    head_of_v = lax.broadcasted_iota(jnp.int32, (1, VAL_W), 1) // DV
    for h in range(N_HEADS):
        sc["kbd"][:, h * C:(h + 1) * C, :] = jnp.where(head_of_k == h, kcat, 0.0).reshape(nc, C, n_off * KEY_W)
        sc["vbd"][:, h * C:(h + 1) * C, :] = jnp.where(head_of_v == h, v_bf, 0.0).reshape(nc, C, VAL_W)

    @pl.when(l == 0)
    def _():
        sc["st"][...] = s0_ref[0]

    def chunk_body(c, carry):
        rows = pl.ds(pl.multiple_of(c * C, C), C)
        st = sc["st"][...]
        a_all = sc["a"][rows, :] + _dot_nt(sc["qcat"][rows, :], sc["kbd"][c])
        sc["o"][rows, :] = (_dot(a_all.astype(BF16), sc["vbd"][c])
                            + _dot_nt(sc["qg"][rows, :], st.astype(BF16)))
        upd = _dot_tn(sc["v"][rows, :], sc["kd"][rows, :])
        sc["st"][...] = st * sc["egl"][pl.ds(c * C, 1), :] + upd * bdm_ref[...]
        return carry

    lax.fori_loop(0, nc, chunk_body, 0)

    @pl.when(l == n_l - 1)
    def _():
        st_out_ref[0] = sc["st"][...]


def _gla_single_chunk(cfg, q, k, v, G, s0_ref, st_out_ref, sc):
    R, S, n_off, W = _mixer_dims(cfg)
    C = cfg.chunk
    G3 = G.reshape(R // C, C, KEY_W)
    glast = jnp.broadcast_to(G3[:, C - 1:C, :], (R // C, C, KEY_W)).reshape(R, KEY_W)
    sc["qg"][...] = q * jnp.exp(G)
    sc["kd"][...] = k * jnp.exp(glast - G)
    sc["egl"][...] = jnp.exp(glast)
    sc["v"][...] = v
    sc["st"][...] = s0_ref[...]

    a_bf = sc["a"][...].astype(BF16)
    v_bf = v.astype(BF16)
    for h in range(N_HEADS):
        sc["o"][:, h * DV:(h + 1) * DV] = _dot(a_bf[:, h * W:(h + 1) * W], v_bf[:, h * DV:(h + 1) * DV])

    kpad = max(C, 16)

    def seq_body(c, carry):
        cs = pl.multiple_of(c * C, C)
        rows = pl.ds(cs, C)
        for h in range(N_HEADS):
            kc = slice(h * DK, (h + 1) * DK)
            vc = slice(h * DV, (h + 1) * DV)
            st = sc["st"][c, h]
            sc["o"][rows, vc] = sc["o"][rows, vc] + _dot_nt(sc["qg"][rows, kc].astype(BF16), st.astype(BF16))
            v_h = sc["v"][rows, vc]
            kd_h = sc["kd"][rows, kc]
            if kpad > C:
                v_h = jnp.concatenate([v_h, jnp.zeros((kpad - C, DV), F32)], axis=0)
                kd_h = jnp.concatenate([kd_h, jnp.zeros((kpad - C, DK), F32)], axis=0)
            sc["st"][c, h] = st * sc["egl"][pl.ds(cs, 1), kc] + _dot_tn(v_h.astype(BF16), kd_h.astype(BF16))
        return carry

    lax.fori_loop(0, R // C, seq_body, 0)
    st_out_ref[...] = sc["st"][...]


def _mixer_kernel(cfg, x_ref, s0_ref, buf0_ref, n1_ref, wmain_ref, wgkh_ref, wgkl_ref, bgk_ref,
                  gn_ref, wpool_ref, pscale_ref, wo_ref, n2_ref, wrh_ref, wrl_ref, br_ref,
                  tri_ref, hb_ref, bdm_ref, *rest):
    rest = rest[cfg.n_alias:]
    x1_ref, h_ref, topi_ref, gates_ref, st_out_ref, bufo_ref = rest[:6]
    sc = dict(zip([name for name, _, _ in _scratch_spec(cfg)], rest[6:]))

    ns, tl, C = cfg.ns, cfg.tl, cfg.chunk
    R, S, n_off, W = _mixer_dims(cfg)
    l = pl.program_id(1)
    n_l = pl.num_programs(1)
    ext_s = sc["ext"]

    x = x_ref[...].reshape(R, D_MODEL)
    xn = _rms(x, n1_ref[...]).astype(BF16)
    p = _dot(xn, wmain_ref[...])
    q = p[:, 0:KEY_W] * (DK ** -0.5)
    k = p[:, KEY_W:2 * KEY_W]
    v = p[:, 2 * KEY_W:2 * KEY_W + VAL_W]
    og = p[:, 2 * KEY_W + VAL_W:2 * KEY_W + 2 * VAL_W]
    u = p[:, 2 * KEY_W + 2 * VAL_W:2 * KEY_W + 2 * VAL_W + POOL_W]
    glr = p[:, MAIN_COLS - LANES:MAIN_COLS]
    ext_s[:, HALO:HALO + tl, :] = u.reshape(ns, tl, POOL_W)

    z = _dot3(glr, wgkh_ref[...], wgkl_ref[...]) + bgk_ref[...]
    g = -(jnp.maximum(-z, 0.0) + jnp.log1p(jnp.exp(-jnp.abs(z)))) / GATE_NORMALIZER
    tri = tri_ref[...]
    g_hi = g.astype(BF16)
    g_r = g - g_hi.astype(F32)
    g_mid = g_r.astype(BF16)
    g_lo = (g_r - g_mid.astype(F32)).astype(BF16)
    G = _dot(tri, g_hi) + _dot(tri, g_mid) + _dot(tri, g_lo)

    sc["kh"][0:HALO, :] = jnp.zeros((HALO, KEY_W), F32)
    sc["gh"][0:HALO, :] = jnp.zeros((HALO, KEY_W), F32)
    sc["kh"][HALO:HALO + R, :] = k
    sc["gh"][HALO:HALO + R, :] = G
    row = lax.broadcasted_iota(jnp.int32, (R, 1), 0)
    row_s = row % S
    row_w = row % W
    col_w = lax.broadcasted_iota(jnp.int32, (1, N_HEADS * W), 1) % W
    head_bcast = hb_ref[...]
    a_all = jnp.zeros((R, N_HEADS * W), F32)
    for d in range(S):
        k_sh = sc["kh"][HALO - d:HALO - d + R, :]
        g_sh = sc["gh"][HALO - d:HALO - d + R, :]
        e = jnp.exp(jnp.where(row_s >= d, G - g_sh, NEG_BIG))
        term = (q * k_sh * e).astype(BF16)
        spread = _dot(term, head_bcast)
        a_all = jnp.where(col_w == row_w - d, spread, a_all)
    sc["a"][...] = a_all

    if n_off:
        _gla_chunked(cfg, l, q, k, v, G, s0_ref, st_out_ref, bdm_ref, sc)
    else:
        _gla_single_chunk(cfg, q, k, v, G, s0_ref, st_out_ref, sc)

    o = sc["o"][...]
    gn = gn_ref[...]
    o_heads = []
    for h in range(N_HEADS):
        vc = slice(h * DV, (h + 1) * DV)
        og_h = og[:, vc]
        o_heads.append(_rms(o[:, vc], gn) * (og_h * jax.nn.sigmoid(og_h)))

    @pl.when(l == 0)
    def _():
        ext_s[:, 0:HALO - POOL_BUF, :] = jnp.zeros((ns, HALO - POOL_BUF, POOL_W), F32)
        ext_s[:, HALO - POOL_BUF:HALO, :] = buf0_ref[...]

    pos = cfg.start_pos + l * tl + lax.broadcasted_iota(jnp.int32, (1, tl, 1), 1)
    z_groups = []
    for gi, w in enumerate(POOL_WINDOWS):
        gc = slice(gi * POOL_GROUP, (gi + 1) * POOL_GROUP)
        s = ext_s[:, HALO:HALO + tl, gc]
        for dd in range(1, w):
            s = s + ext_s[:, HALO - dd:HALO - dd + tl, gc]
        cnt = jnp.minimum(w, pos + 1).astype(F32)
        dmean = (s / cnt - ext_s[:, HALO:HALO + tl, gc]).reshape(R, POOL_GROUP)
        z_groups.append(_dot(dmean.astype(BF16), wpool_ref[gi]))
    zp = jnp.concatenate(z_groups, axis=1) * pscale_ref[...]

    @pl.when(l == n_l - 1)
    def _():
        bufo_ref[...] = ext_s[:, tl + HALO - POOL_BUF:tl + HALO, :]

    @pl.when(l < n_l - 1)
    def _():
        ext_s[:, 0:HALO, :] = ext_s[:, tl:tl + HALO, :]

    cat = jnp.concatenate(o_heads + [zp], axis=1).astype(BF16)
    x1 = x + _dot(cat, wo_ref[...])
    x1_ref[...] = x1
    hn = _rms(x1, n2_ref[...])
    for c in range(CHUNKS):
        h_ref[pl.ds(c, R, stride=CHUNKS), :] = hn[:, c * LANES:(c + 1) * LANES]
    logits = _dot3(hn, wrh_ref[...], wrl_ref[...]) + br_ref[...]
    lane = lax.broadcasted_iota(jnp.int32, (R, LANES), 1)
    lg = jnp.where(lane < N_EXPERTS, logits, -jnp.inf)
    vals, idxs = [], []
    for _ in range(TOP_K):
        m = jnp.max(lg, axis=1, keepdims=True)
        idx = jnp.min(jnp.where(lg == m, lane, LANES), axis=1, keepdims=True)
        vals.append(m)
        idxs.append(idx)
        lg = jnp.where(lane == idx, -jnp.inf, lg)
    exps = [jnp.exp(vv - vals[0]) for vv in vals]
    den = exps[0] + exps[1] + exps[2] + exps[3]
    ti = jnp.zeros((R, LANES), jnp.int32)
    gt = jnp.zeros((R, LANES), F32)
    for kk in range(TOP_K):
        ti = jnp.where(lane == kk, idxs[kk], ti)
        gt = jnp.where(lane == kk, exps[kk] / den, gt)
    topi_ref[...] = ti[:, 0:TOP_K]
    gates_ref[...] = gt[:, 0:TOP_K]


def _mixer_constants(cfg):
    R, S, n_off, W = _mixer_dims(cfg)
    C = cfg.chunk
    r = jnp.arange(R)
    tri = ((r[:, None] // C == r[None, :] // C) & (r[None, :] <= r[:, None])).astype(BF16)
    head_bcast = (jnp.arange(KEY_W)[:, None] // DK == jnp.arange(N_HEADS * W)[None, :] // W).astype(BF16)
    block_diag = (jnp.arange(VAL_W)[:, None] // DV == jnp.arange(KEY_W)[None, :] // DK).astype(F32)
    return tri, head_bcast, block_diag


def _mixer_call(cfg, n_tok_all, row_block0, x, s0, buf0, weights, aliased):
    B, L, _ = x.shape
    ns, tl, C = cfg.ns, cfg.tl, cfg.chunk
    R, S, n_off, W = _mixer_dims(cfg)
    n_b, n_l = B // ns, L // tl
    assert B % ns == 0 and L % tl == 0 and tl % C == 0 and R % 8 == 0
    assert (n_l == 1 or tl >= HALO) and (n_off == 0 or ns == 1) and (n_off > 0 or tl == C)

    def const(shape):
        return pl.BlockSpec(shape, lambda b, l: (0,) * len(shape))

    state_block = (1, VAL_W, KEY_W) if n_off else (ns, N_HEADS, DV, DK)
    state_spec = pl.BlockSpec(state_block, lambda b, l: (b,) + (0,) * (len(state_block) - 1))
    consts = _mixer_constants(cfg)
    operands = (x, s0, buf0) + tuple(weights) + consts
    in_specs = [
        pl.BlockSpec((ns, tl, D_MODEL), lambda b, l: (b, l, 0)),
        state_spec,
        pl.BlockSpec((ns, POOL_BUF, POOL_W), lambda b, l: (b, 0, 0)),
    ] + [const(w.shape) for w in tuple(weights) + consts] + [pl.BlockSpec(memory_space=pl.ANY)] * len(aliased)

    def tok_block(rows, width):
        return pl.BlockSpec((rows, width), lambda b, l: (row_block0 + b * n_l + l, 0))

    out_specs = [
        tok_block(R, D_MODEL), tok_block(R * CHUNKS, LANES), tok_block(R, TOP_K), tok_block(R, TOP_K),
        state_spec,
        pl.BlockSpec((ns, POOL_BUF, POOL_W), lambda b, l: (b, 0, 0)),
    ]
    out_shape = [
        jax.ShapeDtypeStruct((n_tok_all, D_MODEL), F32),
        jax.ShapeDtypeStruct((n_tok_all * CHUNKS, LANES), F32),
        jax.ShapeDtypeStruct((n_tok_all, TOP_K), jnp.int32),
        jax.ShapeDtypeStruct((n_tok_all, TOP_K), F32),
        jax.ShapeDtypeStruct((B,) + state_block[1:], F32),
        jax.ShapeDtypeStruct((B, POOL_BUF, POOL_W), F32),
    ]
    aliases = {len(operands) + i: i for i in range(len(aliased))}
    return pl.pallas_call(
        functools.partial(_mixer_kernel, cfg),
        grid=(n_b, n_l),
        in_specs=in_specs,
        out_specs=out_specs,
        out_shape=out_shape,
        scratch_shapes=[pltpu.VMEM(shape, dtype) for _, shape, dtype in _scratch_spec(cfg)],
        input_output_aliases=aliases,
        compiler_params=pltpu.CompilerParams(
            dimension_semantics=("arbitrary", "arbitrary"), vmem_limit_bytes=VMEM_LIMIT),
        name="mixer",
    )(*operands, *aliased)


def _route_kernel(topi_ref, pos_ref, counts_ref, cnt_s, carry_s, gstart_s):
    ph = pl.program_id(0)
    i = pl.program_id(1)
    TT = topi_ref.shape[0]
    topi = topi_ref[...]
    lane = lax.broadcasted_iota(jnp.int32, (TT, LANES), 1)
    hot = jnp.zeros((TT, LANES), F32)
    for kk in range(TOP_K):
        hot = hot + (lane == topi[:, kk:kk + 1]).astype(F32)
    colsum = jnp.sum(hot, axis=0, keepdims=True)

    @pl.when((ph == 0) & (i == 0))
    def _():
        cnt_s[...] = jnp.zeros_like(cnt_s)

    @pl.when(ph == 0)
    def _():
        cnt_s[...] = cnt_s[...] + colsum

    @pl.when((ph == 1) & (i == 0))
    def _():
        cnt = cnt_s[...]
        counts_ref[...] = cnt.astype(jnp.int32)
        tiles = jnp.floor((cnt + (ROW_TILE - 1)) * (1.0 / ROW_TILE))
        tiles8 = jnp.broadcast_to(tiles, (8, LANES))
        ur = lax.broadcasted_iota(jnp.int32, (LANES, LANES), 0)
        uc = lax.broadcasted_iota(jnp.int32, (LANES, LANES), 1)
        upper = (ur < uc).astype(BF16)
        t_hi, t_lo = _split2(tiles8)
        excl = _dot(t_hi, upper) + _dot(t_lo, upper)
        gstart_s[...] = excl[0:1, :] * float(ROW_TILE)
        carry_s[...] = jnp.zeros_like(carry_s)

    @pl.when(ph == 1)
    def _():
        lr = lax.broadcasted_iota(jnp.int32, (TT, TT), 0)
        lc = lax.broadcasted_iota(jnp.int32, (TT, TT), 1)
        lower = (lc < lr).astype(BF16)
        rank = _dot(lower, hot.astype(BF16)) + carry_s[...] + gstart_s[...]
        out = jnp.zeros((TT, LANES), F32)
        for kk in range(TOP_K):
            pk = jnp.sum(jnp.where(lane == topi[:, kk:kk + 1], rank, 0.0), axis=1, keepdims=True)
            out = jnp.where(lane == kk, pk, out)
        pos_ref[...] = out[:, 0:TOP_K].astype(jnp.int32)
        carry_s[...] = carry_s[...] + colsum


def _route_call(topi):
    T = topi.shape[0]
    assert T % ROUTE_TILE == 0
    return pl.pallas_call(
        _route_kernel,
        grid=(2, T // ROUTE_TILE),
        in_specs=[pl.BlockSpec((ROUTE_TILE, TOP_K), lambda ph, i: (i, 0))],
        out_specs=[pl.BlockSpec((ROUTE_TILE, TOP_K), lambda ph, i: (i * ph, 0)),
                   pl.BlockSpec((1, LANES), lambda ph, i: (0, 0))],
        out_shape=[jax.ShapeDtypeStruct((T, TOP_K), jnp.int32),
                   jax.ShapeDtypeStruct((1, LANES), jnp.int32)],
        scratch_shapes=[pltpu.VMEM((1, LANES), F32)] * 3,
        compiler_params=pltpu.CompilerParams(dimension_semantics=("arbitrary", "arbitrary")),
        name="route",
    )(topi)


def _invperm_kernel(n_zero_steps, pos_ref, tok_ref):
    i = pl.program_id(0)

    @pl.when(i < n_zero_steps)
    def _():
        def clear(r, c):
            tok_ref[i * INV_BLOCK + r] = -1
            return c

        lax.fori_loop(0, INV_BLOCK, clear, 0, unroll=16)

    @pl.when(i >= n_zero_steps)
    def _():
        base = (i - n_zero_steps) * INV_BLOCK

        def put(n, c):
            tok_ref[pos_ref[n]] = base + n
            return c

        lax.fori_loop(0, INV_BLOCK, put, 0, unroll=16)


def _invperm_call(pos_flat, n_rows):
    n_pairs = pos_flat.shape[0]
    assert n_rows % INV_BLOCK == 0 and n_pairs % INV_BLOCK == 0
    n_zero = n_rows // INV_BLOCK
    return pl.pallas_call(
        functools.partial(_invperm_kernel, n_zero),
        grid=(n_zero + n_pairs // INV_BLOCK,),
        in_specs=[pl.BlockSpec((INV_BLOCK,), lambda i: (jnp.maximum(i - n_zero, 0),),
                               memory_space=pltpu.SMEM)],
        out_specs=pl.BlockSpec(memory_space=pltpu.SMEM),
        out_shape=jax.ShapeDtypeStruct((n_rows,), jnp.int32),
        compiler_params=pltpu.CompilerParams(dimension_semantics=("arbitrary",)),
        name="invperm",
    )(pos_flat)


def _row_copy_in(h_hbm, xb, sem, r, pair):
    tok = jnp.maximum(pair, 0) >> 2
    return pltpu.make_async_copy(h_hbm.at[pl.ds(pl.multiple_of(tok * CHUNKS, CHUNKS), CHUNKS), :],
                                 xb.at[pl.ds(r * CHUNKS, CHUNKS), :], sem)


def _row_copy_out(yb, y_hbm, sem, r, dst_row):
    return pltpu.make_async_copy(yb.at[pl.ds(r * CHUNKS, CHUNKS), :],
                                 y_hbm.at[pl.ds(pl.multiple_of(dst_row * CHUNKS, CHUNKS), CHUNKS), :], sem)


def _gather_tile(tab_ref, h_hbm, xb, sem, tile, unrolled):
    base = tile * ROW_TILE
    if unrolled:
        for r in range(ROW_TILE):
            _row_copy_in(h_hbm, xb, sem, r, tab_ref[base + r]).start()
    else:
        def issue(r, c):
            _row_copy_in(h_hbm, xb, sem, r, tab_ref[base + r]).start()
            return c

        lax.fori_loop(0, ROW_TILE, issue, 0)


def _scatter_tile(n_pairs, tab_ref, te_ref, yb, y_hbm, sem, tile, unrolled):
    base = tile * ROW_TILE
    spare0 = n_pairs + te_ref[tile] * ROW_TILE

    def dst_row(r):
        pair = tab_ref[base + r]
        return jnp.where(pair >= 0, pair, spare0 + r)

    if unrolled:
        for r in range(ROW_TILE):
            _row_copy_out(yb, y_hbm, sem, r, dst_row(r)).start(priority=SCATTER_DMA_QUEUE)
    else:
        def issue(r, c):
            _row_copy_out(yb, y_hbm, sem, r, dst_row(r)).start(priority=SCATTER_DMA_QUEUE)
            return c

        lax.fori_loop(0, ROW_TILE, issue, 0)


def _wait_all_rows(buf, sem):
    pltpu.make_async_copy(buf, buf, sem).wait()


def _expert_mlp(xb, yb, wgu_s, wd_s, bgu_ref, bd_ref):
    xs = jnp.concatenate([xb[pl.ds(c, ROW_TILE, stride=CHUNKS), :] for c in range(CHUNKS)],
                         axis=1).astype(BF16)
    gu = _dot(xs, wgu_s[...]) + bgu_ref[0]
    gate = jnp.minimum(gu[:, 0:D_FF], SWIGLU_LIMIT)
    up = jnp.clip(gu[:, D_FF:2 * D_FF], -SWIGLU_LIMIT, SWIGLU_LIMIT)
    act = ((up + 1.0) * gate * jax.nn.sigmoid(SWIGLU_ALPHA * gate)).astype(BF16)
    y = _dot(act, wd_s[...]) + bd_ref[0]
    for c in range(CHUNKS):
        yb[pl.ds(c, ROW_TILE, stride=CHUNKS), :] = y[:, c * LANES:(c + 1) * LANES]


def _experts_kernel(n_pairs, te_ref, nt_ref, tab_ref, h_hbm, wgu_ref, bgu_ref, wd_ref, bd_ref, y_hbm,
                    x0, x1, y0, y1, gsem, ssem, wgu_s, wd_s):
    s = pl.program_id(0)
    n_used = nt_ref[0]
    par = s % 2
    tc = jnp.clip(s - 1, 0, te_ref.shape[0] - 1)
    xs, ys = (x0, x1), (y0, y1)

    @pl.when((s >= 1) & (s <= n_used) & ((s == 1) | (te_ref[tc] != te_ref[jnp.maximum(tc - 1, 0)])))
    def _():
        wgu_s[...] = wgu_ref[0].astype(BF16)
        wd_s[...] = wd_ref[0].astype(BF16)

    steady = (s >= 2) & (s < n_used)
    edge = jnp.logical_not(steady) & (s <= n_used + 1)
    for p in (0, 1):
        q = 1 - p
        mine = par == p

        @pl.when(mine & (s >= 3) & (s <= n_used + 1))
        def _(q=q):
            _wait_all_rows(ys[q], ssem.at[q])

        @pl.when(mine & steady)
        def _(p=p, q=q):
            _wait_all_rows(xs[q], gsem.at[q])
            _gather_tile(tab_ref, h_hbm, xs[p], gsem.at[p], s, True)
            _scatter_tile(n_pairs, tab_ref, te_ref, ys[p], y_hbm, ssem.at[p], s - 2, True)
            _expert_mlp(xs[q], ys[q], wgu_s, wd_s, bgu_ref, bd_ref)

        @pl.when(mine & edge & (s >= 1) & (s <= n_used))
        def _(q=q):
            _wait_all_rows(xs[q], gsem.at[q])

        @pl.when(mine & edge & (s < n_used))
        def _(p=p):
            _gather_tile(tab_ref, h_hbm, xs[p], gsem.at[p], s, False)

        @pl.when(mine & edge & (s >= 2))
        def _(p=p):
            _scatter_tile(n_pairs, tab_ref, te_ref, ys[p], y_hbm, ssem.at[p], s - 2, False)

        @pl.when(mine & edge & (s >= 1) & (s <= n_used))
        def _(q=q):
            _expert_mlp(xs[q], ys[q], wgu_s, wd_s, bgu_ref, bd_ref)

        @pl.when(mine & (s == n_used + 1))
        def _(p=p):
            _wait_all_rows(ys[p], ssem.at[p])


def _experts_call(n_pairs, tile_e, n_used, pair_of_row, h_rows, w_gu, b_gu, w_down, b_down):
    n_tiles = tile_e.shape[0]

    def expert_block(s, te, nt, tab):
        return (te[jnp.clip(s - 1, 0, n_tiles - 1)], 0, 0)

    grid_spec = pltpu.PrefetchScalarGridSpec(
        num_scalar_prefetch=3,
        grid=(n_tiles + 2,),
        in_specs=[
            pl.BlockSpec(memory_space=pl.ANY),
            pl.BlockSpec((1, D_MODEL, 2 * D_FF), expert_block),
            pl.BlockSpec((1, 1, 2 * D_FF), expert_block),
            pl.BlockSpec((1, D_FF, D_MODEL), expert_block),
            pl.BlockSpec((1, 1, D_MODEL), expert_block),
        ],
        out_specs=pl.BlockSpec(memory_space=pl.ANY),
        scratch_shapes=[
            pltpu.VMEM((ROW_TILE * CHUNKS, LANES), F32),
            pltpu.VMEM((ROW_TILE * CHUNKS, LANES), F32),
            pltpu.VMEM((ROW_TILE * CHUNKS, LANES), F32),
            pltpu.VMEM((ROW_TILE * CHUNKS, LANES), F32),
            pltpu.SemaphoreType.DMA((2,)),
            pltpu.SemaphoreType.DMA((2,)),
            pltpu.VMEM((D_MODEL, 2 * D_FF), BF16),
            pltpu.VMEM((D_FF, D_MODEL), BF16),
        ],
    )
    return pl.pallas_call(
        functools.partial(_experts_kernel, n_pairs),
        grid_spec=grid_spec,
        out_shape=jax.ShapeDtypeStruct(((n_pairs + N_EXPERTS * ROW_TILE) * CHUNKS, LANES), F32),
        compiler_params=pltpu.CompilerParams(
            dimension_semantics=("arbitrary",), vmem_limit_bytes=VMEM_LIMIT),
        name="experts",
    )(tile_e, n_used, pair_of_row, h_rows, w_gu, b_gu.reshape(N_EXPERTS, 1, 2 * D_FF),
      w_down, b_down.reshape(N_EXPERTS, 1, D_MODEL))


def _final_kernel(y_ref, x1_ref, gates_ref, fn_ref, out_ref):
    gates = gates_ref[...]
    cols = []
    for c in range(CHUNKS):
        acc = y_ref[pl.ds(c, TOK_TILE, stride=TOP_K * CHUNKS), :] * gates[:, 0:1]
        for kk in range(1, TOP_K):
            acc = acc + y_ref[pl.ds(kk * CHUNKS + c, TOK_TILE, stride=TOP_K * CHUNKS), :] * gates[:, kk:kk + 1]
        cols.append(acc)
    out_ref[...] = _rms(x1_ref[...] + jnp.concatenate(cols, axis=1), fn_ref[...])


def _final_call(tok0, n_tok, y_rows, x1_all, gates_all, final_norm):
    assert tok0 % TOK_TILE == 0 and n_tok % TOK_TILE == 0
    tile0 = tok0 // TOK_TILE
    return pl.pallas_call(
        _final_kernel,
        grid=(n_tok // TOK_TILE,),
        in_specs=[
            pl.BlockSpec((TOK_TILE * TOP_K * CHUNKS, LANES), lambda i: (tile0 + i, 0)),
            pl.BlockSpec((TOK_TILE, D_MODEL), lambda i: (tile0 + i, 0)),
            pl.BlockSpec((TOK_TILE, TOP_K), lambda i: (tile0 + i, 0)),
            pl.BlockSpec((1, D_MODEL), lambda i: (0, 0)),
        ],
        out_specs=pl.BlockSpec((TOK_TILE, D_MODEL), lambda i: (i, 0)),
        out_shape=jax.ShapeDtypeStruct((n_tok, D_MODEL), F32),
        compiler_params=pltpu.CompilerParams(
            dimension_semantics=("arbitrary",), vmem_limit_bytes=VMEM_LIMIT),
        name="final",
    )(y_rows, x1_all, gates_all, final_norm)


def _pick_tile(n, target):
    t = min(n, target)
    while n % t:
        t -= 1
    return t


def kernel(x_prompt, x_sample, state_gla, state_pool, norm1, w_in, w_gk2, b_gk, gla_norm, w_pool,
           pool_scale, w_o, norm2, w_router, b_router, w_gate_up, b_gate_up, w_down, b_down, final_norm):
    depth = w_in.shape[0]
    assert depth == 1
    B, L, _ = x_prompt.shape
    BS, LS, _ = x_sample.shape
    n_p, n_s = B * L, BS * LS
    n_all = n_p + n_s

    wi = w_in[0]
    o_glr = 2 * KEY_W + 2 * VAL_W
    wmain = jnp.concatenate(
        [wi[:, 0:o_glr], wi[:, o_glr + GATE_RANK:], wi[:, o_glr:o_glr + GATE_RANK],
         jnp.zeros((D_MODEL, LANES - GATE_RANK), F32)], axis=1).astype(BF16)
    wgk = jnp.concatenate([w_gk2[0], jnp.zeros((LANES - GATE_RANK, KEY_W), F32)], axis=0)
    wgkh, wgkl = _split2(wgk)
    wr = jnp.concatenate([w_router[0], jnp.zeros((D_MODEL, LANES - N_EXPERTS), F32)], axis=1)
    wrh, wrl = _split2(wr)
    br = jnp.concatenate([b_router[0], jnp.zeros((LANES - N_EXPERTS,), F32)]).reshape(1, LANES)
    weights = (norm1[0].reshape(1, D_MODEL), wmain, wgkh, wgkl, b_gk[0].reshape(1, KEY_W),
               gla_norm[0].reshape(1, DV), w_pool[0].astype(BF16), pool_scale[0].reshape(1, POOL_W),
               w_o[0].astype(BF16), norm2[0].reshape(1, D_MODEL), wrh, wrl, br)

    assert L % GLA_CHUNK == 0 and LS in (8, 16)
    cfg_p = MixerCfg(ns=1, tl=_pick_tile(L, 256), chunk=GLA_CHUNK, start_pos=0, n_alias=0)
    s0_p = jnp.zeros((B, VAL_W, KEY_W), F32)
    buf0_p = jnp.zeros((B, POOL_BUF, POOL_W), F32)
    x1_all, h_all, topi_all, gates_all, st_p, buf_p = _mixer_call(
        cfg_p, n_all, 0, x_prompt, s0_p, buf0_p, weights, ())

    cfg_s = MixerCfg(ns=_pick_tile(BS, 128 // LS), tl=LS, chunk=LS, start_pos=PAST_LEN, n_alias=4)
    r_s = cfg_s.ns * cfg_s.tl
    assert n_p % r_s == 0
    x1_all, h_all, topi_all, gates_all, st_s, buf_s = _mixer_call(
        cfg_s, n_all, n_p // r_s, x_sample, jnp.swapaxes(state_gla[0], -1, -2), state_pool[0], weights,
        (x1_all, h_all, topi_all, gates_all))

    pos, counts = _route_call(topi_all)
    n_tiles = (n_all * TOP_K + N_EXPERTS * (ROW_TILE - 1)) // ROW_TILE
    tiles_per_e = (counts[0, :N_EXPERTS] + (ROW_TILE - 1)) // ROW_TILE
    ends = jnp.cumsum(tiles_per_e)
    n_used = ends[-1].astype(jnp.int32)
    tile_ids = jnp.minimum(jnp.arange(n_tiles, dtype=jnp.int32), n_used - 1)
    tile_e = jnp.sum(tile_ids[:, None] >= ends[None, :], axis=1).astype(jnp.int32)
    n_pairs = n_all * TOP_K
    pair_of_row = _invperm_call(pos.reshape(n_pairs), -(-n_tiles * ROW_TILE // INV_BLOCK) * INV_BLOCK)

    y_rows = _experts_call(n_pairs, tile_e, n_used.reshape(1), pair_of_row, h_all,
                           w_gate_up[0], b_gate_up[0], w_down[0], b_down[0])

    fn = final_norm.reshape(1, D_MODEL)
    y_p = _final_call(0, n_p, y_rows, x1_all, gates_all, fn)
    y_s = _final_call(n_p, n_s, y_rows, x1_all, gates_all, fn)

    st_p = jnp.stack([st_p[:, h * DV:(h + 1) * DV, h * DK:(h + 1) * DK] for h in range(N_HEADS)], axis=1)
    return (y_p.reshape(B, L, D_MODEL), y_s.reshape(BS, LS, D_MODEL),
            jnp.swapaxes(st_p, -1, -2)[None], buf_p[None],
            jnp.swapaxes(st_s, -1, -2)[None], buf_s[None])
```

```python
import functools
from typing import NamedTuple

import jax
import jax.numpy as jnp
from jax import lax
from jax.experimental import pallas as pl
from jax.experimental.pallas import tpu as pltpu

F32 = jnp.float32
BF16 = jnp.bfloat16

D_MODEL = 1024
N_HEADS = 4
DK = 64
DV = 128
KEY_W = N_HEADS * DK
VAL_W = N_HEADS * DV
GATE_RANK = 16
GATE_NORMALIZER = 16.0
GLA_CHUNK = 64
POOL_WINDOWS = (2, 4, 8, 16)
POOL_W = 512
POOL_GROUP = 128
POOL_BUF = 15
N_EXPERTS = 32
TOP_K = 4
D_FF = 1024
SWIGLU_LIMIT = 7.0
SWIGLU_ALPHA = 1.702
EPS = 1e-5
PAST_LEN = 16384

LANES = 128
CHUNKS = D_MODEL // LANES
HALO = 16
DIAG = 16
MAIN_COLS = 2 * KEY_W + 2 * VAL_W + POOL_W + LANES
ROW_TILE = 256
TOK_TILE = 256
ROUTE_TILE = 512
INV_BLOCK = 2048
SCATTER_DMA_QUEUE = 1
N_BUF = 3
COMPUTE_LAG = 2
SCATTER_LAG = 3
VMEM_LIMIT = 56 * 1024 * 1024
NEG_BIG = -1e30


def _dot(a, b):
    return jnp.dot(a, b, preferred_element_type=F32)


def _dot_nt(a, b):
    return lax.dot_general(a, b, (((1,), (1,)), ((), ())), preferred_element_type=F32)


def _dot_tn(a, b):
    return lax.dot_general(a, b, (((0,), (0,)), ((), ())), preferred_element_type=F32)


def _split2(a):
    hi = a.astype(BF16)
    lo = (a - hi.astype(F32)).astype(BF16)
    return hi, lo


def _dot3(a, b_hi, b_lo):
    a_hi, a_lo = _split2(a)
    return _dot(a_hi, b_hi) + _dot(a_lo, b_hi) + _dot(a_hi, b_lo)


def _rms(x, w):
    return x * lax.rsqrt(jnp.mean(x * x, axis=-1, keepdims=True) + EPS) * w


class MixerCfg(NamedTuple):
    ns: int
    tl: int
    chunk: int
    start_pos: int
    n_alias: int


def _mixer_dims(cfg):
    rows = cfg.ns * cfg.tl
    diag = min(cfg.chunk, DIAG)
    n_off = cfg.chunk // diag - 1
    width = cfg.chunk if n_off else rows
    return rows, diag, n_off, width


def _scratch_spec(cfg):
    R, S, n_off, W = _mixer_dims(cfg)
    C, ns, tl = cfg.chunk, cfg.ns, cfg.tl
    spec = [
        ("ext", (ns, tl + HALO, POOL_W), F32),
        ("kh", (R + HALO, KEY_W), F32),
        ("gh", (R + HALO, KEY_W), F32),
        ("egl", (R, KEY_W), F32),
        ("a", (R, N_HEADS * W), F32),
        ("o", (R, VAL_W), F32),
    ]
    if n_off:
        nc = R // C
        spec += [
            ("st", (VAL_W, KEY_W), F32),
            ("qg", (R, KEY_W), BF16),
            ("kd", (R, KEY_W), BF16),
            ("v", (R, VAL_W), BF16),
            ("qcat", (R, n_off * KEY_W), BF16),
            ("kbd", (nc, N_HEADS * C, n_off * KEY_W), BF16),
            ("vbd", (nc, N_HEADS * C, VAL_W), BF16),
        ]
    else:
        spec += [
            ("st", (ns, N_HEADS, DV, DK), F32),
            ("qg", (R, KEY_W), F32),
            ("kd", (R, KEY_W), F32),
            ("v", (R, VAL_W), F32),
        ]
    return spec


def _gla_chunked(cfg, l, q, k, v, G, s0_ref, st_out_ref, bdm_ref, sc):
    R, S, n_off, W = _mixer_dims(cfg)
    C = cfg.chunk
    nc = R // C
    n_l = pl.num_programs(1)
    G3 = G.reshape(nc, C, KEY_W)
    glast = jnp.broadcast_to(G3[:, C - 1:C, :], (nc, C, KEY_W)).reshape(R, KEY_W)
    sc["qg"][...] = (q * jnp.exp(G)).astype(BF16)
    sc["kd"][...] = (k * jnp.exp(glast - G)).astype(BF16)
    sc["egl"][...] = jnp.exp(glast)
    v_bf = v.astype(BF16)
    sc["v"][...] = v_bf

    row_c = lax.broadcasted_iota(jnp.int32, (R, 1), 0) % C
    q_parts, k_parts = [], []
    for a in range(1, n_off + 1):
        ra = jnp.broadcast_to(G3[:, a * S - 1:a * S, :], (nc, C, KEY_W)).reshape(R, KEY_W)
        in_block = (row_c >= a * S) & (row_c < (a + 1) * S)
        q_parts.append(jnp.where(in_block, q * jnp.exp(jnp.minimum(G - ra, 0.0)), 0.0))
        k_parts.append(jnp.where(row_c < a * S, k * jnp.exp(jnp.minimum(ra - G, 0.0)), 0.0))
    sc["qcat"][...] = jnp.concatenate(q_parts, axis=1).astype(BF16)
    kcat = jnp.concatenate(k_parts, axis=1).astype(BF16)
    head_of_k = (lax.broadcasted_iota(jnp.int32, (1, n_off * KEY_W), 1) % KEY_W) // DK
    head_of_v = lax.broadcasted_iota(jnp.int32, (1, VAL_W), 1) // DV
    for h in range(N_HEADS):
        sc["kbd"][:, h * C:(h + 1) * C, :] = jnp.where(head_of_k == h, kcat, 0.0).reshape(nc, C, n_off * KEY_W)
        sc["vbd"][:, h * C:(h + 1) * C, :] = jnp.where(head_of_v == h, v_bf, 0.0).reshape(nc, C, VAL_W)

    @pl.when(l == 0)
    def _():
        sc["st"][...] = s0_ref[0]

    def chunk_body(c, carry):
        rows = pl.ds(pl.multiple_of(c * C, C), C)
        st = sc["st"][...]
        a_all = sc["a"][rows, :] + _dot_nt(sc["qcat"][rows, :], sc["kbd"][c])
        sc["o"][rows, :] = (_dot(a_all.astype(BF16), sc["vbd"][c])
                            + _dot_nt(sc["qg"][rows, :], st.astype(BF16)))
        upd = _dot_tn(sc["v"][rows, :], sc["kd"][rows, :])
        sc["st"][...] = st * sc["egl"][pl.ds(c * C, 1), :] + upd * bdm_ref[...]
        return carry

    lax.fori_loop(0, nc, chunk_body, 0)

    @pl.when(l == n_l - 1)
    def _():
        st_out_ref[0] = sc["st"][...]


def _gla_single_chunk(cfg, q, k, v, G, s0_ref, st_out_ref, sc):
    R, S, n_off, W = _mixer_dims(cfg)
    C = cfg.chunk
    G3 = G.reshape(R // C, C, KEY_W)
    glast = jnp.broadcast_to(G3[:, C - 1:C, :], (R // C, C, KEY_W)).reshape(R, KEY_W)
    sc["qg"][...] = q * jnp.exp(G)
    sc["kd"][...] = k * jnp.exp(glast - G)
    sc["egl"][...] = jnp.exp(glast)
    sc["v"][...] = v
    sc["st"][...] = s0_ref[...]

    a_bf = sc["a"][...].astype(BF16)
    v_bf = v.astype(BF16)
    for h in range(N_HEADS):
        sc["o"][:, h * DV:(h + 1) * DV] = _dot(a_bf[:, h * W:(h + 1) * W], v_bf[:, h * DV:(h + 1) * DV])

    kpad = max(C, 16)

    def seq_body(c, carry):
        cs = pl.multiple_of(c * C, C)
        rows = pl.ds(cs, C)
        for h in range(N_HEADS):
            kc = slice(h * DK, (h + 1) * DK)
            vc = slice(h * DV, (h + 1) * DV)
            st = sc["st"][c, h]
            sc["o"][rows, vc] = sc["o"][rows, vc] + _dot_nt(sc["qg"][rows, kc].astype(BF16), st.astype(BF16))
            v_h = sc["v"][rows, vc]
            kd_h = sc["kd"][rows, kc]
            if kpad > C:
                v_h = jnp.concatenate([v_h, jnp.zeros((kpad - C, DV), F32)], axis=0)
                kd_h = jnp.concatenate([kd_h, jnp.zeros((kpad - C, DK), F32)], axis=0)
            sc["st"][c, h] = st * sc["egl"][pl.ds(cs, 1), kc] + _dot_tn(v_h.astype(BF16), kd_h.astype(BF16))
        return carry

    lax.fori_loop(0, R // C, seq_body, 0)
    st_out_ref[...] = sc["st"][...]


def _mixer_kernel(cfg, x_ref, s0_ref, buf0_ref, n1_ref, wmain_ref, wgkh_ref, wgkl_ref, bgk_ref,
                  gn_ref, wpool_ref, pscale_ref, wo_ref, n2_ref, wrh_ref, wrl_ref, br_ref,
                  tri_ref, hb_ref, bdm_ref, *rest):
    rest = rest[cfg.n_alias:]
    x1_ref, h_ref, topi_ref, gates_ref, st_out_ref, bufo_ref = rest[:6]
    sc = dict(zip([name for name, _, _ in _scratch_spec(cfg)], rest[6:]))

    ns, tl, C = cfg.ns, cfg.tl, cfg.chunk
    R, S, n_off, W = _mixer_dims(cfg)
    l = pl.program_id(1)
    n_l = pl.num_programs(1)
    ext_s = sc["ext"]

    x = x_ref[...].reshape(R, D_MODEL)
    xn = _rms(x, n1_ref[...]).astype(BF16)
    p = _dot(xn, wmain_ref[...])
    q = p[:, 0:KEY_W] * (DK ** -0.5)
    k = p[:, KEY_W:2 * KEY_W]
    v = p[:, 2 * KEY_W:2 * KEY_W + VAL_W]
    og = p[:, 2 * KEY_W + VAL_W:2 * KEY_W + 2 * VAL_W]
    u = p[:, 2 * KEY_W + 2 * VAL_W:2 * KEY_W + 2 * VAL_W + POOL_W]
    glr = p[:, MAIN_COLS - LANES:MAIN_COLS]
    ext_s[:, HALO:HALO + tl, :] = u.reshape(ns, tl, POOL_W)

    z = _dot3(glr, wgkh_ref[...], wgkl_ref[...]) + bgk_ref[...]
    g = -(jnp.maximum(-z, 0.0) + jnp.log1p(jnp.exp(-jnp.abs(z)))) / GATE_NORMALIZER
    tri = tri_ref[...]
    g_hi = g.astype(BF16)
    g_r = g - g_hi.astype(F32)
    g_mid = g_r.astype(BF16)
    g_lo = (g_r - g_mid.astype(F32)).astype(BF16)
    G = _dot(tri, g_hi) + _dot(tri, g_mid) + _dot(tri, g_lo)

    sc["kh"][0:HALO, :] = jnp.zeros((HALO, KEY_W), F32)
    sc["gh"][0:HALO, :] = jnp.zeros((HALO, KEY_W), F32)
    sc["kh"][HALO:HALO + R, :] = k
    sc["gh"][HALO:HALO + R, :] = G
    row = lax.broadcasted_iota(jnp.int32, (R, 1), 0)
    row_s = row % S
    row_w = row % W
    col_w = lax.broadcasted_iota(jnp.int32, (1, N_HEADS * W), 1) % W
    head_bcast = hb_ref[...]
    a_all = jnp.zeros((R, N_HEADS * W), F32)
    for d in range(S):
        k_sh = sc["kh"][HALO - d:HALO - d + R, :]
        g_sh = sc["gh"][HALO - d:HALO - d + R, :]
        e = jnp.exp(jnp.where(row_s >= d, G - g_sh, NEG_BIG))
        term = (q * k_sh * e).astype(BF16)
        spread = _dot(term, head_bcast)
        a_all = jnp.where(col_w == row_w - d, spread, a_all)
    sc["a"][...] = a_all

    if n_off:
        _gla_chunked(cfg, l, q, k, v, G, s0_ref, st_out_ref, bdm_ref, sc)
    else:
        _gla_single_chunk(cfg, q, k, v, G, s0_ref, st_out_ref, sc)

    o = sc["o"][...]
    gn = gn_ref[...]
    o_heads = []
    for h in range(N_HEADS):
        vc = slice(h * DV, (h + 1) * DV)
        og_h = og[:, vc]
        o_heads.append(_rms(o[:, vc], gn) * (og_h * jax.nn.sigmoid(og_h)))

    @pl.when(l == 0)
    def _():
        ext_s[:, 0:HALO - POOL_BUF, :] = jnp.zeros((ns, HALO - POOL_BUF, POOL_W), F32)
        ext_s[:, HALO - POOL_BUF:HALO, :] = buf0_ref[...]

    pos = cfg.start_pos + l * tl + lax.broadcasted_iota(jnp.int32, (1, tl, 1), 1)
    z_groups = []
    for gi, w in enumerate(POOL_WINDOWS):
        gc = slice(gi * POOL_GROUP, (gi + 1) * POOL_GROUP)
        s = ext_s[:, HALO:HALO + tl, gc]
        for dd in range(1, w):
            s = s + ext_s[:, HALO - dd:HALO - dd + tl, gc]
        cnt = jnp.minimum(w, pos + 1).astype(F32)
        dmean = (s / cnt - ext_s[:, HALO:HALO + tl, gc]).reshape(R, POOL_GROUP)
        z_groups.append(_dot(dmean.astype(BF16), wpool_ref[gi]))
    zp = jnp.concatenate(z_groups, axis=1) * pscale_ref[...]

    @pl.when(l == n_l - 1)
    def _():
        bufo_ref[...] = ext_s[:, tl + HALO - POOL_BUF:tl + HALO, :]

    @pl.when(l < n_l - 1)
    def _():
        ext_s[:, 0:HALO, :] = ext_s[:, tl:tl + HALO, :]

    cat = jnp.concatenate(o_heads + [zp], axis=1).astype(BF16)
    x1 = x + _dot(cat, wo_ref[...])
    x1_ref[...] = x1
    hn = _rms(x1, n2_ref[...])
    for c in range(CHUNKS):
        h_ref[pl.ds(c, R, stride=CHUNKS), :] = hn[:, c * LANES:(c + 1) * LANES]
    logits = _dot3(hn, wrh_ref[...], wrl_ref[...]) + br_ref[...]
    lane = lax.broadcasted_iota(jnp.int32, (R, LANES), 1)
    lg = jnp.where(lane < N_EXPERTS, logits, -jnp.inf)
    vals, idxs = [], []
    for _ in range(TOP_K):
        m = jnp.max(lg, axis=1, keepdims=True)
        idx = jnp.min(jnp.where(lg == m, lane, LANES), axis=1, keepdims=True)
        vals.append(m)
        idxs.append(idx)
        lg = jnp.where(lane == idx, -jnp.inf, lg)
    exps = [jnp.exp(vv - vals[0]) for vv in vals]
    den = exps[0] + exps[1] + exps[2] + exps[3]
    ti = jnp.zeros((R, LANES), jnp.int32)
    gt = jnp.zeros((R, LANES), F32)
    for kk in range(TOP_K):
        ti = jnp.where(lane == kk, idxs[kk], ti)
        gt = jnp.where(lane == kk, exps[kk] / den, gt)
    topi_ref[...] = ti[:, 0:TOP_K]
    gates_ref[...] = gt[:, 0:TOP_K]


def _mixer_constants(cfg):
    R, S, n_off, W = _mixer_dims(cfg)
    C = cfg.chunk
    r = jnp.arange(R)
    tri = ((r[:, None] // C == r[None, :] // C) & (r[None, :] <= r[:, None])).astype(BF16)
    head_bcast = (jnp.arange(KEY_W)[:, None] // DK == jnp.arange(N_HEADS * W)[None, :] // W).astype(BF16)
    block_diag = (jnp.arange(VAL_W)[:, None] // DV == jnp.arange(KEY_W)[None, :] // DK).astype(F32)
    return tri, head_bcast, block_diag


def _mixer_call(cfg, n_tok_all, row_block0, x, s0, buf0, weights, aliased):
    B, L, _ = x.shape
    ns, tl, C = cfg.ns, cfg.tl, cfg.chunk
    R, S, n_off, W = _mixer_dims(cfg)
    n_b, n_l = B // ns, L // tl
    assert B % ns == 0 and L % tl == 0 and tl % C == 0 and R % 8 == 0
    assert (n_l == 1 or tl >= HALO) and (n_off == 0 or ns == 1) and (n_off > 0 or tl == C)

    def const(shape):
        return pl.BlockSpec(shape, lambda b, l: (0,) * len(shape))

    state_block = (1, VAL_W, KEY_W) if n_off else (ns, N_HEADS, DV, DK)
    state_spec = pl.BlockSpec(state_block, lambda b, l: (b,) + (0,) * (len(state_block) - 1))
    consts = _mixer_constants(cfg)
    operands = (x, s0, buf0) + tuple(weights) + consts
    in_specs = [
        pl.BlockSpec((ns, tl, D_MODEL), lambda b, l: (b, l, 0)),
        state_spec,
        pl.BlockSpec((ns, POOL_BUF, POOL_W), lambda b, l: (b, 0, 0)),
    ] + [const(w.shape) for w in tuple(weights) + consts] + [pl.BlockSpec(memory_space=pl.ANY)] * len(aliased)

    def tok_block(rows, width):
        return pl.BlockSpec((rows, width), lambda b, l: (row_block0 + b * n_l + l, 0))

    out_specs = [
        tok_block(R, D_MODEL), tok_block(R * CHUNKS, LANES), tok_block(R, TOP_K), tok_block(R, TOP_K),
        state_spec,
        pl.BlockSpec((ns, POOL_BUF, POOL_W), lambda b, l: (b, 0, 0)),
    ]
    out_shape = [
        jax.ShapeDtypeStruct((n_tok_all, D_MODEL), F32),
        jax.ShapeDtypeStruct((n_tok_all * CHUNKS, LANES), F32),
        jax.ShapeDtypeStruct((n_tok_all, TOP_K), jnp.int32),
        jax.ShapeDtypeStruct((n_tok_all, TOP_K), F32),
        jax.ShapeDtypeStruct((B,) + state_block[1:], F32),
        jax.ShapeDtypeStruct((B, POOL_BUF, POOL_W), F32),
    ]
    aliases = {len(operands) + i: i for i in range(len(aliased))}
    return pl.pallas_call(
        functools.partial(_mixer_kernel, cfg),
        grid=(n_b, n_l),
        in_specs=in_specs,
        out_specs=out_specs,
        out_shape=out_shape,
        scratch_shapes=[pltpu.VMEM(shape, dtype) for _, shape, dtype in _scratch_spec(cfg)],
        input_output_aliases=aliases,
        compiler_params=pltpu.CompilerParams(
            dimension_semantics=("arbitrary", "arbitrary"), vmem_limit_bytes=VMEM_LIMIT),
        name="mixer",
    )(*operands, *aliased)


def _route_kernel(topi_ref, pos_ref, counts_ref, cnt_s, carry_s, gstart_s):
    ph = pl.program_id(0)
    i = pl.program_id(1)
    TT = topi_ref.shape[0]
    topi = topi_ref[...]
    lane = lax.broadcasted_iota(jnp.int32, (TT, LANES), 1)
    hot = jnp.zeros((TT, LANES), F32)
    for kk in range(TOP_K):
        hot = hot + (lane == topi[:, kk:kk + 1]).astype(F32)
    colsum = jnp.sum(hot, axis=0, keepdims=True)

    @pl.when((ph == 0) & (i == 0))
    def _():
        cnt_s[...] = jnp.zeros_like(cnt_s)

    @pl.when(ph == 0)
    def _():
        cnt_s[...] = cnt_s[...] + colsum

    @pl.when((ph == 1) & (i == 0))
    def _():
        cnt = cnt_s[...]
        counts_ref[...] = cnt.astype(jnp.int32)
        tiles = jnp.floor((cnt + (ROW_TILE - 1)) * (1.0 / ROW_TILE))
        tiles8 = jnp.broadcast_to(tiles, (8, LANES))
        ur = lax.broadcasted_iota(jnp.int32, (LANES, LANES), 0)
        uc = lax.broadcasted_iota(jnp.int32, (LANES, LANES), 1)
        upper = (ur < uc).astype(BF16)
        t_hi, t_lo = _split2(tiles8)
        excl = _dot(t_hi, upper) + _dot(t_lo, upper)
        gstart_s[...] = excl[0:1, :] * float(ROW_TILE)
        carry_s[...] = jnp.zeros_like(carry_s)

    @pl.when(ph == 1)
    def _():
        lr = lax.broadcasted_iota(jnp.int32, (TT, TT), 0)
        lc = lax.broadcasted_iota(jnp.int32, (TT, TT), 1)
        lower = (lc < lr).astype(BF16)
        rank = _dot(lower, hot.astype(BF16)) + carry_s[...] + gstart_s[...]
        out = jnp.zeros((TT, LANES), F32)
        for kk in range(TOP_K):
            pk = jnp.sum(jnp.where(lane == topi[:, kk:kk + 1], rank, 0.0), axis=1, keepdims=True)
            out = jnp.where(lane == kk, pk, out)
        pos_ref[...] = out[:, 0:TOP_K].astype(jnp.int32)
        carry_s[...] = carry_s[...] + colsum


def _route_call(topi):
    T = topi.shape[0]
    assert T % ROUTE_TILE == 0
    return pl.pallas_call(
        _route_kernel,
        grid=(2, T // ROUTE_TILE),
        in_specs=[pl.BlockSpec((ROUTE_TILE, TOP_K), lambda ph, i: (i, 0))],
        out_specs=[pl.BlockSpec((ROUTE_TILE, TOP_K), lambda ph, i: (i * ph, 0)),
                   pl.BlockSpec((1, LANES), lambda ph, i: (0, 0))],
        out_shape=[jax.ShapeDtypeStruct((T, TOP_K), jnp.int32),
                   jax.ShapeDtypeStruct((1, LANES), jnp.int32)],
        scratch_shapes=[pltpu.VMEM((1, LANES), F32)] * 3,
        compiler_params=pltpu.CompilerParams(dimension_semantics=("arbitrary", "arbitrary")),
        name="route",
    )(topi)


def _invperm_kernel(n_zero_steps, pos_ref, tok_ref):
    i = pl.program_id(0)

    @pl.when(i < n_zero_steps)
    def _():
        def clear(r, c):
            tok_ref[i * INV_BLOCK + r] = -1
            return c

        lax.fori_loop(0, INV_BLOCK, clear, 0, unroll=16)

    @pl.when(i >= n_zero_steps)
    def _():
        base = (i - n_zero_steps) * INV_BLOCK

        def put(n, c):
            tok_ref[pos_ref[n]] = base + n
            return c

        lax.fori_loop(0, INV_BLOCK, put, 0, unroll=16)


def _invperm_call(pos_flat, n_rows):
    n_pairs = pos_flat.shape[0]
    assert n_rows % INV_BLOCK == 0 and n_pairs % INV_BLOCK == 0
    n_zero = n_rows // INV_BLOCK
    return pl.pallas_call(
        functools.partial(_invperm_kernel, n_zero),
        grid=(n_zero + n_pairs // INV_BLOCK,),
        in_specs=[pl.BlockSpec((INV_BLOCK,), lambda i: (jnp.maximum(i - n_zero, 0),),
                               memory_space=pltpu.SMEM)],
        out_specs=pl.BlockSpec(memory_space=pltpu.SMEM),
        out_shape=jax.ShapeDtypeStruct((n_rows,), jnp.int32),
        compiler_params=pltpu.CompilerParams(dimension_semantics=("arbitrary",)),
        name="invperm",
    )(pos_flat)


def _row_copy_in(h_hbm, xb, sem, r, pair):
    tok = jnp.maximum(pair, 0) >> 2
    return pltpu.make_async_copy(h_hbm.at[pl.ds(pl.multiple_of(tok * CHUNKS, CHUNKS), CHUNKS), :],
                                 xb.at[pl.ds(r * CHUNKS, CHUNKS), :], sem)


def _row_copy_out(yb, y_hbm, sem, r, dst_row):
    return pltpu.make_async_copy(yb.at[pl.ds(r * CHUNKS, CHUNKS), :],
                                 y_hbm.at[pl.ds(pl.multiple_of(dst_row * CHUNKS, CHUNKS), CHUNKS), :], sem)


def _gather_tile(tab_ref, h_hbm, xb, sem, tile, unrolled):
    base = tile * ROW_TILE
    if unrolled:
        for r in range(ROW_TILE):
            _row_copy_in(h_hbm, xb, sem, r, tab_ref[base + r]).start()
    else:
        def issue(r, c):
            _row_copy_in(h_hbm, xb, sem, r, tab_ref[base + r]).start()
            return c

        lax.fori_loop(0, ROW_TILE, issue, 0)


def _scatter_tile(n_pairs, tab_ref, te_ref, yb, y_hbm, sem, tile, unrolled):
    base = tile * ROW_TILE
    spare0 = n_pairs + te_ref[tile] * ROW_TILE

    def dst_row(r):
        pair = tab_ref[base + r]
        return jnp.where(pair >= 0, pair, spare0 + r)

    if unrolled:
        for r in range(ROW_TILE):
            _row_copy_out(yb, y_hbm, sem, r, dst_row(r)).start(priority=SCATTER_DMA_QUEUE)
    else:
        def issue(r, c):
            _row_copy_out(yb, y_hbm, sem, r, dst_row(r)).start(priority=SCATTER_DMA_QUEUE)
            return c

        lax.fori_loop(0, ROW_TILE, issue, 0)


def _wait_all_rows(buf, sem):
    pltpu.make_async_copy(buf, buf, sem).wait()


def _expert_mlp(xb, yb, wgu_s, wd_s, bgu_ref, bd_ref):
    xs = jnp.concatenate([xb[pl.ds(c, ROW_TILE, stride=CHUNKS), :] for c in range(CHUNKS)],
                         axis=1).astype(BF16)
    gu = _dot(xs, wgu_s[...]) + bgu_ref[0]
    gate = jnp.minimum(gu[:, 0:D_FF], SWIGLU_LIMIT)
    up = jnp.clip(gu[:, D_FF:2 * D_FF], -SWIGLU_LIMIT, SWIGLU_LIMIT)
    act = ((up + 1.0) * gate * jax.nn.sigmoid(SWIGLU_ALPHA * gate)).astype(BF16)
    y = _dot(act, wd_s[...]) + bd_ref[0]
    for c in range(CHUNKS):
        yb[pl.ds(c, ROW_TILE, stride=CHUNKS), :] = y[:, c * LANES:(c + 1) * LANES]


def _experts_kernel(n_pairs, te_ref, nt_ref, tab_ref, h_hbm, wgu_ref, bgu_ref, wd_ref, bd_ref, y_hbm,
                    x0, x1, x2, y0, y1, y2, gsem, ssem, wgu_s, wd_s):
    s = pl.program_id(0)
    n_used = nt_ref[0]
    par = s % N_BUF
    tc = jnp.clip(s - COMPUTE_LAG, 0, te_ref.shape[0] - 1)
    xs, ys = (x0, x1, x2), (y0, y1, y2)
    computing = (s >= COMPUTE_LAG) & (s < n_used + COMPUTE_LAG)

    @pl.when(computing & ((s == COMPUTE_LAG) | (te_ref[tc] != te_ref[jnp.maximum(tc - 1, 0)])))
    def _():
        wgu_s[...] = wgu_ref[0].astype(BF16)
        wd_s[...] = wd_ref[0].astype(BF16)

    last = n_used + SCATTER_LAG - 1
    steady = (s >= SCATTER_LAG) & (s < n_used)
    edge = jnp.logical_not(steady) & (s <= last)
    for p in range(N_BUF):
        c = (p + N_BUF - COMPUTE_LAG) % N_BUF
        mine = par == p

        @pl.when(mine & (s >= COMPUTE_LAG + N_BUF) & (s <= last))
        def _(c=c):
            _wait_all_rows(ys[c], ssem.at[c])

        @pl.when(mine & steady)
        def _(p=p, c=c):
            _wait_all_rows(xs[c], gsem.at[c])
            _gather_tile(tab_ref, h_hbm, xs[p], gsem.at[p], s, True)
            _scatter_tile(n_pairs, tab_ref, te_ref, ys[p], y_hbm, ssem.at[p], s - SCATTER_LAG, True)
            _expert_mlp(xs[c], ys[c], wgu_s, wd_s, bgu_ref, bd_ref)

        @pl.when(mine & edge & computing)
        def _(c=c):
            _wait_all_rows(xs[c], gsem.at[c])

        @pl.when(mine & edge & (s < n_used))
        def _(p=p):
            _gather_tile(tab_ref, h_hbm, xs[p], gsem.at[p], s, False)

        @pl.when(mine & edge & (s >= SCATTER_LAG))
        def _(p=p):
            _scatter_tile(n_pairs, tab_ref, te_ref, ys[p], y_hbm, ssem.at[p], s - SCATTER_LAG, False)

        @pl.when(mine & edge & computing)
        def _(c=c):
            _expert_mlp(xs[c], ys[c], wgu_s, wd_s, bgu_ref, bd_ref)

        @pl.when(mine & (s == last))
        def _(p=p):
            _wait_all_rows(ys[p], ssem.at[p])
            _wait_all_rows(ys[(p + N_BUF - 1) % N_BUF], ssem.at[(p + N_BUF - 1) % N_BUF])


def _experts_call(n_pairs, tile_e, n_used, pair_of_row, h_rows, w_gu, b_gu, w_down, b_down):
    n_tiles = tile_e.shape[0]

    def expert_block(s, te, nt, tab):
        return (te[jnp.clip(s - COMPUTE_LAG, 0, n_tiles - 1)], 0, 0)

    row_buffer = pltpu.VMEM((ROW_TILE * CHUNKS, LANES), F32)
    grid_spec = pltpu.PrefetchScalarGridSpec(
        num_scalar_prefetch=3,
        grid=(n_tiles + SCATTER_LAG,),
        in_specs=[
            pl.BlockSpec(memory_space=pl.ANY),
            pl.BlockSpec((1, D_MODEL, 2 * D_FF), expert_block),
            pl.BlockSpec((1, 1, 2 * D_FF), expert_block),
            pl.BlockSpec((1, D_FF, D_MODEL), expert_block),
            pl.BlockSpec((1, 1, D_MODEL), expert_block),
        ],
        out_specs=pl.BlockSpec(memory_space=pl.ANY),
        scratch_shapes=[row_buffer] * N_BUF + [row_buffer] * N_BUF + [
            pltpu.SemaphoreType.DMA((N_BUF,)),
            pltpu.SemaphoreType.DMA((N_BUF,)),
            pltpu.VMEM((D_MODEL, 2 * D_FF), BF16),
            pltpu.VMEM((D_FF, D_MODEL), BF16),
        ],
    )
    return pl.pallas_call(
        functools.partial(_experts_kernel, n_pairs),
        grid_spec=grid_spec,
        out_shape=jax.ShapeDtypeStruct(((n_pairs + N_EXPERTS * ROW_TILE) * CHUNKS, LANES), F32),
        compiler_params=pltpu.CompilerParams(
            dimension_semantics=("arbitrary",), vmem_limit_bytes=VMEM_LIMIT),
        name="experts",
    )(tile_e, n_used, pair_of_row, h_rows, w_gu, b_gu.reshape(N_EXPERTS, 1, 2 * D_FF),
      w_down, b_down.reshape(N_EXPERTS, 1, D_MODEL))


def _final_kernel(y_ref, x1_ref, gates_ref, fn_ref, out_ref):
    gates = gates_ref[...]
    cols = []
    for c in range(CHUNKS):
        acc = y_ref[pl.ds(c, TOK_TILE, stride=TOP_K * CHUNKS), :] * gates[:, 0:1]
        for kk in range(1, TOP_K):
            acc = acc + y_ref[pl.ds(kk * CHUNKS + c, TOK_TILE, stride=TOP_K * CHUNKS), :] * gates[:, kk:kk + 1]
        cols.append(acc)
    out_ref[...] = _rms(x1_ref[...] + jnp.concatenate(cols, axis=1), fn_ref[...])


def _final_call(tok0, n_tok, y_rows, x1_all, gates_all, final_norm):
    assert tok0 % TOK_TILE == 0 and n_tok % TOK_TILE == 0
    tile0 = tok0 // TOK_TILE
    return pl.pallas_call(
        _final_kernel,
        grid=(n_tok // TOK_TILE,),
        in_specs=[
            pl.BlockSpec((TOK_TILE * TOP_K * CHUNKS, LANES), lambda i: (tile0 + i, 0)),
            pl.BlockSpec((TOK_TILE, D_MODEL), lambda i: (tile0 + i, 0)),
            pl.BlockSpec((TOK_TILE, TOP_K), lambda i: (tile0 + i, 0)),
            pl.BlockSpec((1, D_MODEL), lambda i: (0, 0)),
        ],
        out_specs=pl.BlockSpec((TOK_TILE, D_MODEL), lambda i: (i, 0)),
        out_shape=jax.ShapeDtypeStruct((n_tok, D_MODEL), F32),
        compiler_params=pltpu.CompilerParams(
            dimension_semantics=("arbitrary",), vmem_limit_bytes=VMEM_LIMIT),
        name="final",
    )(y_rows, x1_all, gates_all, final_norm)


def _pick_tile(n, target):
    t = min(n, target)
    while n % t:
        t -= 1
    return t


def kernel(x_prompt, x_sample, state_gla, state_pool, norm1, w_in, w_gk2, b_gk, gla_norm, w_pool,
           pool_scale, w_o, norm2, w_router, b_router, w_gate_up, b_gate_up, w_down, b_down, final_norm):
    depth = w_in.shape[0]
    assert depth == 1
    B, L, _ = x_prompt.shape
    BS, LS, _ = x_sample.shape
    n_p, n_s = B * L, BS * LS
    n_all = n_p + n_s

    wi = w_in[0]
    o_glr = 2 * KEY_W + 2 * VAL_W
    wmain = jnp.concatenate(
        [wi[:, 0:o_glr], wi[:, o_glr + GATE_RANK:], wi[:, o_glr:o_glr + GATE_RANK],
         jnp.zeros((D_MODEL, LANES - GATE_RANK), F32)], axis=1).astype(BF16)
    wgk = jnp.concatenate([w_gk2[0], jnp.zeros((LANES - GATE_RANK, KEY_W), F32)], axis=0)
    wgkh, wgkl = _split2(wgk)
    wr = jnp.concatenate([w_router[0], jnp.zeros((D_MODEL, LANES - N_EXPERTS), F32)], axis=1)
    wrh, wrl = _split2(wr)
    br = jnp.concatenate([b_router[0], jnp.zeros((LANES - N_EXPERTS,), F32)]).reshape(1, LANES)
    weights = (norm1[0].reshape(1, D_MODEL), wmain, wgkh, wgkl, b_gk[0].reshape(1, KEY_W),
               gla_norm[0].reshape(1, DV), w_pool[0].astype(BF16), pool_scale[0].reshape(1, POOL_W),
               w_o[0].astype(BF16), norm2[0].reshape(1, D_MODEL), wrh, wrl, br)

    assert L % GLA_CHUNK == 0 and LS in (8, 16)
    cfg_p = MixerCfg(ns=1, tl=_pick_tile(L, 256), chunk=GLA_CHUNK, start_pos=0, n_alias=0)
    s0_p = jnp.zeros((B, VAL_W, KEY_W), F32)
    buf0_p = jnp.zeros((B, POOL_BUF, POOL_W), F32)
    x1_all, h_all, topi_all, gates_all, st_p, buf_p = _mixer_call(
        cfg_p, n_all, 0, x_prompt, s0_p, buf0_p, weights, ())

    cfg_s = MixerCfg(ns=_pick_tile(BS, 128 // LS), tl=LS, chunk=LS, start_pos=PAST_LEN, n_alias=4)
    r_s = cfg_s.ns * cfg_s.tl
    assert n_p % r_s == 0
    x1_all, h_all, topi_all, gates_all, st_s, buf_s = _mixer_call(
        cfg_s, n_all, n_p // r_s, x_sample, jnp.swapaxes(state_gla[0], -1, -2), state_pool[0], weights,
        (x1_all, h_all, topi_all, gates_all))

    pos, counts = _route_call(topi_all)
    n_tiles = (n_all * TOP_K + N_EXPERTS * (ROW_TILE - 1)) // ROW_TILE
    tiles_per_e = (counts[0, :N_EXPERTS] + (ROW_TILE - 1)) // ROW_TILE
    ends = jnp.cumsum(tiles_per_e)
    n_used = ends[-1].astype(jnp.int32)
    tile_ids = jnp.minimum(jnp.arange(n_tiles, dtype=jnp.int32), n_used - 1)
    tile_e = jnp.sum(tile_ids[:, None] >= ends[None, :], axis=1).astype(jnp.int32)
    n_pairs = n_all * TOP_K
    pair_of_row = _invperm_call(pos.reshape(n_pairs), -(-n_tiles * ROW_TILE // INV_BLOCK) * INV_BLOCK)

    y_rows = _experts_call(n_pairs, tile_e, n_used.reshape(1), pair_of_row, h_all,
                           w_gate_up[0], b_gate_up[0], w_down[0], b_down[0])

    fn = final_norm.reshape(1, D_MODEL)
    y_p = _final_call(0, n_p, y_rows, x1_all, gates_all, fn)
    y_s = _final_call(n_p, n_s, y_rows, x1_all, gates_all, fn)

    st_p = jnp.stack([st_p[:, h * DV:(h + 1) * DV, h * DK:(h + 1) * DK] for h in range(N_HEADS)], axis=1)
    return (y_p.reshape(B, L, D_MODEL), y_s.reshape(BS, LS, D_MODEL),
            jnp.swapaxes(st_p, -1, -2)[None], buf_p[None],
            jnp.swapaxes(st_s, -1, -2)[None], buf_s[None])
```

```python
import functools
from typing import NamedTuple

import jax
import jax.numpy as jnp
from jax import lax
from jax.experimental import pallas as pl
from jax.experimental.pallas import tpu as pltpu

F32 = jnp.float32
BF16 = jnp.bfloat16

D_MODEL = 1024
N_HEADS = 4
DK = 64
DV = 128
KEY_W = N_HEADS * DK
VAL_W = N_HEADS * DV
GATE_RANK = 16
GATE_NORMALIZER = 16.0
GLA_CHUNK = 64
POOL_WINDOWS = (2, 4, 8, 16)
POOL_W = 512
POOL_GROUP = 128
POOL_BUF = 15
N_EXPERTS = 32
TOP_K = 4
D_FF = 1024
SWIGLU_LIMIT = 7.0
SWIGLU_ALPHA = 1.702
EPS = 1e-5
PAST_LEN = 16384

LANES = 128
CHUNKS = D_MODEL // LANES
HALO = 16
DIAG = 16
MAIN_COLS = 2 * KEY_W + 2 * VAL_W + POOL_W + LANES
ROW_TILE = 256
TOK_TILE = 128
ROUTE_TILE = 512
INV_BLOCK = 2048
WRITE_DMA_QUEUE = 1
N_BUF = 3
COMPUTE_LAG = 2
VMEM_LIMIT = 56 * 1024 * 1024
NEG_BIG = -1e30


def _dot(a, b):
    return jnp.dot(a, b, preferred_element_type=F32)


def _dot_nt(a, b):
    return lax.dot_general(a, b, (((1,), (1,)), ((), ())), preferred_element_type=F32)


def _dot_tn(a, b):
    return lax.dot_general(a, b, (((0,), (0,)), ((), ())), preferred_element_type=F32)


def _split2(a):
    hi = a.astype(BF16)
    lo = (a - hi.astype(F32)).astype(BF16)
    return hi, lo


def _dot3(a, b_hi, b_lo):
    a_hi, a_lo = _split2(a)
    return _dot(a_hi, b_hi) + _dot(a_lo, b_hi) + _dot(a_hi, b_lo)


def _rms(x, w):
    return x * lax.rsqrt(jnp.mean(x * x, axis=-1, keepdims=True) + EPS) * w


class MixerCfg(NamedTuple):
    ns: int
    tl: int
    chunk: int
    start_pos: int
    n_alias: int


def _mixer_dims(cfg):
    rows = cfg.ns * cfg.tl
    diag = min(cfg.chunk, DIAG)
    n_off = cfg.chunk // diag - 1
    width = cfg.chunk if n_off else rows
    return rows, diag, n_off, width


def _scratch_spec(cfg):
    R, S, n_off, W = _mixer_dims(cfg)
    C, ns, tl = cfg.chunk, cfg.ns, cfg.tl
    spec = [
        ("ext", (ns, tl + HALO, POOL_W), F32),
        ("kh", (R + HALO, KEY_W), F32),
        ("gh", (R + HALO, KEY_W), F32),
        ("egl", (R, KEY_W), F32),
        ("a", (R, N_HEADS * W), F32),
        ("o", (R, VAL_W), F32),
    ]
    if n_off:
        nc = R // C
        spec += [
            ("st", (VAL_W, KEY_W), F32),
            ("qg", (R, KEY_W), BF16),
            ("kd", (R, KEY_W), BF16),
            ("v", (R, VAL_W), BF16),
            ("qcat", (R, n_off * KEY_W), BF16),
            ("kbd", (nc, N_HEADS * C, n_off * KEY_W), BF16),
            ("vbd", (nc, N_HEADS * C, VAL_W), BF16),
        ]
    else:
        spec += [
            ("st", (ns, N_HEADS, DV, DK), F32),
            ("qg", (R, KEY_W), F32),
            ("kd", (R, KEY_W), F32),
            ("v", (R, VAL_W), F32),
        ]
    return spec


def _gla_chunked(cfg, l, q, k, v, G, s0_ref, st_out_ref, bdm_ref, sc):
    R, S, n_off, W = _mixer_dims(cfg)
    C = cfg.chunk
    nc = R // C
    n_l = pl.num_programs(1)
    G3 = G.reshape(nc, C, KEY_W)
    glast = jnp.broadcast_to(G3[:, C - 1:C, :], (nc, C, KEY_W)).reshape(R, KEY_W)
    sc["qg"][...] = (q * jnp.exp(G)).astype(BF16)
    sc["kd"][...] = (k * jnp.exp(glast - G)).astype(BF16)
    sc["egl"][...] = jnp.exp(glast)
    v_bf = v.astype(BF16)
    sc["v"][...] = v_bf

    row_c = lax.broadcasted_iota(jnp.int32, (R, 1), 0) % C
    q_parts, k_parts = [], []
    for a in range(1, n_off + 1):
        ra = jnp.broadcast_to(G3[:, a * S - 1:a * S, :], (nc, C, KEY_W)).reshape(R, KEY_W)
        in_block = (row_c >= a * S) & (row_c < (a + 1) * S)
        q_parts.append(jnp.where(in_block, q * jnp.exp(jnp.minimum(G - ra, 0.0)), 0.0))
        k_parts.append(jnp.where(row_c < a * S, k * jnp.exp(jnp.minimum(ra - G, 0.0)), 0.0))
    sc["qcat"][...] = jnp.concatenate(q_parts, axis=1).astype(BF16)
    kcat = jnp.concatenate(k_parts, axis=1).astype(BF16)
    head_of_k = (lax.broadcasted_iota(jnp.int32, (1, n_off * KEY_W), 1) % KEY_W) // DK
    head_of_v = lax.broadcasted_iota(jnp.int32, (1, VAL_W), 1) // DV
    for h in range(N_HEADS):
        sc["kbd"][:, h * C:(h + 1) * C, :] = jnp.where(head_of_k == h, kcat, 0.0).reshape(nc, C, n_off * KEY_W)
        sc["vbd"][:, h * C:(h + 1) * C, :] = jnp.where(head_of_v == h, v_bf, 0.0).reshape(nc, C, VAL_W)

    @pl.when(l == 0)
    def _():
        sc["st"][...] = s0_ref[0]

    def chunk_body(c, carry):
        rows = pl.ds(pl.multiple_of(c * C, C), C)
        st = sc["st"][...]
        a_all = sc["a"][rows, :] + _dot_nt(sc["qcat"][rows, :], sc["kbd"][c])
        sc["o"][rows, :] = (_dot(a_all.astype(BF16), sc["vbd"][c])
                            + _dot_nt(sc["qg"][rows, :], st.astype(BF16)))
        upd = _dot_tn(sc["v"][rows, :], sc["kd"][rows, :])
        sc["st"][...] = st * sc["egl"][pl.ds(c * C, 1), :] + upd * bdm_ref[...]
        return carry

    lax.fori_loop(0, nc, chunk_body, 0)

    @pl.when(l == n_l - 1)
    def _():
        st_out_ref[0] = sc["st"][...]


def _gla_single_chunk(cfg, q, k, v, G, s0_ref, st_out_ref, sc):
    R, S, n_off, W = _mixer_dims(cfg)
    C = cfg.chunk
    G3 = G.reshape(R // C, C, KEY_W)
    glast = jnp.broadcast_to(G3[:, C - 1:C, :], (R // C, C, KEY_W)).reshape(R, KEY_W)
    sc["qg"][...] = q * jnp.exp(G)
    sc["kd"][...] = k * jnp.exp(glast - G)
    sc["egl"][...] = jnp.exp(glast)
    sc["v"][...] = v
    sc["st"][...] = s0_ref[...]

    a_bf = sc["a"][...].astype(BF16)
    v_bf = v.astype(BF16)
    for h in range(N_HEADS):
        sc["o"][:, h * DV:(h + 1) * DV] = _dot(a_bf[:, h * W:(h + 1) * W], v_bf[:, h * DV:(h + 1) * DV])

    kpad = max(C, 16)

    def seq_body(c, carry):
        cs = pl.multiple_of(c * C, C)
        rows = pl.ds(cs, C)
        for h in range(N_HEADS):
            kc = slice(h * DK, (h + 1) * DK)
            vc = slice(h * DV, (h + 1) * DV)
            st = sc["st"][c, h]
            sc["o"][rows, vc] = sc["o"][rows, vc] + _dot_nt(sc["qg"][rows, kc].astype(BF16), st.astype(BF16))
            v_h = sc["v"][rows, vc]
            kd_h = sc["kd"][rows, kc]
            if kpad > C:
                v_h = jnp.concatenate([v_h, jnp.zeros((kpad - C, DV), F32)], axis=0)
                kd_h = jnp.concatenate([kd_h, jnp.zeros((kpad - C, DK), F32)], axis=0)
            sc["st"][c, h] = st * sc["egl"][pl.ds(cs, 1), kc] + _dot_tn(v_h.astype(BF16), kd_h.astype(BF16))
        return carry

    lax.fori_loop(0, R // C, seq_body, 0)
    st_out_ref[...] = sc["st"][...]


def _mixer_kernel(cfg, x_ref, s0_ref, buf0_ref, n1_ref, wmain_ref, wgkh_ref, wgkl_ref, bgk_ref,
                  gn_ref, wpool_ref, pscale_ref, wo_ref, n2_ref, wrh_ref, wrl_ref, br_ref,
                  tri_ref, hb_ref, bdm_ref, *rest):
    rest = rest[cfg.n_alias:]
    x1_ref, h_ref, topi_ref, gates_ref, st_out_ref, bufo_ref = rest[:6]
    sc = dict(zip([name for name, _, _ in _scratch_spec(cfg)], rest[6:]))

    ns, tl, C = cfg.ns, cfg.tl, cfg.chunk
    R, S, n_off, W = _mixer_dims(cfg)
    l = pl.program_id(1)
    n_l = pl.num_programs(1)
    ext_s = sc["ext"]

    x = x_ref[...].reshape(R, D_MODEL)
    xn = _rms(x, n1_ref[...]).astype(BF16)
    p = _dot(xn, wmain_ref[...])
    q = p[:, 0:KEY_W] * (DK ** -0.5)
    k = p[:, KEY_W:2 * KEY_W]
    v = p[:, 2 * KEY_W:2 * KEY_W + VAL_W]
    og = p[:, 2 * KEY_W + VAL_W:2 * KEY_W + 2 * VAL_W]
    u = p[:, 2 * KEY_W + 2 * VAL_W:2 * KEY_W + 2 * VAL_W + POOL_W]
    glr = p[:, MAIN_COLS - LANES:MAIN_COLS]
    ext_s[:, HALO:HALO + tl, :] = u.reshape(ns, tl, POOL_W)

    z = _dot3(glr, wgkh_ref[...], wgkl_ref[...]) + bgk_ref[...]
    g = -(jnp.maximum(-z, 0.0) + jnp.log1p(jnp.exp(-jnp.abs(z)))) / GATE_NORMALIZER
    tri = tri_ref[...]
    g_hi = g.astype(BF16)
    g_r = g - g_hi.astype(F32)
    g_mid = g_r.astype(BF16)
    g_lo = (g_r - g_mid.astype(F32)).astype(BF16)
    G = _dot(tri, g_hi) + _dot(tri, g_mid) + _dot(tri, g_lo)

    sc["kh"][0:HALO, :] = jnp.zeros((HALO, KEY_W), F32)
    sc["gh"][0:HALO, :] = jnp.zeros((HALO, KEY_W), F32)
    sc["kh"][HALO:HALO + R, :] = k
    sc["gh"][HALO:HALO + R, :] = G
    row = lax.broadcasted_iota(jnp.int32, (R, 1), 0)
    row_s = row % S
    row_w = row % W
    col_w = lax.broadcasted_iota(jnp.int32, (1, N_HEADS * W), 1) % W
    head_bcast = hb_ref[...]
    a_all = jnp.zeros((R, N_HEADS * W), F32)
    for d in range(S):
        k_sh = sc["kh"][HALO - d:HALO - d + R, :]
        g_sh = sc["gh"][HALO - d:HALO - d + R, :]
        e = jnp.exp(jnp.where(row_s >= d, G - g_sh, NEG_BIG))
        term = (q * k_sh * e).astype(BF16)
        spread = _dot(term, head_bcast)
        a_all = jnp.where(col_w == row_w - d, spread, a_all)
    sc["a"][...] = a_all

    if n_off:
        _gla_chunked(cfg, l, q, k, v, G, s0_ref, st_out_ref, bdm_ref, sc)
    else:
        _gla_single_chunk(cfg, q, k, v, G, s0_ref, st_out_ref, sc)

    o = sc["o"][...]
    gn = gn_ref[...]
    o_heads = []
    for h in range(N_HEADS):
        vc = slice(h * DV, (h + 1) * DV)
        og_h = og[:, vc]
        o_heads.append(_rms(o[:, vc], gn) * (og_h * jax.nn.sigmoid(og_h)))

    @pl.when(l == 0)
    def _():
        ext_s[:, 0:HALO - POOL_BUF, :] = jnp.zeros((ns, HALO - POOL_BUF, POOL_W), F32)
        ext_s[:, HALO - POOL_BUF:HALO, :] = buf0_ref[...]

    pos = cfg.start_pos + l * tl + lax.broadcasted_iota(jnp.int32, (1, tl, 1), 1)
    z_groups = []
    for gi, w in enumerate(POOL_WINDOWS):
        gc = slice(gi * POOL_GROUP, (gi + 1) * POOL_GROUP)
        s = ext_s[:, HALO:HALO + tl, gc]
        for dd in range(1, w):
            s = s + ext_s[:, HALO - dd:HALO - dd + tl, gc]
        cnt = jnp.minimum(w, pos + 1).astype(F32)
        dmean = (s / cnt - ext_s[:, HALO:HALO + tl, gc]).reshape(R, POOL_GROUP)
        z_groups.append(_dot(dmean.astype(BF16), wpool_ref[gi]))
    zp = jnp.concatenate(z_groups, axis=1) * pscale_ref[...]

    @pl.when(l == n_l - 1)
    def _():
        bufo_ref[...] = ext_s[:, tl + HALO - POOL_BUF:tl + HALO, :]

    @pl.when(l < n_l - 1)
    def _():
        ext_s[:, 0:HALO, :] = ext_s[:, tl:tl + HALO, :]

    cat = jnp.concatenate(o_heads + [zp], axis=1).astype(BF16)
    x1 = x + _dot(cat, wo_ref[...])
    x1_ref[...] = x1
    hn = _rms(x1, n2_ref[...])
    for c in range(CHUNKS):
        h_ref[pl.ds(c, R, stride=CHUNKS), :] = hn[:, c * LANES:(c + 1) * LANES]
    logits = _dot3(hn, wrh_ref[...], wrl_ref[...]) + br_ref[...]
    lane = lax.broadcasted_iota(jnp.int32, (R, LANES), 1)
    lg = jnp.where(lane < N_EXPERTS, logits, -jnp.inf)
    vals, idxs = [], []
    for _ in range(TOP_K):
        m = jnp.max(lg, axis=1, keepdims=True)
        idx = jnp.min(jnp.where(lg == m, lane, LANES), axis=1, keepdims=True)
        vals.append(m)
        idxs.append(idx)
        lg = jnp.where(lane == idx, -jnp.inf, lg)
    exps = [jnp.exp(vv - vals[0]) for vv in vals]
    den = exps[0] + exps[1] + exps[2] + exps[3]
    ti = jnp.zeros((R, LANES), jnp.int32)
    gt = jnp.zeros((R, LANES), F32)
    for kk in range(TOP_K):
        ti = jnp.where(lane == kk, idxs[kk], ti)
        gt = jnp.where(lane == kk, exps[kk] / den, gt)
    topi_ref[...] = ti[:, 0:TOP_K]
    gates_ref[...] = gt[:, 0:TOP_K]


def _mixer_constants(cfg):
    R, S, n_off, W = _mixer_dims(cfg)
    C = cfg.chunk
    r = jnp.arange(R)
    tri = ((r[:, None] // C == r[None, :] // C) & (r[None, :] <= r[:, None])).astype(BF16)
    head_bcast = (jnp.arange(KEY_W)[:, None] // DK == jnp.arange(N_HEADS * W)[None, :] // W).astype(BF16)
    block_diag = (jnp.arange(VAL_W)[:, None] // DV == jnp.arange(KEY_W)[None, :] // DK).astype(F32)
    return tri, head_bcast, block_diag


def _mixer_call(cfg, n_tok_all, row_block0, x, s0, buf0, weights, aliased):
    B, L, _ = x.shape
    ns, tl, C = cfg.ns, cfg.tl, cfg.chunk
    R, S, n_off, W = _mixer_dims(cfg)
    n_b, n_l = B // ns, L // tl
    assert B % ns == 0 and L % tl == 0 and tl % C == 0 and R % 8 == 0
    assert (n_l == 1 or tl >= HALO) and (n_off == 0 or ns == 1) and (n_off > 0 or tl == C)

    def const(shape):
        return pl.BlockSpec(shape, lambda b, l: (0,) * len(shape))

    state_block = (1, VAL_W, KEY_W) if n_off else (ns, N_HEADS, DV, DK)
    state_spec = pl.BlockSpec(state_block, lambda b, l: (b,) + (0,) * (len(state_block) - 1))
    consts = _mixer_constants(cfg)
    operands = (x, s0, buf0) + tuple(weights) + consts
    in_specs = [
        pl.BlockSpec((ns, tl, D_MODEL), lambda b, l: (b, l, 0)),
        state_spec,
        pl.BlockSpec((ns, POOL_BUF, POOL_W), lambda b, l: (b, 0, 0)),
    ] + [const(w.shape) for w in tuple(weights) + consts] + [pl.BlockSpec(memory_space=pl.ANY)] * len(aliased)

    def tok_block(rows, width):
        return pl.BlockSpec((rows, width), lambda b, l: (row_block0 + b * n_l + l, 0))

    out_specs = [
        tok_block(R, D_MODEL), tok_block(R * CHUNKS, LANES), tok_block(R, TOP_K), tok_block(R, TOP_K),
        state_spec,
        pl.BlockSpec((ns, POOL_BUF, POOL_W), lambda b, l: (b, 0, 0)),
    ]
    out_shape = [
        jax.ShapeDtypeStruct((n_tok_all, D_MODEL), F32),
        jax.ShapeDtypeStruct((n_tok_all * CHUNKS, LANES), F32),
        jax.ShapeDtypeStruct((n_tok_all, TOP_K), jnp.int32),
        jax.ShapeDtypeStruct((n_tok_all, TOP_K), F32),
        jax.ShapeDtypeStruct((B,) + state_block[1:], F32),
        jax.ShapeDtypeStruct((B, POOL_BUF, POOL_W), F32),
    ]
    aliases = {len(operands) + i: i for i in range(len(aliased))}
    return pl.pallas_call(
        functools.partial(_mixer_kernel, cfg),
        grid=(n_b, n_l),
        in_specs=in_specs,
        out_specs=out_specs,
        out_shape=out_shape,
        scratch_shapes=[pltpu.VMEM(shape, dtype) for _, shape, dtype in _scratch_spec(cfg)],
        input_output_aliases=aliases,
        compiler_params=pltpu.CompilerParams(
            dimension_semantics=("arbitrary", "arbitrary"), vmem_limit_bytes=VMEM_LIMIT),
        name="mixer",
    )(*operands, *aliased)


def _route_kernel(topi_ref, pos_ref, counts_ref, cnt_s, carry_s, gstart_s):
    ph = pl.program_id(0)
    i = pl.program_id(1)
    TT = topi_ref.shape[0]
    topi = topi_ref[...]
    lane = lax.broadcasted_iota(jnp.int32, (TT, LANES), 1)
    hot = jnp.zeros((TT, LANES), F32)
    for kk in range(TOP_K):
        hot = hot + (lane == topi[:, kk:kk + 1]).astype(F32)
    colsum = jnp.sum(hot, axis=0, keepdims=True)

    @pl.when((ph == 0) & (i == 0))
    def _():
        cnt_s[...] = jnp.zeros_like(cnt_s)

    @pl.when(ph == 0)
    def _():
        cnt_s[...] = cnt_s[...] + colsum

    @pl.when((ph == 1) & (i == 0))
    def _():
        cnt = cnt_s[...]
        counts_ref[...] = cnt.astype(jnp.int32)
        tiles = jnp.floor((cnt + (ROW_TILE - 1)) * (1.0 / ROW_TILE))
        tiles8 = jnp.broadcast_to(tiles, (8, LANES))
        ur = lax.broadcasted_iota(jnp.int32, (LANES, LANES), 0)
        uc = lax.broadcasted_iota(jnp.int32, (LANES, LANES), 1)
        upper = (ur < uc).astype(BF16)
        t_hi, t_lo = _split2(tiles8)
        excl = _dot(t_hi, upper) + _dot(t_lo, upper)
        gstart_s[...] = excl[0:1, :] * float(ROW_TILE)
        carry_s[...] = jnp.zeros_like(carry_s)

    @pl.when(ph == 1)
    def _():
        lr = lax.broadcasted_iota(jnp.int32, (TT, TT), 0)
        lc = lax.broadcasted_iota(jnp.int32, (TT, TT), 1)
        lower = (lc < lr).astype(BF16)
        rank = _dot(lower, hot.astype(BF16)) + carry_s[...] + gstart_s[...]
        out = jnp.zeros((TT, LANES), F32)
        for kk in range(TOP_K):
            pk = jnp.sum(jnp.where(lane == topi[:, kk:kk + 1], rank, 0.0), axis=1, keepdims=True)
            out = jnp.where(lane == kk, pk, out)
        pos_ref[...] = out[:, 0:TOP_K].astype(jnp.int32)
        carry_s[...] = carry_s[...] + colsum


def _route_call(topi):
    T = topi.shape[0]
    assert T % ROUTE_TILE == 0
    return pl.pallas_call(
        _route_kernel,
        grid=(2, T // ROUTE_TILE),
        in_specs=[pl.BlockSpec((ROUTE_TILE, TOP_K), lambda ph, i: (i, 0))],
        out_specs=[pl.BlockSpec((ROUTE_TILE, TOP_K), lambda ph, i: (i * ph, 0)),
                   pl.BlockSpec((1, LANES), lambda ph, i: (0, 0))],
        out_shape=[jax.ShapeDtypeStruct((T, TOP_K), jnp.int32),
                   jax.ShapeDtypeStruct((1, LANES), jnp.int32)],
        scratch_shapes=[pltpu.VMEM((1, LANES), F32)] * 3,
        compiler_params=pltpu.CompilerParams(dimension_semantics=("arbitrary", "arbitrary")),
        name="route",
    )(topi)


def _invperm_kernel(n_zero_steps, pos_ref, tok_ref):
    i = pl.program_id(0)

    @pl.when(i < n_zero_steps)
    def _():
        def clear(r, c):
            tok_ref[i * INV_BLOCK + r] = -1
            return c

        lax.fori_loop(0, INV_BLOCK, clear, 0, unroll=16)

    @pl.when(i >= n_zero_steps)
    def _():
        base = (i - n_zero_steps) * INV_BLOCK

        def put(n, c):
            tok_ref[pos_ref[n]] = base + n
            return c

        lax.fori_loop(0, INV_BLOCK, put, 0, unroll=16)


def _invperm_call(pos_flat, n_rows):
    n_pairs = pos_flat.shape[0]
    assert n_rows % INV_BLOCK == 0 and n_pairs % INV_BLOCK == 0
    n_zero = n_rows // INV_BLOCK
    return pl.pallas_call(
        functools.partial(_invperm_kernel, n_zero),
        grid=(n_zero + n_pairs // INV_BLOCK,),
        in_specs=[pl.BlockSpec((INV_BLOCK,), lambda i: (jnp.maximum(i - n_zero, 0),),
                               memory_space=pltpu.SMEM)],
        out_specs=pl.BlockSpec(memory_space=pltpu.SMEM),
        out_shape=jax.ShapeDtypeStruct((n_rows,), jnp.int32),
        compiler_params=pltpu.CompilerParams(dimension_semantics=("arbitrary",)),
        name="invperm",
    )(pos_flat)


def _row_copy_in(h_hbm, xb, sem, r, pair):
    tok = jnp.maximum(pair, 0) >> 2
    return pltpu.make_async_copy(h_hbm.at[pl.ds(pl.multiple_of(tok * CHUNKS, CHUNKS), CHUNKS), :],
                                 xb.at[pl.ds(r * CHUNKS, CHUNKS), :], sem)


def _gather_tile(tab_ref, h_hbm, xb, sem, tile, unrolled):
    base = tile * ROW_TILE
    if unrolled:
        for r in range(ROW_TILE):
            _row_copy_in(h_hbm, xb, sem, r, tab_ref[base + r]).start()
    else:
        def issue(r, c):
            _row_copy_in(h_hbm, xb, sem, r, tab_ref[base + r]).start()
            return c

        lax.fori_loop(0, ROW_TILE, issue, 0)


def _wait_all_rows(buf, sem):
    pltpu.make_async_copy(buf, buf, sem).wait()


def _expert_mlp(xb, yb, wgu_s, wd_s, bgu_ref, bd_ref):
    xs = jnp.concatenate([xb[pl.ds(c, ROW_TILE, stride=CHUNKS), :] for c in range(CHUNKS)],
                         axis=1).astype(BF16)
    gu = _dot(xs, wgu_s[...]) + bgu_ref[0]
    gate = jnp.minimum(gu[:, 0:D_FF], SWIGLU_LIMIT)
    up = jnp.clip(gu[:, D_FF:2 * D_FF], -SWIGLU_LIMIT, SWIGLU_LIMIT)
    act = ((up + 1.0) * gate * jax.nn.sigmoid(SWIGLU_ALPHA * gate)).astype(BF16)
    y = _dot(act, wd_s[...]) + bd_ref[0]
    for c in range(CHUNKS):
        yb[pl.ds(c, ROW_TILE, stride=CHUNKS), :] = y[:, c * LANES:(c + 1) * LANES]


def _experts_kernel(te_ref, nt_ref, tab_ref, h_hbm, wgu_ref, bgu_ref, wd_ref, bd_ref, y_hbm,
                    x0, x1, x2, y0, y1, y2, gsem, ssem, wgu_s, wd_s):
    s = pl.program_id(0)
    n_used = nt_ref[0]
    par = s % N_BUF
    tc = jnp.clip(s - COMPUTE_LAG, 0, te_ref.shape[0] - 1)
    xs, ys = (x0, x1, x2), (y0, y1, y2)
    computing = (s >= COMPUTE_LAG) & (s < n_used + COMPUTE_LAG)

    @pl.when(computing & ((s == COMPUTE_LAG) | (te_ref[tc] != te_ref[jnp.maximum(tc - 1, 0)])))
    def _():
        wgu_s[...] = wgu_ref[0].astype(BF16)
        wd_s[...] = wd_ref[0].astype(BF16)

    def write_out(c):
        rows = pl.ds(pl.multiple_of(tc * (ROW_TILE * CHUNKS), ROW_TILE * CHUNKS), ROW_TILE * CHUNKS)
        pltpu.make_async_copy(ys[c], y_hbm.at[rows, :], ssem.at[c]).start(priority=WRITE_DMA_QUEUE)

    last = n_used + COMPUTE_LAG - 1
    steady = (s >= COMPUTE_LAG) & (s < n_used)
    edge = jnp.logical_not(steady) & (s <= last)
    for p in range(N_BUF):
        c = (p + N_BUF - COMPUTE_LAG) % N_BUF
        mine = par == p

        @pl.when(mine & (s >= COMPUTE_LAG + N_BUF) & (s <= last))
        def _(c=c):
            _wait_all_rows(ys[c], ssem.at[c])

        @pl.when(mine & steady)
        def _(p=p, c=c):
            _wait_all_rows(xs[c], gsem.at[c])
            _gather_tile(tab_ref, h_hbm, xs[p], gsem.at[p], s, True)
            _expert_mlp(xs[c], ys[c], wgu_s, wd_s, bgu_ref, bd_ref)
            write_out(c)

        @pl.when(mine & edge & computing)
        def _(c=c):
            _wait_all_rows(xs[c], gsem.at[c])

        @pl.when(mine & edge & (s < n_used))
        def _(p=p):
            _gather_tile(tab_ref, h_hbm, xs[p], gsem.at[p], s, False)

        @pl.when(mine & edge & computing)
        def _(c=c):
            _expert_mlp(xs[c], ys[c], wgu_s, wd_s, bgu_ref, bd_ref)
            write_out(c)

    @pl.when(s == last)
    def _():
        for c in range(N_BUF):
            _wait_all_rows(ys[c], ssem.at[c])


def _experts_call(tile_e, n_used, pair_of_row, h_rows, w_gu, b_gu, w_down, b_down):
    n_tiles = tile_e.shape[0]

    def expert_block(s, te, nt, tab):
        return (te[jnp.clip(s - COMPUTE_LAG, 0, n_tiles - 1)], 0, 0)

    row_buffer = pltpu.VMEM((ROW_TILE * CHUNKS, LANES), F32)
    grid_spec = pltpu.PrefetchScalarGridSpec(
        num_scalar_prefetch=3,
        grid=(n_tiles + COMPUTE_LAG,),
        in_specs=[
            pl.BlockSpec(memory_space=pl.ANY),
            pl.BlockSpec((1, D_MODEL, 2 * D_FF), expert_block),
            pl.BlockSpec((1, 1, 2 * D_FF), expert_block),
            pl.BlockSpec((1, D_FF, D_MODEL), expert_block),
            pl.BlockSpec((1, 1, D_MODEL), expert_block),
        ],
        out_specs=pl.BlockSpec(memory_space=pl.ANY),
        scratch_shapes=[row_buffer] * N_BUF + [row_buffer] * N_BUF + [
            pltpu.SemaphoreType.DMA((N_BUF,)),
            pltpu.SemaphoreType.DMA((N_BUF,)),
            pltpu.VMEM((D_MODEL, 2 * D_FF), BF16),
            pltpu.VMEM((D_FF, D_MODEL), BF16),
        ],
    )
    return pl.pallas_call(
        _experts_kernel,
        grid_spec=grid_spec,
        out_shape=jax.ShapeDtypeStruct((n_tiles * ROW_TILE * CHUNKS, LANES), F32),
        compiler_params=pltpu.CompilerParams(
            dimension_semantics=("arbitrary",), vmem_limit_bytes=VMEM_LIMIT),
        name="experts",
    )(tile_e, n_used, pair_of_row, h_rows, w_gu, b_gu.reshape(N_EXPERTS, 1, 2 * D_FF),
      w_down, b_down.reshape(N_EXPERTS, 1, D_MODEL))


def _gather_pairs(pos_ref, y_hbm, gb, sem, tile, unrolled):
    base = tile * (TOK_TILE * TOP_K)

    def copy(n):
        row = pos_ref[base + n]
        return pltpu.make_async_copy(y_hbm.at[pl.ds(pl.multiple_of(row * CHUNKS, CHUNKS), CHUNKS), :],
                                     gb.at[pl.ds(n * CHUNKS, CHUNKS), :], sem)

    if unrolled:
        for n in range(TOK_TILE * TOP_K):
            copy(n).start()
    else:
        def issue(n, c):
            copy(n).start()
            return c

        lax.fori_loop(0, TOK_TILE * TOP_K, issue, 0)


def _combine(gb, x1_ref, gates_ref, fn_ref, out_ref):
    gates = gates_ref[...]
    cols = []
    for c in range(CHUNKS):
        acc = gb[pl.ds(c, TOK_TILE, stride=TOP_K * CHUNKS), :] * gates[:, 0:1]
        for kk in range(1, TOP_K):
            acc = acc + gb[pl.ds(kk * CHUNKS + c, TOK_TILE, stride=TOP_K * CHUNKS), :] * gates[:, kk:kk + 1]
        cols.append(acc)
    out_ref[...] = _rms(x1_ref[...] + jnp.concatenate(cols, axis=1), fn_ref[...])


def _final_kernel(tile0, pos_ref, y_hbm, x1_ref, gates_ref, fn_ref, out_ref, g0, g1, g2, sem):
    s = pl.program_id(0)
    n = pl.num_programs(0) - COMPUTE_LAG
    par = s % N_BUF
    gs = (g0, g1, g2)
    steady = (s >= COMPUTE_LAG) & (s < n)
    for p in range(N_BUF):
        c = (p + N_BUF - COMPUTE_LAG) % N_BUF
        mine = par == p

        @pl.when(mine & steady)
        def _(p=p, c=c):
            _wait_all_rows(gs[c], sem.at[c])
            _gather_pairs(pos_ref, y_hbm, gs[p], sem.at[p], tile0 + s, True)
            _combine(gs[c], x1_ref, gates_ref, fn_ref, out_ref)

        @pl.when(mine & (s < COMPUTE_LAG))
        def _(p=p):
            _gather_pairs(pos_ref, y_hbm, gs[p], sem.at[p], tile0 + s, False)

        @pl.when(mine & (s >= n))
        def _(c=c):
            _wait_all_rows(gs[c], sem.at[c])
            _combine(gs[c], x1_ref, gates_ref, fn_ref, out_ref)


def _final_call(tok0, n_tok, pos_flat, y_rows, x1_all, gates_all, final_norm):
    assert tok0 % TOK_TILE == 0 and n_tok % TOK_TILE == 0 and n_tok // TOK_TILE >= COMPUTE_LAG
    tile0 = tok0 // TOK_TILE

    def tok_block(s, pos):
        return (tile0 + jnp.maximum(s - COMPUTE_LAG, 0), 0)

    pair_buffer = pltpu.VMEM((TOK_TILE * TOP_K * CHUNKS, LANES), F32)
    grid_spec = pltpu.PrefetchScalarGridSpec(
        num_scalar_prefetch=1,
        grid=(n_tok // TOK_TILE + COMPUTE_LAG,),
        in_specs=[
            pl.BlockSpec(memory_space=pl.ANY),
            pl.BlockSpec((TOK_TILE, D_MODEL), tok_block),
            pl.BlockSpec((TOK_TILE, TOP_K), tok_block),
            pl.BlockSpec((1, D_MODEL), lambda s, pos: (0, 0)),
        ],
        out_specs=pl.BlockSpec((TOK_TILE, D_MODEL), lambda s, pos: (jnp.maximum(s - COMPUTE_LAG, 0), 0)),
        scratch_shapes=[pair_buffer] * N_BUF + [pltpu.SemaphoreType.DMA((N_BUF,))],
    )
    return pl.pallas_call(
        functools.partial(_final_kernel, tile0),
        grid_spec=grid_spec,
        out_shape=jax.ShapeDtypeStruct((n_tok, D_MODEL), F32),
        compiler_params=pltpu.CompilerParams(
            dimension_semantics=("arbitrary",), vmem_limit_bytes=VMEM_LIMIT),
        name="final",
    )(pos_flat, y_rows, x1_all, gates_all, final_norm)


def _pick_tile(n, target):
    t = min(n, target)
    while n % t:
        t -= 1
    return t


def kernel(x_prompt, x_sample, state_gla, state_pool, norm1, w_in, w_gk2, b_gk, gla_norm, w_pool,
           pool_scale, w_o, norm2, w_router, b_router, w_gate_up, b_gate_up, w_down, b_down, final_norm):
    depth = w_in.shape[0]
    assert depth == 1
    B, L, _ = x_prompt.shape
    BS, LS, _ = x_sample.shape
    n_p, n_s = B * L, BS * LS
    n_all = n_p + n_s

    wi = w_in[0]
    o_glr = 2 * KEY_W + 2 * VAL_W
    wmain = jnp.concatenate(
        [wi[:, 0:o_glr], wi[:, o_glr + GATE_RANK:], wi[:, o_glr:o_glr + GATE_RANK],
         jnp.zeros((D_MODEL, LANES - GATE_RANK), F32)], axis=1).astype(BF16)
    wgk = jnp.concatenate([w_gk2[0], jnp.zeros((LANES - GATE_RANK, KEY_W), F32)], axis=0)
    wgkh, wgkl = _split2(wgk)
    wr = jnp.concatenate([w_router[0], jnp.zeros((D_MODEL, LANES - N_EXPERTS), F32)], axis=1)
    wrh, wrl = _split2(wr)
    br = jnp.concatenate([b_router[0], jnp.zeros((LANES - N_EXPERTS,), F32)]).reshape(1, LANES)
    weights = (norm1[0].reshape(1, D_MODEL), wmain, wgkh, wgkl, b_gk[0].reshape(1, KEY_W),
               gla_norm[0].reshape(1, DV), w_pool[0].astype(BF16), pool_scale[0].reshape(1, POOL_W),
               w_o[0].astype(BF16), norm2[0].reshape(1, D_MODEL), wrh, wrl, br)

    assert L % GLA_CHUNK == 0 and LS in (8, 16)
    cfg_p = MixerCfg(ns=1, tl=_pick_tile(L, 256), chunk=GLA_CHUNK, start_pos=0, n_alias=0)
    s0_p = jnp.zeros((B, VAL_W, KEY_W), F32)
    buf0_p = jnp.zeros((B, POOL_BUF, POOL_W), F32)
    x1_all, h_all, topi_all, gates_all, st_p, buf_p = _mixer_call(
        cfg_p, n_all, 0, x_prompt, s0_p, buf0_p, weights, ())

    cfg_s = MixerCfg(ns=_pick_tile(BS, 128 // LS), tl=LS, chunk=LS, start_pos=PAST_LEN, n_alias=4)
    r_s = cfg_s.ns * cfg_s.tl
    assert n_p % r_s == 0
    x1_all, h_all, topi_all, gates_all, st_s, buf_s = _mixer_call(
        cfg_s, n_all, n_p // r_s, x_sample, jnp.swapaxes(state_gla[0], -1, -2), state_pool[0], weights,
        (x1_all, h_all, topi_all, gates_all))

    pos, counts = _route_call(topi_all)
    n_tiles = (n_all * TOP_K + N_EXPERTS * (ROW_TILE - 1)) // ROW_TILE
    tiles_per_e = (counts[0, :N_EXPERTS] + (ROW_TILE - 1)) // ROW_TILE
    ends = jnp.cumsum(tiles_per_e)
    n_used = ends[-1].astype(jnp.int32)
    tile_ids = jnp.minimum(jnp.arange(n_tiles, dtype=jnp.int32), n_used - 1)
    tile_e = jnp.sum(tile_ids[:, None] >= ends[None, :], axis=1).astype(jnp.int32)
    pos_flat = pos.reshape(n_all * TOP_K)
    pair_of_row = _invperm_call(pos_flat, -(-n_tiles * ROW_TILE // INV_BLOCK) * INV_BLOCK)

    y_rows = _experts_call(tile_e, n_used.reshape(1), pair_of_row, h_all,
                           w_gate_up[0], b_gate_up[0], w_down[0], b_down[0])

    fn = final_norm.reshape(1, D_MODEL)
    y_p = _final_call(0, n_p, pos_flat, y_rows, x1_all, gates_all, fn)
    y_s = _final_call(n_p, n_s, pos_flat, y_rows, x1_all, gates_all, fn)

    st_p = jnp.stack([st_p[:, h * DV:(h + 1) * DV, h * DK:(h + 1) * DK] for h in range(N_HEADS)], axis=1)
    return (y_p.reshape(B, L, D_MODEL), y_s.reshape(BS, LS, D_MODEL),
            jnp.swapaxes(st_p, -1, -2)[None], buf_p[None],
            jnp.swapaxes(st_s, -1, -2)[None], buf_s[None])
```

```python
import functools
from typing import NamedTuple

import jax
import jax.numpy as jnp
from jax import lax
from jax.experimental import pallas as pl
from jax.experimental.pallas import tpu as pltpu

F32 = jnp.float32
BF16 = jnp.bfloat16

D_MODEL = 1024
N_HEADS = 4
DK = 64
DV = 128
KEY_W = N_HEADS * DK
VAL_W = N_HEADS * DV
GATE_RANK = 16
GATE_NORMALIZER = 16.0
GLA_CHUNK = 64
POOL_WINDOWS = (2, 4, 8, 16)
POOL_W = 512
POOL_GROUP = 128
POOL_BUF = 15
N_EXPERTS = 32
TOP_K = 4
D_FF = 1024
SWIGLU_LIMIT = 7.0
SWIGLU_ALPHA = 1.702
EPS = 1e-5
PAST_LEN = 16384

LANES = 128
CHUNKS = D_MODEL // LANES
HALO = 16
DIAG = 16
MAIN_COLS = 2 * KEY_W + 2 * VAL_W + POOL_W + LANES
ROW_TILE = 256
TOK_TILE = 128
ROUTE_TILE = 512
INV_BLOCK = 2048
WRITE_DMA_QUEUE = 1
N_BUF = 3
COMPUTE_LAG = 2
VMEM_LIMIT = 56 * 1024 * 1024


def _dot(a, b):
    return jnp.dot(a, b, preferred_element_type=F32)


def _dot_nt(a, b):
    return lax.dot_general(a, b, (((1,), (1,)), ((), ())), preferred_element_type=F32)


def _dot_tn(a, b):
    return lax.dot_general(a, b, (((0,), (0,)), ((), ())), preferred_element_type=F32)


def _split2(a):
    hi = a.astype(BF16)
    lo = (a - hi.astype(F32)).astype(BF16)
    return hi, lo


def _dot3(a, b_hi, b_lo):
    a_hi, a_lo = _split2(a)
    return _dot(a_hi, b_hi) + _dot(a_lo, b_hi) + _dot(a_hi, b_lo)


def _rms(x, w):
    return x * lax.rsqrt(jnp.mean(x * x, axis=-1, keepdims=True) + EPS) * w


class MixerCfg(NamedTuple):
    ns: int
    tl: int
    chunk: int
    start_pos: int
    n_alias: int


def _mixer_dims(cfg):
    rows = cfg.ns * cfg.tl
    diag = min(cfg.chunk, DIAG)
    n_off = cfg.chunk // diag - 1
    width = cfg.chunk if n_off else rows
    return rows, diag, n_off, width


def _scratch_spec(cfg):
    R, S, n_off, W = _mixer_dims(cfg)
    C, ns, tl = cfg.chunk, cfg.ns, cfg.tl
    spec = [
        ("ext", (ns, tl + HALO, POOL_W), F32),
        ("kh", (R + HALO, KEY_W), F32),
        ("gh", (R + HALO, KEY_W), F32),
        ("egl", (R, KEY_W), F32),
        ("a", (R, N_HEADS * W), F32),
        ("o", (R, VAL_W), F32),
    ]
    if n_off:
        nc = R // C
        spec += [
            ("st", (VAL_W, KEY_W), F32),
            ("qg", (R, KEY_W), BF16),
            ("kd", (R, KEY_W), BF16),
            ("v", (R, VAL_W), BF16),
            ("qcat", (R, n_off * KEY_W), BF16),
            ("kbd", (nc, N_HEADS * C, n_off * KEY_W), BF16),
            ("vbd", (nc, N_HEADS * C, VAL_W), BF16),
        ]
    else:
        spec += [
            ("st", (ns, N_HEADS, DV, DK), F32),
            ("qg", (R, KEY_W), F32),
            ("kd", (R, KEY_W), F32),
            ("v", (R, VAL_W), F32),
        ]
    return spec


def _gla_chunked(cfg, l, q, k, v, G, s0_ref, st_out_ref, bdm_ref, sc):
    R, S, n_off, W = _mixer_dims(cfg)
    C = cfg.chunk
    nc = R // C
    n_l = pl.num_programs(1)
    G3 = G.reshape(nc, C, KEY_W)
    glast = jnp.broadcast_to(G3[:, C - 1:C, :], (nc, C, KEY_W)).reshape(R, KEY_W)
    sc["qg"][...] = (q * jnp.exp(G)).astype(BF16)
    sc["kd"][...] = (k * jnp.exp(glast - G)).astype(BF16)
    sc["egl"][...] = jnp.exp(glast)
    v_bf = v.astype(BF16)
    sc["v"][...] = v_bf

    row_c = lax.broadcasted_iota(jnp.int32, (R, 1), 0) % C
    q_parts, k_parts = [], []
    for a in range(1, n_off + 1):
        ra = jnp.broadcast_to(G3[:, a * S - 1:a * S, :], (nc, C, KEY_W)).reshape(R, KEY_W)
        in_block = (row_c >= a * S) & (row_c < (a + 1) * S)
        q_parts.append(jnp.where(in_block, q * jnp.exp(jnp.minimum(G - ra, 0.0)), 0.0))
        k_parts.append(jnp.where(row_c < a * S, k * jnp.exp(jnp.minimum(ra - G, 0.0)), 0.0))
    sc["qcat"][...] = jnp.concatenate(q_parts, axis=1).astype(BF16)
    kcat = jnp.concatenate(k_parts, axis=1).astype(BF16)
    head_of_k = (lax.broadcasted_iota(jnp.int32, (1, n_off * KEY_W), 1) % KEY_W) // DK
    head_of_v = lax.broadcasted_iota(jnp.int32, (1, VAL_W), 1) // DV
    for h in range(N_HEADS):
        sc["kbd"][:, h * C:(h + 1) * C, :] = jnp.where(head_of_k == h, kcat, 0.0).reshape(nc, C, n_off * KEY_W)
        sc["vbd"][:, h * C:(h + 1) * C, :] = jnp.where(head_of_v == h, v_bf, 0.0).reshape(nc, C, VAL_W)

    @pl.when(l == 0)
    def _():
        sc["st"][...] = s0_ref[0]

    def chunk_body(c, carry):
        rows = pl.ds(pl.multiple_of(c * C, C), C)
        st = sc["st"][...]
        a_all = sc["a"][rows, :] + _dot_nt(sc["qcat"][rows, :], sc["kbd"][c])
        sc["o"][rows, :] = (_dot(a_all.astype(BF16), sc["vbd"][c])
                            + _dot_nt(sc["qg"][rows, :], st.astype(BF16)))
        upd = _dot_tn(sc["v"][rows, :], sc["kd"][rows, :])
        sc["st"][...] = st * sc["egl"][pl.ds(c * C, 1), :] + upd * bdm_ref[...]
        return carry

    lax.fori_loop(0, nc, chunk_body, 0)

    @pl.when(l == n_l - 1)
    def _():
        st_out_ref[0] = sc["st"][...]


def _gla_single_chunk(cfg, q, k, v, G, s0_ref, st_out_ref, sc):
    R, S, n_off, W = _mixer_dims(cfg)
    C = cfg.chunk
    G3 = G.reshape(R // C, C, KEY_W)
    glast = jnp.broadcast_to(G3[:, C - 1:C, :], (R // C, C, KEY_W)).reshape(R, KEY_W)
    sc["qg"][...] = q * jnp.exp(G)
    sc["kd"][...] = k * jnp.exp(glast - G)
    sc["egl"][...] = jnp.exp(glast)
    sc["v"][...] = v
    sc["st"][...] = s0_ref[...]

    a_bf = sc["a"][...].astype(BF16)
    v_bf = v.astype(BF16)
    for h in range(N_HEADS):
        sc["o"][:, h * DV:(h + 1) * DV] = _dot(a_bf[:, h * W:(h + 1) * W], v_bf[:, h * DV:(h + 1) * DV])

    kpad = max(C, 16)

    def seq_body(c, carry):
        cs = pl.multiple_of(c * C, C)
        rows = pl.ds(cs, C)
        for h in range(N_HEADS):
            kc = slice(h * DK, (h + 1) * DK)
            vc = slice(h * DV, (h + 1) * DV)
            st = sc["st"][c, h]
            sc["o"][rows, vc] = sc["o"][rows, vc] + _dot_nt(sc["qg"][rows, kc].astype(BF16), st.astype(BF16))
            v_h = sc["v"][rows, vc]
            kd_h = sc["kd"][rows, kc]
            if kpad > C:
                v_h = jnp.concatenate([v_h, jnp.zeros((kpad - C, DV), F32)], axis=0)
                kd_h = jnp.concatenate([kd_h, jnp.zeros((kpad - C, DK), F32)], axis=0)
            sc["st"][c, h] = st * sc["egl"][pl.ds(cs, 1), kc] + _dot_tn(v_h.astype(BF16), kd_h.astype(BF16))
        return carry

    lax.fori_loop(0, R // C, seq_body, 0)
    st_out_ref[...] = sc["st"][...]


def _mixer_kernel(cfg, x_ref, s0_ref, buf0_ref, n1_ref, wmain_ref, wgkh_ref, wgkl_ref, bgk_ref,
                  gn_ref, wpool_ref, pscale_ref, wo_ref, n2_ref, wrh_ref, wrl_ref, br_ref,
                  tri_ref, hb_ref, bdm_ref, *rest):
    rest = rest[cfg.n_alias:]
    x1_ref, h_ref, topi_ref, gates_ref, st_out_ref, bufo_ref = rest[:6]
    sc = dict(zip([name for name, _, _ in _scratch_spec(cfg)], rest[6:]))

    ns, tl, C = cfg.ns, cfg.tl, cfg.chunk
    R, S, n_off, W = _mixer_dims(cfg)
    l = pl.program_id(1)
    n_l = pl.num_programs(1)
    ext_s = sc["ext"]

    x = x_ref[...].reshape(R, D_MODEL)
    xn = _rms(x, n1_ref[...]).astype(BF16)
    p = _dot(xn, wmain_ref[...])
    q = p[:, 0:KEY_W] * (DK ** -0.5)
    k = p[:, KEY_W:2 * KEY_W]
    v = p[:, 2 * KEY_W:2 * KEY_W + VAL_W]
    og = p[:, 2 * KEY_W + VAL_W:2 * KEY_W + 2 * VAL_W]
    u = p[:, 2 * KEY_W + 2 * VAL_W:2 * KEY_W + 2 * VAL_W + POOL_W]
    glr = p[:, MAIN_COLS - LANES:MAIN_COLS]
    ext_s[:, HALO:HALO + tl, :] = u.reshape(ns, tl, POOL_W)

    z = _dot3(glr, wgkh_ref[...], wgkl_ref[...]) + bgk_ref[...]
    g = -(jnp.maximum(-z, 0.0) + jnp.log(1.0 + jnp.exp(-jnp.abs(z)))) / GATE_NORMALIZER
    tri = tri_ref[...]
    g_hi = g.astype(BF16)
    g_r = g - g_hi.astype(F32)
    g_mid = g_r.astype(BF16)
    g_lo = (g_r - g_mid.astype(F32)).astype(BF16)
    G = _dot(tri, g_hi) + _dot(tri, g_mid) + _dot(tri, g_lo)

    sc["kh"][0:HALO, :] = jnp.zeros((HALO, KEY_W), F32)
    sc["gh"][0:HALO, :] = jnp.zeros((HALO, KEY_W), F32)
    sc["kh"][HALO:HALO + R, :] = k
    sc["gh"][HALO:HALO + R, :] = G
    row = lax.broadcasted_iota(jnp.int32, (R, 1), 0)
    row_s = row % S
    row_w = row % W
    col_w = lax.broadcasted_iota(jnp.int32, (1, N_HEADS * W), 1) % W
    lag = jnp.where((col_w <= row_w) & (row_w - col_w <= row_s), row_w - col_w, -1)
    head_bcast = hb_ref[...]
    a_all = jnp.zeros((R, N_HEADS * W), F32)
    for d in range(S):
        k_sh = sc["kh"][HALO - d:HALO - d + R, :]
        g_sh = sc["gh"][HALO - d:HALO - d + R, :]
        e = jnp.exp(jnp.minimum(G - g_sh, 0.0))
        term = (q * k_sh * e).astype(BF16)
        spread = _dot(term, head_bcast)
        a_all = jnp.where(lag == d, spread, a_all)
    sc["a"][...] = a_all

    if n_off:
        _gla_chunked(cfg, l, q, k, v, G, s0_ref, st_out_ref, bdm_ref, sc)
    else:
        _gla_single_chunk(cfg, q, k, v, G, s0_ref, st_out_ref, sc)

    o = sc["o"][...]
    gn = gn_ref[...]
    o_heads = []
    for h in range(N_HEADS):
        vc = slice(h * DV, (h + 1) * DV)
        og_h = og[:, vc]
        o_heads.append(_rms(o[:, vc], gn) * (og_h * jax.nn.sigmoid(og_h)))

    @pl.when(l == 0)
    def _():
        ext_s[:, 0:HALO - POOL_BUF, :] = jnp.zeros((ns, HALO - POOL_BUF, POOL_W), F32)
        ext_s[:, HALO - POOL_BUF:HALO, :] = buf0_ref[...]

    pos = cfg.start_pos + l * tl + lax.broadcasted_iota(jnp.int32, (1, tl, 1), 1)
    z_groups = []
    for gi, w in enumerate(POOL_WINDOWS):
        gc = slice(gi * POOL_GROUP, (gi + 1) * POOL_GROUP)
        s = ext_s[:, HALO:HALO + tl, gc]
        for dd in range(1, w):
            s = s + ext_s[:, HALO - dd:HALO - dd + tl, gc]
        cnt = jnp.minimum(w, pos + 1).astype(F32)
        dmean = (s / cnt - ext_s[:, HALO:HALO + tl, gc]).reshape(R, POOL_GROUP)
        z_groups.append(_dot(dmean.astype(BF16), wpool_ref[gi]))
    zp = jnp.concatenate(z_groups, axis=1) * pscale_ref[...]

    @pl.when(l == n_l - 1)
    def _():
        bufo_ref[...] = ext_s[:, tl + HALO - POOL_BUF:tl + HALO, :]

    @pl.when(l < n_l - 1)
    def _():
        ext_s[:, 0:HALO, :] = ext_s[:, tl:tl + HALO, :]

    cat = jnp.concatenate(o_heads + [zp], axis=1).astype(BF16)
    x1 = x + _dot(cat, wo_ref[...])
    x1_ref[...] = x1
    hn = _rms(x1, n2_ref[...])
    for c in range(CHUNKS):
        h_ref[pl.ds(c, R, stride=CHUNKS), :] = hn[:, c * LANES:(c + 1) * LANES]
    logits = _dot3(hn, wrh_ref[...], wrl_ref[...]) + br_ref[...]
    lane = lax.broadcasted_iota(jnp.int32, (R, LANES), 1)
    lg = jnp.where(lane < N_EXPERTS, logits, -jnp.inf)
    vals, idxs = [], []
    for _ in range(TOP_K):
        m = jnp.max(lg, axis=1, keepdims=True)
        idx = jnp.min(jnp.where(lg == m, lane, LANES), axis=1, keepdims=True)
        vals.append(m)
        idxs.append(idx)
        lg = jnp.where(lane == idx, -jnp.inf, lg)
    exps = [jnp.exp(vv - vals[0]) for vv in vals]
    den = exps[0] + exps[1] + exps[2] + exps[3]
    ti = jnp.zeros((R, LANES), jnp.int32)
    gt = jnp.zeros((R, LANES), F32)
    for kk in range(TOP_K):
        ti = jnp.where(lane == kk, idxs[kk], ti)
        gt = jnp.where(lane == kk, exps[kk] / den, gt)
    topi_ref[...] = ti[:, 0:TOP_K]
    gates_ref[...] = gt[:, 0:TOP_K]


def _mixer_constants(cfg):
    R, S, n_off, W = _mixer_dims(cfg)
    C = cfg.chunk
    r = jnp.arange(R)
    tri = ((r[:, None] // C == r[None, :] // C) & (r[None, :] <= r[:, None])).astype(BF16)
    head_bcast = (jnp.arange(KEY_W)[:, None] // DK == jnp.arange(N_HEADS * W)[None, :] // W).astype(BF16)
    block_diag = (jnp.arange(VAL_W)[:, None] // DV == jnp.arange(KEY_W)[None, :] // DK).astype(F32)
    return tri, head_bcast, block_diag


def _mixer_call(cfg, n_tok_all, row_block0, x, s0, buf0, weights, aliased):
    B, L, _ = x.shape
    ns, tl, C = cfg.ns, cfg.tl, cfg.chunk
    R, S, n_off, W = _mixer_dims(cfg)
    n_b, n_l = B // ns, L // tl
    assert B % ns == 0 and L % tl == 0 and tl % C == 0 and R % 8 == 0
    assert (n_l == 1 or tl >= HALO) and (n_off == 0 or ns == 1) and (n_off > 0 or tl == C)

    def const(shape):
        return pl.BlockSpec(shape, lambda b, l: (0,) * len(shape))

    state_block = (1, VAL_W, KEY_W) if n_off else (ns, N_HEADS, DV, DK)
    state_spec = pl.BlockSpec(state_block, lambda b, l: (b,) + (0,) * (len(state_block) - 1))
    consts = _mixer_constants(cfg)
    operands = (x, s0, buf0) + tuple(weights) + consts
    in_specs = [
        pl.BlockSpec((ns, tl, D_MODEL), lambda b, l: (b, l, 0)),
        state_spec,
        pl.BlockSpec((ns, POOL_BUF, POOL_W), lambda b, l: (b, 0, 0)),
    ] + [const(w.shape) for w in tuple(weights) + consts] + [pl.BlockSpec(memory_space=pl.ANY)] * len(aliased)

    def tok_block(rows, width):
        return pl.BlockSpec((rows, width), lambda b, l: (row_block0 + b * n_l + l, 0))

    out_specs = [
        tok_block(R, D_MODEL), tok_block(R * CHUNKS, LANES), tok_block(R, TOP_K), tok_block(R, TOP_K),
        state_spec,
        pl.BlockSpec((ns, POOL_BUF, POOL_W), lambda b, l: (b, 0, 0)),
    ]
    out_shape = [
        jax.ShapeDtypeStruct((n_tok_all, D_MODEL), F32),
        jax.ShapeDtypeStruct((n_tok_all * CHUNKS, LANES), F32),
        jax.ShapeDtypeStruct((n_tok_all, TOP_K), jnp.int32),
        jax.ShapeDtypeStruct((n_tok_all, TOP_K), F32),
        jax.ShapeDtypeStruct((B,) + state_block[1:], F32),
        jax.ShapeDtypeStruct((B, POOL_BUF, POOL_W), F32),
    ]
    aliases = {len(operands) + i: i for i in range(len(aliased))}
    return pl.pallas_call(
        functools.partial(_mixer_kernel, cfg),
        grid=(n_b, n_l),
        in_specs=in_specs,
        out_specs=out_specs,
        out_shape=out_shape,
        scratch_shapes=[pltpu.VMEM(shape, dtype) for _, shape, dtype in _scratch_spec(cfg)],
        input_output_aliases=aliases,
        compiler_params=pltpu.CompilerParams(
            dimension_semantics=("arbitrary", "arbitrary"), vmem_limit_bytes=VMEM_LIMIT),
        name="mixer",
    )(*operands, *aliased)


def _route_kernel(topi_ref, pos_ref, counts_ref, cnt_s, carry_s, gstart_s):
    ph = pl.program_id(0)
    i = pl.program_id(1)
    TT = topi_ref.shape[0]
    topi = topi_ref[...]
    lane = lax.broadcasted_iota(jnp.int32, (TT, LANES), 1)
    hot = jnp.zeros((TT, LANES), F32)
    for kk in range(TOP_K):
        hot = hot + (lane == topi[:, kk:kk + 1]).astype(F32)
    colsum = jnp.sum(hot, axis=0, keepdims=True)

    @pl.when((ph == 0) & (i == 0))
    def _():
        cnt_s[...] = jnp.zeros_like(cnt_s)

    @pl.when(ph == 0)
    def _():
        cnt_s[...] = cnt_s[...] + colsum

    @pl.when((ph == 1) & (i == 0))
    def _():
        cnt = cnt_s[...]
        counts_ref[...] = cnt.astype(jnp.int32)
        tiles = jnp.floor((cnt + (ROW_TILE - 1)) * (1.0 / ROW_TILE))
        tiles8 = jnp.broadcast_to(tiles, (8, LANES))
        ur = lax.broadcasted_iota(jnp.int32, (LANES, LANES), 0)
        uc = lax.broadcasted_iota(jnp.int32, (LANES, LANES), 1)
        upper = (ur < uc).astype(BF16)
        t_hi, t_lo = _split2(tiles8)
        excl = _dot(t_hi, upper) + _dot(t_lo, upper)
        gstart_s[...] = excl[0:1, :] * float(ROW_TILE)
        carry_s[...] = jnp.zeros_like(carry_s)

    @pl.when(ph == 1)
    def _():
        lr = lax.broadcasted_iota(jnp.int32, (TT, TT), 0)
        lc = lax.broadcasted_iota(jnp.int32, (TT, TT), 1)
        lower = (lc < lr).astype(BF16)
        rank = _dot(lower, hot.astype(BF16)) + carry_s[...] + gstart_s[...]
        out = jnp.zeros((TT, LANES), F32)
        for kk in range(TOP_K):
            pk = jnp.sum(jnp.where(lane == topi[:, kk:kk + 1], rank, 0.0), axis=1, keepdims=True)
            out = jnp.where(lane == kk, pk, out)
        pos_ref[...] = out[:, 0:TOP_K].astype(jnp.int32)
        carry_s[...] = carry_s[...] + colsum


def _route_call(topi):
    T = topi.shape[0]
    assert T % ROUTE_TILE == 0
    return pl.pallas_call(
        _route_kernel,
        grid=(2, T // ROUTE_TILE),
        in_specs=[pl.BlockSpec((ROUTE_TILE, TOP_K), lambda ph, i: (i, 0))],
        out_specs=[pl.BlockSpec((ROUTE_TILE, TOP_K), lambda ph, i: (i * ph, 0)),
                   pl.BlockSpec((1, LANES), lambda ph, i: (0, 0))],
        out_shape=[jax.ShapeDtypeStruct((T, TOP_K), jnp.int32),
                   jax.ShapeDtypeStruct((1, LANES), jnp.int32)],
        scratch_shapes=[pltpu.VMEM((1, LANES), F32)] * 3,
        compiler_params=pltpu.CompilerParams(dimension_semantics=("arbitrary", "arbitrary")),
        name="route",
    )(topi)


def _invperm_kernel(pad_lo_ref, pad_hi_ref, pos_ref, tok_ref):
    i = pl.program_id(0)

    @pl.when(i == 0)
    def _():
        def clear(r, c):
            tok_ref[r] = -1
            return c

        for e in range(N_EXPERTS):
            lax.fori_loop(pad_lo_ref[e], pad_hi_ref[e], clear, 0)

    @pl.when(i > 0)
    def _():
        base = (i - 1) * INV_BLOCK

        def put(n, c):
            tok_ref[pos_ref[n]] = base + n
            return c

        lax.fori_loop(0, INV_BLOCK, put, 0, unroll=16)


def _invperm_call(pad_lo, pad_hi, pos_flat, n_rows):
    n_pairs = pos_flat.shape[0]
    assert n_pairs % INV_BLOCK == 0
    grid_spec = pltpu.PrefetchScalarGridSpec(
        num_scalar_prefetch=2,
        grid=(1 + n_pairs // INV_BLOCK,),
        in_specs=[pl.BlockSpec((INV_BLOCK,), lambda i, lo, hi: (jnp.maximum(i - 1, 0),),
                               memory_space=pltpu.SMEM)],
        out_specs=pl.BlockSpec(memory_space=pltpu.SMEM),
    )
    return pl.pallas_call(
        _invperm_kernel,
        grid_spec=grid_spec,
        out_shape=jax.ShapeDtypeStruct((n_rows,), jnp.int32),
        compiler_params=pltpu.CompilerParams(dimension_semantics=("arbitrary",)),
        name="invperm",
    )(pad_lo, pad_hi, pos_flat)


def _row_copy_in(h_hbm, xb, sem, r, pair):
    tok = jnp.maximum(pair, 0) >> 2
    return pltpu.make_async_copy(h_hbm.at[pl.ds(pl.multiple_of(tok * CHUNKS, CHUNKS), CHUNKS), :],
                                 xb.at[pl.ds(r * CHUNKS, CHUNKS), :], sem)


def _gather_tile(tab_ref, h_hbm, xb, sem, tile, unrolled):
    base = tile * ROW_TILE
    if unrolled:
        for r in range(ROW_TILE):
            _row_copy_in(h_hbm, xb, sem, r, tab_ref[base + r]).start()
    else:
        def issue(r, c):
            _row_copy_in(h_hbm, xb, sem, r, tab_ref[base + r]).start()
            return c

        lax.fori_loop(0, ROW_TILE, issue, 0)


def _wait_all_rows(buf, sem):
    pltpu.make_async_copy(buf, buf, sem).wait()


def _expert_mlp(xb, yb, wgu_s, wd_s, bgu_ref, bd_ref):
    xs = jnp.concatenate([xb[pl.ds(c, ROW_TILE, stride=CHUNKS), :] for c in range(CHUNKS)],
                         axis=1).astype(BF16)
    gu = _dot(xs, wgu_s[...]) + bgu_ref[0]
    gate = jnp.minimum(gu[:, 0:D_FF], SWIGLU_LIMIT)
    up = jnp.clip(gu[:, D_FF:2 * D_FF], -SWIGLU_LIMIT, SWIGLU_LIMIT)
    act = ((up + 1.0) * gate * jax.nn.sigmoid(SWIGLU_ALPHA * gate)).astype(BF16)
    y = _dot(act, wd_s[...]) + bd_ref[0]
    for c in range(CHUNKS):
        yb[pl.ds(c, ROW_TILE, stride=CHUNKS), :] = y[:, c * LANES:(c + 1) * LANES]


def _experts_kernel(te_ref, nt_ref, tab_ref, h_hbm, wgu_ref, bgu_ref, wd_ref, bd_ref, y_hbm,
                    x0, x1, x2, y0, y1, y2, gsem, ssem, wgu_s, wd_s):
    s = pl.program_id(0)
    n_used = nt_ref[0]
    par = s % N_BUF
    tc = jnp.clip(s - COMPUTE_LAG, 0, te_ref.shape[0] - 1)
    xs, ys = (x0, x1, x2), (y0, y1, y2)
    computing = (s >= COMPUTE_LAG) & (s < n_used + COMPUTE_LAG)

    @pl.when(computing & ((s == COMPUTE_LAG) | (te_ref[tc] != te_ref[jnp.maximum(tc - 1, 0)])))
    def _():
        wgu_s[...] = wgu_ref[0].astype(BF16)
        wd_s[...] = wd_ref[0].astype(BF16)

    def write_out(c):
        rows = pl.ds(pl.multiple_of(tc * (ROW_TILE * CHUNKS), ROW_TILE * CHUNKS), ROW_TILE * CHUNKS)
        pltpu.make_async_copy(ys[c], y_hbm.at[rows, :], ssem.at[c]).start(priority=WRITE_DMA_QUEUE)

    last = n_used + COMPUTE_LAG - 1
    steady = (s >= COMPUTE_LAG) & (s < n_used)
    edge = jnp.logical_not(steady) & (s <= last)
    for p in range(N_BUF):
        c = (p + N_BUF - COMPUTE_LAG) % N_BUF
        mine = par == p

        @pl.when(mine & (s >= COMPUTE_LAG + N_BUF) & (s <= last))
        def _(c=c):
            _wait_all_rows(ys[c], ssem.at[c])

        @pl.when(mine & steady)
        def _(p=p, c=c):
            _wait_all_rows(xs[c], gsem.at[c])
            _gather_tile(tab_ref, h_hbm, xs[p], gsem.at[p], s, True)
            _expert_mlp(xs[c], ys[c], wgu_s, wd_s, bgu_ref, bd_ref)
            write_out(c)

        @pl.when(mine & edge & computing)
        def _(c=c):
            _wait_all_rows(xs[c], gsem.at[c])

        @pl.when(mine & edge & (s < n_used))
        def _(p=p):
            _gather_tile(tab_ref, h_hbm, xs[p], gsem.at[p], s, False)

        @pl.when(mine & edge & computing)
        def _(c=c):
            _expert_mlp(xs[c], ys[c], wgu_s, wd_s, bgu_ref, bd_ref)
            write_out(c)

    @pl.when(s == last)
    def _():
        for c in range(N_BUF):
            _wait_all_rows(ys[c], ssem.at[c])


def _experts_call(tile_e, n_used, pair_of_row, h_rows, w_gu, b_gu, w_down, b_down):
    n_tiles = tile_e.shape[0]

    def expert_block(s, te, nt, tab):
        return (te[jnp.clip(s - COMPUTE_LAG, 0, n_tiles - 1)], 0, 0)

    row_buffer = pltpu.VMEM((ROW_TILE * CHUNKS, LANES), F32)
    grid_spec = pltpu.PrefetchScalarGridSpec(
        num_scalar_prefetch=3,
        grid=(n_tiles + COMPUTE_LAG,),
        in_specs=[
            pl.BlockSpec(memory_space=pl.ANY),
            pl.BlockSpec((1, D_MODEL, 2 * D_FF), expert_block),
            pl.BlockSpec((1, 1, 2 * D_FF), expert_block),
            pl.BlockSpec((1, D_FF, D_MODEL), expert_block),
            pl.BlockSpec((1, 1, D_MODEL), expert_block),
        ],
        out_specs=pl.BlockSpec(memory_space=pl.ANY),
        scratch_shapes=[row_buffer] * N_BUF + [row_buffer] * N_BUF + [
            pltpu.SemaphoreType.DMA((N_BUF,)),
            pltpu.SemaphoreType.DMA((N_BUF,)),
            pltpu.VMEM((D_MODEL, 2 * D_FF), BF16),
            pltpu.VMEM((D_FF, D_MODEL), BF16),
        ],
    )
    return pl.pallas_call(
        _experts_kernel,
        grid_spec=grid_spec,
        out_shape=jax.ShapeDtypeStruct((n_tiles * ROW_TILE * CHUNKS, LANES), F32),
        compiler_params=pltpu.CompilerParams(
            dimension_semantics=("arbitrary",), vmem_limit_bytes=VMEM_LIMIT),
        name="experts",
    )(tile_e, n_used, pair_of_row, h_rows, w_gu, b_gu.reshape(N_EXPERTS, 1, 2 * D_FF),
      w_down, b_down.reshape(N_EXPERTS, 1, D_MODEL))


def _gather_pairs(pos_ref, y_hbm, gb, sem, tile, unrolled):
    base = tile * (TOK_TILE * TOP_K)

    def copy(n):
        row = pos_ref[base + n]
        dst = ((n % TOP_K) * TOK_TILE + n // TOP_K) * CHUNKS
        if not isinstance(dst, int):
            dst = pl.multiple_of(dst, CHUNKS)
        return pltpu.make_async_copy(y_hbm.at[pl.ds(pl.multiple_of(row * CHUNKS, CHUNKS), CHUNKS), :],
                                     gb.at[pl.ds(dst, CHUNKS), :], sem)

    if unrolled:
        for n in range(TOK_TILE * TOP_K):
            copy(n).start()
    else:
        def issue(n, c):
            copy(n).start()
            return c

        lax.fori_loop(0, TOK_TILE * TOP_K, issue, 0)


def _combine(gb, x1_ref, gates_ref, fn_ref, out_ref):
    gates = gates_ref[...]
    cols = []
    for c in range(CHUNKS):
        acc = gb[pl.ds(c, TOK_TILE, stride=CHUNKS), :] * gates[:, 0:1]
        for kk in range(1, TOP_K):
            acc = acc + gb[pl.ds(kk * TOK_TILE * CHUNKS + c, TOK_TILE, stride=CHUNKS), :] * gates[:, kk:kk + 1]
        cols.append(acc)
    out_ref[...] = _rms(x1_ref[...] + jnp.concatenate(cols, axis=1), fn_ref[...])


def _final_kernel(tile0, pos_ref, y_hbm, x1_ref, gates_ref, fn_ref, out_ref, g0, g1, g2, sem):
    s = pl.program_id(0)
    n = pl.num_programs(0) - COMPUTE_LAG
    par = s % N_BUF
    gs = (g0, g1, g2)
    steady = (s >= COMPUTE_LAG) & (s < n)
    for p in range(N_BUF):
        c = (p + N_BUF - COMPUTE_LAG) % N_BUF
        mine = par == p

        @pl.when(mine & steady)
        def _(p=p, c=c):
            _wait_all_rows(gs[c], sem.at[c])
            _gather_pairs(pos_ref, y_hbm, gs[p], sem.at[p], tile0 + s, True)
            _combine(gs[c], x1_ref, gates_ref, fn_ref, out_ref)

        @pl.when(mine & (s < COMPUTE_LAG))
        def _(p=p):
            _gather_pairs(pos_ref, y_hbm, gs[p], sem.at[p], tile0 + s, False)

        @pl.when(mine & (s >= n))
        def _(c=c):
            _wait_all_rows(gs[c], sem.at[c])
            _combine(gs[c], x1_ref, gates_ref, fn_ref, out_ref)


def _final_call(tok0, n_tok, pos_flat, y_rows, x1_all, gates_all, final_norm):
    assert tok0 % TOK_TILE == 0 and n_tok % TOK_TILE == 0 and n_tok // TOK_TILE >= COMPUTE_LAG
    tile0 = tok0 // TOK_TILE

    def tok_block(s, pos):
        return (tile0 + jnp.maximum(s - COMPUTE_LAG, 0), 0)

    pair_buffer = pltpu.VMEM((TOK_TILE * TOP_K * CHUNKS, LANES), F32)
    grid_spec = pltpu.PrefetchScalarGridSpec(
        num_scalar_prefetch=1,
        grid=(n_tok // TOK_TILE + COMPUTE_LAG,),
        in_specs=[
            pl.BlockSpec(memory_space=pl.ANY),
            pl.BlockSpec((TOK_TILE, D_MODEL), tok_block),
            pl.BlockSpec((TOK_TILE, TOP_K), tok_block),
            pl.BlockSpec((1, D_MODEL), lambda s, pos: (0, 0)),
        ],
        out_specs=pl.BlockSpec((TOK_TILE, D_MODEL), lambda s, pos: (jnp.maximum(s - COMPUTE_LAG, 0), 0)),
        scratch_shapes=[pair_buffer] * N_BUF + [pltpu.SemaphoreType.DMA((N_BUF,))],
    )
    return pl.pallas_call(
        functools.partial(_final_kernel, tile0),
        grid_spec=grid_spec,
        out_shape=jax.ShapeDtypeStruct((n_tok, D_MODEL), F32),
        compiler_params=pltpu.CompilerParams(
            dimension_semantics=("arbitrary",), vmem_limit_bytes=VMEM_LIMIT),
        name="final",
    )(pos_flat, y_rows, x1_all, gates_all, final_norm)


def _pick_tile(n, target):
    t = min(n, target)
    while n % t:
        t -= 1
    return t


def kernel(x_prompt, x_sample, state_gla, state_pool, norm1, w_in, w_gk2, b_gk, gla_norm, w_pool,
           pool_scale, w_o, norm2, w_router, b_router, w_gate_up, b_gate_up, w_down, b_down, final_norm):
    depth = w_in.shape[0]
    assert depth == 1
    B, L, _ = x_prompt.shape
    BS, LS, _ = x_sample.shape
    n_p, n_s = B * L, BS * LS
    n_all = n_p + n_s

    wi = w_in[0]
    o_glr = 2 * KEY_W + 2 * VAL_W
    wmain = jnp.concatenate(
        [wi[:, 0:o_glr], wi[:, o_glr + GATE_RANK:], wi[:, o_glr:o_glr + GATE_RANK],
         jnp.zeros((D_MODEL, LANES - GATE_RANK), F32)], axis=1).astype(BF16)
    wgk = jnp.concatenate([w_gk2[0], jnp.zeros((LANES - GATE_RANK, KEY_W), F32)], axis=0)
    wgkh, wgkl = _split2(wgk)
    wr = jnp.concatenate([w_router[0], jnp.zeros((D_MODEL, LANES - N_EXPERTS), F32)], axis=1)
    wrh, wrl = _split2(wr)
    br = jnp.concatenate([b_router[0], jnp.zeros((LANES - N_EXPERTS,), F32)]).reshape(1, LANES)
    weights = (norm1[0].reshape(1, D_MODEL), wmain, wgkh, wgkl, b_gk[0].reshape(1, KEY_W),
               gla_norm[0].reshape(1, DV), w_pool[0].astype(BF16), pool_scale[0].reshape(1, POOL_W),
               w_o[0].astype(BF16), norm2[0].reshape(1, D_MODEL), wrh, wrl, br)

    assert L % GLA_CHUNK == 0 and LS in (8, 16)
    cfg_p = MixerCfg(ns=1, tl=_pick_tile(L, 256), chunk=GLA_CHUNK, start_pos=0, n_alias=0)
    s0_p = jnp.zeros((B, VAL_W, KEY_W), F32)
    buf0_p = jnp.zeros((B, POOL_BUF, POOL_W), F32)
    x1_all, h_all, topi_all, gates_all, st_p, buf_p = _mixer_call(
        cfg_p, n_all, 0, x_prompt, s0_p, buf0_p, weights, ())

    cfg_s = MixerCfg(ns=_pick_tile(BS, 128 // LS), tl=LS, chunk=LS, start_pos=PAST_LEN, n_alias=4)
    r_s = cfg_s.ns * cfg_s.tl
    assert n_p % r_s == 0
    x1_all, h_all, topi_all, gates_all, st_s, buf_s = _mixer_call(
        cfg_s, n_all, n_p // r_s, x_sample, jnp.swapaxes(state_gla[0], -1, -2), state_pool[0], weights,
        (x1_all, h_all, topi_all, gates_all))

    pos, counts = _route_call(topi_all)
    n_tiles = (n_all * TOP_K + N_EXPERTS * (ROW_TILE - 1)) // ROW_TILE
    tiles_per_e = (counts[0, :N_EXPERTS] + (ROW_TILE - 1)) // ROW_TILE
    ends = jnp.cumsum(tiles_per_e)
    n_used = ends[-1].astype(jnp.int32)
    tile_ids = jnp.minimum(jnp.arange(n_tiles, dtype=jnp.int32), n_used - 1)
    tile_e = jnp.sum(tile_ids[:, None] >= ends[None, :], axis=1).astype(jnp.int32)
    pos_flat = pos.reshape(n_all * TOP_K)
    pad_hi = (ends * ROW_TILE).astype(jnp.int32)
    pad_lo = pad_hi - (tiles_per_e * ROW_TILE - counts[0, :N_EXPERTS]).astype(jnp.int32)
    pair_of_row = _invperm_call(pad_lo, pad_hi, pos_flat, n_tiles * ROW_TILE)

    y_rows = _experts_call(tile_e, n_used.reshape(1), pair_of_row, h_all,
                           w_gate_up[0], b_gate_up[0], w_down[0], b_down[0])

    fn = final_norm.reshape(1, D_MODEL)
    y_p = _final_call(0, n_p, pos_flat, y_rows, x1_all, gates_all, fn)
    y_s = _final_call(n_p, n_s, pos_flat, y_rows, x1_all, gates_all, fn)

    st_p = jnp.stack([st_p[:, h * DV:(h + 1) * DV, h * DK:(h + 1) * DK] for h in range(N_HEADS)], axis=1)
    return (y_p.reshape(B, L, D_MODEL), y_s.reshape(BS, LS, D_MODEL),
            jnp.swapaxes(st_p, -1, -2)[None], buf_p[None],
            jnp.swapaxes(st_s, -1, -2)[None], buf_s[None])
```

```python
import functools
from typing import NamedTuple

import jax
import jax.numpy as jnp
from jax import lax
from jax.experimental import pallas as pl
from jax.experimental.pallas import tpu as pltpu

F32 = jnp.float32
BF16 = jnp.bfloat16

D_MODEL = 1024
N_HEADS = 4
DK = 64
DV = 128
KEY_W = N_HEADS * DK
VAL_W = N_HEADS * DV
GATE_RANK = 16
GATE_NORMALIZER = 16.0
GLA_CHUNK = 64
POOL_WINDOWS = (2, 4, 8, 16)
POOL_W = 512
POOL_GROUP = 128
POOL_BUF = 15
N_EXPERTS = 32
TOP_K = 4
D_FF = 1024
SWIGLU_LIMIT = 7.0
SWIGLU_ALPHA = 1.702
EPS = 1e-5
PAST_LEN = 16384

LANES = 128
CHUNKS = D_MODEL // LANES
HALO = 16
DIAG = 16
MAIN_COLS = 2 * KEY_W + 2 * VAL_W + POOL_W + LANES
ROW_TILE = 256
TOK_TILE = 128
ROUTE_TILE = 512
INV_BLOCK = 2048
N_DMA_QUEUES = 2
WRITE_DMA_QUEUE = 1
N_BUF = 3
COMPUTE_LAG = 2
VMEM_LIMIT = 56 * 1024 * 1024
MASKED_EXPONENT = -1e30


def _dot(a, b):
    return jnp.dot(a, b, preferred_element_type=F32)


def _dot_nt(a, b):
    return lax.dot_general(a, b, (((1,), (1,)), ((), ())), preferred_element_type=F32)


def _dot_tn(a, b):
    return lax.dot_general(a, b, (((0,), (0,)), ((), ())), preferred_element_type=F32)


def _split2(a):
    hi = a.astype(BF16)
    lo = (a - hi.astype(F32)).astype(BF16)
    return hi, lo


def _dot3(a, b_hi, b_lo):
    a_hi, a_lo = _split2(a)
    return _dot(a_hi, b_hi) + _dot(a_lo, b_hi) + _dot(a_hi, b_lo)


def _rms(x, w):
    return x * lax.rsqrt(jnp.mean(x * x, axis=-1, keepdims=True) + EPS) * w


class MixerCfg(NamedTuple):
    ns: int
    tl: int
    chunk: int
    start_pos: int
    n_alias: int


def _mixer_dims(cfg):
    rows = cfg.ns * cfg.tl
    diag = min(cfg.chunk, DIAG)
    n_off = cfg.chunk // diag - 1
    width = cfg.chunk if n_off else rows
    return rows, diag, n_off, width


def _scratch_spec(cfg):
    R, S, n_off, W = _mixer_dims(cfg)
    C, ns, tl = cfg.chunk, cfg.ns, cfg.tl
    spec = [
        ("ext", (ns, tl + HALO, POOL_W), F32),
        ("kh", (R + HALO, KEY_W), F32),
        ("gh", (R + HALO, KEY_W), F32),
        ("egl", (R, KEY_W), F32),
        ("a", (R, N_HEADS * W), F32),
        ("o", (R, VAL_W), F32),
    ]
    if n_off:
        nc = R // C
        spec += [
            ("st", (VAL_W, KEY_W), F32),
            ("qg", (R, KEY_W), BF16),
            ("kd", (R, KEY_W), BF16),
            ("v", (R, VAL_W), BF16),
            ("qcat", (R, n_off * KEY_W), BF16),
            ("kbd", (nc, N_HEADS * C, n_off * KEY_W), BF16),
            ("vbd", (nc, N_HEADS * C, VAL_W), BF16),
        ]
    else:
        spec += [
            ("st", (ns, N_HEADS, DV, DK), F32),
            ("qg", (R, KEY_W), F32),
            ("kd", (R, KEY_W), F32),
            ("v", (R, VAL_W), F32),
        ]
    return spec


def _gla_chunked(cfg, l, q, k, v, G, s0_ref, st_out_ref, bdm_ref, sc):
    R, S, n_off, W = _mixer_dims(cfg)
    C = cfg.chunk
    nc = R // C
    n_l = pl.num_programs(1)
    G3 = G.reshape(nc, C, KEY_W)
    glast = jnp.broadcast_to(G3[:, C - 1:C, :], (nc, C, KEY_W)).reshape(R, KEY_W)
    sc["qg"][...] = (q * jnp.exp(G)).astype(BF16)
    sc["kd"][...] = (k * jnp.exp(glast - G)).astype(BF16)
    sc["egl"][...] = jnp.exp(glast)
    v_bf = v.astype(BF16)
    sc["v"][...] = v_bf

    row_c = lax.broadcasted_iota(jnp.int32, (R, 1), 0) % C
    q_parts, k_parts = [], []
    for a in range(1, n_off + 1):
        ra = jnp.broadcast_to(G3[:, a * S - 1:a * S, :], (nc, C, KEY_W)).reshape(R, KEY_W)
        in_block = (row_c >= a * S) & (row_c < (a + 1) * S)
        q_parts.append(jnp.where(in_block, q * jnp.exp(jnp.minimum(G - ra, 0.0)), 0.0))
        k_parts.append(jnp.where(row_c < a * S, k * jnp.exp(jnp.minimum(ra - G, 0.0)), 0.0))
    sc["qcat"][...] = jnp.concatenate(q_parts, axis=1).astype(BF16)
    kcat = jnp.concatenate(k_parts, axis=1).astype(BF16)
    head_of_k = (lax.broadcasted_iota(jnp.int32, (1, n_off * KEY_W), 1) % KEY_W) // DK
    head_of_v = lax.broadcasted_iota(jnp.int32, (1, VAL_W), 1) // DV
    for h in range(N_HEADS):
        sc["kbd"][:, h * C:(h + 1) * C, :] = jnp.where(head_of_k == h, kcat, 0.0).reshape(nc, C, n_off * KEY_W)
        sc["vbd"][:, h * C:(h + 1) * C, :] = jnp.where(head_of_v == h, v_bf, 0.0).reshape(nc, C, VAL_W)

    @pl.when(l == 0)
    def _():
        sc["st"][...] = s0_ref[0]

    def chunk_body(c, carry):
        rows = pl.ds(pl.multiple_of(c * C, C), C)
        st = sc["st"][...]
        a_all = sc["a"][rows, :] + _dot_nt(sc["qcat"][rows, :], sc["kbd"][c])
        sc["o"][rows, :] = (_dot(a_all.astype(BF16), sc["vbd"][c])
                            + _dot_nt(sc["qg"][rows, :], st.astype(BF16)))
        upd = _dot_tn(sc["v"][rows, :], sc["kd"][rows, :])
        sc["st"][...] = st * sc["egl"][pl.ds(c * C, 1), :] + upd * bdm_ref[...]
        return carry

    lax.fori_loop(0, nc, chunk_body, 0)

    @pl.when(l == n_l - 1)
    def _():
        st_out_ref[0] = sc["st"][...]


def _gla_single_chunk(cfg, q, k, v, G, s0_ref, st_out_ref, sc):
    R, S, n_off, W = _mixer_dims(cfg)
    C = cfg.chunk
    G3 = G.reshape(R // C, C, KEY_W)
    glast = jnp.broadcast_to(G3[:, C - 1:C, :], (R // C, C, KEY_W)).reshape(R, KEY_W)
    sc["qg"][...] = q * jnp.exp(G)
    sc["kd"][...] = k * jnp.exp(glast - G)
    sc["egl"][...] = jnp.exp(glast)
    sc["v"][...] = v
    sc["st"][...] = s0_ref[...]

    a_bf = sc["a"][...].astype(BF16)
    v_bf = v.astype(BF16)
    for h in range(N_HEADS):
        sc["o"][:, h * DV:(h + 1) * DV] = _dot(a_bf[:, h * W:(h + 1) * W], v_bf[:, h * DV:(h + 1) * DV])

    kpad = max(C, 16)

    def seq_body(c, carry):
        cs = pl.multiple_of(c * C, C)
        rows = pl.ds(cs, C)
        for h in range(N_HEADS):
            kc = slice(h * DK, (h + 1) * DK)
            vc = slice(h * DV, (h + 1) * DV)
            st = sc["st"][c, h]
            sc["o"][rows, vc] = sc["o"][rows, vc] + _dot_nt(sc["qg"][rows, kc].astype(BF16), st.astype(BF16))
            v_h = sc["v"][rows, vc]
            kd_h = sc["kd"][rows, kc]
            if kpad > C:
                v_h = jnp.concatenate([v_h, jnp.zeros((kpad - C, DV), F32)], axis=0)
                kd_h = jnp.concatenate([kd_h, jnp.zeros((kpad - C, DK), F32)], axis=0)
            sc["st"][c, h] = st * sc["egl"][pl.ds(cs, 1), kc] + _dot_tn(v_h.astype(BF16), kd_h.astype(BF16))
        return carry

    lax.fori_loop(0, R // C, seq_body, 0)
    st_out_ref[...] = sc["st"][...]


def _mixer_kernel(cfg, x_ref, s0_ref, buf0_ref, n1_ref, wmain_ref, wgkh_ref, wgkl_ref, bgk_ref,
                  gn_ref, wpool_ref, pscale_ref, wo_ref, n2_ref, wrh_ref, wrl_ref, br_ref,
                  tri_ref, hb_ref, bdm_ref, *rest):
    rest = rest[cfg.n_alias:]
    x1_ref, h_ref, topi_ref, gates_ref, st_out_ref, bufo_ref = rest[:6]
    sc = dict(zip([name for name, _, _ in _scratch_spec(cfg)], rest[6:]))

    ns, tl, C = cfg.ns, cfg.tl, cfg.chunk
    R, S, n_off, W = _mixer_dims(cfg)
    l = pl.program_id(1)
    n_l = pl.num_programs(1)
    ext_s = sc["ext"]

    x = x_ref[...].reshape(R, D_MODEL)
    xn = _rms(x, n1_ref[...]).astype(BF16)
    p = _dot(xn, wmain_ref[...])
    q = p[:, 0:KEY_W] * (DK ** -0.5)
    k = p[:, KEY_W:2 * KEY_W]
    v = p[:, 2 * KEY_W:2 * KEY_W + VAL_W]
    og = p[:, 2 * KEY_W + VAL_W:2 * KEY_W + 2 * VAL_W]
    u = p[:, 2 * KEY_W + 2 * VAL_W:2 * KEY_W + 2 * VAL_W + POOL_W]
    glr = p[:, MAIN_COLS - LANES:MAIN_COLS]
    ext_s[:, HALO:HALO + tl, :] = u.reshape(ns, tl, POOL_W)

    z = _dot3(glr, wgkh_ref[...], wgkl_ref[...]) + bgk_ref[...]
    g = -(jnp.maximum(-z, 0.0) + jnp.log(1.0 + jnp.exp(-jnp.abs(z)))) / GATE_NORMALIZER
    tri = tri_ref[...]
    g_hi = g.astype(BF16)
    g_r = g - g_hi.astype(F32)
    g_mid = g_r.astype(BF16)
    g_lo = (g_r - g_mid.astype(F32)).astype(BF16)
    G = _dot(tri, g_hi) + _dot(tri, g_mid) + _dot(tri, g_lo)

    sc["kh"][0:HALO, :] = jnp.zeros((HALO, KEY_W), F32)
    sc["gh"][0:HALO, :] = jnp.zeros((HALO, KEY_W), F32)
    sc["kh"][HALO:HALO + R, :] = k
    sc["gh"][HALO:HALO + R, :] = G
    row = lax.broadcasted_iota(jnp.int32, (R, 1), 0)
    row_s = row % S
    row_w = row % W
    col_w = lax.broadcasted_iota(jnp.int32, (1, N_HEADS * W), 1) % W
    head_bcast = hb_ref[...]
    a_all = jnp.zeros((R, N_HEADS * W), F32)
    for d in range(S):
        k_sh = sc["kh"][HALO - d:HALO - d + R, :]
        g_sh = sc["gh"][HALO - d:HALO - d + R, :]
        e = jnp.exp(jnp.where(row_s >= d, G - g_sh, MASKED_EXPONENT))
        term = (q * k_sh * e).astype(BF16)
        spread = _dot(term, head_bcast)
        a_all = jnp.where(col_w == row_w - d, spread, a_all)
    sc["a"][...] = a_all

    if n_off:
        _gla_chunked(cfg, l, q, k, v, G, s0_ref, st_out_ref, bdm_ref, sc)
    else:
        _gla_single_chunk(cfg, q, k, v, G, s0_ref, st_out_ref, sc)

    o = sc["o"][...]
    gn = gn_ref[...]
    o_heads = []
    for h in range(N_HEADS):
        vc = slice(h * DV, (h + 1) * DV)
        og_h = og[:, vc]
        o_heads.append(_rms(o[:, vc], gn) * (og_h * jax.nn.sigmoid(og_h)))

    @pl.when(l == 0)
    def _():
        ext_s[:, 0:HALO - POOL_BUF, :] = jnp.zeros((ns, HALO - POOL_BUF, POOL_W), F32)
        ext_s[:, HALO - POOL_BUF:HALO, :] = buf0_ref[...]

    pos = cfg.start_pos + l * tl + lax.broadcasted_iota(jnp.int32, (1, tl, 1), 1)
    z_groups = []
    for gi, w in enumerate(POOL_WINDOWS):
        gc = slice(gi * POOL_GROUP, (gi + 1) * POOL_GROUP)
        s = ext_s[:, HALO:HALO + tl, gc]
        for dd in range(1, w):
            s = s + ext_s[:, HALO - dd:HALO - dd + tl, gc]
        cnt = jnp.minimum(w, pos + 1).astype(F32)
        dmean = (s / cnt - ext_s[:, HALO:HALO + tl, gc]).reshape(R, POOL_GROUP)
        z_groups.append(_dot(dmean.astype(BF16), wpool_ref[gi]))
    zp = jnp.concatenate(z_groups, axis=1) * pscale_ref[...]

    @pl.when(l == n_l - 1)
    def _():
        bufo_ref[...] = ext_s[:, tl + HALO - POOL_BUF:tl + HALO, :]

    @pl.when(l < n_l - 1)
    def _():
        ext_s[:, 0:HALO, :] = ext_s[:, tl:tl + HALO, :]

    cat = jnp.concatenate(o_heads + [zp], axis=1).astype(BF16)
    x1 = x + _dot(cat, wo_ref[...])
    x1_ref[...] = x1
    hn = _rms(x1, n2_ref[...])
    for c in range(CHUNKS):
        h_ref[pl.ds(c, R, stride=CHUNKS), :] = hn[:, c * LANES:(c + 1) * LANES]
    logits = _dot3(hn, wrh_ref[...], wrl_ref[...]) + br_ref[...]
    lane = lax.broadcasted_iota(jnp.int32, (R, LANES), 1)
    lg = jnp.where(lane < N_EXPERTS, logits, -jnp.inf)
    vals, idxs = [], []
    for _ in range(TOP_K):
        m = jnp.max(lg, axis=1, keepdims=True)
        idx = jnp.min(jnp.where(lg == m, lane, LANES), axis=1, keepdims=True)
        vals.append(m)
        idxs.append(idx)
        lg = jnp.where(lane == idx, -jnp.inf, lg)
    exps = [jnp.exp(vv - vals[0]) for vv in vals]
    den = exps[0] + exps[1] + exps[2] + exps[3]
    ti = jnp.zeros((R, LANES), jnp.int32)
    gt = jnp.zeros((R, LANES), F32)
    for kk in range(TOP_K):
        ti = jnp.where(lane == kk, idxs[kk], ti)
        gt = jnp.where(lane == kk, exps[kk] / den, gt)
    topi_ref[...] = ti[:, 0:TOP_K]
    gates_ref[...] = gt[:, 0:TOP_K]


def _mixer_constants(cfg):
    R, S, n_off, W = _mixer_dims(cfg)
    C = cfg.chunk
    r = jnp.arange(R)
    tri = ((r[:, None] // C == r[None, :] // C) & (r[None, :] <= r[:, None])).astype(BF16)
    head_bcast = (jnp.arange(KEY_W)[:, None] // DK == jnp.arange(N_HEADS * W)[None, :] // W).astype(BF16)
    block_diag = (jnp.arange(VAL_W)[:, None] // DV == jnp.arange(KEY_W)[None, :] // DK).astype(F32)
    return tri, head_bcast, block_diag


def _mixer_call(cfg, n_tok_all, row_block0, x, s0, buf0, weights, aliased):
    B, L, _ = x.shape
    ns, tl, C = cfg.ns, cfg.tl, cfg.chunk
    R, S, n_off, W = _mixer_dims(cfg)
    n_b, n_l = B // ns, L // tl
    assert B % ns == 0 and L % tl == 0 and tl % C == 0 and R % 8 == 0
    assert (n_l == 1 or tl >= HALO) and (n_off == 0 or ns == 1) and (n_off > 0 or tl == C)

    def const(shape):
        return pl.BlockSpec(shape, lambda b, l: (0,) * len(shape))

    state_block = (1, VAL_W, KEY_W) if n_off else (ns, N_HEADS, DV, DK)
    state_spec = pl.BlockSpec(state_block, lambda b, l: (b,) + (0,) * (len(state_block) - 1))
    consts = _mixer_constants(cfg)
    operands = (x, s0, buf0) + tuple(weights) + consts
    in_specs = [
        pl.BlockSpec((ns, tl, D_MODEL), lambda b, l: (b, l, 0)),
        state_spec,
        pl.BlockSpec((ns, POOL_BUF, POOL_W), lambda b, l: (b, 0, 0)),
    ] + [const(w.shape) for w in tuple(weights) + consts] + [pl.BlockSpec(memory_space=pl.ANY)] * len(aliased)

    def tok_block(rows, width):
        return pl.BlockSpec((rows, width), lambda b, l: (row_block0 + b * n_l + l, 0))

    out_specs = [
        tok_block(R, D_MODEL), tok_block(R * CHUNKS, LANES), tok_block(R, TOP_K), tok_block(R, TOP_K),
        state_spec,
        pl.BlockSpec((ns, POOL_BUF, POOL_W), lambda b, l: (b, 0, 0)),
    ]
    out_shape = [
        jax.ShapeDtypeStruct((n_tok_all, D_MODEL), F32),
        jax.ShapeDtypeStruct((n_tok_all * CHUNKS, LANES), F32),
        jax.ShapeDtypeStruct((n_tok_all, TOP_K), jnp.int32),
        jax.ShapeDtypeStruct((n_tok_all, TOP_K), F32),
        jax.ShapeDtypeStruct((B,) + state_block[1:], F32),
        jax.ShapeDtypeStruct((B, POOL_BUF, POOL_W), F32),
    ]
    aliases = {len(operands) + i: i for i in range(len(aliased))}
    return pl.pallas_call(
        functools.partial(_mixer_kernel, cfg),
        grid=(n_b, n_l),
        in_specs=in_specs,
        out_specs=out_specs,
        out_shape=out_shape,
        scratch_shapes=[pltpu.VMEM(shape, dtype) for _, shape, dtype in _scratch_spec(cfg)],
        input_output_aliases=aliases,
        compiler_params=pltpu.CompilerParams(
            dimension_semantics=("arbitrary", "arbitrary"), vmem_limit_bytes=VMEM_LIMIT),
        name="mixer",
    )(*operands, *aliased)


def _route_kernel(topi_ref, pos_ref, counts_ref, cnt_s, carry_s, gstart_s):
    ph = pl.program_id(0)
    i = pl.program_id(1)
    TT = topi_ref.shape[0]
    topi = topi_ref[...]
    lane = lax.broadcasted_iota(jnp.int32, (TT, LANES), 1)
    hot = jnp.zeros((TT, LANES), F32)
    for kk in range(TOP_K):
        hot = hot + (lane == topi[:, kk:kk + 1]).astype(F32)
    colsum = jnp.sum(hot, axis=0, keepdims=True)

    @pl.when((ph == 0) & (i == 0))
    def _():
        cnt_s[...] = jnp.zeros_like(cnt_s)

    @pl.when(ph == 0)
    def _():
        cnt_s[...] = cnt_s[...] + colsum

    @pl.when((ph == 1) & (i == 0))
    def _():
        cnt = cnt_s[...]
        counts_ref[...] = cnt.astype(jnp.int32)
        tiles = jnp.floor((cnt + (ROW_TILE - 1)) * (1.0 / ROW_TILE))
        tiles8 = jnp.broadcast_to(tiles, (8, LANES))
        ur = lax.broadcasted_iota(jnp.int32, (LANES, LANES), 0)
        uc = lax.broadcasted_iota(jnp.int32, (LANES, LANES), 1)
        upper = (ur < uc).astype(BF16)
        t_hi, t_lo = _split2(tiles8)
        excl = _dot(t_hi, upper) + _dot(t_lo, upper)
        gstart_s[...] = excl[0:1, :] * float(ROW_TILE)
        carry_s[...] = jnp.zeros_like(carry_s)

    @pl.when(ph == 1)
    def _():
        lr = lax.broadcasted_iota(jnp.int32, (TT, TT), 0)
        lc = lax.broadcasted_iota(jnp.int32, (TT, TT), 1)
        lower = (lc < lr).astype(BF16)
        rank = _dot(lower, hot.astype(BF16)) + carry_s[...] + gstart_s[...]
        out = jnp.zeros((TT, LANES), F32)
        for kk in range(TOP_K):
            pk = jnp.sum(jnp.where(lane == topi[:, kk:kk + 1], rank, 0.0), axis=1, keepdims=True)
            out = jnp.where(lane == kk, pk, out)
        pos_ref[...] = out[:, 0:TOP_K].astype(jnp.int32)
        carry_s[...] = carry_s[...] + colsum


def _route_call(topi):
    T = topi.shape[0]
    assert T % ROUTE_TILE == 0
    return pl.pallas_call(
        _route_kernel,
        grid=(2, T // ROUTE_TILE),
        in_specs=[pl.BlockSpec((ROUTE_TILE, TOP_K), lambda ph, i: (i, 0))],
        out_specs=[pl.BlockSpec((ROUTE_TILE, TOP_K), lambda ph, i: (i * ph, 0)),
                   pl.BlockSpec((1, LANES), lambda ph, i: (0, 0))],
        out_shape=[jax.ShapeDtypeStruct((T, TOP_K), jnp.int32),
                   jax.ShapeDtypeStruct((1, LANES), jnp.int32)],
        scratch_shapes=[pltpu.VMEM((1, LANES), F32)] * 3,
        compiler_params=pltpu.CompilerParams(dimension_semantics=("arbitrary", "arbitrary")),
        name="route",
    )(topi)


def _invperm_kernel(pad_lo_ref, pad_hi_ref, pos_ref, tok_ref):
    i = pl.program_id(0)

    @pl.when(i == 0)
    def _():
        def clear(r, c):
            tok_ref[r] = -1
            return c

        for e in range(N_EXPERTS):
            lax.fori_loop(pad_lo_ref[e], pad_hi_ref[e], clear, 0)

    @pl.when(i > 0)
    def _():
        base = (i - 1) * INV_BLOCK

        def put(n, c):
            tok_ref[pos_ref[n]] = base + n
            return c

        lax.fori_loop(0, INV_BLOCK, put, 0, unroll=16)


def _invperm_call(pad_lo, pad_hi, pos_flat, n_rows):
    n_pairs = pos_flat.shape[0]
    assert n_pairs % INV_BLOCK == 0
    grid_spec = pltpu.PrefetchScalarGridSpec(
        num_scalar_prefetch=2,
        grid=(1 + n_pairs // INV_BLOCK,),
        in_specs=[pl.BlockSpec((INV_BLOCK,), lambda i, lo, hi: (jnp.maximum(i - 1, 0),),
                               memory_space=pltpu.SMEM)],
        out_specs=pl.BlockSpec(memory_space=pltpu.SMEM),
    )
    return pl.pallas_call(
        _invperm_kernel,
        grid_spec=grid_spec,
        out_shape=jax.ShapeDtypeStruct((n_rows,), jnp.int32),
        compiler_params=pltpu.CompilerParams(dimension_semantics=("arbitrary",)),
        name="invperm",
    )(pad_lo, pad_hi, pos_flat)


def _row_copy_in(h_hbm, xb, sem, r, pair):
    tok = jnp.maximum(pair, 0) >> 2
    return pltpu.make_async_copy(h_hbm.at[pl.ds(pl.multiple_of(tok * CHUNKS, CHUNKS), CHUNKS), :],
                                 xb.at[pl.ds(r * CHUNKS, CHUNKS), :], sem)


def _gather_tile(tab_ref, h_hbm, xb, sem, tile, unrolled):
    base = tile * ROW_TILE
    if unrolled:
        for r in range(ROW_TILE):
            _row_copy_in(h_hbm, xb, sem, r, tab_ref[base + r]).start(priority=r % N_DMA_QUEUES)
    else:
        def issue(r, c):
            _row_copy_in(h_hbm, xb, sem, r, tab_ref[base + r]).start()
            return c

        lax.fori_loop(0, ROW_TILE, issue, 0)


def _wait_all_rows(buf, sem):
    pltpu.make_async_copy(buf, buf, sem).wait()


def _expert_mlp(xb, yb, wgu_s, wd_s, bgu_ref, bd_ref):
    xs = jnp.concatenate([xb[pl.ds(c, ROW_TILE, stride=CHUNKS), :] for c in range(CHUNKS)],
                         axis=1).astype(BF16)
    gu = _dot(xs, wgu_s[...]) + bgu_ref[0]
    gate = jnp.minimum(gu[:, 0:D_FF], SWIGLU_LIMIT)
    up = jnp.clip(gu[:, D_FF:2 * D_FF], -SWIGLU_LIMIT, SWIGLU_LIMIT)
    act = ((up + 1.0) * gate * jax.nn.sigmoid(SWIGLU_ALPHA * gate)).astype(BF16)
    y = _dot(act, wd_s[...]) + bd_ref[0]
    for c in range(CHUNKS):
        yb[pl.ds(c, ROW_TILE, stride=CHUNKS), :] = y[:, c * LANES:(c + 1) * LANES]


def _experts_kernel(te_ref, nt_ref, tab_ref, h_hbm, wgu_ref, bgu_ref, wd_ref, bd_ref, y_hbm,
                    x0, x1, x2, y0, y1, y2, gsem, ssem, wgu_s, wd_s):
    s = pl.program_id(0)
    n_used = nt_ref[0]
    par = s % N_BUF
    tc = jnp.clip(s - COMPUTE_LAG, 0, te_ref.shape[0] - 1)
    xs, ys = (x0, x1, x2), (y0, y1, y2)
    computing = (s >= COMPUTE_LAG) & (s < n_used + COMPUTE_LAG)

    @pl.when(computing & ((s == COMPUTE_LAG) | (te_ref[tc] != te_ref[jnp.maximum(tc - 1, 0)])))
    def _():
        wgu_s[...] = wgu_ref[0].astype(BF16)
        wd_s[...] = wd_ref[0].astype(BF16)

    def write_out(c):
        rows = pl.ds(pl.multiple_of(tc * (ROW_TILE * CHUNKS), ROW_TILE * CHUNKS), ROW_TILE * CHUNKS)
        pltpu.make_async_copy(ys[c], y_hbm.at[rows, :], ssem.at[c]).start(priority=WRITE_DMA_QUEUE)

    last = n_used + COMPUTE_LAG - 1
    steady = (s >= COMPUTE_LAG) & (s < n_used)
    edge = jnp.logical_not(steady) & (s <= last)
    for p in range(N_BUF):
        c = (p + N_BUF - COMPUTE_LAG) % N_BUF
        mine = par == p

        @pl.when(mine & (s >= COMPUTE_LAG + N_BUF) & (s <= last))
        def _(c=c):
            _wait_all_rows(ys[c], ssem.at[c])

        @pl.when(mine & steady)
        def _(p=p, c=c):
            _wait_all_rows(xs[c], gsem.at[c])
            _gather_tile(tab_ref, h_hbm, xs[p], gsem.at[p], s, True)
            _expert_mlp(xs[c], ys[c], wgu_s, wd_s, bgu_ref, bd_ref)
            write_out(c)

        @pl.when(mine & edge & computing)
        def _(c=c):
            _wait_all_rows(xs[c], gsem.at[c])

        @pl.when(mine & edge & (s < n_used))
        def _(p=p):
            _gather_tile(tab_ref, h_hbm, xs[p], gsem.at[p], s, False)

        @pl.when(mine & edge & computing)
        def _(c=c):
            _expert_mlp(xs[c], ys[c], wgu_s, wd_s, bgu_ref, bd_ref)
            write_out(c)

    @pl.when(s == last)
    def _():
        for c in range(N_BUF):
            _wait_all_rows(ys[c], ssem.at[c])


def _experts_call(tile_e, n_used, pair_of_row, h_rows, w_gu, b_gu, w_down, b_down):
    n_tiles = tile_e.shape[0]

    def expert_block(s, te, nt, tab):
        return (te[jnp.clip(s - COMPUTE_LAG, 0, n_tiles - 1)], 0, 0)

    row_buffer = pltpu.VMEM((ROW_TILE * CHUNKS, LANES), F32)
    grid_spec = pltpu.PrefetchScalarGridSpec(
        num_scalar_prefetch=3,
        grid=(n_tiles + COMPUTE_LAG,),
        in_specs=[
            pl.BlockSpec(memory_space=pl.ANY),
            pl.BlockSpec((1, D_MODEL, 2 * D_FF), expert_block),
            pl.BlockSpec((1, 1, 2 * D_FF), expert_block),
            pl.BlockSpec((1, D_FF, D_MODEL), expert_block),
            pl.BlockSpec((1, 1, D_MODEL), expert_block),
        ],
        out_specs=pl.BlockSpec(memory_space=pl.ANY),
        scratch_shapes=[row_buffer] * N_BUF + [row_buffer] * N_BUF + [
            pltpu.SemaphoreType.DMA((N_BUF,)),
            pltpu.SemaphoreType.DMA((N_BUF,)),
            pltpu.VMEM((D_MODEL, 2 * D_FF), BF16),
            pltpu.VMEM((D_FF, D_MODEL), BF16),
        ],
    )
    return pl.pallas_call(
        _experts_kernel,
        grid_spec=grid_spec,
        out_shape=jax.ShapeDtypeStruct((n_tiles * ROW_TILE * CHUNKS, LANES), F32),
        compiler_params=pltpu.CompilerParams(
            dimension_semantics=("arbitrary",), vmem_limit_bytes=VMEM_LIMIT),
        name="experts",
    )(tile_e, n_used, pair_of_row, h_rows, w_gu, b_gu.reshape(N_EXPERTS, 1, 2 * D_FF),
      w_down, b_down.reshape(N_EXPERTS, 1, D_MODEL))


def _gather_pairs(pos_ref, y_hbm, gb, sem, tile, unrolled):
    base = tile * (TOK_TILE * TOP_K)

    def copy(n):
        row = pos_ref[base + n]
        dst = ((n % TOP_K) * TOK_TILE + n // TOP_K) * CHUNKS
        if not isinstance(dst, int):
            dst = pl.multiple_of(dst, CHUNKS)
        return pltpu.make_async_copy(y_hbm.at[pl.ds(pl.multiple_of(row * CHUNKS, CHUNKS), CHUNKS), :],
                                     gb.at[pl.ds(dst, CHUNKS), :], sem)

    if unrolled:
        for n in range(TOK_TILE * TOP_K):
            copy(n).start(priority=n % N_DMA_QUEUES)
    else:
        def issue(n, c):
            copy(n).start()
            return c

        lax.fori_loop(0, TOK_TILE * TOP_K, issue, 0)


def _combine(gb, x1_ref, gates_ref, fn_ref, out_ref):
    gates = gates_ref[...]
    cols = []
    for c in range(CHUNKS):
        acc = gb[pl.ds(c, TOK_TILE, stride=CHUNKS), :] * gates[:, 0:1]
        for kk in range(1, TOP_K):
            acc = acc + gb[pl.ds(kk * TOK_TILE * CHUNKS + c, TOK_TILE, stride=CHUNKS), :] * gates[:, kk:kk + 1]
        cols.append(acc)
    out_ref[...] = _rms(x1_ref[...] + jnp.concatenate(cols, axis=1), fn_ref[...])


def _final_kernel(tile0, pos_ref, y_hbm, x1_ref, gates_ref, fn_ref, out_ref, g0, g1, g2, sem):
    s = pl.program_id(0)
    n = pl.num_programs(0) - COMPUTE_LAG
    par = s % N_BUF
    gs = (g0, g1, g2)
    steady = (s >= COMPUTE_LAG) & (s < n)
    for p in range(N_BUF):
        c = (p + N_BUF - COMPUTE_LAG) % N_BUF
        mine = par == p

        @pl.when(mine & steady)
        def _(p=p, c=c):
            _wait_all_rows(gs[c], sem.at[c])
            _gather_pairs(pos_ref, y_hbm, gs[p], sem.at[p], tile0 + s, True)
            _combine(gs[c], x1_ref, gates_ref, fn_ref, out_ref)

        @pl.when(mine & (s < COMPUTE_LAG))
        def _(p=p):
            _gather_pairs(pos_ref, y_hbm, gs[p], sem.at[p], tile0 + s, False)

        @pl.when(mine & (s >= n))
        def _(c=c):
            _wait_all_rows(gs[c], sem.at[c])
            _combine(gs[c], x1_ref, gates_ref, fn_ref, out_ref)


def _final_call(tok0, n_tok, pos_flat, y_rows, x1_all, gates_all, final_norm):
    assert tok0 % TOK_TILE == 0 and n_tok % TOK_TILE == 0 and n_tok // TOK_TILE >= COMPUTE_LAG
    tile0 = tok0 // TOK_TILE

    def tok_block(s, pos):
        return (tile0 + jnp.maximum(s - COMPUTE_LAG, 0), 0)

    pair_buffer = pltpu.VMEM((TOK_TILE * TOP_K * CHUNKS, LANES), F32)
    grid_spec = pltpu.PrefetchScalarGridSpec(
        num_scalar_prefetch=1,
        grid=(n_tok // TOK_TILE + COMPUTE_LAG,),
        in_specs=[
            pl.BlockSpec(memory_space=pl.ANY),
            pl.BlockSpec((TOK_TILE, D_MODEL), tok_block),
            pl.BlockSpec((TOK_TILE, TOP_K), tok_block),
            pl.BlockSpec((1, D_MODEL), lambda s, pos: (0, 0)),
        ],
        out_specs=pl.BlockSpec((TOK_TILE, D_MODEL), lambda s, pos: (jnp.maximum(s - COMPUTE_LAG, 0), 0)),
        scratch_shapes=[pair_buffer] * N_BUF + [pltpu.SemaphoreType.DMA((N_BUF,))],
    )
    return pl.pallas_call(
        functools.partial(_final_kernel, tile0),
        grid_spec=grid_spec,
        out_shape=jax.ShapeDtypeStruct((n_tok, D_MODEL), F32),
        compiler_params=pltpu.CompilerParams(
            dimension_semantics=("arbitrary",), vmem_limit_bytes=VMEM_LIMIT),
        name="final",
    )(pos_flat, y_rows, x1_all, gates_all, final_norm)


def _pick_tile(n, target):
    t = min(n, target)
    while n % t:
        t -= 1
    return t


def kernel(x_prompt, x_sample, state_gla, state_pool, norm1, w_in, w_gk2, b_gk, gla_norm, w_pool,
           pool_scale, w_o, norm2, w_router, b_router, w_gate_up, b_gate_up, w_down, b_down, final_norm):
    depth = w_in.shape[0]
    assert depth == 1
    B, L, _ = x_prompt.shape
    BS, LS, _ = x_sample.shape
    n_p, n_s = B * L, BS * LS
    n_all = n_p + n_s

    wi = w_in[0]
    o_glr = 2 * KEY_W + 2 * VAL_W
    wmain = jnp.concatenate(
        [wi[:, 0:o_glr], wi[:, o_glr + GATE_RANK:], wi[:, o_glr:o_glr + GATE_RANK],
         jnp.zeros((D_MODEL, LANES - GATE_RANK), F32)], axis=1).astype(BF16)
    wgk = jnp.concatenate([w_gk2[0], jnp.zeros((LANES - GATE_RANK, KEY_W), F32)], axis=0)
    wgkh, wgkl = _split2(wgk)
    wr = jnp.concatenate([w_router[0], jnp.zeros((D_MODEL, LANES - N_EXPERTS), F32)], axis=1)
    wrh, wrl = _split2(wr)
    br = jnp.concatenate([b_router[0], jnp.zeros((LANES - N_EXPERTS,), F32)]).reshape(1, LANES)
    weights = (norm1[0].reshape(1, D_MODEL), wmain, wgkh, wgkl, b_gk[0].reshape(1, KEY_W),
               gla_norm[0].reshape(1, DV), w_pool[0].astype(BF16), pool_scale[0].reshape(1, POOL_W),
               w_o[0].astype(BF16), norm2[0].reshape(1, D_MODEL), wrh, wrl, br)

    assert L % GLA_CHUNK == 0 and LS in (8, 16)
    cfg_p = MixerCfg(ns=1, tl=_pick_tile(L, 256), chunk=GLA_CHUNK, start_pos=0, n_alias=0)
    s0_p = jnp.zeros((B, VAL_W, KEY_W), F32)
    buf0_p = jnp.zeros((B, POOL_BUF, POOL_W), F32)
    x1_all, h_all, topi_all, gates_all, st_p, buf_p = _mixer_call(
        cfg_p, n_all, 0, x_prompt, s0_p, buf0_p, weights, ())

    cfg_s = MixerCfg(ns=_pick_tile(BS, 128 // LS), tl=LS, chunk=LS, start_pos=PAST_LEN, n_alias=4)
    r_s = cfg_s.ns * cfg_s.tl
    assert n_p % r_s == 0
    x1_all, h_all, topi_all, gates_all, st_s, buf_s = _mixer_call(
        cfg_s, n_all, n_p // r_s, x_sample, jnp.swapaxes(state_gla[0], -1, -2), state_pool[0], weights,
        (x1_all, h_all, topi_all, gates_all))

    pos, counts = _route_call(topi_all)
    n_tiles = (n_all * TOP_K + N_EXPERTS * (ROW_TILE - 1)) // ROW_TILE
    tiles_per_e = (counts[0, :N_EXPERTS] + (ROW_TILE - 1)) // ROW_TILE
    ends = jnp.cumsum(tiles_per_e)
    n_used = ends[-1].astype(jnp.int32)
    tile_ids = jnp.minimum(jnp.arange(n_tiles, dtype=jnp.int32), n_used - 1)
    tile_e = jnp.sum(tile_ids[:, None] >= ends[None, :], axis=1).astype(jnp.int32)
    pos_flat = pos.reshape(n_all * TOP_K)
    pad_hi = (ends * ROW_TILE).astype(jnp.int32)
    pad_lo = pad_hi - (tiles_per_e * ROW_TILE - counts[0, :N_EXPERTS]).astype(jnp.int32)
    pair_of_row = _invperm_call(pad_lo, pad_hi, pos_flat, n_tiles * ROW_TILE)

    y_rows = _experts_call(tile_e, n_used.reshape(1), pair_of_row, h_all,
                           w_gate_up[0], b_gate_up[0], w_down[0], b_down[0])

    fn = final_norm.reshape(1, D_MODEL)
    y_p = _final_call(0, n_p, pos_flat, y_rows, x1_all, gates_all, fn)
    y_s = _final_call(n_p, n_s, pos_flat, y_rows, x1_all, gates_all, fn)

    st_p = jnp.stack([st_p[:, h * DV:(h + 1) * DV, h * DK:(h + 1) * DK] for h in range(N_HEADS)], axis=1)
    return (y_p.reshape(B, L, D_MODEL), y_s.reshape(BS, LS, D_MODEL),
            jnp.swapaxes(st_p, -1, -2)[None], buf_p[None],
            jnp.swapaxes(st_s, -1, -2)[None], buf_s[None])
```

```python
import functools
from typing import NamedTuple

import jax
import jax.numpy as jnp
from jax import lax
from jax.experimental import pallas as pl
from jax.experimental.pallas import tpu as pltpu

F32 = jnp.float32
BF16 = jnp.bfloat16

D_MODEL = 1024
N_HEADS = 4
DK = 64
DV = 128
KEY_W = N_HEADS * DK
VAL_W = N_HEADS * DV
GATE_RANK = 16
GATE_NORMALIZER = 16.0
GLA_CHUNK = 64
POOL_WINDOWS = (2, 4, 8, 16)
POOL_W = 512
POOL_GROUP = 128
POOL_BUF = 15
N_EXPERTS = 32
TOP_K = 4
D_FF = 1024
SWIGLU_LIMIT = 7.0
SWIGLU_ALPHA = 1.702
EPS = 1e-5
PAST_LEN = 16384

LANES = 128
CHUNKS = D_MODEL // LANES
HALO = 16
DIAG = 16
MAIN_COLS = 2 * KEY_W + 2 * VAL_W + POOL_W + LANES
PROMPT_TILE = 512
ROW_TILE = 256
TOK_TILE = 128
ROUTE_TILE = 1024
INV_BLOCK = 2048
N_DMA_QUEUES = 2
WRITE_DMA_QUEUE = 1
N_BUF = 3
COMPUTE_LAG = 2
VMEM_LIMIT = 56 * 1024 * 1024
MASKED_EXPONENT = -1e30


def _dot(a, b):
    return jnp.dot(a, b, preferred_element_type=F32)


def _dot_nt(a, b):
    return lax.dot_general(a, b, (((1,), (1,)), ((), ())), preferred_element_type=F32)


def _dot_tn(a, b):
    return lax.dot_general(a, b, (((0,), (0,)), ((), ())), preferred_element_type=F32)


def _split2(a):
    hi = a.astype(BF16)
    lo = (a - hi.astype(F32)).astype(BF16)
    return hi, lo


def _dot3(a, b_hi, b_lo):
    a_hi, a_lo = _split2(a)
    return _dot(a_hi, b_hi) + _dot(a_lo, b_hi) + _dot(a_hi, b_lo)


def _rms(x, w):
    return x * lax.rsqrt(jnp.mean(x * x, axis=-1, keepdims=True) + EPS) * w


class MixerCfg(NamedTuple):
    ns: int
    tl: int
    chunk: int
    start_pos: int
    n_alias: int


def _mixer_dims(cfg):
    rows = cfg.ns * cfg.tl
    diag = min(cfg.chunk, DIAG)
    n_off = cfg.chunk // diag - 1
    width = cfg.chunk if n_off else rows
    return rows, diag, n_off, width


def _scratch_spec(cfg):
    R, S, n_off, W = _mixer_dims(cfg)
    C, ns, tl = cfg.chunk, cfg.ns, cfg.tl
    spec = [
        ("ext", (ns, tl + HALO, POOL_W), F32),
        ("kh", (R + HALO, KEY_W), F32),
        ("gh", (R + HALO, KEY_W), F32),
        ("egl", (R, KEY_W), F32),
        ("a", (R, N_HEADS * W), F32),
        ("o", (R, VAL_W), F32),
    ]
    if n_off:
        nc = R // C
        spec += [
            ("st", (VAL_W, KEY_W), F32),
            ("qg", (R, KEY_W), BF16),
            ("kd", (R, KEY_W), BF16),
            ("v", (R, VAL_W), BF16),
            ("qcat", (R, n_off * KEY_W), BF16),
            ("kbd", (nc, N_HEADS * C, n_off * KEY_W), BF16),
            ("vbd", (nc, N_HEADS * C, VAL_W), BF16),
        ]
    else:
        spec += [
            ("st", (ns, N_HEADS, DV, DK), F32),
            ("qg", (R, KEY_W), F32),
            ("kd", (R, KEY_W), F32),
            ("v", (R, VAL_W), F32),
        ]
    return spec


def _gla_chunked(cfg, l, q, k, v, G, s0_ref, st_out_ref, bdm_ref, sc):
    R, S, n_off, W = _mixer_dims(cfg)
    C = cfg.chunk
    nc = R // C
    n_l = pl.num_programs(1)
    G3 = G.reshape(nc, C, KEY_W)
    glast = jnp.broadcast_to(G3[:, C - 1:C, :], (nc, C, KEY_W)).reshape(R, KEY_W)
    sc["qg"][...] = (q * jnp.exp(G)).astype(BF16)
    sc["kd"][...] = (k * jnp.exp(glast - G)).astype(BF16)
    sc["egl"][...] = jnp.exp(glast)
    v_bf = v.astype(BF16)
    sc["v"][...] = v_bf

    row_c = lax.broadcasted_iota(jnp.int32, (R, 1), 0) % C
    q_parts, k_parts = [], []
    for a in range(1, n_off + 1):
        ra = jnp.broadcast_to(G3[:, a * S - 1:a * S, :], (nc, C, KEY_W)).reshape(R, KEY_W)
        in_block = (row_c >= a * S) & (row_c < (a + 1) * S)
        q_parts.append(jnp.where(in_block, q * jnp.exp(jnp.minimum(G - ra, 0.0)), 0.0))
        k_parts.append(jnp.where(row_c < a * S, k * jnp.exp(jnp.minimum(ra - G, 0.0)), 0.0))
    sc["qcat"][...] = jnp.concatenate(q_parts, axis=1).astype(BF16)
    kcat = jnp.concatenate(k_parts, axis=1).astype(BF16)
    head_of_k = (lax.broadcasted_iota(jnp.int32, (1, n_off * KEY_W), 1) % KEY_W) // DK
    head_of_v = lax.broadcasted_iota(jnp.int32, (1, VAL_W), 1) // DV
    for h in range(N_HEADS):
        sc["kbd"][:, h * C:(h + 1) * C, :] = jnp.where(head_of_k == h, kcat, 0.0).reshape(nc, C, n_off * KEY_W)
        sc["vbd"][:, h * C:(h + 1) * C, :] = jnp.where(head_of_v == h, v_bf, 0.0).reshape(nc, C, VAL_W)

    @pl.when(l == 0)
    def _():
        sc["st"][...] = s0_ref[0]

    def chunk_body(c, carry):
        rows = pl.ds(pl.multiple_of(c * C, C), C)
        st = sc["st"][...]
        a_all = sc["a"][rows, :] + _dot_nt(sc["qcat"][rows, :], sc["kbd"][c])
        sc["o"][rows, :] = (_dot(a_all.astype(BF16), sc["vbd"][c])
                            + _dot_nt(sc["qg"][rows, :], st.astype(BF16)))
        upd = _dot_tn(sc["v"][rows, :], sc["kd"][rows, :])
        sc["st"][...] = st * sc["egl"][pl.ds(c * C, 1), :] + upd * bdm_ref[...]
        return carry

    lax.fori_loop(0, nc, chunk_body, 0)

    @pl.when(l == n_l - 1)
    def _():
        st_out_ref[0] = sc["st"][...]


def _gla_single_chunk(cfg, q, k, v, G, s0_ref, st_out_ref, sc):
    R, S, n_off, W = _mixer_dims(cfg)
    C = cfg.chunk
    G3 = G.reshape(R // C, C, KEY_W)
    glast = jnp.broadcast_to(G3[:, C - 1:C, :], (R // C, C, KEY_W)).reshape(R, KEY_W)
    sc["qg"][...] = q * jnp.exp(G)
    sc["kd"][...] = k * jnp.exp(glast - G)
    sc["egl"][...] = jnp.exp(glast)
    sc["v"][...] = v
    sc["st"][...] = s0_ref[...]

    a_bf = sc["a"][...].astype(BF16)
    v_bf = v.astype(BF16)
    for h in range(N_HEADS):
        sc["o"][:, h * DV:(h + 1) * DV] = _dot(a_bf[:, h * W:(h + 1) * W], v_bf[:, h * DV:(h + 1) * DV])

    kpad = max(C, 16)

    def seq_body(c, carry):
        cs = pl.multiple_of(c * C, C)
        rows = pl.ds(cs, C)
        for h in range(N_HEADS):
            kc = slice(h * DK, (h + 1) * DK)
            vc = slice(h * DV, (h + 1) * DV)
            st = sc["st"][c, h]
            sc["o"][rows, vc] = sc["o"][rows, vc] + _dot_nt(sc["qg"][rows, kc].astype(BF16), st.astype(BF16))
            v_h = sc["v"][rows, vc]
            kd_h = sc["kd"][rows, kc]
            if kpad > C:
                v_h = jnp.concatenate([v_h, jnp.zeros((kpad - C, DV), F32)], axis=0)
                kd_h = jnp.concatenate([kd_h, jnp.zeros((kpad - C, DK), F32)], axis=0)
            sc["st"][c, h] = st * sc["egl"][pl.ds(cs, 1), kc] + _dot_tn(v_h.astype(BF16), kd_h.astype(BF16))
        return carry

    lax.fori_loop(0, R // C, seq_body, 0)
    st_out_ref[...] = sc["st"][...]


def _mixer_kernel(cfg, x_ref, s0_ref, buf0_ref, n1_ref, wmain_ref, wgkh_ref, wgkl_ref, bgk_ref,
                  gn_ref, wpool_ref, pscale_ref, wo_ref, n2_ref, wrh_ref, wrl_ref, br_ref,
                  tri_ref, hb_ref, bdm_ref, *rest):
    rest = rest[cfg.n_alias:]
    x1_ref, h_ref, topi_ref, gates_ref, st_out_ref, bufo_ref = rest[:6]
    sc = dict(zip([name for name, _, _ in _scratch_spec(cfg)], rest[6:]))

    ns, tl, C = cfg.ns, cfg.tl, cfg.chunk
    R, S, n_off, W = _mixer_dims(cfg)
    l = pl.program_id(1)
    n_l = pl.num_programs(1)
    ext_s = sc["ext"]

    x = x_ref[...].reshape(R, D_MODEL)
    xn = _rms(x, n1_ref[...]).astype(BF16)
    p = _dot(xn, wmain_ref[...])
    q = p[:, 0:KEY_W] * (DK ** -0.5)
    k = p[:, KEY_W:2 * KEY_W]
    v = p[:, 2 * KEY_W:2 * KEY_W + VAL_W]
    og = p[:, 2 * KEY_W + VAL_W:2 * KEY_W + 2 * VAL_W]
    u = p[:, 2 * KEY_W + 2 * VAL_W:2 * KEY_W + 2 * VAL_W + POOL_W]
    glr = p[:, MAIN_COLS - LANES:MAIN_COLS]
    ext_s[:, HALO:HALO + tl, :] = u.reshape(ns, tl, POOL_W)

    z = _dot3(glr, wgkh_ref[...], wgkl_ref[...]) + bgk_ref[...]
    g = -(jnp.maximum(-z, 0.0) + jnp.log(1.0 + jnp.exp(-jnp.abs(z)))) / GATE_NORMALIZER
    tri = tri_ref[...]
    g_hi = g.astype(BF16)
    g_r = g - g_hi.astype(F32)
    g_mid = g_r.astype(BF16)
    g_lo = (g_r - g_mid.astype(F32)).astype(BF16)
    G = _dot(tri, g_hi) + _dot(tri, g_mid) + _dot(tri, g_lo)

    sc["kh"][0:HALO, :] = jnp.zeros((HALO, KEY_W), F32)
    sc["gh"][0:HALO, :] = jnp.zeros((HALO, KEY_W), F32)
    sc["kh"][HALO:HALO + R, :] = k
    sc["gh"][HALO:HALO + R, :] = G
    row = lax.broadcasted_iota(jnp.int32, (R, 1), 0)
    row_s = row % S
    row_w = row % W
    col_w = lax.broadcasted_iota(jnp.int32, (1, N_HEADS * W), 1) % W
    head_bcast = hb_ref[...]
    a_all = jnp.zeros((R, N_HEADS * W), F32)
    for d in range(S):
        k_sh = sc["kh"][HALO - d:HALO - d + R, :]
        g_sh = sc["gh"][HALO - d:HALO - d + R, :]
        e = jnp.exp(jnp.where(row_s >= d, G - g_sh, MASKED_EXPONENT))
        term = (q * k_sh * e).astype(BF16)
        spread = _dot(term, head_bcast)
        a_all = jnp.where(col_w == row_w - d, spread, a_all)
    sc["a"][...] = a_all

    if n_off:
        _gla_chunked(cfg, l, q, k, v, G, s0_ref, st_out_ref, bdm_ref, sc)
    else:
        _gla_single_chunk(cfg, q, k, v, G, s0_ref, st_out_ref, sc)

    o = sc["o"][...]
    gn = gn_ref[...]
    o_heads = []
    for h in range(N_HEADS):
        vc = slice(h * DV, (h + 1) * DV)
        og_h = og[:, vc]
        o_heads.append(_rms(o[:, vc], gn) * (og_h * jax.nn.sigmoid(og_h)))

    @pl.when(l == 0)
    def _():
        ext_s[:, 0:HALO - POOL_BUF, :] = jnp.zeros((ns, HALO - POOL_BUF, POOL_W), F32)
        ext_s[:, HALO - POOL_BUF:HALO, :] = buf0_ref[...]

    pos = cfg.start_pos + l * tl + lax.broadcasted_iota(jnp.int32, (1, tl, 1), 1)
    z_groups = []
    for gi, w in enumerate(POOL_WINDOWS):
        gc = slice(gi * POOL_GROUP, (gi + 1) * POOL_GROUP)
        s = ext_s[:, HALO:HALO + tl, gc]
        for dd in range(1, w):
            s = s + ext_s[:, HALO - dd:HALO - dd + tl, gc]
        cnt = jnp.minimum(w, pos + 1).astype(F32)
        dmean = (s / cnt - ext_s[:, HALO:HALO + tl, gc]).reshape(R, POOL_GROUP)
        z_groups.append(_dot(dmean.astype(BF16), wpool_ref[gi]))
    zp = jnp.concatenate(z_groups, axis=1) * pscale_ref[...]

    @pl.when(l == n_l - 1)
    def _():
        bufo_ref[...] = ext_s[:, tl + HALO - POOL_BUF:tl + HALO, :]

    @pl.when(l < n_l - 1)
    def _():
        ext_s[:, 0:HALO, :] = ext_s[:, tl:tl + HALO, :]

    cat = jnp.concatenate(o_heads + [zp], axis=1).astype(BF16)
    x1 = x + _dot(cat, wo_ref[...])
    x1_ref[...] = x1
    hn = _rms(x1, n2_ref[...])
    for c in range(CHUNKS):
        h_ref[pl.ds(c, R, stride=CHUNKS), :] = hn[:, c * LANES:(c + 1) * LANES]
    logits = _dot3(hn, wrh_ref[...], wrl_ref[...]) + br_ref[...]
    lane = lax.broadcasted_iota(jnp.int32, (R, LANES), 1)
    lg = jnp.where(lane < N_EXPERTS, logits, -jnp.inf)
    vals, idxs = [], []
    for _ in range(TOP_K):
        m = jnp.max(lg, axis=1, keepdims=True)
        idx = jnp.min(jnp.where(lg == m, lane, LANES), axis=1, keepdims=True)
        vals.append(m)
        idxs.append(idx)
        lg = jnp.where(lane == idx, -jnp.inf, lg)
    exps = [jnp.exp(vv - vals[0]) for vv in vals]
    den = exps[0] + exps[1] + exps[2] + exps[3]
    ti = jnp.zeros((R, LANES), jnp.int32)
    gt = jnp.zeros((R, LANES), F32)
    for kk in range(TOP_K):
        ti = jnp.where(lane == kk, idxs[kk], ti)
        gt = jnp.where(lane == kk, exps[kk] / den, gt)
    topi_ref[...] = ti[:, 0:TOP_K]
    gates_ref[...] = gt[:, 0:TOP_K]


def _mixer_constants(cfg):
    R, S, n_off, W = _mixer_dims(cfg)
    C = cfg.chunk
    r = jnp.arange(R)
    tri = ((r[:, None] // C == r[None, :] // C) & (r[None, :] <= r[:, None])).astype(BF16)
    head_bcast = (jnp.arange(KEY_W)[:, None] // DK == jnp.arange(N_HEADS * W)[None, :] // W).astype(BF16)
    block_diag = (jnp.arange(VAL_W)[:, None] // DV == jnp.arange(KEY_W)[None, :] // DK).astype(F32)
    return tri, head_bcast, block_diag


def _mixer_call(cfg, n_tok_all, row_block0, x, s0, buf0, weights, aliased):
    B, L, _ = x.shape
    ns, tl, C = cfg.ns, cfg.tl, cfg.chunk
    R, S, n_off, W = _mixer_dims(cfg)
    n_b, n_l = B // ns, L // tl
    assert B % ns == 0 and L % tl == 0 and tl % C == 0 and R % 8 == 0
    assert (n_l == 1 or tl >= HALO) and (n_off == 0 or ns == 1) and (n_off > 0 or tl == C)

    def const(shape):
        return pl.BlockSpec(shape, lambda b, l: (0,) * len(shape))

    state_block = (1, VAL_W, KEY_W) if n_off else (ns, N_HEADS, DV, DK)
    state_spec = pl.BlockSpec(state_block, lambda b, l: (b,) + (0,) * (len(state_block) - 1))
    consts = _mixer_constants(cfg)
    operands = (x, s0, buf0) + tuple(weights) + consts
    in_specs = [
        pl.BlockSpec((ns, tl, D_MODEL), lambda b, l: (b, l, 0)),
        state_spec,
        pl.BlockSpec((ns, POOL_BUF, POOL_W), lambda b, l: (b, 0, 0)),
    ] + [const(w.shape) for w in tuple(weights) + consts] + [pl.BlockSpec(memory_space=pl.ANY)] * len(aliased)

    def tok_block(rows, width):
        return pl.BlockSpec((rows, width), lambda b, l: (row_block0 + b * n_l + l, 0))

    out_specs = [
        tok_block(R, D_MODEL), tok_block(R * CHUNKS, LANES), tok_block(R, TOP_K), tok_block(R, TOP_K),
        state_spec,
        pl.BlockSpec((ns, POOL_BUF, POOL_W), lambda b, l: (b, 0, 0)),
    ]
    out_shape = [
        jax.ShapeDtypeStruct((n_tok_all, D_MODEL), F32),
        jax.ShapeDtypeStruct((n_tok_all * CHUNKS, LANES), F32),
        jax.ShapeDtypeStruct((n_tok_all, TOP_K), jnp.int32),
        jax.ShapeDtypeStruct((n_tok_all, TOP_K), F32),
        jax.ShapeDtypeStruct((B,) + state_block[1:], F32),
        jax.ShapeDtypeStruct((B, POOL_BUF, POOL_W), F32),
    ]
    aliases = {len(operands) + i: i for i in range(len(aliased))}
    return pl.pallas_call(
        functools.partial(_mixer_kernel, cfg),
        grid=(n_b, n_l),
        in_specs=in_specs,
        out_specs=out_specs,
        out_shape=out_shape,
        scratch_shapes=[pltpu.VMEM(shape, dtype) for _, shape, dtype in _scratch_spec(cfg)],
        input_output_aliases=aliases,
        compiler_params=pltpu.CompilerParams(
            dimension_semantics=("arbitrary", "arbitrary"), vmem_limit_bytes=VMEM_LIMIT),
        name="mixer",
    )(*operands, *aliased)


def _route_kernel(topi_ref, lower_ref, pos_ref, counts_ref, cnt_s, carry_s, gstart_s):
    ph = pl.program_id(0)
    i = pl.program_id(1)
    TT = topi_ref.shape[0]
    topi = topi_ref[...]
    lane = lax.broadcasted_iota(jnp.int32, (TT, LANES), 1)
    hot = jnp.zeros((TT, LANES), F32)
    for kk in range(TOP_K):
        hot = hot + (lane == topi[:, kk:kk + 1]).astype(F32)
    colsum = jnp.sum(hot, axis=0, keepdims=True)

    @pl.when((ph == 0) & (i == 0))
    def _():
        cnt_s[...] = jnp.zeros_like(cnt_s)

    @pl.when(ph == 0)
    def _():
        cnt_s[...] = cnt_s[...] + colsum

    @pl.when((ph == 1) & (i == 0))
    def _():
        cnt = cnt_s[...]
        counts_ref[...] = cnt.astype(jnp.int32)
        tiles = jnp.floor((cnt + (ROW_TILE - 1)) * (1.0 / ROW_TILE))
        tiles8 = jnp.broadcast_to(tiles, (8, LANES))
        ur = lax.broadcasted_iota(jnp.int32, (LANES, LANES), 0)
        uc = lax.broadcasted_iota(jnp.int32, (LANES, LANES), 1)
        upper = (ur < uc).astype(BF16)
        t_hi, t_lo = _split2(tiles8)
        excl = _dot(t_hi, upper) + _dot(t_lo, upper)
        gstart_s[...] = excl[0:1, :] * float(ROW_TILE)
        carry_s[...] = jnp.zeros_like(carry_s)

    @pl.when(ph == 1)
    def _():
        rank = _dot(lower_ref[...], hot.astype(BF16)) + carry_s[...] + gstart_s[...]
        out = jnp.zeros((TT, LANES), F32)
        for kk in range(TOP_K):
            pk = jnp.sum(jnp.where(lane == topi[:, kk:kk + 1], rank, 0.0), axis=1, keepdims=True)
            out = jnp.where(lane == kk, pk, out)
        pos_ref[...] = out[:, 0:TOP_K].astype(jnp.int32)
        carry_s[...] = carry_s[...] + colsum


def _route_call(topi):
    T = topi.shape[0]
    assert T % ROUTE_TILE == 0
    r = jnp.arange(ROUTE_TILE)
    lower = (r[None, :] < r[:, None]).astype(BF16)
    return pl.pallas_call(
        _route_kernel,
        grid=(2, T // ROUTE_TILE),
        in_specs=[pl.BlockSpec((ROUTE_TILE, TOP_K), lambda ph, i: (i, 0)),
                  pl.BlockSpec((ROUTE_TILE, ROUTE_TILE), lambda ph, i: (0, 0))],
        out_specs=[pl.BlockSpec((ROUTE_TILE, TOP_K), lambda ph, i: (i * ph, 0)),
                   pl.BlockSpec((1, LANES), lambda ph, i: (0, 0))],
        out_shape=[jax.ShapeDtypeStruct((T, TOP_K), jnp.int32),
                   jax.ShapeDtypeStruct((1, LANES), jnp.int32)],
        scratch_shapes=[pltpu.VMEM((1, LANES), F32)] * 3,
        compiler_params=pltpu.CompilerParams(dimension_semantics=("arbitrary", "arbitrary")),
        name="route",
    )(topi, lower)


def _invperm_kernel(pad_lo_ref, pad_hi_ref, pos_ref, tok_ref):
    i = pl.program_id(0)

    @pl.when(i == 0)
    def _():
        def clear(r, c):
            tok_ref[r] = -1
            return c

        for e in range(N_EXPERTS):
            lax.fori_loop(pad_lo_ref[e], pad_hi_ref[e], clear, 0)

    @pl.when(i > 0)
    def _():
        base = (i - 1) * INV_BLOCK

        def put(n, c):
            tok_ref[pos_ref[n]] = base + n
            return c

        lax.fori_loop(0, INV_BLOCK, put, 0, unroll=16)


def _invperm_call(pad_lo, pad_hi, pos_flat, n_rows):
    n_pairs = pos_flat.shape[0]
    assert n_pairs % INV_BLOCK == 0
    grid_spec = pltpu.PrefetchScalarGridSpec(
        num_scalar_prefetch=2,
        grid=(1 + n_pairs // INV_BLOCK,),
        in_specs=[pl.BlockSpec((INV_BLOCK,), lambda i, lo, hi: (jnp.maximum(i - 1, 0),),
                               memory_space=pltpu.SMEM)],
        out_specs=pl.BlockSpec(memory_space=pltpu.SMEM),
    )
    return pl.pallas_call(
        _invperm_kernel,
        grid_spec=grid_spec,
        out_shape=jax.ShapeDtypeStruct((n_rows,), jnp.int32),
        compiler_params=pltpu.CompilerParams(dimension_semantics=("arbitrary",)),
        name="invperm",
    )(pad_lo, pad_hi, pos_flat)


def _row_copy_in(h_hbm, xb, sem, r, pair):
    tok = jnp.maximum(pair, 0) >> 2
    return pltpu.make_async_copy(h_hbm.at[pl.ds(pl.multiple_of(tok * CHUNKS, CHUNKS), CHUNKS), :],
                                 xb.at[pl.ds(r * CHUNKS, CHUNKS), :], sem)


def _gather_tile(tab_ref, h_hbm, xb, sem, tile, unrolled):
    base = tile * ROW_TILE
    if unrolled:
        for r in range(ROW_TILE):
            _row_copy_in(h_hbm, xb, sem, r, tab_ref[base + r]).start(priority=r % N_DMA_QUEUES)
    else:
        def issue(r, c):
            _row_copy_in(h_hbm, xb, sem, r, tab_ref[base + r]).start()
            return c

        lax.fori_loop(0, ROW_TILE, issue, 0)


def _wait_all_rows(buf, sem):
    pltpu.make_async_copy(buf, buf, sem).wait()


def _expert_mlp(xb, yb, wgu_s, wd_s, bgu_ref, bd_ref):
    xs = jnp.concatenate([xb[pl.ds(c, ROW_TILE, stride=CHUNKS), :] for c in range(CHUNKS)],
                         axis=1).astype(BF16)
    gu = _dot(xs, wgu_s[...]) + bgu_ref[0]
    gate = jnp.minimum(gu[:, 0:D_FF], SWIGLU_LIMIT)
    up = jnp.clip(gu[:, D_FF:2 * D_FF], -SWIGLU_LIMIT, SWIGLU_LIMIT)
    act = ((up + 1.0) * gate * jax.nn.sigmoid(SWIGLU_ALPHA * gate)).astype(BF16)
    y = _dot(act, wd_s[...]) + bd_ref[0]
    for c in range(CHUNKS):
        yb[pl.ds(c, ROW_TILE, stride=CHUNKS), :] = y[:, c * LANES:(c + 1) * LANES]


def _experts_kernel(te_ref, nt_ref, tab_ref, h_hbm, wgu_ref, bgu_ref, wd_ref, bd_ref, y_hbm,
                    x0, x1, x2, y0, y1, y2, gsem, ssem, wgu_s, wd_s):
    s = pl.program_id(0)
    n_used = nt_ref[0]
    par = s % N_BUF
    tc = jnp.clip(s - COMPUTE_LAG, 0, te_ref.shape[0] - 1)
    xs, ys = (x0, x1, x2), (y0, y1, y2)
    computing = (s >= COMPUTE_LAG) & (s < n_used + COMPUTE_LAG)

    @pl.when(computing & ((s == COMPUTE_LAG) | (te_ref[tc] != te_ref[jnp.maximum(tc - 1, 0)])))
    def _():
        wgu_s[...] = wgu_ref[0].astype(BF16)
        wd_s[...] = wd_ref[0].astype(BF16)

    def write_out(c):
        rows = pl.ds(pl.multiple_of(tc * (ROW_TILE * CHUNKS), ROW_TILE * CHUNKS), ROW_TILE * CHUNKS)
        pltpu.make_async_copy(ys[c], y_hbm.at[rows, :], ssem.at[c]).start(priority=WRITE_DMA_QUEUE)

    last = n_used + COMPUTE_LAG - 1
    steady = (s >= COMPUTE_LAG) & (s < n_used)
    edge = jnp.logical_not(steady) & (s <= last)
    for p in range(N_BUF):
        c = (p + N_BUF - COMPUTE_LAG) % N_BUF
        mine = par == p

        @pl.when(mine & (s >= COMPUTE_LAG + N_BUF) & (s <= last))
        def _(c=c):
            _wait_all_rows(ys[c], ssem.at[c])

        @pl.when(mine & steady)
        def _(p=p, c=c):
            _wait_all_rows(xs[c], gsem.at[c])
            _gather_tile(tab_ref, h_hbm, xs[p], gsem.at[p], s, True)
            _expert_mlp(xs[c], ys[c], wgu_s, wd_s, bgu_ref, bd_ref)
            write_out(c)

        @pl.when(mine & edge & computing)
        def _(c=c):
            _wait_all_rows(xs[c], gsem.at[c])

        @pl.when(mine & edge & (s < n_used))
        def _(p=p):
            _gather_tile(tab_ref, h_hbm, xs[p], gsem.at[p], s, False)

        @pl.when(mine & edge & computing)
        def _(c=c):
            _expert_mlp(xs[c], ys[c], wgu_s, wd_s, bgu_ref, bd_ref)
            write_out(c)

    @pl.when(s == last)
    def _():
        for c in range(N_BUF):
            _wait_all_rows(ys[c], ssem.at[c])


def _experts_call(tile_e, n_used, pair_of_row, h_rows, w_gu, b_gu, w_down, b_down):
    n_tiles = tile_e.shape[0]

    def expert_block(s, te, nt, tab):
        return (te[jnp.clip(s - COMPUTE_LAG, 0, n_tiles - 1)], 0, 0)

    row_buffer = pltpu.VMEM((ROW_TILE * CHUNKS, LANES), F32)
    grid_spec = pltpu.PrefetchScalarGridSpec(
        num_scalar_prefetch=3,
        grid=(n_tiles + COMPUTE_LAG,),
        in_specs=[
            pl.BlockSpec(memory_space=pl.ANY),
            pl.BlockSpec((1, D_MODEL, 2 * D_FF), expert_block),
            pl.BlockSpec((1, 1, 2 * D_FF), expert_block),
            pl.BlockSpec((1, D_FF, D_MODEL), expert_block),
            pl.BlockSpec((1, 1, D_MODEL), expert_block),
        ],
        out_specs=pl.BlockSpec(memory_space=pl.ANY),
        scratch_shapes=[row_buffer] * N_BUF + [row_buffer] * N_BUF + [
            pltpu.SemaphoreType.DMA((N_BUF,)),
            pltpu.SemaphoreType.DMA((N_BUF,)),
            pltpu.VMEM((D_MODEL, 2 * D_FF), BF16),
            pltpu.VMEM((D_FF, D_MODEL), BF16),
        ],
    )
    return pl.pallas_call(
        _experts_kernel,
        grid_spec=grid_spec,
        out_shape=jax.ShapeDtypeStruct((n_tiles * ROW_TILE * CHUNKS, LANES), F32),
        compiler_params=pltpu.CompilerParams(
            dimension_semantics=("arbitrary",), vmem_limit_bytes=VMEM_LIMIT),
        name="experts",
    )(tile_e, n_used, pair_of_row, h_rows, w_gu, b_gu.reshape(N_EXPERTS, 1, 2 * D_FF),
      w_down, b_down.reshape(N_EXPERTS, 1, D_MODEL))


def _gather_pairs(pos_ref, y_hbm, gb, sem, tile, unrolled):
    base = tile * (TOK_TILE * TOP_K)

    def copy(n):
        row = pos_ref[base + n]
        dst = ((n % TOP_K) * TOK_TILE + n // TOP_K) * CHUNKS
        if not isinstance(dst, int):
            dst = pl.multiple_of(dst, CHUNKS)
        return pltpu.make_async_copy(y_hbm.at[pl.ds(pl.multiple_of(row * CHUNKS, CHUNKS), CHUNKS), :],
                                     gb.at[pl.ds(dst, CHUNKS), :], sem)

    if unrolled:
        for n in range(TOK_TILE * TOP_K):
            copy(n).start(priority=n % N_DMA_QUEUES)
    else:
        def issue(n, c):
            copy(n).start()
            return c

        lax.fori_loop(0, TOK_TILE * TOP_K, issue, 0)


def _combine(gb, x1_ref, gates_ref, fn_ref, out_ref):
    gates = gates_ref[...]
    cols = []
    for c in range(CHUNKS):
        acc = gb[pl.ds(c, TOK_TILE, stride=CHUNKS), :] * gates[:, 0:1]
        for kk in range(1, TOP_K):
            acc = acc + gb[pl.ds(kk * TOK_TILE * CHUNKS + c, TOK_TILE, stride=CHUNKS), :] * gates[:, kk:kk + 1]
        cols.append(acc)
    out_ref[...] = _rms(x1_ref[...] + jnp.concatenate(cols, axis=1), fn_ref[...])


def _final_kernel(tile0, pos_ref, y_hbm, x1_ref, gates_ref, fn_ref, out_ref, g0, g1, g2, sem):
    s = pl.program_id(0)
    n = pl.num_programs(0) - COMPUTE_LAG
    par = s % N_BUF
    gs = (g0, g1, g2)
    steady = (s >= COMPUTE_LAG) & (s < n)
    for p in range(N_BUF):
        c = (p + N_BUF - COMPUTE_LAG) % N_BUF
        mine = par == p

        @pl.when(mine & steady)
        def _(p=p, c=c):
            _wait_all_rows(gs[c], sem.at[c])
            _gather_pairs(pos_ref, y_hbm, gs[p], sem.at[p], tile0 + s, True)
            _combine(gs[c], x1_ref, gates_ref, fn_ref, out_ref)

        @pl.when(mine & (s < COMPUTE_LAG))
        def _(p=p):
            _gather_pairs(pos_ref, y_hbm, gs[p], sem.at[p], tile0 + s, False)

        @pl.when(mine & (s >= n))
        def _(c=c):
            _wait_all_rows(gs[c], sem.at[c])
            _combine(gs[c], x1_ref, gates_ref, fn_ref, out_ref)


def _final_call(tok0, n_tok, pos_flat, y_rows, x1_all, gates_all, final_norm):
    assert tok0 % TOK_TILE == 0 and n_tok % TOK_TILE == 0 and n_tok // TOK_TILE >= COMPUTE_LAG
    tile0 = tok0 // TOK_TILE

    def tok_block(s, pos):
        return (tile0 + jnp.maximum(s - COMPUTE_LAG, 0), 0)

    pair_buffer = pltpu.VMEM((TOK_TILE * TOP_K * CHUNKS, LANES), F32)
    grid_spec = pltpu.PrefetchScalarGridSpec(
        num_scalar_prefetch=1,
        grid=(n_tok // TOK_TILE + COMPUTE_LAG,),
        in_specs=[
            pl.BlockSpec(memory_space=pl.ANY),
            pl.BlockSpec((TOK_TILE, D_MODEL), tok_block),
            pl.BlockSpec((TOK_TILE, TOP_K), tok_block),
            pl.BlockSpec((1, D_MODEL), lambda s, pos: (0, 0)),
        ],
        out_specs=pl.BlockSpec((TOK_TILE, D_MODEL), lambda s, pos: (jnp.maximum(s - COMPUTE_LAG, 0), 0)),
        scratch_shapes=[pair_buffer] * N_BUF + [pltpu.SemaphoreType.DMA((N_BUF,))],
    )
    return pl.pallas_call(
        functools.partial(_final_kernel, tile0),
        grid_spec=grid_spec,
        out_shape=jax.ShapeDtypeStruct((n_tok, D_MODEL), F32),
        compiler_params=pltpu.CompilerParams(
            dimension_semantics=("arbitrary",), vmem_limit_bytes=VMEM_LIMIT),
        name="final",
    )(pos_flat, y_rows, x1_all, gates_all, final_norm)


def _pick_tile(n, target):
    t = min(n, target)
    while n % t:
        t -= 1
    return t


def kernel(x_prompt, x_sample, state_gla, state_pool, norm1, w_in, w_gk2, b_gk, gla_norm, w_pool,
           pool_scale, w_o, norm2, w_router, b_router, w_gate_up, b_gate_up, w_down, b_down, final_norm):
    depth = w_in.shape[0]
    assert depth == 1
    B, L, _ = x_prompt.shape
    BS, LS, _ = x_sample.shape
    n_p, n_s = B * L, BS * LS
    n_all = n_p + n_s

    wi = w_in[0]
    o_glr = 2 * KEY_W + 2 * VAL_W
    wmain = jnp.concatenate(
        [wi[:, 0:o_glr], wi[:, o_glr + GATE_RANK:], wi[:, o_glr:o_glr + GATE_RANK],
         jnp.zeros((D_MODEL, LANES - GATE_RANK), F32)], axis=1).astype(BF16)
    wgk = jnp.concatenate([w_gk2[0], jnp.zeros((LANES - GATE_RANK, KEY_W), F32)], axis=0)
    wgkh, wgkl = _split2(wgk)
    wr = jnp.concatenate([w_router[0], jnp.zeros((D_MODEL, LANES - N_EXPERTS), F32)], axis=1)
    wrh, wrl = _split2(wr)
    br = jnp.concatenate([b_router[0], jnp.zeros((LANES - N_EXPERTS,), F32)]).reshape(1, LANES)
    weights = (norm1[0].reshape(1, D_MODEL), wmain, wgkh, wgkl, b_gk[0].reshape(1, KEY_W),
               gla_norm[0].reshape(1, DV), w_pool[0].astype(BF16), pool_scale[0].reshape(1, POOL_W),
               w_o[0].astype(BF16), norm2[0].reshape(1, D_MODEL), wrh, wrl, br)

    assert L % GLA_CHUNK == 0 and LS in (8, 16)
    cfg_p = MixerCfg(ns=1, tl=_pick_tile(L, PROMPT_TILE), chunk=GLA_CHUNK, start_pos=0, n_alias=0)
    s0_p = jnp.zeros((B, VAL_W, KEY_W), F32)
    buf0_p = jnp.zeros((B, POOL_BUF, POOL_W), F32)
    x1_all, h_all, topi_all, gates_all, st_p, buf_p = _mixer_call(
        cfg_p, n_all, 0, x_prompt, s0_p, buf0_p, weights, ())

    cfg_s = MixerCfg(ns=_pick_tile(BS, 128 // LS), tl=LS, chunk=LS, start_pos=PAST_LEN, n_alias=4)
    r_s = cfg_s.ns * cfg_s.tl
    assert n_p % r_s == 0
    x1_all, h_all, topi_all, gates_all, st_s, buf_s = _mixer_call(
        cfg_s, n_all, n_p // r_s, x_sample, jnp.swapaxes(state_gla[0], -1, -2), state_pool[0], weights,
        (x1_all, h_all, topi_all, gates_all))

    pos, counts = _route_call(topi_all)
    n_tiles = (n_all * TOP_K + N_EXPERTS * (ROW_TILE - 1)) // ROW_TILE
    tiles_per_e = (counts[0, :N_EXPERTS] + (ROW_TILE - 1)) // ROW_TILE
    ends = jnp.cumsum(tiles_per_e)
    n_used = ends[-1].astype(jnp.int32)
    tile_ids = jnp.minimum(jnp.arange(n_tiles, dtype=jnp.int32), n_used - 1)
    tile_e = jnp.sum(tile_ids[:, None] >= ends[None, :], axis=1).astype(jnp.int32)
    pos_flat = pos.reshape(n_all * TOP_K)
    pad_hi = (ends * ROW_TILE).astype(jnp.int32)
    pad_lo = pad_hi - (tiles_per_e * ROW_TILE - counts[0, :N_EXPERTS]).astype(jnp.int32)
    pair_of_row = _invperm_call(pad_lo, pad_hi, pos_flat, n_tiles * ROW_TILE)

    y_rows = _experts_call(tile_e, n_used.reshape(1), pair_of_row, h_all,
                           w_gate_up[0], b_gate_up[0], w_down[0], b_down[0])

    fn = final_norm.reshape(1, D_MODEL)
    y_p = _final_call(0, n_p, pos_flat, y_rows, x1_all, gates_all, fn)
    y_s = _final_call(n_p, n_s, pos_flat, y_rows, x1_all, gates_all, fn)

    st_p = jnp.stack([st_p[:, h * DV:(h + 1) * DV, h * DK:(h + 1) * DK] for h in range(N_HEADS)], axis=1)
    return (y_p.reshape(B, L, D_MODEL), y_s.reshape(BS, LS, D_MODEL),
            jnp.swapaxes(st_p, -1, -2)[None], buf_p[None],
            jnp.swapaxes(st_s, -1, -2)[None], buf_s[None])
```

```python
import functools
from typing import NamedTuple

import jax
import jax.numpy as jnp
from jax import lax
from jax.experimental import pallas as pl
from jax.experimental.pallas import tpu as pltpu

F32 = jnp.float32
BF16 = jnp.bfloat16

D_MODEL = 1024
N_HEADS = 4
DK = 64
DV = 128
KEY_W = N_HEADS * DK
VAL_W = N_HEADS * DV
GATE_RANK = 16
GATE_NORMALIZER = 16.0
GLA_CHUNK = 64
POOL_WINDOWS = (2, 4, 8, 16)
POOL_W = 512
POOL_GROUP = 128
POOL_BUF = 15
N_EXPERTS = 32
TOP_K = 4
D_FF = 1024
SWIGLU_LIMIT = 7.0
SWIGLU_ALPHA = 1.702
EPS = 1e-5
PAST_LEN = 16384

LANES = 128
CHUNKS = D_MODEL // LANES
HALO = 16
DIAG = 16
MAIN_COLS = 2 * KEY_W + 2 * VAL_W + POOL_W + LANES
PROMPT_TILE = 512
ROW_TILE = 256
TOK_TILE = 128
ROUTE_TILE = 1024
INV_BLOCK = 2048
N_DMA_QUEUES = 2
WRITE_DMA_QUEUE = 1
N_BUF = 3
COMPUTE_LAG = 2
VMEM_LIMIT = 56 * 1024 * 1024
MASKED_EXPONENT = -1e30


def _dot(a, b):
    return jnp.dot(a, b, preferred_element_type=F32)


def _dot_nt(a, b):
    return lax.dot_general(a, b, (((1,), (1,)), ((), ())), preferred_element_type=F32)


def _dot_tn(a, b):
    return lax.dot_general(a, b, (((0,), (0,)), ((), ())), preferred_element_type=F32)


def _split2(a):
    hi = a.astype(BF16)
    lo = (a - hi.astype(F32)).astype(BF16)
    return hi, lo


def _dot3(a, b_hi, b_lo):
    a_hi, a_lo = _split2(a)
    return _dot(a_hi, b_hi) + _dot(a_lo, b_hi) + _dot(a_hi, b_lo)


def _rms(x, w):
    return x * lax.rsqrt(jnp.mean(x * x, axis=-1, keepdims=True) + EPS) * w


class MixerCfg(NamedTuple):
    ns: int
    tl: int
    chunk: int
    start_pos: int
    n_alias: int


def _mixer_dims(cfg):
    rows = cfg.ns * cfg.tl
    diag = min(cfg.chunk, DIAG)
    n_off = cfg.chunk // diag - 1
    width = cfg.chunk if n_off else rows
    return rows, diag, n_off, width


def _scratch_spec(cfg):
    R, S, n_off, W = _mixer_dims(cfg)
    C, ns, tl = cfg.chunk, cfg.ns, cfg.tl
    spec = [
        ("ext", (ns, tl + HALO, POOL_W), F32),
        ("kh", (R + HALO, KEY_W), F32),
        ("gh", (R + HALO, KEY_W), F32),
        ("egl", (R, KEY_W), F32),
        ("a", (R, N_HEADS * W), F32),
        ("o", (R, VAL_W), F32),
    ]
    if n_off:
        nc = R // C
        spec += [
            ("st", (VAL_W, KEY_W), F32),
            ("qg", (R, KEY_W), BF16),
            ("kd", (R, KEY_W), BF16),
            ("v", (R, VAL_W), BF16),
            ("qcat", (R, n_off * KEY_W), BF16),
            ("kbd", (nc, N_HEADS * C, n_off * KEY_W), BF16),
            ("vbd", (nc, N_HEADS * C, VAL_W), BF16),
        ]
    else:
        spec += [
            ("st", (ns, N_HEADS, DV, DK), F32),
            ("qg", (R, KEY_W), F32),
            ("kd", (R, KEY_W), F32),
            ("v", (R, VAL_W), F32),
        ]
    return spec


def _gla_chunked(cfg, l, q, k, v, G, s0_ref, st_out_ref, bdm_ref, sc):
    R, S, n_off, W = _mixer_dims(cfg)
    C = cfg.chunk
    nc = R // C
    n_l = pl.num_programs(1)
    G3 = G.reshape(nc, C, KEY_W)
    glast = jnp.broadcast_to(G3[:, C - 1:C, :], (nc, C, KEY_W)).reshape(R, KEY_W)
    sc["qg"][...] = (q * jnp.exp(G)).astype(BF16)
    sc["kd"][...] = (k * jnp.exp(glast - G)).astype(BF16)
    sc["egl"][...] = jnp.exp(glast)
    v_bf = v.astype(BF16)
    sc["v"][...] = v_bf

    row_c = lax.broadcasted_iota(jnp.int32, (R, 1), 0) % C
    q_parts, k_parts = [], []
    for a in range(1, n_off + 1):
        ra = jnp.broadcast_to(G3[:, a * S - 1:a * S, :], (nc, C, KEY_W)).reshape(R, KEY_W)
        in_block = (row_c >= a * S) & (row_c < (a + 1) * S)
        q_parts.append(jnp.where(in_block, q * jnp.exp(jnp.minimum(G - ra, 0.0)), 0.0))
        k_parts.append(jnp.where(row_c < a * S, k * jnp.exp(jnp.minimum(ra - G, 0.0)), 0.0))
    sc["qcat"][...] = jnp.concatenate(q_parts, axis=1).astype(BF16)
    kcat = jnp.concatenate(k_parts, axis=1).astype(BF16)
    head_of_k = (lax.broadcasted_iota(jnp.int32, (1, n_off * KEY_W), 1) % KEY_W) // DK
    head_of_v = lax.broadcasted_iota(jnp.int32, (1, VAL_W), 1) // DV
    for h in range(N_HEADS):
        sc["kbd"][:, h * C:(h + 1) * C, :] = jnp.where(head_of_k == h, kcat, 0.0).reshape(nc, C, n_off * KEY_W)
        sc["vbd"][:, h * C:(h + 1) * C, :] = jnp.where(head_of_v == h, v_bf, 0.0).reshape(nc, C, VAL_W)

    @pl.when(l == 0)
    def _():
        sc["st"][...] = s0_ref[0]

    def chunk_body(c, carry):
        rows = pl.ds(pl.multiple_of(c * C, C), C)
        st = sc["st"][...]
        a_all = sc["a"][rows, :] + _dot_nt(sc["qcat"][rows, :], sc["kbd"][c])
        sc["o"][rows, :] = (_dot(a_all.astype(BF16), sc["vbd"][c])
                            + _dot_nt(sc["qg"][rows, :], st.astype(BF16)))
        upd = _dot_tn(sc["v"][rows, :], sc["kd"][rows, :])
        sc["st"][...] = st * sc["egl"][pl.ds(c * C, 1), :] + upd * bdm_ref[...]
        return carry

    lax.fori_loop(0, nc, chunk_body, 0)

    @pl.when(l == n_l - 1)
    def _():
        st_out_ref[0] = sc["st"][...]


def _gla_single_chunk(cfg, q, k, v, G, s0_ref, st_out_ref, sc):
    R, S, n_off, W = _mixer_dims(cfg)
    C = cfg.chunk
    G3 = G.reshape(R // C, C, KEY_W)
    glast = jnp.broadcast_to(G3[:, C - 1:C, :], (R // C, C, KEY_W)).reshape(R, KEY_W)
    sc["qg"][...] = q * jnp.exp(G)
    sc["kd"][...] = k * jnp.exp(glast - G)
    sc["egl"][...] = jnp.exp(glast)
    sc["v"][...] = v
    sc["st"][...] = s0_ref[...]

    a_bf = sc["a"][...].astype(BF16)
    v_bf = v.astype(BF16)
    for h in range(N_HEADS):
        sc["o"][:, h * DV:(h + 1) * DV] = _dot(a_bf[:, h * W:(h + 1) * W], v_bf[:, h * DV:(h + 1) * DV])

    kpad = max(C, 16)

    def seq_body(c, carry):
        cs = pl.multiple_of(c * C, C)
        rows = pl.ds(cs, C)
        for h in range(N_HEADS):
            kc = slice(h * DK, (h + 1) * DK)
            vc = slice(h * DV, (h + 1) * DV)
            st = sc["st"][c, h]
            sc["o"][rows, vc] = sc["o"][rows, vc] + _dot_nt(sc["qg"][rows, kc].astype(BF16), st.astype(BF16))
            v_h = sc["v"][rows, vc]
            kd_h = sc["kd"][rows, kc]
            if kpad > C:
                v_h = jnp.concatenate([v_h, jnp.zeros((kpad - C, DV), F32)], axis=0)
                kd_h = jnp.concatenate([kd_h, jnp.zeros((kpad - C, DK), F32)], axis=0)
            sc["st"][c, h] = st * sc["egl"][pl.ds(cs, 1), kc] + _dot_tn(v_h.astype(BF16), kd_h.astype(BF16))
        return carry

    lax.fori_loop(0, R // C, seq_body, 0)
    st_out_ref[...] = sc["st"][...]


def _mixer_kernel(cfg, x_ref, s0_ref, buf0_ref, n1_ref, wmain_ref, wgkh_ref, wgkl_ref, bgk_ref,
                  gn_ref, wpool_ref, pscale_ref, wo_ref, n2_ref, wrh_ref, wrl_ref, br_ref,
                  tri_ref, hb_ref, bdm_ref, *rest):
    rest = rest[cfg.n_alias:]
    x1_ref, h_ref, topi_ref, gates_ref, st_out_ref, bufo_ref = rest[:6]
    sc = dict(zip([name for name, _, _ in _scratch_spec(cfg)], rest[6:]))

    ns, tl, C = cfg.ns, cfg.tl, cfg.chunk
    R, S, n_off, W = _mixer_dims(cfg)
    l = pl.program_id(1)
    n_l = pl.num_programs(1)
    ext_s = sc["ext"]

    x = x_ref[...].reshape(R, D_MODEL)
    xn = _rms(x, n1_ref[...]).astype(BF16)
    p = _dot(xn, wmain_ref[...])
    q = p[:, 0:KEY_W] * (DK ** -0.5)
    k = p[:, KEY_W:2 * KEY_W]
    v = p[:, 2 * KEY_W:2 * KEY_W + VAL_W]
    og = p[:, 2 * KEY_W + VAL_W:2 * KEY_W + 2 * VAL_W]
    u = p[:, 2 * KEY_W + 2 * VAL_W:2 * KEY_W + 2 * VAL_W + POOL_W]
    glr = p[:, MAIN_COLS - LANES:MAIN_COLS]
    ext_s[:, HALO:HALO + tl, :] = u.reshape(ns, tl, POOL_W)

    z = _dot3(glr, wgkh_ref[...], wgkl_ref[...]) + bgk_ref[...]
    g = -(jnp.maximum(-z, 0.0) + jnp.log(1.0 + jnp.exp(-jnp.abs(z)))) / GATE_NORMALIZER
    tri = tri_ref[...]
    g_hi = g.astype(BF16)
    g_r = g - g_hi.astype(F32)
    g_mid = g_r.astype(BF16)
    g_lo = (g_r - g_mid.astype(F32)).astype(BF16)
    G = _dot(tri, g_hi) + _dot(tri, g_mid) + _dot(tri, g_lo)

    sc["kh"][0:HALO, :] = jnp.zeros((HALO, KEY_W), F32)
    sc["gh"][0:HALO, :] = jnp.zeros((HALO, KEY_W), F32)
    sc["kh"][HALO:HALO + R, :] = k
    sc["gh"][HALO:HALO + R, :] = G
    row = lax.broadcasted_iota(jnp.int32, (R, 1), 0)
    row_s = row % S
    row_w = row % W
    col_w = lax.broadcasted_iota(jnp.int32, (1, N_HEADS * W), 1) % W
    head_bcast = hb_ref[...]
    a_all = jnp.zeros((R, N_HEADS * W), F32)
    for d in range(S):
        k_sh = sc["kh"][HALO - d:HALO - d + R, :]
        g_sh = sc["gh"][HALO - d:HALO - d + R, :]
        e = jnp.exp(jnp.where(row_s >= d, G - g_sh, MASKED_EXPONENT))
        term = (q * k_sh * e).astype(BF16)
        spread = _dot(term, head_bcast)
        a_all = jnp.where(col_w == row_w - d, spread, a_all)
    sc["a"][...] = a_all

    if n_off:
        _gla_chunked(cfg, l, q, k, v, G, s0_ref, st_out_ref, bdm_ref, sc)
    else:
        _gla_single_chunk(cfg, q, k, v, G, s0_ref, st_out_ref, sc)

    o = sc["o"][...]
    gn = gn_ref[...]
    o_heads = []
    for h in range(N_HEADS):
        vc = slice(h * DV, (h + 1) * DV)
        og_h = og[:, vc]
        o_heads.append(_rms(o[:, vc], gn) * (og_h * jax.nn.sigmoid(og_h)))

    @pl.when(l == 0)
    def _():
        ext_s[:, 0:HALO - POOL_BUF, :] = jnp.zeros((ns, HALO - POOL_BUF, POOL_W), F32)
        ext_s[:, HALO - POOL_BUF:HALO, :] = buf0_ref[...]

    pos = cfg.start_pos + l * tl + lax.broadcasted_iota(jnp.int32, (1, tl, 1), 1)
    z_groups = []
    for gi, w in enumerate(POOL_WINDOWS):
        gc = slice(gi * POOL_GROUP, (gi + 1) * POOL_GROUP)
        s = ext_s[:, HALO:HALO + tl, gc]
        for dd in range(1, w):
            s = s + ext_s[:, HALO - dd:HALO - dd + tl, gc]
        cnt = jnp.minimum(w, pos + 1).astype(F32)
        dmean = (s / cnt - ext_s[:, HALO:HALO + tl, gc]).reshape(R, POOL_GROUP)
        z_groups.append(_dot(dmean.astype(BF16), wpool_ref[gi]))
    zp = jnp.concatenate(z_groups, axis=1) * pscale_ref[...]

    @pl.when(l == n_l - 1)
    def _():
        bufo_ref[...] = ext_s[:, tl + HALO - POOL_BUF:tl + HALO, :]

    @pl.when(l < n_l - 1)
    def _():
        ext_s[:, 0:HALO, :] = ext_s[:, tl:tl + HALO, :]

    cat = jnp.concatenate(o_heads + [zp], axis=1).astype(BF16)
    x1 = x + _dot(cat, wo_ref[...])
    x1_ref[...] = x1
    hn = _rms(x1, n2_ref[...])
    for c in range(CHUNKS):
        h_ref[pl.ds(c, R, stride=CHUNKS), :] = hn[:, c * LANES:(c + 1) * LANES]
    logits = _dot3(hn, wrh_ref[...], wrl_ref[...]) + br_ref[...]
    lane = lax.broadcasted_iota(jnp.int32, (R, LANES), 1)
    lg = jnp.where(lane < N_EXPERTS, logits, -jnp.inf)
    vals, idxs = [], []
    for _ in range(TOP_K):
        m = jnp.max(lg, axis=1, keepdims=True)
        idx = jnp.min(jnp.where(lg == m, lane, LANES), axis=1, keepdims=True)
        vals.append(m)
        idxs.append(idx)
        lg = jnp.where(lane == idx, -jnp.inf, lg)
    exps = [jnp.exp(vv - vals[0]) for vv in vals]
    den = exps[0] + exps[1] + exps[2] + exps[3]
    ti = jnp.zeros((R, LANES), jnp.int32)
    gt = jnp.zeros((R, LANES), F32)
    for kk in range(TOP_K):
        ti = jnp.where(lane == kk, idxs[kk], ti)
        gt = jnp.where(lane == kk, exps[kk] / den, gt)
    topi_ref[...] = ti[:, 0:TOP_K]
    gates_ref[...] = gt[:, 0:TOP_K]


def _mixer_constants(cfg):
    R, S, n_off, W = _mixer_dims(cfg)
    C = cfg.chunk
    r = jnp.arange(R)
    tri = ((r[:, None] // C == r[None, :] // C) & (r[None, :] <= r[:, None])).astype(BF16)
    head_bcast = (jnp.arange(KEY_W)[:, None] // DK == jnp.arange(N_HEADS * W)[None, :] // W).astype(BF16)
    block_diag = (jnp.arange(VAL_W)[:, None] // DV == jnp.arange(KEY_W)[None, :] // DK).astype(F32)
    return tri, head_bcast, block_diag


def _mixer_call(cfg, n_tok_all, row_block0, x, s0, buf0, weights, aliased):
    B, L, _ = x.shape
    ns, tl, C = cfg.ns, cfg.tl, cfg.chunk
    R, S, n_off, W = _mixer_dims(cfg)
    n_b, n_l = B // ns, L // tl
    assert B % ns == 0 and L % tl == 0 and tl % C == 0 and R % 8 == 0
    assert (n_l == 1 or tl >= HALO) and (n_off == 0 or ns == 1) and (n_off > 0 or tl == C)

    def const(shape):
        return pl.BlockSpec(shape, lambda b, l: (0,) * len(shape))

    state_block = (1, VAL_W, KEY_W) if n_off else (ns, N_HEADS, DV, DK)
    state_spec = pl.BlockSpec(state_block, lambda b, l: (b,) + (0,) * (len(state_block) - 1))
    consts = _mixer_constants(cfg)
    operands = (x, s0, buf0) + tuple(weights) + consts
    in_specs = [
        pl.BlockSpec((ns, tl, D_MODEL), lambda b, l: (b, l, 0)),
        state_spec,
        pl.BlockSpec((ns, POOL_BUF, POOL_W), lambda b, l: (b, 0, 0)),
    ] + [const(w.shape) for w in tuple(weights) + consts] + [pl.BlockSpec(memory_space=pl.ANY)] * len(aliased)

    def tok_block(rows, width):
        return pl.BlockSpec((rows, width), lambda b, l: (row_block0 + b * n_l + l, 0))

    out_specs = [
        tok_block(R, D_MODEL), tok_block(R * CHUNKS, LANES), tok_block(R, TOP_K), tok_block(R, TOP_K),
        state_spec,
        pl.BlockSpec((ns, POOL_BUF, POOL_W), lambda b, l: (b, 0, 0)),
    ]
    out_shape = [
        jax.ShapeDtypeStruct((n_tok_all, D_MODEL), F32),
        jax.ShapeDtypeStruct((n_tok_all * CHUNKS, LANES), F32),
        jax.ShapeDtypeStruct((n_tok_all, TOP_K), jnp.int32),
        jax.ShapeDtypeStruct((n_tok_all, TOP_K), F32),
        jax.ShapeDtypeStruct((B,) + state_block[1:], F32),
        jax.ShapeDtypeStruct((B, POOL_BUF, POOL_W), F32),
    ]
    aliases = {len(operands) + i: i for i in range(len(aliased))}
    return pl.pallas_call(
        functools.partial(_mixer_kernel, cfg),
        grid=(n_b, n_l),
        in_specs=in_specs,
        out_specs=out_specs,
        out_shape=out_shape,
        scratch_shapes=[pltpu.VMEM(shape, dtype) for _, shape, dtype in _scratch_spec(cfg)],
        input_output_aliases=aliases,
        compiler_params=pltpu.CompilerParams(
            dimension_semantics=("arbitrary", "arbitrary"), vmem_limit_bytes=VMEM_LIMIT),
        name="mixer",
    )(*operands, *aliased)


def _route_kernel(topi_ref, lower_ref, pos_ref, counts_ref, cnt_s, carry_s, gstart_s):
    ph = pl.program_id(0)
    i = pl.program_id(1)
    TT = topi_ref.shape[0]
    topi = topi_ref[...]
    lane = lax.broadcasted_iota(jnp.int32, (TT, LANES), 1)
    hot = jnp.zeros((TT, LANES), F32)
    for kk in range(TOP_K):
        hot = hot + (lane == topi[:, kk:kk + 1]).astype(F32)
    colsum = jnp.sum(hot, axis=0, keepdims=True)

    @pl.when((ph == 0) & (i == 0))
    def _():
        cnt_s[...] = jnp.zeros_like(cnt_s)

    @pl.when(ph == 0)
    def _():
        cnt_s[...] = cnt_s[...] + colsum

    @pl.when((ph == 1) & (i == 0))
    def _():
        cnt = cnt_s[...]
        counts_ref[...] = cnt.astype(jnp.int32)
        tiles = jnp.floor((cnt + (ROW_TILE - 1)) * (1.0 / ROW_TILE))
        tiles8 = jnp.broadcast_to(tiles, (8, LANES))
        ur = lax.broadcasted_iota(jnp.int32, (LANES, LANES), 0)
        uc = lax.broadcasted_iota(jnp.int32, (LANES, LANES), 1)
        upper = (ur < uc).astype(BF16)
        t_hi, t_lo = _split2(tiles8)
        excl = _dot(t_hi, upper) + _dot(t_lo, upper)
        gstart_s[...] = excl[0:1, :] * float(ROW_TILE)
        carry_s[...] = jnp.zeros_like(carry_s)

    @pl.when(ph == 1)
    def _():
        rank = _dot(lower_ref[...], hot.astype(BF16)) + carry_s[...] + gstart_s[...]
        out = jnp.zeros((TT, LANES), F32)
        for kk in range(TOP_K):
            pk = jnp.sum(jnp.where(lane == topi[:, kk:kk + 1], rank, 0.0), axis=1, keepdims=True)
            out = jnp.where(lane == kk, pk, out)
        pos_ref[...] = out[:, 0:TOP_K].astype(jnp.int32)
        carry_s[...] = carry_s[...] + colsum


def _route_call(topi):
    T = topi.shape[0]
    assert T % ROUTE_TILE == 0
    r = jnp.arange(ROUTE_TILE)
    lower = (r[None, :] < r[:, None]).astype(BF16)
    return pl.pallas_call(
        _route_kernel,
        grid=(2, T // ROUTE_TILE),
        in_specs=[pl.BlockSpec((ROUTE_TILE, TOP_K), lambda ph, i: (i, 0)),
                  pl.BlockSpec((ROUTE_TILE, ROUTE_TILE), lambda ph, i: (0, 0))],
        out_specs=[pl.BlockSpec((ROUTE_TILE, TOP_K), lambda ph, i: (i * ph, 0)),
                   pl.BlockSpec((1, LANES), lambda ph, i: (0, 0))],
        out_shape=[jax.ShapeDtypeStruct((T, TOP_K), jnp.int32),
                   jax.ShapeDtypeStruct((1, LANES), jnp.int32)],
        scratch_shapes=[pltpu.VMEM((1, LANES), F32)] * 3,
        compiler_params=pltpu.CompilerParams(dimension_semantics=("arbitrary", "arbitrary")),
        name="route",
    )(topi, lower)


def _invperm_kernel(pad_lo_ref, pad_hi_ref, pos_ref, tok_ref):
    i = pl.program_id(0)

    @pl.when(i == 0)
    def _():
        def clear(r, c):
            tok_ref[r] = -1
            return c

        for e in range(N_EXPERTS):
            lax.fori_loop(pad_lo_ref[e], pad_hi_ref[e], clear, 0)

    @pl.when(i > 0)
    def _():
        base = (i - 1) * INV_BLOCK

        def put(n, c):
            tok_ref[pos_ref[n]] = base + n
            return c

        lax.fori_loop(0, INV_BLOCK, put, 0, unroll=16)


def _invperm_call(pad_lo, pad_hi, pos_flat, n_rows):
    n_pairs = pos_flat.shape[0]
    assert n_pairs % INV_BLOCK == 0
    grid_spec = pltpu.PrefetchScalarGridSpec(
        num_scalar_prefetch=2,
        grid=(1 + n_pairs // INV_BLOCK,),
        in_specs=[pl.BlockSpec((INV_BLOCK,), lambda i, lo, hi: (jnp.maximum(i - 1, 0),),
                               memory_space=pltpu.SMEM)],
        out_specs=pl.BlockSpec(memory_space=pltpu.SMEM),
    )
    return pl.pallas_call(
        _invperm_kernel,
        grid_spec=grid_spec,
        out_shape=jax.ShapeDtypeStruct((n_rows,), jnp.int32),
        compiler_params=pltpu.CompilerParams(dimension_semantics=("arbitrary",)),
        name="invperm",
    )(pad_lo, pad_hi, pos_flat)


def _row_copy_in(h_hbm, xb, sem, r, pair):
    tok = jnp.maximum(pair, 0) >> 2
    return pltpu.make_async_copy(h_hbm.at[pl.ds(pl.multiple_of(tok * CHUNKS, CHUNKS), CHUNKS), :],
                                 xb.at[pl.ds(r * CHUNKS, CHUNKS), :], sem)


def _gather_tile(tab_ref, h_hbm, xb, sem, tile, unrolled):
    base = tile * ROW_TILE
    if unrolled:
        for r in range(ROW_TILE):
            _row_copy_in(h_hbm, xb, sem, r, tab_ref[base + r]).start(priority=r % N_DMA_QUEUES)
    else:
        def issue(r, c):
            _row_copy_in(h_hbm, xb, sem, r, tab_ref[base + r]).start()
            return c

        lax.fori_loop(0, ROW_TILE, issue, 0)


def _wait_all_rows(buf, sem):
    pltpu.make_async_copy(buf, buf, sem).wait()


def _expert_mlp(xb, yb, wgu_s, wd_s, bgu_ref, bd_ref):
    xs = jnp.concatenate([xb[pl.ds(c, ROW_TILE, stride=CHUNKS), :] for c in range(CHUNKS)],
                         axis=1).astype(BF16)
    gu = _dot(xs, wgu_s[...]) + bgu_ref[0]
    gate = jnp.minimum(gu[:, 0:D_FF], SWIGLU_LIMIT)
    up = jnp.clip(gu[:, D_FF:2 * D_FF], -SWIGLU_LIMIT, SWIGLU_LIMIT)
    act = ((up + 1.0) * gate * jax.nn.sigmoid(SWIGLU_ALPHA * gate)).astype(BF16)
    y = _dot(act, wd_s[...]) + bd_ref[0]
    for c in range(CHUNKS):
        yb[pl.ds(c, ROW_TILE, stride=CHUNKS), :] = y[:, c * LANES:(c + 1) * LANES]


def _experts_kernel(te_ref, nt_ref, nxt_ref, tab_ref, h_hbm, wgu_hbm, bgu_ref, wd_hbm, bd_ref, y_hbm,
                    x0, x1, x2, y0, y1, y2, gsem, ssem, wgu_s, wd_s, wgu_f, wd_f, wsem):
    s = pl.program_id(0)
    n_used = nt_ref[0]
    par = s % N_BUF
    tc = jnp.clip(s - COMPUTE_LAG, 0, te_ref.shape[0] - 1)
    xs, ys = (x0, x1, x2), (y0, y1, y2)
    computing = (s >= COMPUTE_LAG) & (s < n_used + COMPUTE_LAG)

    def weight_copies(e):
        return (pltpu.make_async_copy(wgu_hbm.at[e], wgu_f, wsem.at[0]),
                pltpu.make_async_copy(wd_hbm.at[e], wd_f, wsem.at[1]))

    @pl.when(s == 0)
    def _():
        for cp in weight_copies(te_ref[0]):
            cp.start()

    @pl.when(computing & ((s == COMPUTE_LAG) | (te_ref[tc] != te_ref[jnp.maximum(tc - 1, 0)])))
    def _():
        for cp in weight_copies(te_ref[tc]):
            cp.wait()
        wgu_s[...] = wgu_f[...].astype(BF16)
        wd_s[...] = wd_f[...].astype(BF16)
        nxt = nxt_ref[te_ref[tc]]

        @pl.when(nxt >= 0)
        def _():
            for cp in weight_copies(nxt):
                cp.start()

    def write_out(c):
        rows = pl.ds(pl.multiple_of(tc * (ROW_TILE * CHUNKS), ROW_TILE * CHUNKS), ROW_TILE * CHUNKS)
        pltpu.make_async_copy(ys[c], y_hbm.at[rows, :], ssem.at[c]).start(priority=WRITE_DMA_QUEUE)

    last = n_used + COMPUTE_LAG - 1
    steady = (s >= COMPUTE_LAG) & (s < n_used)
    edge = jnp.logical_not(steady) & (s <= last)
    for p in range(N_BUF):
        c = (p + N_BUF - COMPUTE_LAG) % N_BUF
        mine = par == p

        @pl.when(mine & (s >= COMPUTE_LAG + N_BUF) & (s <= last))
        def _(c=c):
            _wait_all_rows(ys[c], ssem.at[c])

        @pl.when(mine & steady)
        def _(p=p, c=c):
            _wait_all_rows(xs[c], gsem.at[c])
            _gather_tile(tab_ref, h_hbm, xs[p], gsem.at[p], s, True)
            _expert_mlp(xs[c], ys[c], wgu_s, wd_s, bgu_ref, bd_ref)
            write_out(c)

        @pl.when(mine & edge & computing)
        def _(c=c):
            _wait_all_rows(xs[c], gsem.at[c])

        @pl.when(mine & edge & (s < n_used))
        def _(p=p):
            _gather_tile(tab_ref, h_hbm, xs[p], gsem.at[p], s, False)

        @pl.when(mine & edge & computing)
        def _(c=c):
            _expert_mlp(xs[c], ys[c], wgu_s, wd_s, bgu_ref, bd_ref)
            write_out(c)

    @pl.when(s == last)
    def _():
        for c in range(N_BUF):
            _wait_all_rows(ys[c], ssem.at[c])


def _experts_call(tile_e, n_used, next_e, pair_of_row, h_rows, w_gu, b_gu, w_down, b_down):
    n_tiles = tile_e.shape[0]

    def expert_block(s, te, nt, nxt, tab):
        return (te[jnp.clip(s - COMPUTE_LAG, 0, n_tiles - 1)], 0, 0)

    row_buffer = pltpu.VMEM((ROW_TILE * CHUNKS, LANES), F32)
    grid_spec = pltpu.PrefetchScalarGridSpec(
        num_scalar_prefetch=4,
        grid=(n_tiles + COMPUTE_LAG,),
        in_specs=[
            pl.BlockSpec(memory_space=pl.ANY),
            pl.BlockSpec(memory_space=pl.ANY),
            pl.BlockSpec((1, 1, 2 * D_FF), expert_block),
            pl.BlockSpec(memory_space=pl.ANY),
            pl.BlockSpec((1, 1, D_MODEL), expert_block),
        ],
        out_specs=pl.BlockSpec(memory_space=pl.ANY),
        scratch_shapes=[row_buffer] * N_BUF + [row_buffer] * N_BUF + [
            pltpu.SemaphoreType.DMA((N_BUF,)),
            pltpu.SemaphoreType.DMA((N_BUF,)),
            pltpu.VMEM((D_MODEL, 2 * D_FF), BF16),
            pltpu.VMEM((D_FF, D_MODEL), BF16),
            pltpu.VMEM((D_MODEL, 2 * D_FF), F32),
            pltpu.VMEM((D_FF, D_MODEL), F32),
            pltpu.SemaphoreType.DMA((2,)),
        ],
    )
    return pl.pallas_call(
        _experts_kernel,
        grid_spec=grid_spec,
        out_shape=jax.ShapeDtypeStruct((n_tiles * ROW_TILE * CHUNKS, LANES), F32),
        compiler_params=pltpu.CompilerParams(
            dimension_semantics=("arbitrary",), vmem_limit_bytes=VMEM_LIMIT),
        name="experts",
    )(tile_e, n_used, next_e, pair_of_row, h_rows, w_gu, b_gu.reshape(N_EXPERTS, 1, 2 * D_FF),
      w_down, b_down.reshape(N_EXPERTS, 1, D_MODEL))


def _gather_pairs(pos_ref, y_hbm, gb, sem, tile, unrolled):
    base = tile * (TOK_TILE * TOP_K)

    def copy(n):
        row = pos_ref[base + n]
        dst = ((n % TOP_K) * TOK_TILE + n // TOP_K) * CHUNKS
        if not isinstance(dst, int):
            dst = pl.multiple_of(dst, CHUNKS)
        return pltpu.make_async_copy(y_hbm.at[pl.ds(pl.multiple_of(row * CHUNKS, CHUNKS), CHUNKS), :],
                                     gb.at[pl.ds(dst, CHUNKS), :], sem)

    if unrolled:
        for n in range(TOK_TILE * TOP_K):
            copy(n).start(priority=n % N_DMA_QUEUES)
    else:
        def issue(n, c):
            copy(n).start()
            return c

        lax.fori_loop(0, TOK_TILE * TOP_K, issue, 0)


def _combine(gb, x1_ref, gates_ref, fn_ref, out_ref):
    gates = gates_ref[...]
    cols = []
    for c in range(CHUNKS):
        acc = gb[pl.ds(c, TOK_TILE, stride=CHUNKS), :] * gates[:, 0:1]
        for kk in range(1, TOP_K):
            acc = acc + gb[pl.ds(kk * TOK_TILE * CHUNKS + c, TOK_TILE, stride=CHUNKS), :] * gates[:, kk:kk + 1]
        cols.append(acc)
    out_ref[...] = _rms(x1_ref[...] + jnp.concatenate(cols, axis=1), fn_ref[...])


def _final_kernel(tile0, pos_ref, y_hbm, x1_ref, gates_ref, fn_ref, out_ref, g0, g1, g2, sem):
    s = pl.program_id(0)
    n = pl.num_programs(0) - COMPUTE_LAG
    par = s % N_BUF
    gs = (g0, g1, g2)
    steady = (s >= COMPUTE_LAG) & (s < n)
    for p in range(N_BUF):
        c = (p + N_BUF - COMPUTE_LAG) % N_BUF
        mine = par == p

        @pl.when(mine & steady)
        def _(p=p, c=c):
            _wait_all_rows(gs[c], sem.at[c])
            _gather_pairs(pos_ref, y_hbm, gs[p], sem.at[p], tile0 + s, True)
            _combine(gs[c], x1_ref, gates_ref, fn_ref, out_ref)

        @pl.when(mine & (s < COMPUTE_LAG))
        def _(p=p):
            _gather_pairs(pos_ref, y_hbm, gs[p], sem.at[p], tile0 + s, False)

        @pl.when(mine & (s >= n))
        def _(c=c):
            _wait_all_rows(gs[c], sem.at[c])
            _combine(gs[c], x1_ref, gates_ref, fn_ref, out_ref)


def _final_call(tok0, n_tok, pos_flat, y_rows, x1_all, gates_all, final_norm):
    assert tok0 % TOK_TILE == 0 and n_tok % TOK_TILE == 0 and n_tok // TOK_TILE >= COMPUTE_LAG
    tile0 = tok0 // TOK_TILE

    def tok_block(s, pos):
        return (tile0 + jnp.maximum(s - COMPUTE_LAG, 0), 0)

    pair_buffer = pltpu.VMEM((TOK_TILE * TOP_K * CHUNKS, LANES), F32)
    grid_spec = pltpu.PrefetchScalarGridSpec(
        num_scalar_prefetch=1,
        grid=(n_tok // TOK_TILE + COMPUTE_LAG,),
        in_specs=[
            pl.BlockSpec(memory_space=pl.ANY),
            pl.BlockSpec((TOK_TILE, D_MODEL), tok_block),
            pl.BlockSpec((TOK_TILE, TOP_K), tok_block),
            pl.BlockSpec((1, D_MODEL), lambda s, pos: (0, 0)),
        ],
        out_specs=pl.BlockSpec((TOK_TILE, D_MODEL), lambda s, pos: (jnp.maximum(s - COMPUTE_LAG, 0), 0)),
        scratch_shapes=[pair_buffer] * N_BUF + [pltpu.SemaphoreType.DMA((N_BUF,))],
    )
    return pl.pallas_call(
        functools.partial(_final_kernel, tile0),
        grid_spec=grid_spec,
        out_shape=jax.ShapeDtypeStruct((n_tok, D_MODEL), F32),
        compiler_params=pltpu.CompilerParams(
            dimension_semantics=("arbitrary",), vmem_limit_bytes=VMEM_LIMIT),
        name="final",
    )(pos_flat, y_rows, x1_all, gates_all, final_norm)


def _pick_tile(n, target):
    t = min(n, target)
    while n % t:
        t -= 1
    return t


def kernel(x_prompt, x_sample, state_gla, state_pool, norm1, w_in, w_gk2, b_gk, gla_norm, w_pool,
           pool_scale, w_o, norm2, w_router, b_router, w_gate_up, b_gate_up, w_down, b_down, final_norm):
    depth = w_in.shape[0]
    assert depth == 1
    B, L, _ = x_prompt.shape
    BS, LS, _ = x_sample.shape
    n_p, n_s = B * L, BS * LS
    n_all = n_p + n_s

    wi = w_in[0]
    o_glr = 2 * KEY_W + 2 * VAL_W
    wmain = jnp.concatenate(
        [wi[:, 0:o_glr], wi[:, o_glr + GATE_RANK:], wi[:, o_glr:o_glr + GATE_RANK],
         jnp.zeros((D_MODEL, LANES - GATE_RANK), F32)], axis=1).astype(BF16)
    wgk = jnp.concatenate([w_gk2[0], jnp.zeros((LANES - GATE_RANK, KEY_W), F32)], axis=0)
    wgkh, wgkl = _split2(wgk)
    wr = jnp.concatenate([w_router[0], jnp.zeros((D_MODEL, LANES - N_EXPERTS), F32)], axis=1)
    wrh, wrl = _split2(wr)
    br = jnp.concatenate([b_router[0], jnp.zeros((LANES - N_EXPERTS,), F32)]).reshape(1, LANES)
    weights = (norm1[0].reshape(1, D_MODEL), wmain, wgkh, wgkl, b_gk[0].reshape(1, KEY_W),
               gla_norm[0].reshape(1, DV), w_pool[0].astype(BF16), pool_scale[0].reshape(1, POOL_W),
               w_o[0].astype(BF16), norm2[0].reshape(1, D_MODEL), wrh, wrl, br)

    assert L % GLA_CHUNK == 0 and LS in (8, 16)
    cfg_p = MixerCfg(ns=1, tl=_pick_tile(L, PROMPT_TILE), chunk=GLA_CHUNK, start_pos=0, n_alias=0)
    s0_p = jnp.zeros((B, VAL_W, KEY_W), F32)
    buf0_p = jnp.zeros((B, POOL_BUF, POOL_W), F32)
    x1_all, h_all, topi_all, gates_all, st_p, buf_p = _mixer_call(
        cfg_p, n_all, 0, x_prompt, s0_p, buf0_p, weights, ())

    cfg_s = MixerCfg(ns=_pick_tile(BS, 128 // LS), tl=LS, chunk=LS, start_pos=PAST_LEN, n_alias=4)
    r_s = cfg_s.ns * cfg_s.tl
    assert n_p % r_s == 0
    x1_all, h_all, topi_all, gates_all, st_s, buf_s = _mixer_call(
        cfg_s, n_all, n_p // r_s, x_sample, jnp.swapaxes(state_gla[0], -1, -2), state_pool[0], weights,
        (x1_all, h_all, topi_all, gates_all))

    pos, counts = _route_call(topi_all)
    n_tiles = (n_all * TOP_K + N_EXPERTS * (ROW_TILE - 1)) // ROW_TILE
    tiles_per_e = (counts[0, :N_EXPERTS] + (ROW_TILE - 1)) // ROW_TILE
    ends = jnp.cumsum(tiles_per_e)
    n_used = ends[-1].astype(jnp.int32)
    tile_ids = jnp.minimum(jnp.arange(n_tiles, dtype=jnp.int32), n_used - 1)
    tile_e = jnp.sum(tile_ids[:, None] >= ends[None, :], axis=1).astype(jnp.int32)
    e_ids = jnp.arange(N_EXPERTS, dtype=jnp.int32)
    later = (tiles_per_e[None, :] > 0) & (e_ids[None, :] > e_ids[:, None])
    next_e = jnp.min(jnp.where(later, e_ids[None, :], N_EXPERTS), axis=1)
    next_e = jnp.where(next_e == N_EXPERTS, -1, next_e).astype(jnp.int32)
    pos_flat = pos.reshape(n_all * TOP_K)
    pad_hi = (ends * ROW_TILE).astype(jnp.int32)
    pad_lo = pad_hi - (tiles_per_e * ROW_TILE - counts[0, :N_EXPERTS]).astype(jnp.int32)
    pair_of_row = _invperm_call(pad_lo, pad_hi, pos_flat, n_tiles * ROW_TILE)

    y_rows = _experts_call(tile_e, n_used.reshape(1), next_e, pair_of_row, h_all,
                           w_gate_up[0], b_gate_up[0], w_down[0], b_down[0])

    fn = final_norm.reshape(1, D_MODEL)
    y_p = _final_call(0, n_p, pos_flat, y_rows, x1_all, gates_all, fn)
    y_s = _final_call(n_p, n_s, pos_flat, y_rows, x1_all, gates_all, fn)

    st_p = jnp.stack([st_p[:, h * DV:(h + 1) * DV, h * DK:(h + 1) * DK] for h in range(N_HEADS)], axis=1)
    return (y_p.reshape(B, L, D_MODEL), y_s.reshape(BS, LS, D_MODEL),
            jnp.swapaxes(st_p, -1, -2)[None], buf_p[None],
            jnp.swapaxes(st_s, -1, -2)[None], buf_s[None])
```

```python
import functools
from typing import NamedTuple

import jax
import jax.numpy as jnp
from jax import lax
from jax.experimental import pallas as pl
from jax.experimental.pallas import tpu as pltpu

F32 = jnp.float32
BF16 = jnp.bfloat16

D_MODEL = 1024
N_HEADS = 4
DK = 64
DV = 128
KEY_W = N_HEADS * DK
VAL_W = N_HEADS * DV
GATE_RANK = 16
GATE_NORMALIZER = 16.0
GLA_CHUNK = 64
POOL_WINDOWS = (2, 4, 8, 16)
POOL_W = 512
POOL_GROUP = 128
POOL_BUF = 15
N_EXPERTS = 32
TOP_K = 4
D_FF = 1024
SWIGLU_LIMIT = 7.0
SWIGLU_ALPHA = 1.702
EPS = 1e-5
PAST_LEN = 16384

LANES = 128
CHUNKS = D_MODEL // LANES
HALO = 16
DIAG = 16
MAIN_COLS = 2 * KEY_W + 2 * VAL_W + POOL_W + LANES
PROMPT_TILE = 512
ROW_TILE = 256
TOK_TILE = 256
ROUTE_TILE = 1024
INV_BLOCK = 2048
N_DMA_QUEUES = 2
WRITE_DMA_QUEUE = 1
N_BUF = 3
COMPUTE_LAG = 2
VMEM_LIMIT = 56 * 1024 * 1024
MASKED_EXPONENT = -1e30


def _dot(a, b):
    return jnp.dot(a, b, preferred_element_type=F32)


def _dot_nt(a, b):
    return lax.dot_general(a, b, (((1,), (1,)), ((), ())), preferred_element_type=F32)


def _dot_tn(a, b):
    return lax.dot_general(a, b, (((0,), (0,)), ((), ())), preferred_element_type=F32)


def _split2(a):
    hi = a.astype(BF16)
    lo = (a - hi.astype(F32)).astype(BF16)
    return hi, lo


def _dot3(a, b_hi, b_lo):
    a_hi, a_lo = _split2(a)
    return _dot(a_hi, b_hi) + _dot(a_lo, b_hi) + _dot(a_hi, b_lo)


def _rms(x, w):
    return x * lax.rsqrt(jnp.mean(x * x, axis=-1, keepdims=True) + EPS) * w


class MixerCfg(NamedTuple):
    ns: int
    tl: int
    chunk: int
    start_pos: int
    n_alias: int


def _mixer_dims(cfg):
    rows = cfg.ns * cfg.tl
    diag = min(cfg.chunk, DIAG)
    n_off = cfg.chunk // diag - 1
    width = cfg.chunk if n_off else rows
    return rows, diag, n_off, width


def _scratch_spec(cfg):
    R, S, n_off, W = _mixer_dims(cfg)
    C, ns, tl = cfg.chunk, cfg.ns, cfg.tl
    spec = [
        ("ext", (ns, tl + HALO, POOL_W), F32),
        ("kh", (R + HALO, KEY_W), F32),
        ("gh", (R + HALO, KEY_W), F32),
        ("egl", (R, KEY_W), F32),
        ("a", (R, N_HEADS * W), F32),
        ("o", (R, VAL_W), F32),
    ]
    if n_off:
        nc = R // C
        spec += [
            ("st", (VAL_W, KEY_W), F32),
            ("qg", (R, KEY_W), BF16),
            ("kd", (R, KEY_W), BF16),
            ("v", (R, VAL_W), BF16),
            ("qcat", (R, n_off * KEY_W), BF16),
            ("kbd", (nc, N_HEADS * C, n_off * KEY_W), BF16),
            ("vbd", (nc, N_HEADS * C, VAL_W), BF16),
        ]
    else:
        spec += [
            ("st", (ns, N_HEADS, DV, DK), F32),
            ("qg", (R, KEY_W), F32),
            ("kd", (R, KEY_W), F32),
            ("v", (R, VAL_W), F32),
        ]
    return spec


def _gla_chunked(cfg, l, q, k, v, G, s0_ref, st_out_ref, bdm_ref, sc):
    R, S, n_off, W = _mixer_dims(cfg)
    C = cfg.chunk
    nc = R // C
    n_l = pl.num_programs(1)
    G3 = G.reshape(nc, C, KEY_W)
    glast = jnp.broadcast_to(G3[:, C - 1:C, :], (nc, C, KEY_W)).reshape(R, KEY_W)
    sc["qg"][...] = (q * jnp.exp(G)).astype(BF16)
    sc["kd"][...] = (k * jnp.exp(glast - G)).astype(BF16)
    sc["egl"][...] = jnp.exp(glast)
    v_bf = v.astype(BF16)
    sc["v"][...] = v_bf

    row_c = lax.broadcasted_iota(jnp.int32, (R, 1), 0) % C
    q_parts, k_parts = [], []
    for a in range(1, n_off + 1):
        ra = jnp.broadcast_to(G3[:, a * S - 1:a * S, :], (nc, C, KEY_W)).reshape(R, KEY_W)
        in_block = (row_c >= a * S) & (row_c < (a + 1) * S)
        q_parts.append(jnp.where(in_block, q * jnp.exp(jnp.minimum(G - ra, 0.0)), 0.0))
        k_parts.append(jnp.where(row_c < a * S, k * jnp.exp(jnp.minimum(ra - G, 0.0)), 0.0))
    sc["qcat"][...] = jnp.concatenate(q_parts, axis=1).astype(BF16)
    kcat = jnp.concatenate(k_parts, axis=1).astype(BF16)
    head_of_k = (lax.broadcasted_iota(jnp.int32, (1, n_off * KEY_W), 1) % KEY_W) // DK
    head_of_v = lax.broadcasted_iota(jnp.int32, (1, VAL_W), 1) // DV
    for h in range(N_HEADS):
        sc["kbd"][:, h * C:(h + 1) * C, :] = jnp.where(head_of_k == h, kcat, 0.0).reshape(nc, C, n_off * KEY_W)
        sc["vbd"][:, h * C:(h + 1) * C, :] = jnp.where(head_of_v == h, v_bf, 0.0).reshape(nc, C, VAL_W)

    @pl.when(l == 0)
    def _():
        sc["st"][...] = s0_ref[0]

    def chunk_body(c, carry):
        rows = pl.ds(pl.multiple_of(c * C, C), C)
        st = sc["st"][...]
        a_all = sc["a"][rows, :] + _dot_nt(sc["qcat"][rows, :], sc["kbd"][c])
        sc["o"][rows, :] = (_dot(a_all.astype(BF16), sc["vbd"][c])
                            + _dot_nt(sc["qg"][rows, :], st.astype(BF16)))
        upd = _dot_tn(sc["v"][rows, :], sc["kd"][rows, :])
        sc["st"][...] = st * sc["egl"][pl.ds(c * C, 1), :] + upd * bdm_ref[...]
        return carry

    lax.fori_loop(0, nc, chunk_body, 0)

    @pl.when(l == n_l - 1)
    def _():
        st_out_ref[0] = sc["st"][...]


def _gla_single_chunk(cfg, q, k, v, G, s0_ref, st_out_ref, sc):
    R, S, n_off, W = _mixer_dims(cfg)
    C = cfg.chunk
    G3 = G.reshape(R // C, C, KEY_W)
    glast = jnp.broadcast_to(G3[:, C - 1:C, :], (R // C, C, KEY_W)).reshape(R, KEY_W)
    sc["qg"][...] = q * jnp.exp(G)
    sc["kd"][...] = k * jnp.exp(glast - G)
    sc["egl"][...] = jnp.exp(glast)
    sc["v"][...] = v
    sc["st"][...] = s0_ref[...]

    a_bf = sc["a"][...].astype(BF16)
    v_bf = v.astype(BF16)
    for h in range(N_HEADS):
        sc["o"][:, h * DV:(h + 1) * DV] = _dot(a_bf[:, h * W:(h + 1) * W], v_bf[:, h * DV:(h + 1) * DV])

    kpad = max(C, 16)

    def seq_body(c, carry):
        cs = pl.multiple_of(c * C, C)
        rows = pl.ds(cs, C)
        for h in range(N_HEADS):
            kc = slice(h * DK, (h + 1) * DK)
            vc = slice(h * DV, (h + 1) * DV)
            st = sc["st"][c, h]
            sc["o"][rows, vc] = sc["o"][rows, vc] + _dot_nt(sc["qg"][rows, kc].astype(BF16), st.astype(BF16))
            v_h = sc["v"][rows, vc]
            kd_h = sc["kd"][rows, kc]
            if kpad > C:
                v_h = jnp.concatenate([v_h, jnp.zeros((kpad - C, DV), F32)], axis=0)
                kd_h = jnp.concatenate([kd_h, jnp.zeros((kpad - C, DK), F32)], axis=0)
            sc["st"][c, h] = st * sc["egl"][pl.ds(cs, 1), kc] + _dot_tn(v_h.astype(BF16), kd_h.astype(BF16))
        return carry

    lax.fori_loop(0, R // C, seq_body, 0)
    st_out_ref[...] = sc["st"][...]


def _mixer_kernel(cfg, x_ref, s0_ref, buf0_ref, n1_ref, wmain_ref, wgkh_ref, wgkl_ref, bgk_ref,
                  gn_ref, wpool_ref, pscale_ref, wo_ref, n2_ref, wrh_ref, wrl_ref, br_ref,
                  tri_ref, hb_ref, bdm_ref, *rest):
    rest = rest[cfg.n_alias:]
    x1_ref, h_ref, topi_ref, gates_ref, st_out_ref, bufo_ref = rest[:6]
    sc = dict(zip([name for name, _, _ in _scratch_spec(cfg)], rest[6:]))

    ns, tl, C = cfg.ns, cfg.tl, cfg.chunk
    R, S, n_off, W = _mixer_dims(cfg)
    l = pl.program_id(1)
    n_l = pl.num_programs(1)
    ext_s = sc["ext"]

    x = x_ref[...].reshape(R, D_MODEL)
    xn = _rms(x, n1_ref[...]).astype(BF16)
    p = _dot(xn, wmain_ref[...])
    q = p[:, 0:KEY_W] * (DK ** -0.5)
    k = p[:, KEY_W:2 * KEY_W]
    v = p[:, 2 * KEY_W:2 * KEY_W + VAL_W]
    og = p[:, 2 * KEY_W + VAL_W:2 * KEY_W + 2 * VAL_W]
    u = p[:, 2 * KEY_W + 2 * VAL_W:2 * KEY_W + 2 * VAL_W + POOL_W]
    glr = p[:, MAIN_COLS - LANES:MAIN_COLS]
    ext_s[:, HALO:HALO + tl, :] = u.reshape(ns, tl, POOL_W)

    z = _dot3(glr, wgkh_ref[...], wgkl_ref[...]) + bgk_ref[...]
    g = -(jnp.maximum(-z, 0.0) + jnp.log(1.0 + jnp.exp(-jnp.abs(z)))) / GATE_NORMALIZER
    tri = tri_ref[...]
    g_hi = g.astype(BF16)
    g_r = g - g_hi.astype(F32)
    g_mid = g_r.astype(BF16)
    g_lo = (g_r - g_mid.astype(F32)).astype(BF16)
    G = _dot(tri, g_hi) + _dot(tri, g_mid) + _dot(tri, g_lo)

    sc["kh"][0:HALO, :] = jnp.zeros((HALO, KEY_W), F32)
    sc["gh"][0:HALO, :] = jnp.zeros((HALO, KEY_W), F32)
    sc["kh"][HALO:HALO + R, :] = k
    sc["gh"][HALO:HALO + R, :] = G
    row = lax.broadcasted_iota(jnp.int32, (R, 1), 0)
    row_s = row % S
    row_w = row % W
    col_w = lax.broadcasted_iota(jnp.int32, (1, N_HEADS * W), 1) % W
    head_bcast = hb_ref[...]
    a_all = jnp.zeros((R, N_HEADS * W), F32)
    for d in range(S):
        k_sh = sc["kh"][HALO - d:HALO - d + R, :]
        g_sh = sc["gh"][HALO - d:HALO - d + R, :]
        e = jnp.exp(jnp.where(row_s >= d, G - g_sh, MASKED_EXPONENT))
        term = (q * k_sh * e).astype(BF16)
        spread = _dot(term, head_bcast)
        a_all = jnp.where(col_w == row_w - d, spread, a_all)
    sc["a"][...] = a_all

    if n_off:
        _gla_chunked(cfg, l, q, k, v, G, s0_ref, st_out_ref, bdm_ref, sc)
    else:
        _gla_single_chunk(cfg, q, k, v, G, s0_ref, st_out_ref, sc)

    o = sc["o"][...]
    gn = gn_ref[...]
    o_heads = []
    for h in range(N_HEADS):
        vc = slice(h * DV, (h + 1) * DV)
        og_h = og[:, vc]
        o_heads.append(_rms(o[:, vc], gn) * (og_h * jax.nn.sigmoid(og_h)))

    @pl.when(l == 0)
    def _():
        ext_s[:, 0:HALO - POOL_BUF, :] = jnp.zeros((ns, HALO - POOL_BUF, POOL_W), F32)
        ext_s[:, HALO - POOL_BUF:HALO, :] = buf0_ref[...]

    pos = cfg.start_pos + l * tl + lax.broadcasted_iota(jnp.int32, (1, tl, 1), 1)
    z_groups = []
    for gi, w in enumerate(POOL_WINDOWS):
        gc = slice(gi * POOL_GROUP, (gi + 1) * POOL_GROUP)
        s = ext_s[:, HALO:HALO + tl, gc]
        for dd in range(1, w):
            s = s + ext_s[:, HALO - dd:HALO - dd + tl, gc]
        cnt = jnp.minimum(w, pos + 1).astype(F32)
        dmean = (s / cnt - ext_s[:, HALO:HALO + tl, gc]).reshape(R, POOL_GROUP)
        z_groups.append(_dot(dmean.astype(BF16), wpool_ref[gi]))
    zp = jnp.concatenate(z_groups, axis=1) * pscale_ref[...]

    @pl.when(l == n_l - 1)
    def _():
        bufo_ref[...] = ext_s[:, tl + HALO - POOL_BUF:tl + HALO, :]

    @pl.when(l < n_l - 1)
    def _():
        ext_s[:, 0:HALO, :] = ext_s[:, tl:tl + HALO, :]

    cat = jnp.concatenate(o_heads + [zp], axis=1).astype(BF16)
    x1 = x + _dot(cat, wo_ref[...])
    x1_ref[...] = x1
    hn = _rms(x1, n2_ref[...])
    for c in range(CHUNKS):
        h_ref[pl.ds(c, R, stride=CHUNKS), :] = hn[:, c * LANES:(c + 1) * LANES]
    logits = _dot3(hn, wrh_ref[...], wrl_ref[...]) + br_ref[...]
    lane = lax.broadcasted_iota(jnp.int32, (R, LANES), 1)
    lg = jnp.where(lane < N_EXPERTS, logits, -jnp.inf)
    vals, idxs = [], []
    for _ in range(TOP_K):
        m = jnp.max(lg, axis=1, keepdims=True)
        idx = jnp.min(jnp.where(lg == m, lane, LANES), axis=1, keepdims=True)
        vals.append(m)
        idxs.append(idx)
        lg = jnp.where(lane == idx, -jnp.inf, lg)
    exps = [jnp.exp(vv - vals[0]) for vv in vals]
    den = exps[0] + exps[1] + exps[2] + exps[3]
    ti = jnp.zeros((R, LANES), jnp.int32)
    gt = jnp.zeros((R, LANES), F32)
    for kk in range(TOP_K):
        ti = jnp.where(lane == kk, idxs[kk], ti)
        gt = jnp.where(lane == kk, exps[kk] / den, gt)
    topi_ref[...] = ti[:, 0:TOP_K]
    gates_ref[...] = gt[:, 0:TOP_K]


def _mixer_constants(cfg):
    R, S, n_off, W = _mixer_dims(cfg)
    C = cfg.chunk
    r = jnp.arange(R)
    tri = ((r[:, None] // C == r[None, :] // C) & (r[None, :] <= r[:, None])).astype(BF16)
    head_bcast = (jnp.arange(KEY_W)[:, None] // DK == jnp.arange(N_HEADS * W)[None, :] // W).astype(BF16)
    block_diag = (jnp.arange(VAL_W)[:, None] // DV == jnp.arange(KEY_W)[None, :] // DK).astype(F32)
    return tri, head_bcast, block_diag


def _mixer_call(cfg, n_tok_all, row_block0, x, s0, buf0, weights, aliased):
    B, L, _ = x.shape
    ns, tl, C = cfg.ns, cfg.tl, cfg.chunk
    R, S, n_off, W = _mixer_dims(cfg)
    n_b, n_l = B // ns, L // tl
    assert B % ns == 0 and L % tl == 0 and tl % C == 0 and R % 8 == 0
    assert (n_l == 1 or tl >= HALO) and (n_off == 0 or ns == 1) and (n_off > 0 or tl == C)

    def const(shape):
        return pl.BlockSpec(shape, lambda b, l: (0,) * len(shape))

    state_block = (1, VAL_W, KEY_W) if n_off else (ns, N_HEADS, DV, DK)
    state_spec = pl.BlockSpec(state_block, lambda b, l: (b,) + (0,) * (len(state_block) - 1))
    consts = _mixer_constants(cfg)
    operands = (x, s0, buf0) + tuple(weights) + consts
    in_specs = [
        pl.BlockSpec((ns, tl, D_MODEL), lambda b, l: (b, l, 0)),
        state_spec,
        pl.BlockSpec((ns, POOL_BUF, POOL_W), lambda b, l: (b, 0, 0)),
    ] + [const(w.shape) for w in tuple(weights) + consts] + [pl.BlockSpec(memory_space=pl.ANY)] * len(aliased)

    def tok_block(rows, width):
        return pl.BlockSpec((rows, width), lambda b, l: (row_block0 + b * n_l + l, 0))

    out_specs = [
        tok_block(R, D_MODEL), tok_block(R * CHUNKS, LANES), tok_block(R, TOP_K), tok_block(R, TOP_K),
        state_spec,
        pl.BlockSpec((ns, POOL_BUF, POOL_W), lambda b, l: (b, 0, 0)),
    ]
    out_shape = [
        jax.ShapeDtypeStruct((n_tok_all, D_MODEL), F32),
        jax.ShapeDtypeStruct((n_tok_all * CHUNKS, LANES), F32),
        jax.ShapeDtypeStruct((n_tok_all, TOP_K), jnp.int32),
        jax.ShapeDtypeStruct((n_tok_all, TOP_K), F32),
        jax.ShapeDtypeStruct((B,) + state_block[1:], F32),
        jax.ShapeDtypeStruct((B, POOL_BUF, POOL_W), F32),
    ]
    aliases = {len(operands) + i: i for i in range(len(aliased))}
    return pl.pallas_call(
        functools.partial(_mixer_kernel, cfg),
        grid=(n_b, n_l),
        in_specs=in_specs,
        out_specs=out_specs,
        out_shape=out_shape,
        scratch_shapes=[pltpu.VMEM(shape, dtype) for _, shape, dtype in _scratch_spec(cfg)],
        input_output_aliases=aliases,
        compiler_params=pltpu.CompilerParams(
            dimension_semantics=("arbitrary", "arbitrary"), vmem_limit_bytes=VMEM_LIMIT),
        name="mixer",
    )(*operands, *aliased)


def _route_kernel(topi_ref, lower_ref, pos_ref, counts_ref, cnt_s, carry_s, gstart_s):
    ph = pl.program_id(0)
    i = pl.program_id(1)
    TT = topi_ref.shape[0]
    topi = topi_ref[...]
    lane = lax.broadcasted_iota(jnp.int32, (TT, LANES), 1)
    hot = jnp.zeros((TT, LANES), F32)
    for kk in range(TOP_K):
        hot = hot + (lane == topi[:, kk:kk + 1]).astype(F32)
    colsum = jnp.sum(hot, axis=0, keepdims=True)

    @pl.when((ph == 0) & (i == 0))
    def _():
        cnt_s[...] = jnp.zeros_like(cnt_s)

    @pl.when(ph == 0)
    def _():
        cnt_s[...] = cnt_s[...] + colsum

    @pl.when((ph == 1) & (i == 0))
    def _():
        cnt = cnt_s[...]
        counts_ref[...] = cnt.astype(jnp.int32)
        tiles = jnp.floor((cnt + (ROW_TILE - 1)) * (1.0 / ROW_TILE))
        tiles8 = jnp.broadcast_to(tiles, (8, LANES))
        ur = lax.broadcasted_iota(jnp.int32, (LANES, LANES), 0)
        uc = lax.broadcasted_iota(jnp.int32, (LANES, LANES), 1)
        upper = (ur < uc).astype(BF16)
        t_hi, t_lo = _split2(tiles8)
        excl = _dot(t_hi, upper) + _dot(t_lo, upper)
        gstart_s[...] = excl[0:1, :] * float(ROW_TILE)
        carry_s[...] = jnp.zeros_like(carry_s)

    @pl.when(ph == 1)
    def _():
        rank = _dot(lower_ref[...], hot.astype(BF16)) + carry_s[...] + gstart_s[...]
        out = jnp.zeros((TT, LANES), F32)
        for kk in range(TOP_K):
            pk = jnp.sum(jnp.where(lane == topi[:, kk:kk + 1], rank, 0.0), axis=1, keepdims=True)
            out = jnp.where(lane == kk, pk, out)
        pos_ref[...] = out[:, 0:TOP_K].astype(jnp.int32)
        carry_s[...] = carry_s[...] + colsum


def _route_call(topi):
    T = topi.shape[0]
    assert T % ROUTE_TILE == 0
    r = jnp.arange(ROUTE_TILE)
    lower = (r[None, :] < r[:, None]).astype(BF16)
    return pl.pallas_call(
        _route_kernel,
        grid=(2, T // ROUTE_TILE),
        in_specs=[pl.BlockSpec((ROUTE_TILE, TOP_K), lambda ph, i: (i, 0)),
                  pl.BlockSpec((ROUTE_TILE, ROUTE_TILE), lambda ph, i: (0, 0))],
        out_specs=[pl.BlockSpec((ROUTE_TILE, TOP_K), lambda ph, i: (i * ph, 0)),
                   pl.BlockSpec((1, LANES), lambda ph, i: (0, 0))],
        out_shape=[jax.ShapeDtypeStruct((T, TOP_K), jnp.int32),
                   jax.ShapeDtypeStruct((1, LANES), jnp.int32)],
        scratch_shapes=[pltpu.VMEM((1, LANES), F32)] * 3,
        compiler_params=pltpu.CompilerParams(dimension_semantics=("arbitrary", "arbitrary")),
        name="route",
    )(topi, lower)


def _invperm_kernel(pad_lo_ref, pad_hi_ref, pos_ref, tok_ref):
    i = pl.program_id(0)

    @pl.when(i == 0)
    def _():
        def clear(r, c):
            tok_ref[r] = -1
            return c

        for e in range(N_EXPERTS):
            lax.fori_loop(pad_lo_ref[e], pad_hi_ref[e], clear, 0)

    @pl.when(i > 0)
    def _():
        base = (i - 1) * INV_BLOCK

        def put(n, c):
            tok_ref[pos_ref[n]] = base + n
            return c

        lax.fori_loop(0, INV_BLOCK, put, 0, unroll=32)


def _invperm_call(pad_lo, pad_hi, pos_flat, n_rows):
    n_pairs = pos_flat.shape[0]
    assert n_pairs % INV_BLOCK == 0
    grid_spec = pltpu.PrefetchScalarGridSpec(
        num_scalar_prefetch=2,
        grid=(1 + n_pairs // INV_BLOCK,),
        in_specs=[pl.BlockSpec((INV_BLOCK,), lambda i, lo, hi: (jnp.maximum(i - 1, 0),),
                               memory_space=pltpu.SMEM)],
        out_specs=pl.BlockSpec(memory_space=pltpu.SMEM),
    )
    return pl.pallas_call(
        _invperm_kernel,
        grid_spec=grid_spec,
        out_shape=jax.ShapeDtypeStruct((n_rows,), jnp.int32),
        compiler_params=pltpu.CompilerParams(dimension_semantics=("arbitrary",)),
        name="invperm",
    )(pad_lo, pad_hi, pos_flat)


def _row_copy_in(h_hbm, xb, sem, r, pair):
    tok = jnp.maximum(pair, 0) >> 2
    return pltpu.make_async_copy(h_hbm.at[pl.ds(pl.multiple_of(tok * CHUNKS, CHUNKS), CHUNKS), :],
                                 xb.at[pl.ds(r * CHUNKS, CHUNKS), :], sem)


def _gather_tile(tab_ref, h_hbm, xb, sem, tile, unrolled):
    base = tile * ROW_TILE
    if unrolled:
        for r in range(ROW_TILE):
            _row_copy_in(h_hbm, xb, sem, r, tab_ref[base + r]).start(priority=r % N_DMA_QUEUES)
    else:
        def issue(r, c):
            _row_copy_in(h_hbm, xb, sem, r, tab_ref[base + r]).start()
            return c

        lax.fori_loop(0, ROW_TILE, issue, 0)


def _wait_all_rows(buf, sem):
    pltpu.make_async_copy(buf, buf, sem).wait()


def _expert_mlp(xb, yb, wgu_s, wd_s, bgu_ref, bd_ref):
    xs = jnp.concatenate([xb[pl.ds(c, ROW_TILE, stride=CHUNKS), :] for c in range(CHUNKS)],
                         axis=1).astype(BF16)
    gu = _dot(xs, wgu_s[...]) + bgu_ref[0]
    gate = jnp.minimum(gu[:, 0:D_FF], SWIGLU_LIMIT)
    up = jnp.clip(gu[:, D_FF:2 * D_FF], -SWIGLU_LIMIT, SWIGLU_LIMIT)
    act = ((up + 1.0) * gate * jax.nn.sigmoid(SWIGLU_ALPHA * gate)).astype(BF16)
    y = _dot(act, wd_s[...]) + bd_ref[0]
    for c in range(CHUNKS):
        yb[pl.ds(c, ROW_TILE, stride=CHUNKS), :] = y[:, c * LANES:(c + 1) * LANES]


def _experts_kernel(te_ref, nt_ref, nxt_ref, tab_ref, h_hbm, wgu_hbm, bgu_ref, wd_hbm, bd_ref, y_hbm,
                    x0, x1, x2, y0, y1, y2, gsem, ssem, wgu_s, wd_s, wgu_f, wd_f, wsem):
    s = pl.program_id(0)
    n_used = nt_ref[0]
    par = s % N_BUF
    tc = jnp.clip(s - COMPUTE_LAG, 0, te_ref.shape[0] - 1)
    xs, ys = (x0, x1, x2), (y0, y1, y2)
    computing = (s >= COMPUTE_LAG) & (s < n_used + COMPUTE_LAG)

    def weight_copies(e):
        return (pltpu.make_async_copy(wgu_hbm.at[e], wgu_f, wsem.at[0]),
                pltpu.make_async_copy(wd_hbm.at[e], wd_f, wsem.at[1]))

    @pl.when(s == 0)
    def _():
        for cp in weight_copies(te_ref[0]):
            cp.start()

    @pl.when(computing & ((s == COMPUTE_LAG) | (te_ref[tc] != te_ref[jnp.maximum(tc - 1, 0)])))
    def _():
        for cp in weight_copies(te_ref[tc]):
            cp.wait()
        wgu_s[...] = wgu_f[...].astype(BF16)
        wd_s[...] = wd_f[...].astype(BF16)
        nxt = nxt_ref[te_ref[tc]]

        @pl.when(nxt >= 0)
        def _():
            for cp in weight_copies(nxt):
                cp.start()

    def write_out(c):
        rows = pl.ds(pl.multiple_of(tc * (ROW_TILE * CHUNKS), ROW_TILE * CHUNKS), ROW_TILE * CHUNKS)
        pltpu.make_async_copy(ys[c], y_hbm.at[rows, :], ssem.at[c]).start(priority=WRITE_DMA_QUEUE)

    last = n_used + COMPUTE_LAG - 1
    steady = (s >= COMPUTE_LAG) & (s < n_used)
    edge = jnp.logical_not(steady) & (s <= last)
    for p in range(N_BUF):
        c = (p + N_BUF - COMPUTE_LAG) % N_BUF
        mine = par == p

        @pl.when(mine & (s >= COMPUTE_LAG + N_BUF) & (s <= last))
        def _(c=c):
            _wait_all_rows(ys[c], ssem.at[c])

        @pl.when(mine & steady)
        def _(p=p, c=c):
            _wait_all_rows(xs[c], gsem.at[c])
            _gather_tile(tab_ref, h_hbm, xs[p], gsem.at[p], s, True)
            _expert_mlp(xs[c], ys[c], wgu_s, wd_s, bgu_ref, bd_ref)
            write_out(c)

        @pl.when(mine & edge & computing)
        def _(c=c):
            _wait_all_rows(xs[c], gsem.at[c])

        @pl.when(mine & edge & (s < n_used))
        def _(p=p):
            _gather_tile(tab_ref, h_hbm, xs[p], gsem.at[p], s, False)

        @pl.when(mine & edge & computing)
        def _(c=c):
            _expert_mlp(xs[c], ys[c], wgu_s, wd_s, bgu_ref, bd_ref)
            write_out(c)

    @pl.when(s == last)
    def _():
        for c in range(N_BUF):
            _wait_all_rows(ys[c], ssem.at[c])


def _experts_call(tile_e, n_used, next_e, pair_of_row, h_rows, w_gu, b_gu, w_down, b_down):
    n_tiles = tile_e.shape[0]

    def expert_block(s, te, nt, nxt, tab):
        return (te[jnp.clip(s - COMPUTE_LAG, 0, n_tiles - 1)], 0, 0)

    row_buffer = pltpu.VMEM((ROW_TILE * CHUNKS, LANES), F32)
    grid_spec = pltpu.PrefetchScalarGridSpec(
        num_scalar_prefetch=4,
        grid=(n_tiles + COMPUTE_LAG,),
        in_specs=[
            pl.BlockSpec(memory_space=pl.ANY),
            pl.BlockSpec(memory_space=pl.ANY),
            pl.BlockSpec((1, 1, 2 * D_FF), expert_block),
            pl.BlockSpec(memory_space=pl.ANY),
            pl.BlockSpec((1, 1, D_MODEL), expert_block),
        ],
        out_specs=pl.BlockSpec(memory_space=pl.ANY),
        scratch_shapes=[row_buffer] * N_BUF + [row_buffer] * N_BUF + [
            pltpu.SemaphoreType.DMA((N_BUF,)),
            pltpu.SemaphoreType.DMA((N_BUF,)),
            pltpu.VMEM((D_MODEL, 2 * D_FF), BF16),
            pltpu.VMEM((D_FF, D_MODEL), BF16),
            pltpu.VMEM((D_MODEL, 2 * D_FF), F32),
            pltpu.VMEM((D_FF, D_MODEL), F32),
            pltpu.SemaphoreType.DMA((2,)),
        ],
    )
    return pl.pallas_call(
        _experts_kernel,
        grid_spec=grid_spec,
        out_shape=jax.ShapeDtypeStruct((n_tiles * ROW_TILE * CHUNKS, LANES), F32),
        compiler_params=pltpu.CompilerParams(
            dimension_semantics=("arbitrary",), vmem_limit_bytes=VMEM_LIMIT),
        name="experts",
    )(tile_e, n_used, next_e, pair_of_row, h_rows, w_gu, b_gu.reshape(N_EXPERTS, 1, 2 * D_FF),
      w_down, b_down.reshape(N_EXPERTS, 1, D_MODEL))


def _gather_pairs(pos_ref, y_hbm, gb, sem, tile, unrolled):
    base = tile * (TOK_TILE * TOP_K)

    def copy(n):
        row = pos_ref[base + n]
        dst = ((n % TOP_K) * TOK_TILE + n // TOP_K) * CHUNKS
        if not isinstance(dst, int):
            dst = pl.multiple_of(dst, CHUNKS)
        return pltpu.make_async_copy(y_hbm.at[pl.ds(pl.multiple_of(row * CHUNKS, CHUNKS), CHUNKS), :],
                                     gb.at[pl.ds(dst, CHUNKS), :], sem)

    if unrolled:
        for n in range(TOK_TILE * TOP_K):
            copy(n).start(priority=n % N_DMA_QUEUES)
    else:
        def issue(n, c):
            copy(n).start()
            return c

        lax.fori_loop(0, TOK_TILE * TOP_K, issue, 0)


def _combine(gb, x1_ref, gates_ref, fn_ref, out_ref):
    gates = gates_ref[...]
    cols = []
    for c in range(CHUNKS):
        acc = gb[pl.ds(c, TOK_TILE, stride=CHUNKS), :] * gates[:, 0:1]
        for kk in range(1, TOP_K):
            acc = acc + gb[pl.ds(kk * TOK_TILE * CHUNKS + c, TOK_TILE, stride=CHUNKS), :] * gates[:, kk:kk + 1]
        cols.append(acc)
    out_ref[...] = _rms(x1_ref[...] + jnp.concatenate(cols, axis=1), fn_ref[...])


def _final_kernel(tile0, pos_ref, y_hbm, x1_ref, gates_ref, fn_ref, out_ref, g0, g1, g2, sem):
    s = pl.program_id(0)
    n = pl.num_programs(0) - COMPUTE_LAG
    par = s % N_BUF
    gs = (g0, g1, g2)
    steady = (s >= COMPUTE_LAG) & (s < n)
    for p in range(N_BUF):
        c = (p + N_BUF - COMPUTE_LAG) % N_BUF
        mine = par == p

        @pl.when(mine & steady)
        def _(p=p, c=c):
            _wait_all_rows(gs[c], sem.at[c])
            _gather_pairs(pos_ref, y_hbm, gs[p], sem.at[p], tile0 + s, True)
            _combine(gs[c], x1_ref, gates_ref, fn_ref, out_ref)

        @pl.when(mine & (s < COMPUTE_LAG))
        def _(p=p):
            _gather_pairs(pos_ref, y_hbm, gs[p], sem.at[p], tile0 + s, False)

        @pl.when(mine & (s >= n))
        def _(c=c):
            _wait_all_rows(gs[c], sem.at[c])
            _combine(gs[c], x1_ref, gates_ref, fn_ref, out_ref)


def _final_call(tok0, n_tok, pos_flat, y_rows, x1_all, gates_all, final_norm):
    assert tok0 % TOK_TILE == 0 and n_tok % TOK_TILE == 0 and n_tok // TOK_TILE >= COMPUTE_LAG
    tile0 = tok0 // TOK_TILE

    def tok_block(s, pos):
        return (tile0 + jnp.maximum(s - COMPUTE_LAG, 0), 0)

    pair_buffer = pltpu.VMEM((TOK_TILE * TOP_K * CHUNKS, LANES), F32)
    grid_spec = pltpu.PrefetchScalarGridSpec(
        num_scalar_prefetch=1,
        grid=(n_tok // TOK_TILE + COMPUTE_LAG,),
        in_specs=[
            pl.BlockSpec(memory_space=pl.ANY),
            pl.BlockSpec((TOK_TILE, D_MODEL), tok_block),
            pl.BlockSpec((TOK_TILE, TOP_K), tok_block),
            pl.BlockSpec((1, D_MODEL), lambda s, pos: (0, 0)),
        ],
        out_specs=pl.BlockSpec((TOK_TILE, D_MODEL), lambda s, pos: (jnp.maximum(s - COMPUTE_LAG, 0), 0)),
        scratch_shapes=[pair_buffer] * N_BUF + [pltpu.SemaphoreType.DMA((N_BUF,))],
    )
    return pl.pallas_call(
        functools.partial(_final_kernel, tile0),
        grid_spec=grid_spec,
        out_shape=jax.ShapeDtypeStruct((n_tok, D_MODEL), F32),
        compiler_params=pltpu.CompilerParams(
            dimension_semantics=("arbitrary",), vmem_limit_bytes=VMEM_LIMIT),
        name="final",
    )(pos_flat, y_rows, x1_all, gates_all, final_norm)


def _pick_tile(n, target):
    t = min(n, target)
    while n % t:
        t -= 1
    return t


def kernel(x_prompt, x_sample, state_gla, state_pool, norm1, w_in, w_gk2, b_gk, gla_norm, w_pool,
           pool_scale, w_o, norm2, w_router, b_router, w_gate_up, b_gate_up, w_down, b_down, final_norm):
    depth = w_in.shape[0]
    assert depth == 1
    B, L, _ = x_prompt.shape
    BS, LS, _ = x_sample.shape
    n_p, n_s = B * L, BS * LS
    n_all = n_p + n_s

    wi = w_in[0]
    o_glr = 2 * KEY_W + 2 * VAL_W
    wmain = jnp.concatenate(
        [wi[:, 0:o_glr], wi[:, o_glr + GATE_RANK:], wi[:, o_glr:o_glr + GATE_RANK],
         jnp.zeros((D_MODEL, LANES - GATE_RANK), F32)], axis=1).astype(BF16)
    wgk = jnp.concatenate([w_gk2[0], jnp.zeros((LANES - GATE_RANK, KEY_W), F32)], axis=0)
    wgkh, wgkl = _split2(wgk)
    wr = jnp.concatenate([w_router[0], jnp.zeros((D_MODEL, LANES - N_EXPERTS), F32)], axis=1)
    wrh, wrl = _split2(wr)
    br = jnp.concatenate([b_router[0], jnp.zeros((LANES - N_EXPERTS,), F32)]).reshape(1, LANES)
    weights = (norm1[0].reshape(1, D_MODEL), wmain, wgkh, wgkl, b_gk[0].reshape(1, KEY_W),
               gla_norm[0].reshape(1, DV), w_pool[0].astype(BF16), pool_scale[0].reshape(1, POOL_W),
               w_o[0].astype(BF16), norm2[0].reshape(1, D_MODEL), wrh, wrl, br)

    assert L % GLA_CHUNK == 0 and LS in (8, 16)
    cfg_p = MixerCfg(ns=1, tl=_pick_tile(L, PROMPT_TILE), chunk=GLA_CHUNK, start_pos=0, n_alias=0)
    s0_p = jnp.zeros((B, VAL_W, KEY_W), F32)
    buf0_p = jnp.zeros((B, POOL_BUF, POOL_W), F32)
    x1_all, h_all, topi_all, gates_all, st_p, buf_p = _mixer_call(
        cfg_p, n_all, 0, x_prompt, s0_p, buf0_p, weights, ())

    cfg_s = MixerCfg(ns=_pick_tile(BS, 128 // LS), tl=LS, chunk=LS, start_pos=PAST_LEN, n_alias=4)
    r_s = cfg_s.ns * cfg_s.tl
    assert n_p % r_s == 0
    x1_all, h_all, topi_all, gates_all, st_s, buf_s = _mixer_call(
        cfg_s, n_all, n_p // r_s, x_sample, jnp.swapaxes(state_gla[0], -1, -2), state_pool[0], weights,
        (x1_all, h_all, topi_all, gates_all))

    pos, counts = _route_call(topi_all)
    n_tiles = (n_all * TOP_K + N_EXPERTS * (ROW_TILE - 1)) // ROW_TILE
    tiles_per_e = (counts[0, :N_EXPERTS] + (ROW_TILE - 1)) // ROW_TILE
    ends = jnp.cumsum(tiles_per_e)
    n_used = ends[-1].astype(jnp.int32)
    tile_ids = jnp.minimum(jnp.arange(n_tiles, dtype=jnp.int32), n_used - 1)
    tile_e = jnp.sum(tile_ids[:, None] >= ends[None, :], axis=1).astype(jnp.int32)
    e_ids = jnp.arange(N_EXPERTS, dtype=jnp.int32)
    later = (tiles_per_e[None, :] > 0) & (e_ids[None, :] > e_ids[:, None])
    next_e = jnp.min(jnp.where(later, e_ids[None, :], N_EXPERTS), axis=1)
    next_e = jnp.where(next_e == N_EXPERTS, -1, next_e).astype(jnp.int32)
    pos_flat = pos.reshape(n_all * TOP_K)
    pad_hi = (ends * ROW_TILE).astype(jnp.int32)
    pad_lo = pad_hi - (tiles_per_e * ROW_TILE - counts[0, :N_EXPERTS]).astype(jnp.int32)
    pair_of_row = _invperm_call(pad_lo, pad_hi, pos_flat, n_tiles * ROW_TILE)

    y_rows = _experts_call(tile_e, n_used.reshape(1), next_e, pair_of_row, h_all,
                           w_gate_up[0], b_gate_up[0], w_down[0], b_down[0])

    fn = final_norm.reshape(1, D_MODEL)
    y_p = _final_call(0, n_p, pos_flat, y_rows, x1_all, gates_all, fn)
    y_s = _final_call(n_p, n_s, pos_flat, y_rows, x1_all, gates_all, fn)

    st_p = jnp.stack([st_p[:, h * DV:(h + 1) * DV, h * DK:(h + 1) * DK] for h in range(N_HEADS)], axis=1)
    return (y_p.reshape(B, L, D_MODEL), y_s.reshape(BS, LS, D_MODEL),
            jnp.swapaxes(st_p, -1, -2)[None], buf_p[None],
            jnp.swapaxes(st_s, -1, -2)[None], buf_s[None])
```

```python
import functools
from typing import NamedTuple

import jax
import jax.numpy as jnp
from jax import lax
from jax.experimental import pallas as pl
from jax.experimental.pallas import tpu as pltpu

F32 = jnp.float32
BF16 = jnp.bfloat16

D_MODEL = 1024
N_HEADS = 4
DK = 64
DV = 128
KEY_W = N_HEADS * DK
VAL_W = N_HEADS * DV
GATE_RANK = 16
GATE_NORMALIZER = 16.0
GLA_CHUNK = 64
POOL_WINDOWS = (2, 4, 8, 16)
POOL_W = 512
POOL_GROUP = 128
POOL_BUF = 15
N_EXPERTS = 32
TOP_K = 4
D_FF = 1024
SWIGLU_LIMIT = 7.0
SWIGLU_ALPHA = 1.702
EPS = 1e-5
PAST_LEN = 16384

LANES = 128
CHUNKS = D_MODEL // LANES
HALO = 16
DIAG = 16
MAIN_COLS = 2 * KEY_W + 2 * VAL_W + POOL_W + LANES
PROMPT_TILE = 512
ROW_TILE = 256
SAMPLE_TILE = 256
TOK_TILE = 128
ROUTE_TILE = 1024
INV_BLOCK = 2048
N_DMA_QUEUES = 2
WRITE_DMA_QUEUE = 1
N_BUF = 3
COMPUTE_LAG = 2
VMEM_LIMIT = 56 * 1024 * 1024
MASKED_EXPONENT = -1e30


def _dot(a, b):
    return jnp.dot(a, b, preferred_element_type=F32)


def _dot_nt(a, b):
    return lax.dot_general(a, b, (((1,), (1,)), ((), ())), preferred_element_type=F32)


def _dot_tn(a, b):
    return lax.dot_general(a, b, (((0,), (0,)), ((), ())), preferred_element_type=F32)


def _split2(a):
    hi = a.astype(BF16)
    lo = (a - hi.astype(F32)).astype(BF16)
    return hi, lo


def _dot3(a, b_hi, b_lo):
    a_hi, a_lo = _split2(a)
    return _dot(a_hi, b_hi) + _dot(a_lo, b_hi) + _dot(a_hi, b_lo)


def _rms(x, w):
    return x * lax.rsqrt(jnp.mean(x * x, axis=-1, keepdims=True) + EPS) * w


class MixerCfg(NamedTuple):
    ns: int
    tl: int
    chunk: int
    start_pos: int
    n_alias: int


def _mixer_dims(cfg):
    rows = cfg.ns * cfg.tl
    diag = min(cfg.chunk, DIAG)
    n_off = cfg.chunk // diag - 1
    width = cfg.chunk if n_off else rows
    return rows, diag, n_off, width


def _scratch_spec(cfg):
    R, S, n_off, W = _mixer_dims(cfg)
    C, ns, tl = cfg.chunk, cfg.ns, cfg.tl
    spec = [
        ("ext", (ns, tl + HALO, POOL_W), F32),
        ("kh", (R + HALO, KEY_W), F32),
        ("gh", (R + HALO, KEY_W), F32),
        ("egl", (R, KEY_W), F32),
        ("a", (R, N_HEADS * W), F32),
        ("o", (R, VAL_W), F32),
    ]
    if n_off:
        nc = R // C
        spec += [
            ("st", (VAL_W, KEY_W), F32),
            ("qg", (R, KEY_W), BF16),
            ("kd", (R, KEY_W), BF16),
            ("v", (R, VAL_W), BF16),
            ("qcat", (R, n_off * KEY_W), BF16),
            ("kbd", (nc, N_HEADS * C, n_off * KEY_W), BF16),
            ("vbd", (nc, N_HEADS * C, VAL_W), BF16),
        ]
    else:
        spec += [
            ("qg", (R, KEY_W), F32),
            ("kd", (R, KEY_W), F32),
            ("v", (R, VAL_W), F32),
        ]
    return spec


def _gla_chunked(cfg, l, q, k, v, G, s0_ref, st_out_ref, bdm_ref, sc):
    R, S, n_off, W = _mixer_dims(cfg)
    C = cfg.chunk
    nc = R // C
    n_l = pl.num_programs(1)
    G3 = G.reshape(nc, C, KEY_W)
    glast = jnp.broadcast_to(G3[:, C - 1:C, :], (nc, C, KEY_W)).reshape(R, KEY_W)
    sc["qg"][...] = (q * jnp.exp(G)).astype(BF16)
    sc["kd"][...] = (k * jnp.exp(glast - G)).astype(BF16)
    sc["egl"][...] = jnp.exp(glast)
    v_bf = v.astype(BF16)
    sc["v"][...] = v_bf

    row_c = lax.broadcasted_iota(jnp.int32, (R, 1), 0) % C
    q_parts, k_parts = [], []
    for a in range(1, n_off + 1):
        ra = jnp.broadcast_to(G3[:, a * S - 1:a * S, :], (nc, C, KEY_W)).reshape(R, KEY_W)
        in_block = (row_c >= a * S) & (row_c < (a + 1) * S)
        q_parts.append(jnp.where(in_block, q * jnp.exp(jnp.minimum(G - ra, 0.0)), 0.0))
        k_parts.append(jnp.where(row_c < a * S, k * jnp.exp(jnp.minimum(ra - G, 0.0)), 0.0))
    sc["qcat"][...] = jnp.concatenate(q_parts, axis=1).astype(BF16)
    kcat = jnp.concatenate(k_parts, axis=1).astype(BF16)
    head_of_k = (lax.broadcasted_iota(jnp.int32, (1, n_off * KEY_W), 1) % KEY_W) // DK
    head_of_v = lax.broadcasted_iota(jnp.int32, (1, VAL_W), 1) // DV
    for h in range(N_HEADS):
        sc["kbd"][:, h * C:(h + 1) * C, :] = jnp.where(head_of_k == h, kcat, 0.0).reshape(nc, C, n_off * KEY_W)
        sc["vbd"][:, h * C:(h + 1) * C, :] = jnp.where(head_of_v == h, v_bf, 0.0).reshape(nc, C, VAL_W)

    @pl.when(l == 0)
    def _():
        sc["st"][...] = s0_ref[0]

    def chunk_body(c, carry):
        rows = pl.ds(pl.multiple_of(c * C, C), C)
        st = sc["st"][...]
        a_all = sc["a"][rows, :] + _dot_nt(sc["qcat"][rows, :], sc["kbd"][c])
        sc["o"][rows, :] = (_dot(a_all.astype(BF16), sc["vbd"][c])
                            + _dot_nt(sc["qg"][rows, :], st.astype(BF16)))
        upd = _dot_tn(sc["v"][rows, :], sc["kd"][rows, :])
        sc["st"][...] = st * sc["egl"][pl.ds(c * C, 1), :] + upd * bdm_ref[...]
        return carry

    lax.fori_loop(0, nc, chunk_body, 0)

    @pl.when(l == n_l - 1)
    def _():
        st_out_ref[0] = sc["st"][...]


def _gla_single_chunk(cfg, q, k, v, G, s0_ref, st_out_ref, sc):
    R, S, n_off, W = _mixer_dims(cfg)
    C = cfg.chunk
    G3 = G.reshape(R // C, C, KEY_W)
    glast = jnp.broadcast_to(G3[:, C - 1:C, :], (R // C, C, KEY_W)).reshape(R, KEY_W)
    sc["qg"][...] = q * jnp.exp(G)
    sc["kd"][...] = k * jnp.exp(glast - G)
    sc["egl"][...] = jnp.exp(glast)
    sc["v"][...] = v

    a_bf = sc["a"][...].astype(BF16)
    v_bf = v.astype(BF16)
    for h in range(N_HEADS):
        sc["o"][:, h * DV:(h + 1) * DV] = _dot(a_bf[:, h * W:(h + 1) * W], v_bf[:, h * DV:(h + 1) * DV])

    kpad = max(C, 16)

    def seq_body(c, carry):
        cs = pl.multiple_of(c * C, C)
        rows = pl.ds(cs, C)
        for h in range(N_HEADS):
            kc = slice(h * DK, (h + 1) * DK)
            vc = slice(h * DV, (h + 1) * DV)
            st = s0_ref[c, h].T
            sc["o"][rows, vc] = sc["o"][rows, vc] + _dot_nt(sc["qg"][rows, kc].astype(BF16), st.astype(BF16))
            v_h = sc["v"][rows, vc]
            kd_h = sc["kd"][rows, kc]
            if kpad > C:
                v_h = jnp.concatenate([v_h, jnp.zeros((kpad - C, DV), F32)], axis=0)
                kd_h = jnp.concatenate([kd_h, jnp.zeros((kpad - C, DK), F32)], axis=0)
            st_new = st * sc["egl"][pl.ds(cs, 1), kc] + _dot_tn(v_h.astype(BF16), kd_h.astype(BF16))
            st_out_ref[c, h] = st_new.T
        return carry

    lax.fori_loop(0, R // C, seq_body, 0)


def _mixer_kernel(cfg, x_ref, s0_ref, buf0_ref, n1_ref, wmain_ref, wgkh_ref, wgkl_ref, bgk_ref,
                  gn_ref, wpool_ref, pscale_ref, wo_ref, n2_ref, wrh_ref, wrl_ref, br_ref,
                  tri_ref, hb_ref, bdm_ref, *rest):
    rest = rest[cfg.n_alias:]
    x1_ref, h_ref, topi_ref, gates_ref, st_out_ref, bufo_ref = rest[:6]
    sc = dict(zip([name for name, _, _ in _scratch_spec(cfg)], rest[6:]))

    ns, tl, C = cfg.ns, cfg.tl, cfg.chunk
    R, S, n_off, W = _mixer_dims(cfg)
    l = pl.program_id(1)
    n_l = pl.num_programs(1)
    ext_s = sc["ext"]

    x = x_ref[...].reshape(R, D_MODEL)
    xn = _rms(x, n1_ref[...]).astype(BF16)
    p = _dot(xn, wmain_ref[...])
    q = p[:, 0:KEY_W] * (DK ** -0.5)
    k = p[:, KEY_W:2 * KEY_W]
    v = p[:, 2 * KEY_W:2 * KEY_W + VAL_W]
    og = p[:, 2 * KEY_W + VAL_W:2 * KEY_W + 2 * VAL_W]
    u = p[:, 2 * KEY_W + 2 * VAL_W:2 * KEY_W + 2 * VAL_W + POOL_W]
    glr = p[:, MAIN_COLS - LANES:MAIN_COLS]
    ext_s[:, HALO:HALO + tl, :] = u.reshape(ns, tl, POOL_W)

    z = _dot3(glr, wgkh_ref[...], wgkl_ref[...]) + bgk_ref[...]
    g = -(jnp.maximum(-z, 0.0) + jnp.log(1.0 + jnp.exp(-jnp.abs(z)))) / GATE_NORMALIZER
    tri = tri_ref[...]
    g_hi = g.astype(BF16)
    g_r = g - g_hi.astype(F32)
    g_mid = g_r.astype(BF16)
    g_lo = (g_r - g_mid.astype(F32)).astype(BF16)
    G = _dot(tri, g_hi) + _dot(tri, g_mid) + _dot(tri, g_lo)

    sc["kh"][0:HALO, :] = jnp.zeros((HALO, KEY_W), F32)
    sc["gh"][0:HALO, :] = jnp.zeros((HALO, KEY_W), F32)
    sc["kh"][HALO:HALO + R, :] = k
    sc["gh"][HALO:HALO + R, :] = G
    row = lax.broadcasted_iota(jnp.int32, (R, 1), 0)
    row_s = row % S
    row_w = row % W
    col_w = lax.broadcasted_iota(jnp.int32, (1, N_HEADS * W), 1) % W
    head_bcast = hb_ref[...]
    a_all = jnp.zeros((R, N_HEADS * W), F32)
    for d in range(S):
        k_sh = sc["kh"][HALO - d:HALO - d + R, :]
        g_sh = sc["gh"][HALO - d:HALO - d + R, :]
        e = jnp.exp(jnp.where(row_s >= d, G - g_sh, MASKED_EXPONENT))
        term = (q * k_sh * e).astype(BF16)
        spread = _dot(term, head_bcast)
        a_all = jnp.where(col_w == row_w - d, spread, a_all)
    sc["a"][...] = a_all

    if n_off:
        _gla_chunked(cfg, l, q, k, v, G, s0_ref, st_out_ref, bdm_ref, sc)
    else:
        _gla_single_chunk(cfg, q, k, v, G, s0_ref, st_out_ref, sc)

    o = sc["o"][...]
    gn = gn_ref[...]
    o_heads = []
    for h in range(N_HEADS):
        vc = slice(h * DV, (h + 1) * DV)
        og_h = og[:, vc]
        o_heads.append(_rms(o[:, vc], gn) * (og_h * jax.nn.sigmoid(og_h)))

    @pl.when(l == 0)
    def _():
        ext_s[:, 0:HALO - POOL_BUF, :] = jnp.zeros((ns, HALO - POOL_BUF, POOL_W), F32)
        ext_s[:, HALO - POOL_BUF:HALO, :] = buf0_ref[...]

    pos = cfg.start_pos + l * tl + lax.broadcasted_iota(jnp.int32, (1, tl, 1), 1)
    z_groups = []
    for gi, w in enumerate(POOL_WINDOWS):
        gc = slice(gi * POOL_GROUP, (gi + 1) * POOL_GROUP)
        s = ext_s[:, HALO:HALO + tl, gc]
        for dd in range(1, w):
            s = s + ext_s[:, HALO - dd:HALO - dd + tl, gc]
        cnt = jnp.minimum(w, pos + 1).astype(F32)
        dmean = (s / cnt - ext_s[:, HALO:HALO + tl, gc]).reshape(R, POOL_GROUP)
        z_groups.append(_dot(dmean.astype(BF16), wpool_ref[gi]))
    zp = jnp.concatenate(z_groups, axis=1) * pscale_ref[...]

    @pl.when(l == n_l - 1)
    def _():
        bufo_ref[...] = ext_s[:, tl + HALO - POOL_BUF:tl + HALO, :]

    @pl.when(l < n_l - 1)
    def _():
        ext_s[:, 0:HALO, :] = ext_s[:, tl:tl + HALO, :]

    cat = jnp.concatenate(o_heads + [zp], axis=1).astype(BF16)
    x1 = x + _dot(cat, wo_ref[...])
    x1_ref[...] = x1
    hn = _rms(x1, n2_ref[...])
    for c in range(CHUNKS):
        h_ref[pl.ds(c, R, stride=CHUNKS), :] = hn[:, c * LANES:(c + 1) * LANES]
    logits = _dot3(hn, wrh_ref[...], wrl_ref[...]) + br_ref[...]
    lane = lax.broadcasted_iota(jnp.int32, (R, LANES), 1)
    lg = jnp.where(lane < N_EXPERTS, logits, -jnp.inf)
    vals, idxs = [], []
    for _ in range(TOP_K):
        m = jnp.max(lg, axis=1, keepdims=True)
        idx = jnp.min(jnp.where(lg == m, lane, LANES), axis=1, keepdims=True)
        vals.append(m)
        idxs.append(idx)
        lg = jnp.where(lane == idx, -jnp.inf, lg)
    exps = [jnp.exp(vv - vals[0]) for vv in vals]
    den = exps[0] + exps[1] + exps[2] + exps[3]
    ti = jnp.zeros((R, LANES), jnp.int32)
    gt = jnp.zeros((R, LANES), F32)
    for kk in range(TOP_K):
        ti = jnp.where(lane == kk, idxs[kk], ti)
        gt = jnp.where(lane == kk, exps[kk] / den, gt)
    topi_ref[...] = ti[:, 0:TOP_K]
    gates_ref[...] = gt[:, 0:TOP_K]


def _mixer_constants(cfg):
    R, S, n_off, W = _mixer_dims(cfg)
    C = cfg.chunk
    r = jnp.arange(R)
    tri = ((r[:, None] // C == r[None, :] // C) & (r[None, :] <= r[:, None])).astype(BF16)
    head_bcast = (jnp.arange(KEY_W)[:, None] // DK == jnp.arange(N_HEADS * W)[None, :] // W).astype(BF16)
    block_diag = (jnp.arange(VAL_W)[:, None] // DV == jnp.arange(KEY_W)[None, :] // DK).astype(F32)
    return tri, head_bcast, block_diag


def _mixer_call(cfg, n_tok_all, row_block0, x, s0, buf0, weights, aliased):
    B, L, _ = x.shape
    ns, tl, C = cfg.ns, cfg.tl, cfg.chunk
    R, S, n_off, W = _mixer_dims(cfg)
    n_b, n_l = B // ns, L // tl
    assert B % ns == 0 and L % tl == 0 and tl % C == 0 and R % 8 == 0
    assert (n_l == 1 or tl >= HALO) and (n_off == 0 or ns == 1) and (n_off > 0 or tl == C)

    def const(shape):
        return pl.BlockSpec(shape, lambda b, l: (0,) * len(shape))

    state_block = (1, VAL_W, KEY_W) if n_off else (ns, N_HEADS, DK, DV)
    state_spec = pl.BlockSpec(state_block, lambda b, l: (b,) + (0,) * (len(state_block) - 1))
    consts = _mixer_constants(cfg)
    operands = (x, s0, buf0) + tuple(weights) + consts
    in_specs = [
        pl.BlockSpec((ns, tl, D_MODEL), lambda b, l: (b, l, 0)),
        state_spec,
        pl.BlockSpec((ns, POOL_BUF, POOL_W), lambda b, l: (b, 0, 0)),
    ] + [const(w.shape) for w in tuple(weights) + consts] + [pl.BlockSpec(memory_space=pl.ANY)] * len(aliased)

    def tok_block(rows, width):
        return pl.BlockSpec((rows, width), lambda b, l: (row_block0 + b * n_l + l, 0))

    out_specs = [
        tok_block(R, D_MODEL), tok_block(R * CHUNKS, LANES), tok_block(R, TOP_K), tok_block(R, TOP_K),
        state_spec,
        pl.BlockSpec((ns, POOL_BUF, POOL_W), lambda b, l: (b, 0, 0)),
    ]
    out_shape = [
        jax.ShapeDtypeStruct((n_tok_all, D_MODEL), F32),
        jax.ShapeDtypeStruct((n_tok_all * CHUNKS, LANES), F32),
        jax.ShapeDtypeStruct((n_tok_all, TOP_K), jnp.int32),
        jax.ShapeDtypeStruct((n_tok_all, TOP_K), F32),
        jax.ShapeDtypeStruct((B,) + state_block[1:], F32),
        jax.ShapeDtypeStruct((B, POOL_BUF, POOL_W), F32),
    ]
    aliases = {len(operands) + i: i for i in range(len(aliased))}
    return pl.pallas_call(
        functools.partial(_mixer_kernel, cfg),
        grid=(n_b, n_l),
        in_specs=in_specs,
        out_specs=out_specs,
        out_shape=out_shape,
        scratch_shapes=[pltpu.VMEM(shape, dtype) for _, shape, dtype in _scratch_spec(cfg)],
        input_output_aliases=aliases,
        compiler_params=pltpu.CompilerParams(
            dimension_semantics=("arbitrary", "arbitrary"), vmem_limit_bytes=VMEM_LIMIT),
        name="mixer",
    )(*operands, *aliased)


def _route_kernel(topi_ref, lower_ref, pos_ref, counts_ref, cnt_s, carry_s, gstart_s):
    ph = pl.program_id(0)
    i = pl.program_id(1)
    TT = topi_ref.shape[0]
    topi = topi_ref[...]
    lane = lax.broadcasted_iota(jnp.int32, (TT, LANES), 1)
    hot = jnp.zeros((TT, LANES), F32)
    for kk in range(TOP_K):
        hot = hot + (lane == topi[:, kk:kk + 1]).astype(F32)
    colsum = jnp.sum(hot, axis=0, keepdims=True)

    @pl.when((ph == 0) & (i == 0))
    def _():
        cnt_s[...] = jnp.zeros_like(cnt_s)

    @pl.when(ph == 0)
    def _():
        cnt_s[...] = cnt_s[...] + colsum

    @pl.when((ph == 1) & (i == 0))
    def _():
        cnt = cnt_s[...]
        counts_ref[...] = cnt.astype(jnp.int32)
        tiles = jnp.floor((cnt + (ROW_TILE - 1)) * (1.0 / ROW_TILE))
        tiles8 = jnp.broadcast_to(tiles, (8, LANES))
        ur = lax.broadcasted_iota(jnp.int32, (LANES, LANES), 0)
        uc = lax.broadcasted_iota(jnp.int32, (LANES, LANES), 1)
        upper = (ur < uc).astype(BF16)
        t_hi, t_lo = _split2(tiles8)
        excl = _dot(t_hi, upper) + _dot(t_lo, upper)
        gstart_s[...] = excl[0:1, :] * float(ROW_TILE)
        carry_s[...] = jnp.zeros_like(carry_s)

    @pl.when(ph == 1)
    def _():
        rank = _dot(lower_ref[...], hot.astype(BF16)) + carry_s[...] + gstart_s[...]
        out = jnp.zeros((TT, LANES), F32)
        for kk in range(TOP_K):
            pk = jnp.sum(jnp.where(lane == topi[:, kk:kk + 1], rank, 0.0), axis=1, keepdims=True)
            out = jnp.where(lane == kk, pk, out)
        pos_ref[...] = out[:, 0:TOP_K].astype(jnp.int32)
        carry_s[...] = carry_s[...] + colsum


def _route_call(topi):
    T = topi.shape[0]
    assert T % ROUTE_TILE == 0
    r = jnp.arange(ROUTE_TILE)
    lower = (r[None, :] < r[:, None]).astype(BF16)
    return pl.pallas_call(
        _route_kernel,
        grid=(2, T // ROUTE_TILE),
        in_specs=[pl.BlockSpec((ROUTE_TILE, TOP_K), lambda ph, i: (i, 0)),
                  pl.BlockSpec((ROUTE_TILE, ROUTE_TILE), lambda ph, i: (0, 0))],
        out_specs=[pl.BlockSpec((ROUTE_TILE, TOP_K), lambda ph, i: (i * ph, 0)),
                   pl.BlockSpec((1, LANES), lambda ph, i: (0, 0))],
        out_shape=[jax.ShapeDtypeStruct((T, TOP_K), jnp.int32),
                   jax.ShapeDtypeStruct((1, LANES), jnp.int32)],
        scratch_shapes=[pltpu.VMEM((1, LANES), F32)] * 3,
        compiler_params=pltpu.CompilerParams(dimension_semantics=("arbitrary", "arbitrary")),
        name="route",
    )(topi, lower)


def _invperm_kernel(pad_lo_ref, pad_hi_ref, pos_ref, tok_ref):
    i = pl.program_id(0)

    @pl.when(i == 0)
    def _():
        def clear(r, c):
            tok_ref[r] = 0
            return c

        for e in range(N_EXPERTS):
            lax.fori_loop(pad_lo_ref[e], pad_hi_ref[e], clear, 0)

    @pl.when(i > 0)
    def _():
        base = (i - 1) * INV_BLOCK

        def put(n, c):
            tok_ref[pos_ref[n]] = base + n
            return c

        lax.fori_loop(0, INV_BLOCK, put, 0, unroll=32)


def _invperm_call(pad_lo, pad_hi, pos_flat, n_rows):
    n_pairs = pos_flat.shape[0]
    assert n_pairs % INV_BLOCK == 0
    grid_spec = pltpu.PrefetchScalarGridSpec(
        num_scalar_prefetch=2,
        grid=(1 + n_pairs // INV_BLOCK,),
        in_specs=[pl.BlockSpec((INV_BLOCK,), lambda i, lo, hi: (jnp.maximum(i - 1, 0),),
                               memory_space=pltpu.SMEM)],
        out_specs=pl.BlockSpec(memory_space=pltpu.SMEM),
    )
    return pl.pallas_call(
        _invperm_kernel,
        grid_spec=grid_spec,
        out_shape=jax.ShapeDtypeStruct((n_rows,), jnp.int32),
        compiler_params=pltpu.CompilerParams(dimension_semantics=("arbitrary",)),
        name="invperm",
    )(pad_lo, pad_hi, pos_flat)


def _row_copy_in(h_hbm, xb, sem, r, pair):
    tok = pair >> 2
    return pltpu.make_async_copy(h_hbm.at[pl.ds(pl.multiple_of(tok * CHUNKS, CHUNKS), CHUNKS), :],
                                 xb.at[pl.ds(r * CHUNKS, CHUNKS), :], sem)


def _gather_tile(tab_ref, h_hbm, xb, sem, tile, unrolled):
    base = tile * ROW_TILE
    if unrolled:
        for r in range(ROW_TILE):
            _row_copy_in(h_hbm, xb, sem, r, tab_ref[base + r]).start(priority=r % N_DMA_QUEUES)
    else:
        def issue(r, c):
            _row_copy_in(h_hbm, xb, sem, r, tab_ref[base + r]).start()
            return c

        lax.fori_loop(0, ROW_TILE, issue, 0)


def _wait_all_rows(buf, sem):
    pltpu.make_async_copy(buf, buf, sem).wait()


def _expert_mlp(xb, yb, wgu_s, wd_s, bgu_ref, bd_ref):
    xs = jnp.concatenate([xb[pl.ds(c, ROW_TILE, stride=CHUNKS), :] for c in range(CHUNKS)],
                         axis=1).astype(BF16)
    gu = _dot(xs, wgu_s[...]) + bgu_ref[0]
    gate = jnp.minimum(gu[:, 0:D_FF], SWIGLU_LIMIT)
    up = jnp.clip(gu[:, D_FF:2 * D_FF], -SWIGLU_LIMIT, SWIGLU_LIMIT)
    act = ((up + 1.0) * gate * jax.nn.sigmoid(SWIGLU_ALPHA * gate)).astype(BF16)
    y = _dot(act, wd_s[...]) + bd_ref[0]
    for c in range(CHUNKS):
        yb[pl.ds(c, ROW_TILE, stride=CHUNKS), :] = y[:, c * LANES:(c + 1) * LANES]


def _experts_kernel(te_ref, nt_ref, nxt_ref, tab_ref, h_hbm, wgu_hbm, bgu_ref, wd_hbm, bd_ref, y_hbm,
                    x0, x1, x2, y0, y1, y2, gsem, ssem, wgu_s, wd_s, wgu_f, wd_f, wsem):
    s = pl.program_id(0)
    n_used = nt_ref[0]
    par = s % N_BUF
    tc = jnp.clip(s - COMPUTE_LAG, 0, te_ref.shape[0] - 1)
    xs, ys = (x0, x1, x2), (y0, y1, y2)
    computing = (s >= COMPUTE_LAG) & (s < n_used + COMPUTE_LAG)

    def weight_copies(e):
        return (pltpu.make_async_copy(wgu_hbm.at[e], wgu_f, wsem.at[0]),
                pltpu.make_async_copy(wd_hbm.at[e], wd_f, wsem.at[1]))

    @pl.when(s == 0)
    def _():
        for cp in weight_copies(te_ref[0]):
            cp.start()

    @pl.when(computing & ((s == COMPUTE_LAG) | (te_ref[tc] != te_ref[jnp.maximum(tc - 1, 0)])))
    def _():
        for cp in weight_copies(te_ref[tc]):
            cp.wait()
        wgu_s[...] = wgu_f[...].astype(BF16)
        wd_s[...] = wd_f[...].astype(BF16)
        nxt = nxt_ref[te_ref[tc]]

        @pl.when(nxt >= 0)
        def _():
            for cp in weight_copies(nxt):
                cp.start()

    def write_out(c):
        rows = pl.ds(pl.multiple_of(tc * (ROW_TILE * CHUNKS), ROW_TILE * CHUNKS), ROW_TILE * CHUNKS)
        pltpu.make_async_copy(ys[c], y_hbm.at[rows, :], ssem.at[c]).start(priority=WRITE_DMA_QUEUE)

    last = n_used + COMPUTE_LAG - 1
    steady = (s >= COMPUTE_LAG) & (s < n_used)
    edge = jnp.logical_not(steady) & (s <= last)
    for p in range(N_BUF):
        c = (p + N_BUF - COMPUTE_LAG) % N_BUF
        mine = par == p

        @pl.when(mine & (s >= COMPUTE_LAG + N_BUF) & (s <= last))
        def _(c=c):
            _wait_all_rows(ys[c], ssem.at[c])

        @pl.when(mine & steady)
        def _(p=p, c=c):
            _wait_all_rows(xs[c], gsem.at[c])
            _gather_tile(tab_ref, h_hbm, xs[p], gsem.at[p], s, True)
            _expert_mlp(xs[c], ys[c], wgu_s, wd_s, bgu_ref, bd_ref)
            write_out(c)

        @pl.when(mine & edge & computing)
        def _(c=c):
            _wait_all_rows(xs[c], gsem.at[c])

        @pl.when(mine & edge & (s < n_used))
        def _(p=p):
            _gather_tile(tab_ref, h_hbm, xs[p], gsem.at[p], s, False)

        @pl.when(mine & edge & computing)
        def _(c=c):
            _expert_mlp(xs[c], ys[c], wgu_s, wd_s, bgu_ref, bd_ref)
            write_out(c)

    @pl.when(s == last)
    def _():
        for c in range(N_BUF):
            _wait_all_rows(ys[c], ssem.at[c])


def _experts_call(tile_e, n_used, next_e, pair_of_row, h_rows, w_gu, b_gu, w_down, b_down):
    n_tiles = tile_e.shape[0]

    def expert_block(s, te, nt, nxt, tab):
        return (te[jnp.clip(s - COMPUTE_LAG, 0, n_tiles - 1)], 0, 0)

    row_buffer = pltpu.VMEM((ROW_TILE * CHUNKS, LANES), F32)
    grid_spec = pltpu.PrefetchScalarGridSpec(
        num_scalar_prefetch=4,
        grid=(n_tiles + COMPUTE_LAG,),
        in_specs=[
            pl.BlockSpec(memory_space=pl.ANY),
            pl.BlockSpec(memory_space=pl.ANY),
            pl.BlockSpec((1, 1, 2 * D_FF), expert_block),
            pl.BlockSpec(memory_space=pl.ANY),
            pl.BlockSpec((1, 1, D_MODEL), expert_block),
        ],
        out_specs=pl.BlockSpec(memory_space=pl.ANY),
        scratch_shapes=[row_buffer] * N_BUF + [row_buffer] * N_BUF + [
            pltpu.SemaphoreType.DMA((N_BUF,)),
            pltpu.SemaphoreType.DMA((N_BUF,)),
            pltpu.VMEM((D_MODEL, 2 * D_FF), BF16),
            pltpu.VMEM((D_FF, D_MODEL), BF16),
            pltpu.VMEM((D_MODEL, 2 * D_FF), F32),
            pltpu.VMEM((D_FF, D_MODEL), F32),
            pltpu.SemaphoreType.DMA((2,)),
        ],
    )
    return pl.pallas_call(
        _experts_kernel,
        grid_spec=grid_spec,
        out_shape=jax.ShapeDtypeStruct((n_tiles * ROW_TILE * CHUNKS, LANES), F32),
        compiler_params=pltpu.CompilerParams(
            dimension_semantics=("arbitrary",), vmem_limit_bytes=VMEM_LIMIT),
        name="experts",
    )(tile_e, n_used, next_e, pair_of_row, h_rows, w_gu, b_gu.reshape(N_EXPERTS, 1, 2 * D_FF),
      w_down, b_down.reshape(N_EXPERTS, 1, D_MODEL))


def _gather_pairs(pos_ref, y_hbm, gb, sem, tile, unrolled):
    base = tile * (TOK_TILE * TOP_K)

    def copy(n):
        row = pos_ref[base + n]
        dst = ((n % TOP_K) * TOK_TILE + n // TOP_K) * CHUNKS
        if not isinstance(dst, int):
            dst = pl.multiple_of(dst, CHUNKS)
        return pltpu.make_async_copy(y_hbm.at[pl.ds(pl.multiple_of(row * CHUNKS, CHUNKS), CHUNKS), :],
                                     gb.at[pl.ds(dst, CHUNKS), :], sem)

    if unrolled:
        for n in range(TOK_TILE * TOP_K):
            copy(n).start(priority=n % N_DMA_QUEUES)
    else:
        def issue(n, c):
            copy(n).start()
            return c

        lax.fori_loop(0, TOK_TILE * TOP_K, issue, 0)


def _combine(gb, x1_ref, gates_ref, fn_ref, out_ref):
    gates = gates_ref[...]
    cols = []
    for c in range(CHUNKS):
        acc = gb[pl.ds(c, TOK_TILE, stride=CHUNKS), :] * gates[:, 0:1]
        for kk in range(1, TOP_K):
            acc = acc + gb[pl.ds(kk * TOK_TILE * CHUNKS + c, TOK_TILE, stride=CHUNKS), :] * gates[:, kk:kk + 1]
        cols.append(acc)
    out_ref[...] = _rms(x1_ref[...] + jnp.concatenate(cols, axis=1), fn_ref[...])


def _final_kernel(tile0, pos_ref, y_hbm, x1_ref, gates_ref, fn_ref, out_ref, g0, g1, g2, sem):
    s = pl.program_id(0)
    n = pl.num_programs(0) - COMPUTE_LAG
    par = s % N_BUF
    gs = (g0, g1, g2)
    steady = (s >= COMPUTE_LAG) & (s < n)
    for p in range(N_BUF):
        c = (p + N_BUF - COMPUTE_LAG) % N_BUF
        mine = par == p

        @pl.when(mine & steady)
        def _(p=p, c=c):
            _wait_all_rows(gs[c], sem.at[c])
            _gather_pairs(pos_ref, y_hbm, gs[p], sem.at[p], tile0 + s, True)
            _combine(gs[c], x1_ref, gates_ref, fn_ref, out_ref)

        @pl.when(mine & (s < COMPUTE_LAG))
        def _(p=p):
            _gather_pairs(pos_ref, y_hbm, gs[p], sem.at[p], tile0 + s, False)

        @pl.when(mine & (s >= n))
        def _(c=c):
            _wait_all_rows(gs[c], sem.at[c])
            _combine(gs[c], x1_ref, gates_ref, fn_ref, out_ref)


def _final_call(tok0, n_tok, pos_flat, y_rows, x1_all, gates_all, final_norm):
    assert tok0 % TOK_TILE == 0 and n_tok % TOK_TILE == 0 and n_tok // TOK_TILE >= COMPUTE_LAG
    tile0 = tok0 // TOK_TILE

    def tok_block(s, pos):
        return (tile0 + jnp.maximum(s - COMPUTE_LAG, 0), 0)

    pair_buffer = pltpu.VMEM((TOK_TILE * TOP_K * CHUNKS, LANES), F32)
    grid_spec = pltpu.PrefetchScalarGridSpec(
        num_scalar_prefetch=1,
        grid=(n_tok // TOK_TILE + COMPUTE_LAG,),
        in_specs=[
            pl.BlockSpec(memory_space=pl.ANY),
            pl.BlockSpec((TOK_TILE, D_MODEL), tok_block),
            pl.BlockSpec((TOK_TILE, TOP_K), tok_block),
            pl.BlockSpec((1, D_MODEL), lambda s, pos: (0, 0)),
        ],
        out_specs=pl.BlockSpec((TOK_TILE, D_MODEL), lambda s, pos: (jnp.maximum(s - COMPUTE_LAG, 0), 0)),
        scratch_shapes=[pair_buffer] * N_BUF + [pltpu.SemaphoreType.DMA((N_BUF,))],
    )
    return pl.pallas_call(
        functools.partial(_final_kernel, tile0),
        grid_spec=grid_spec,
        out_shape=jax.ShapeDtypeStruct((n_tok, D_MODEL), F32),
        compiler_params=pltpu.CompilerParams(
            dimension_semantics=("arbitrary",), vmem_limit_bytes=VMEM_LIMIT),
        name="final",
    )(pos_flat, y_rows, x1_all, gates_all, final_norm)


def _pick_tile(n, target):
    t = min(n, target)
    while n % t:
        t -= 1
    return t


def kernel(x_prompt, x_sample, state_gla, state_pool, norm1, w_in, w_gk2, b_gk, gla_norm, w_pool,
           pool_scale, w_o, norm2, w_router, b_router, w_gate_up, b_gate_up, w_down, b_down, final_norm):
    depth = w_in.shape[0]
    assert depth == 1
    B, L, _ = x_prompt.shape
    BS, LS, _ = x_sample.shape
    n_p, n_s = B * L, BS * LS
    n_all = n_p + n_s

    wi = w_in[0]
    o_glr = 2 * KEY_W + 2 * VAL_W
    wmain = jnp.concatenate(
        [wi[:, 0:o_glr], wi[:, o_glr + GATE_RANK:], wi[:, o_glr:o_glr + GATE_RANK],
         jnp.zeros((D_MODEL, LANES - GATE_RANK), F32)], axis=1).astype(BF16)
    wgk = jnp.concatenate([w_gk2[0], jnp.zeros((LANES - GATE_RANK, KEY_W), F32)], axis=0)
    wgkh, wgkl = _split2(wgk)
    wr = jnp.concatenate([w_router[0], jnp.zeros((D_MODEL, LANES - N_EXPERTS), F32)], axis=1)
    wrh, wrl = _split2(wr)
    br = jnp.concatenate([b_router[0], jnp.zeros((LANES - N_EXPERTS,), F32)]).reshape(1, LANES)
    weights = (norm1[0].reshape(1, D_MODEL), wmain, wgkh, wgkl, b_gk[0].reshape(1, KEY_W),
               gla_norm[0].reshape(1, DV), w_pool[0].astype(BF16), pool_scale[0].reshape(1, POOL_W),
               w_o[0].astype(BF16), norm2[0].reshape(1, D_MODEL), wrh, wrl, br)

    assert L % GLA_CHUNK == 0 and LS in (8, 16)
    cfg_p = MixerCfg(ns=1, tl=_pick_tile(L, PROMPT_TILE), chunk=GLA_CHUNK, start_pos=0, n_alias=0)
    s0_p = jnp.zeros((B, VAL_W, KEY_W), F32)
    buf0_p = jnp.zeros((B, POOL_BUF, POOL_W), F32)
    x1_all, h_all, topi_all, gates_all, st_p, buf_p = _mixer_call(
        cfg_p, n_all, 0, x_prompt, s0_p, buf0_p, weights, ())

    cfg_s = MixerCfg(ns=_pick_tile(BS, SAMPLE_TILE // LS), tl=LS, chunk=LS, start_pos=PAST_LEN, n_alias=4)
    r_s = cfg_s.ns * cfg_s.tl
    assert n_p % r_s == 0
    x1_all, h_all, topi_all, gates_all, st_s, buf_s = _mixer_call(
        cfg_s, n_all, n_p // r_s, x_sample, state_gla[0], state_pool[0], weights,
        (x1_all, h_all, topi_all, gates_all))

    pos, counts = _route_call(topi_all)
    n_tiles = (n_all * TOP_K + N_EXPERTS * (ROW_TILE - 1)) // ROW_TILE
    tiles_per_e = (counts[0, :N_EXPERTS] + (ROW_TILE - 1)) // ROW_TILE
    ends = jnp.cumsum(tiles_per_e)
    n_used = ends[-1].astype(jnp.int32)
    tile_ids = jnp.minimum(jnp.arange(n_tiles, dtype=jnp.int32), n_used - 1)
    tile_e = jnp.sum(tile_ids[:, None] >= ends[None, :], axis=1).astype(jnp.int32)
    e_ids = jnp.arange(N_EXPERTS, dtype=jnp.int32)
    later = (tiles_per_e[None, :] > 0) & (e_ids[None, :] > e_ids[:, None])
    next_e = jnp.min(jnp.where(later, e_ids[None, :], N_EXPERTS), axis=1)
    next_e = jnp.where(next_e == N_EXPERTS, -1, next_e).astype(jnp.int32)
    pos_flat = pos.reshape(n_all * TOP_K)
    pad_hi = (ends * ROW_TILE).astype(jnp.int32)
    pad_lo = pad_hi - (tiles_per_e * ROW_TILE - counts[0, :N_EXPERTS]).astype(jnp.int32)
    pair_of_row = _invperm_call(pad_lo, pad_hi, pos_flat, n_tiles * ROW_TILE)

    y_rows = _experts_call(tile_e, n_used.reshape(1), next_e, pair_of_row, h_all,
                           w_gate_up[0], b_gate_up[0], w_down[0], b_down[0])

    fn = final_norm.reshape(1, D_MODEL)
    y_p = _final_call(0, n_p, pos_flat, y_rows, x1_all, gates_all, fn)
    y_s = _final_call(n_p, n_s, pos_flat, y_rows, x1_all, gates_all, fn)

    st_p = jnp.stack([st_p[:, h * DV:(h + 1) * DV, h * DK:(h + 1) * DK] for h in range(N_HEADS)], axis=1)
    return (y_p.reshape(B, L, D_MODEL), y_s.reshape(BS, LS, D_MODEL),
            jnp.swapaxes(st_p, -1, -2)[None], buf_p[None],
            st_s[None], buf_s[None])
```

```python
import functools
from typing import NamedTuple

import jax
import jax.numpy as jnp
from jax import lax
from jax.experimental import pallas as pl
from jax.experimental.pallas import tpu as pltpu

F32 = jnp.float32
BF16 = jnp.bfloat16

D_MODEL = 1024
N_HEADS = 4
DK = 64
DV = 128
KEY_W = N_HEADS * DK
VAL_W = N_HEADS * DV
GATE_RANK = 16
GATE_NORMALIZER = 16.0
GLA_CHUNK = 64
POOL_WINDOWS = (2, 4, 8, 16)
POOL_W = 512
POOL_GROUP = 128
POOL_BUF = 15
N_EXPERTS = 32
TOP_K = 4
D_FF = 1024
SWIGLU_LIMIT = 7.0
SWIGLU_ALPHA = 1.702
EPS = 1e-5
PAST_LEN = 16384

LANES = 128
SUBLANES = 8
BF16_ROWS = 16
CHUNKS = D_MODEL // LANES
HALO = 16
DIAG = 16
MAIN_COLS = 2 * KEY_W + 2 * VAL_W + POOL_W + LANES
PROMPT_TILE = 512
ROW_TILE = 256
SAMPLE_TILE = 256
TOK_TILE = 128
ROUTE_TILE = 1024
INV_BLOCK = 4096
N_DMA_QUEUES = 2
WRITE_DMA_QUEUE = 1
N_BUF = 3
COMPUTE_LAG = 2
VMEM_LIMIT = 56 * 1024 * 1024
MASKED_EXPONENT = -1e30


def _dot(a, b):
    return jnp.dot(a, b, preferred_element_type=F32)


def _dot_nt(a, b):
    return lax.dot_general(a, b, (((1,), (1,)), ((), ())), preferred_element_type=F32)


def _dot_tn(a, b):
    return lax.dot_general(a, b, (((0,), (0,)), ((), ())), preferred_element_type=F32)


def _split2(a):
    hi = a.astype(BF16)
    lo = (a - hi.astype(F32)).astype(BF16)
    return hi, lo


def _dot3(a, b_hi, b_lo):
    a_hi, a_lo = _split2(a)
    return _dot(a_hi, b_hi) + _dot(a_lo, b_hi) + _dot(a_hi, b_lo)


def _rms(x, w):
    return x * lax.rsqrt(jnp.mean(x * x, axis=-1, keepdims=True) + EPS) * w


class MixerCfg(NamedTuple):
    ns: int
    tl: int
    chunk: int
    start_pos: int
    n_alias: int


def _mixer_dims(cfg):
    rows = cfg.ns * cfg.tl
    diag = min(cfg.chunk, DIAG)
    n_off = cfg.chunk // diag - 1
    width = cfg.chunk if n_off else rows
    return rows, diag, n_off, width


def _scratch_spec(cfg):
    R, S, n_off, W = _mixer_dims(cfg)
    C, ns, tl = cfg.chunk, cfg.ns, cfg.tl
    spec = [
        ("ext", (ns, tl + HALO, POOL_W), F32),
        ("kh", (R + HALO, KEY_W), F32),
        ("gh", (R + HALO, KEY_W), F32),
        ("egl", (R, KEY_W), F32),
        ("a", (R, N_HEADS * W), F32),
        ("o", (R, VAL_W), F32),
    ]
    if n_off:
        nc = R // C
        spec += [
            ("st", (VAL_W, KEY_W), F32),
            ("qg", (R, KEY_W), BF16),
            ("kd", (R, KEY_W), BF16),
            ("v", (R, VAL_W), BF16),
            ("qcat", (R, n_off * KEY_W), BF16),
            ("kbd", (nc, N_HEADS * C, n_off * KEY_W), BF16),
            ("vbd", (nc, N_HEADS * C, VAL_W), BF16),
        ]
    else:
        spec += [
            ("qg", (R, KEY_W), F32),
            ("kd", (R, KEY_W), F32),
            ("v", (R, VAL_W), F32),
        ]
    return spec


def _gla_chunked(cfg, l, q, k, v, G, s0_ref, st_out_ref, bdm_ref, sc):
    R, S, n_off, W = _mixer_dims(cfg)
    C = cfg.chunk
    nc = R // C
    n_l = pl.num_programs(1)
    G3 = G.reshape(nc, C, KEY_W)
    glast = jnp.broadcast_to(G3[:, C - 1:C, :], (nc, C, KEY_W)).reshape(R, KEY_W)
    sc["qg"][...] = (q * jnp.exp(G)).astype(BF16)
    sc["kd"][...] = (k * jnp.exp(glast - G)).astype(BF16)
    sc["egl"][...] = jnp.exp(glast)
    v_bf = v.astype(BF16)
    sc["v"][...] = v_bf

    row_c = lax.broadcasted_iota(jnp.int32, (R, 1), 0) % C
    q_parts, k_parts = [], []
    for a in range(1, n_off + 1):
        ra = jnp.broadcast_to(G3[:, a * S - 1:a * S, :], (nc, C, KEY_W)).reshape(R, KEY_W)
        in_block = (row_c >= a * S) & (row_c < (a + 1) * S)
        q_parts.append(jnp.where(in_block, q * jnp.exp(jnp.minimum(G - ra, 0.0)), 0.0))
        k_parts.append(jnp.where(row_c < a * S, k * jnp.exp(jnp.minimum(ra - G, 0.0)), 0.0))
    sc["qcat"][...] = jnp.concatenate(q_parts, axis=1).astype(BF16)
    kcat = jnp.concatenate(k_parts, axis=1).astype(BF16)
    head_of_k = (lax.broadcasted_iota(jnp.int32, (1, n_off * KEY_W), 1) % KEY_W) // DK
    head_of_v = lax.broadcasted_iota(jnp.int32, (1, VAL_W), 1) // DV
    for h in range(N_HEADS):
        sc["kbd"][:, h * C:(h + 1) * C, :] = jnp.where(head_of_k == h, kcat, 0.0).reshape(nc, C, n_off * KEY_W)
        sc["vbd"][:, h * C:(h + 1) * C, :] = jnp.where(head_of_v == h, v_bf, 0.0).reshape(nc, C, VAL_W)

    @pl.when(l == 0)
    def _():
        sc["st"][...] = s0_ref[0]

    def chunk_body(c, carry):
        rows = pl.ds(pl.multiple_of(c * C, C), C)
        st = sc["st"][...]
        a_all = sc["a"][rows, :] + _dot_nt(sc["qcat"][rows, :], sc["kbd"][c])
        sc["o"][rows, :] = (_dot(a_all.astype(BF16), sc["vbd"][c])
                            + _dot_nt(sc["qg"][rows, :], st.astype(BF16)))
        upd = _dot_tn(sc["v"][rows, :], sc["kd"][rows, :])
        sc["st"][...] = st * sc["egl"][pl.ds(c * C, 1), :] + upd * bdm_ref[...]
        return carry

    lax.fori_loop(0, nc, chunk_body, 0)

    @pl.when(l == n_l - 1)
    def _():
        st_out_ref[0] = sc["st"][...]


def _gla_single_chunk(cfg, q, k, v, G, s0_ref, st_out_ref, sc):
    R, S, n_off, W = _mixer_dims(cfg)
    C = cfg.chunk
    G3 = G.reshape(R // C, C, KEY_W)
    glast = jnp.broadcast_to(G3[:, C - 1:C, :], (R // C, C, KEY_W)).reshape(R, KEY_W)
    sc["qg"][...] = q * jnp.exp(G)
    sc["kd"][...] = k * jnp.exp(glast - G)
    sc["egl"][...] = jnp.exp(glast)
    sc["v"][...] = v

    a_bf = sc["a"][...].astype(BF16)
    v_bf = v.astype(BF16)
    for h in range(N_HEADS):
        sc["o"][:, h * DV:(h + 1) * DV] = _dot(a_bf[:, h * W:(h + 1) * W], v_bf[:, h * DV:(h + 1) * DV])

    kpad = max(C, BF16_ROWS)

    def seq_body(c, carry):
        cs = pl.multiple_of(c * C, C)
        rows = pl.ds(cs, C)
        for h in range(N_HEADS):
            kc = slice(h * DK, (h + 1) * DK)
            vc = slice(h * DV, (h + 1) * DV)
            st = s0_ref[c, h].T
            sc["o"][rows, vc] = sc["o"][rows, vc] + _dot_nt(sc["qg"][rows, kc].astype(BF16), st.astype(BF16))
            v_h = sc["v"][rows, vc]
            kd_h = sc["kd"][rows, kc]
            if kpad > C:
                v_h = jnp.concatenate([v_h, jnp.zeros((kpad - C, DV), F32)], axis=0)
                kd_h = jnp.concatenate([kd_h, jnp.zeros((kpad - C, DK), F32)], axis=0)
            st_new = st * sc["egl"][pl.ds(cs, 1), kc] + _dot_tn(v_h.astype(BF16), kd_h.astype(BF16))
            st_out_ref[c, h] = st_new.T
        return carry

    lax.fori_loop(0, R // C, seq_body, 0)


def _mixer_kernel(cfg, x_ref, s0_ref, buf0_ref, n1_ref, wmain_ref, wgkh_ref, wgkl_ref, bgk_ref,
                  gn_ref, wpool_ref, pscale_ref, wo_ref, n2_ref, wrh_ref, wrl_ref, br_ref,
                  tri_ref, hb_ref, bdm_ref, *rest):
    rest = rest[cfg.n_alias:]
    x1_ref, h_ref, topi_ref, gates_ref, st_out_ref, bufo_ref = rest[:6]
    sc = dict(zip([name for name, _, _ in _scratch_spec(cfg)], rest[6:]))

    ns, tl, C = cfg.ns, cfg.tl, cfg.chunk
    R, S, n_off, W = _mixer_dims(cfg)
    l = pl.program_id(1)
    n_l = pl.num_programs(1)
    ext_s = sc["ext"]

    x = x_ref[...].reshape(R, D_MODEL)
    xn = _rms(x, n1_ref[...]).astype(BF16)
    p = _dot(xn, wmain_ref[...])
    q = p[:, 0:KEY_W] * (DK ** -0.5)
    k = p[:, KEY_W:2 * KEY_W]
    v = p[:, 2 * KEY_W:2 * KEY_W + VAL_W]
    og = p[:, 2 * KEY_W + VAL_W:2 * KEY_W + 2 * VAL_W]
    u = p[:, 2 * KEY_W + 2 * VAL_W:2 * KEY_W + 2 * VAL_W + POOL_W]
    glr = p[:, MAIN_COLS - LANES:MAIN_COLS]
    ext_s[:, HALO:HALO + tl, :] = u.reshape(ns, tl, POOL_W)

    z = _dot3(glr, wgkh_ref[...], wgkl_ref[...]) + bgk_ref[...]
    g = -(jnp.maximum(-z, 0.0) + jnp.log(1.0 + jnp.exp(-jnp.abs(z)))) / GATE_NORMALIZER
    tri = tri_ref[...]
    g_hi = g.astype(BF16)
    g_r = g - g_hi.astype(F32)
    g_mid = g_r.astype(BF16)
    g_lo = (g_r - g_mid.astype(F32)).astype(BF16)
    G = _dot(tri, g_hi) + _dot(tri, g_mid) + _dot(tri, g_lo)

    sc["kh"][0:HALO, :] = jnp.zeros((HALO, KEY_W), F32)
    sc["gh"][0:HALO, :] = jnp.zeros((HALO, KEY_W), F32)
    sc["kh"][HALO:HALO + R, :] = k
    sc["gh"][HALO:HALO + R, :] = G
    row = lax.broadcasted_iota(jnp.int32, (R, 1), 0)
    row_s = row % S
    row_w = row % W
    col_w = lax.broadcasted_iota(jnp.int32, (1, N_HEADS * W), 1) % W
    head_bcast = hb_ref[...]
    a_all = jnp.zeros((R, N_HEADS * W), F32)
    for d in range(S):
        k_sh = sc["kh"][HALO - d:HALO - d + R, :]
        g_sh = sc["gh"][HALO - d:HALO - d + R, :]
        e = jnp.exp(jnp.where(row_s >= d, G - g_sh, MASKED_EXPONENT))
        term = (q * k_sh * e).astype(BF16)
        spread = _dot(term, head_bcast)
        a_all = jnp.where(col_w == row_w - d, spread, a_all)
    sc["a"][...] = a_all

    if n_off:
        _gla_chunked(cfg, l, q, k, v, G, s0_ref, st_out_ref, bdm_ref, sc)
    else:
        _gla_single_chunk(cfg, q, k, v, G, s0_ref, st_out_ref, sc)

    o = sc["o"][...]
    gn = gn_ref[...]
    o_heads = []
    for h in range(N_HEADS):
        vc = slice(h * DV, (h + 1) * DV)
        og_h = og[:, vc]
        o_heads.append(_rms(o[:, vc], gn) * (og_h * jax.nn.sigmoid(og_h)))

    @pl.when(l == 0)
    def _():
        ext_s[:, 0:HALO - POOL_BUF, :] = jnp.zeros((ns, HALO - POOL_BUF, POOL_W), F32)
        ext_s[:, HALO - POOL_BUF:HALO, :] = buf0_ref[...]

    pos = cfg.start_pos + l * tl + lax.broadcasted_iota(jnp.int32, (1, tl, 1), 1)
    z_groups = []
    for gi, w in enumerate(POOL_WINDOWS):
        gc = slice(gi * POOL_GROUP, (gi + 1) * POOL_GROUP)
        s = ext_s[:, HALO:HALO + tl, gc]
        for dd in range(1, w):
            s = s + ext_s[:, HALO - dd:HALO - dd + tl, gc]
        cnt = jnp.minimum(w, pos + 1).astype(F32)
        dmean = (s / cnt - ext_s[:, HALO:HALO + tl, gc]).reshape(R, POOL_GROUP)
        z_groups.append(_dot(dmean.astype(BF16), wpool_ref[gi]))
    zp = jnp.concatenate(z_groups, axis=1) * pscale_ref[...]

    @pl.when(l == n_l - 1)
    def _():
        bufo_ref[...] = ext_s[:, tl + HALO - POOL_BUF:tl + HALO, :]

    @pl.when(l < n_l - 1)
    def _():
        ext_s[:, 0:HALO, :] = ext_s[:, tl:tl + HALO, :]

    cat = jnp.concatenate(o_heads + [zp], axis=1).astype(BF16)
    x1 = x + _dot(cat, wo_ref[...])
    x1_ref[...] = x1
    hn = _rms(x1, n2_ref[...])
    for c in range(CHUNKS):
        h_ref[pl.ds(c, R, stride=CHUNKS), :] = hn[:, c * LANES:(c + 1) * LANES]
    logits = _dot3(hn, wrh_ref[...], wrl_ref[...]) + br_ref[...]
    lane = lax.broadcasted_iota(jnp.int32, (R, LANES), 1)
    lg = jnp.where(lane < N_EXPERTS, logits, -jnp.inf)
    vals, idxs = [], []
    for _ in range(TOP_K):
        m = jnp.max(lg, axis=1, keepdims=True)
        idx = jnp.min(jnp.where(lg == m, lane, LANES), axis=1, keepdims=True)
        vals.append(m)
        idxs.append(idx)
        lg = jnp.where(lane == idx, -jnp.inf, lg)
    exps = [jnp.exp(vv - vals[0]) for vv in vals]
    den = exps[0] + exps[1] + exps[2] + exps[3]
    ti = jnp.zeros((R, LANES), jnp.int32)
    gt = jnp.zeros((R, LANES), F32)
    for kk in range(TOP_K):
        ti = jnp.where(lane == kk, idxs[kk], ti)
        gt = jnp.where(lane == kk, exps[kk] / den, gt)
    topi_ref[...] = ti[:, 0:TOP_K]
    gates_ref[...] = gt[:, 0:TOP_K]


def _mixer_constants(cfg):
    R, S, n_off, W = _mixer_dims(cfg)
    C = cfg.chunk
    r = jnp.arange(R)
    tri = ((r[:, None] // C == r[None, :] // C) & (r[None, :] <= r[:, None])).astype(BF16)
    head_bcast = (jnp.arange(KEY_W)[:, None] // DK == jnp.arange(N_HEADS * W)[None, :] // W).astype(BF16)
    block_diag = (jnp.arange(VAL_W)[:, None] // DV == jnp.arange(KEY_W)[None, :] // DK).astype(F32)
    return tri, head_bcast, block_diag


def _mixer_call(cfg, n_tok_all, row_block0, x, s0, buf0, weights, aliased):
    B, L, _ = x.shape
    ns, tl, C = cfg.ns, cfg.tl, cfg.chunk
    R, S, n_off, W = _mixer_dims(cfg)
    n_b, n_l = B // ns, L // tl
    assert B % ns == 0 and L % tl == 0 and tl % C == 0 and R % SUBLANES == 0
    assert (n_l == 1 or tl >= HALO) and (n_off == 0 or ns == 1) and (n_off > 0 or tl == C)

    def const(shape):
        return pl.BlockSpec(shape, lambda b, l: (0,) * len(shape))

    state_block = (1, VAL_W, KEY_W) if n_off else (ns, N_HEADS, DK, DV)
    state_spec = pl.BlockSpec(state_block, lambda b, l: (b,) + (0,) * (len(state_block) - 1))
    consts = _mixer_constants(cfg)
    operands = (x, s0, buf0) + tuple(weights) + consts
    in_specs = [
        pl.BlockSpec((ns, tl, D_MODEL), lambda b, l: (b, l, 0)),
        state_spec,
        pl.BlockSpec((ns, POOL_BUF, POOL_W), lambda b, l: (b, 0, 0)),
    ] + [const(w.shape) for w in tuple(weights) + consts] + [pl.BlockSpec(memory_space=pl.ANY)] * len(aliased)

    def tok_block(rows, width):
        return pl.BlockSpec((rows, width), lambda b, l: (row_block0 + b * n_l + l, 0))

    out_specs = [
        tok_block(R, D_MODEL), tok_block(R * CHUNKS, LANES), tok_block(R, TOP_K), tok_block(R, TOP_K),
        state_spec,
        pl.BlockSpec((ns, POOL_BUF, POOL_W), lambda b, l: (b, 0, 0)),
    ]
    out_shape = [
        jax.ShapeDtypeStruct((n_tok_all, D_MODEL), F32),
        jax.ShapeDtypeStruct((n_tok_all * CHUNKS, LANES), F32),
        jax.ShapeDtypeStruct((n_tok_all, TOP_K), jnp.int32),
        jax.ShapeDtypeStruct((n_tok_all, TOP_K), F32),
        jax.ShapeDtypeStruct((B,) + state_block[1:], F32),
        jax.ShapeDtypeStruct((B, POOL_BUF, POOL_W), F32),
    ]
    aliases = {len(operands) + i: i for i in range(len(aliased))}
    return pl.pallas_call(
        functools.partial(_mixer_kernel, cfg),
        grid=(n_b, n_l),
        in_specs=in_specs,
        out_specs=out_specs,
        out_shape=out_shape,
        scratch_shapes=[pltpu.VMEM(shape, dtype) for _, shape, dtype in _scratch_spec(cfg)],
        input_output_aliases=aliases,
        compiler_params=pltpu.CompilerParams(
            dimension_semantics=("arbitrary", "arbitrary"), vmem_limit_bytes=VMEM_LIMIT),
        name="mixer",
    )(*operands, *aliased)


def _route_kernel(topi_ref, lower_ref, pos_ref, counts_ref, cnt_s, carry_s, gstart_s):
    ph = pl.program_id(0)
    i = pl.program_id(1)
    TT = topi_ref.shape[0]
    topi = topi_ref[...]
    lane = lax.broadcasted_iota(jnp.int32, (TT, LANES), 1)
    hot = jnp.zeros((TT, LANES), F32)
    for kk in range(TOP_K):
        hot = hot + (lane == topi[:, kk:kk + 1]).astype(F32)
    colsum = jnp.sum(hot, axis=0, keepdims=True)

    @pl.when((ph == 0) & (i == 0))
    def _():
        cnt_s[...] = jnp.zeros_like(cnt_s)

    @pl.when(ph == 0)
    def _():
        cnt_s[...] = cnt_s[...] + colsum

    @pl.when((ph == 1) & (i == 0))
    def _():
        cnt = cnt_s[...]
        counts_ref[...] = cnt.astype(jnp.int32)
        tiles = jnp.floor((cnt + (ROW_TILE - 1)) * (1.0 / ROW_TILE))
        tiles8 = jnp.broadcast_to(tiles, (SUBLANES, LANES))
        ur = lax.broadcasted_iota(jnp.int32, (LANES, LANES), 0)
        uc = lax.broadcasted_iota(jnp.int32, (LANES, LANES), 1)
        upper = (ur < uc).astype(BF16)
        t_hi, t_lo = _split2(tiles8)
        excl = _dot(t_hi, upper) + _dot(t_lo, upper)
        gstart_s[...] = excl[0:1, :] * float(ROW_TILE)
        carry_s[...] = jnp.zeros_like(carry_s)

    @pl.when(ph == 1)
    def _():
        rank = _dot(lower_ref[...], hot.astype(BF16)) + carry_s[...] + gstart_s[...]
        out = jnp.zeros((TT, LANES), F32)
        for kk in range(TOP_K):
            pk = jnp.sum(jnp.where(lane == topi[:, kk:kk + 1], rank, 0.0), axis=1, keepdims=True)
            out = jnp.where(lane == kk, pk, out)
        pos_ref[...] = out[:, 0:TOP_K].astype(jnp.int32)
        carry_s[...] = carry_s[...] + colsum


def _route_call(topi):
    T = topi.shape[0]
    assert T % ROUTE_TILE == 0
    r = jnp.arange(ROUTE_TILE)
    lower = (r[None, :] < r[:, None]).astype(BF16)
    return pl.pallas_call(
        _route_kernel,
        grid=(2, T // ROUTE_TILE),
        in_specs=[pl.BlockSpec((ROUTE_TILE, TOP_K), lambda ph, i: (i, 0)),
                  pl.BlockSpec((ROUTE_TILE, ROUTE_TILE), lambda ph, i: (0, 0))],
        out_specs=[pl.BlockSpec((ROUTE_TILE, TOP_K), lambda ph, i: (i * ph, 0)),
                   pl.BlockSpec((1, LANES), lambda ph, i: (0, 0))],
        out_shape=[jax.ShapeDtypeStruct((T, TOP_K), jnp.int32),
                   jax.ShapeDtypeStruct((1, LANES), jnp.int32)],
        scratch_shapes=[pltpu.VMEM((1, LANES), F32)] * 3,
        compiler_params=pltpu.CompilerParams(dimension_semantics=("arbitrary", "arbitrary")),
        name="route",
    )(topi, lower)


def _invperm_kernel(pad_lo_ref, pad_hi_ref, pos_ref, pair_ref):
    i = pl.program_id(0)

    @pl.when(i == 0)
    def _():
        def fill(r, c):
            pair_ref[r] = 0
            return c

        for e in range(N_EXPERTS):
            lax.fori_loop(pad_lo_ref[e], pad_hi_ref[e], fill, 0)

    @pl.when(i > 0)
    def _():
        base = (i - 1) * INV_BLOCK

        def put(n, c):
            pair_ref[pos_ref[n]] = base + n
            return c

        lax.fori_loop(0, INV_BLOCK, put, 0, unroll=32)


def _invperm_call(pad_lo, pad_hi, pos_flat, n_rows):
    n_pairs = pos_flat.shape[0]
    assert n_pairs % INV_BLOCK == 0
    grid_spec = pltpu.PrefetchScalarGridSpec(
        num_scalar_prefetch=2,
        grid=(1 + n_pairs // INV_BLOCK,),
        in_specs=[pl.BlockSpec((INV_BLOCK,), lambda i, lo, hi: (jnp.maximum(i - 1, 0),),
                               memory_space=pltpu.SMEM)],
        out_specs=pl.BlockSpec(memory_space=pltpu.SMEM),
    )
    return pl.pallas_call(
        _invperm_kernel,
        grid_spec=grid_spec,
        out_shape=jax.ShapeDtypeStruct((n_rows,), jnp.int32),
        compiler_params=pltpu.CompilerParams(dimension_semantics=("arbitrary",)),
        name="invperm",
    )(pad_lo, pad_hi, pos_flat)


def _row_copy_in(h_hbm, xb, sem, r, pair):
    tok = pair >> 2
    return pltpu.make_async_copy(h_hbm.at[pl.ds(pl.multiple_of(tok * CHUNKS, CHUNKS), CHUNKS), :],
                                 xb.at[pl.ds(r * CHUNKS, CHUNKS), :], sem)


def _gather_tile(tab_ref, h_hbm, xb, sem, tile, unrolled):
    base = tile * ROW_TILE
    if unrolled:
        for r in range(ROW_TILE):
            _row_copy_in(h_hbm, xb, sem, r, tab_ref[base + r]).start(priority=r % N_DMA_QUEUES)
    else:
        def issue(r, c):
            _row_copy_in(h_hbm, xb, sem, r, tab_ref[base + r]).start()
            return c

        lax.fori_loop(0, ROW_TILE, issue, 0)


def _wait_all_rows(buf, sem):
    pltpu.make_async_copy(buf, buf, sem).wait()


def _expert_mlp(xb, yb, wgu_s, wd_s, bgu_ref, bd_ref):
    xs = jnp.concatenate([xb[pl.ds(c, ROW_TILE, stride=CHUNKS), :] for c in range(CHUNKS)],
                         axis=1).astype(BF16)
    gu = _dot(xs, wgu_s[...]) + bgu_ref[0]
    gate = jnp.minimum(gu[:, 0:D_FF], SWIGLU_LIMIT)
    up = jnp.clip(gu[:, D_FF:2 * D_FF], -SWIGLU_LIMIT, SWIGLU_LIMIT)
    act = ((up + 1.0) * gate * jax.nn.sigmoid(SWIGLU_ALPHA * gate)).astype(BF16)
    y = _dot(act, wd_s[...]) + bd_ref[0]
    for c in range(CHUNKS):
        yb[pl.ds(c, ROW_TILE, stride=CHUNKS), :] = y[:, c * LANES:(c + 1) * LANES]


def _experts_kernel(te_ref, nt_ref, nxt_ref, tab_ref, h_hbm, wgu_hbm, bgu_ref, wd_hbm, bd_ref, y_hbm,
                    x0, x1, x2, y0, y1, y2, gsem, ssem, wgu_s, wd_s, wgu_f, wd_f, wsem):
    s = pl.program_id(0)
    n_used = nt_ref[0]
    par = s % N_BUF
    tc = jnp.clip(s - COMPUTE_LAG, 0, te_ref.shape[0] - 1)
    xs, ys = (x0, x1, x2), (y0, y1, y2)
    computing = (s >= COMPUTE_LAG) & (s < n_used + COMPUTE_LAG)

    def weight_copies(e):
        return (pltpu.make_async_copy(wgu_hbm.at[e], wgu_f, wsem.at[0]),
                pltpu.make_async_copy(wd_hbm.at[e], wd_f, wsem.at[1]))

    @pl.when(s == 0)
    def _():
        for cp in weight_copies(te_ref[0]):
            cp.start()

    @pl.when(computing & ((s == COMPUTE_LAG) | (te_ref[tc] != te_ref[jnp.maximum(tc - 1, 0)])))
    def _():
        for cp in weight_copies(te_ref[tc]):
            cp.wait()
        wgu_s[...] = wgu_f[...].astype(BF16)
        wd_s[...] = wd_f[...].astype(BF16)
        nxt = nxt_ref[te_ref[tc]]

        @pl.when(nxt >= 0)
        def _():
            for cp in weight_copies(nxt):
                cp.start()

    def write_out(c):
        rows = pl.ds(pl.multiple_of(tc * (ROW_TILE * CHUNKS), ROW_TILE * CHUNKS), ROW_TILE * CHUNKS)
        pltpu.make_async_copy(ys[c], y_hbm.at[rows, :], ssem.at[c]).start(priority=WRITE_DMA_QUEUE)

    last = n_used + COMPUTE_LAG - 1
    steady = (s >= COMPUTE_LAG) & (s < n_used)
    edge = jnp.logical_not(steady) & (s <= last)
    for p in range(N_BUF):
        c = (p + N_BUF - COMPUTE_LAG) % N_BUF
        mine = par == p

        @pl.when(mine & (s >= COMPUTE_LAG + N_BUF) & (s <= last))
        def _(c=c):
            _wait_all_rows(ys[c], ssem.at[c])

        @pl.when(mine & steady)
        def _(p=p, c=c):
            _wait_all_rows(xs[c], gsem.at[c])
            _gather_tile(tab_ref, h_hbm, xs[p], gsem.at[p], s, True)
            _expert_mlp(xs[c], ys[c], wgu_s, wd_s, bgu_ref, bd_ref)
            write_out(c)

        @pl.when(mine & edge & computing)
        def _(c=c):
            _wait_all_rows(xs[c], gsem.at[c])

        @pl.when(mine & edge & (s < n_used))
        def _(p=p):
            _gather_tile(tab_ref, h_hbm, xs[p], gsem.at[p], s, False)

        @pl.when(mine & edge & computing)
        def _(c=c):
            _expert_mlp(xs[c], ys[c], wgu_s, wd_s, bgu_ref, bd_ref)
            write_out(c)

    @pl.when(s == last)
    def _():
        for c in range(N_BUF):
            _wait_all_rows(ys[c], ssem.at[c])


def _experts_call(tile_e, n_used, next_e, pair_of_row, h_rows, w_gu, b_gu, w_down, b_down):
    n_tiles = tile_e.shape[0]

    def expert_block(s, te, nt, nxt, tab):
        return (te[jnp.clip(s - COMPUTE_LAG, 0, n_tiles - 1)], 0, 0)

    row_buffer = pltpu.VMEM((ROW_TILE * CHUNKS, LANES), F32)
    grid_spec = pltpu.PrefetchScalarGridSpec(
        num_scalar_prefetch=4,
        grid=(n_tiles + COMPUTE_LAG,),
        in_specs=[
            pl.BlockSpec(memory_space=pl.ANY),
            pl.BlockSpec(memory_space=pl.ANY),
            pl.BlockSpec((1, 1, 2 * D_FF), expert_block),
            pl.BlockSpec(memory_space=pl.ANY),
            pl.BlockSpec((1, 1, D_MODEL), expert_block),
        ],
        out_specs=pl.BlockSpec(memory_space=pl.ANY),
        scratch_shapes=[row_buffer] * N_BUF + [row_buffer] * N_BUF + [
            pltpu.SemaphoreType.DMA((N_BUF,)),
            pltpu.SemaphoreType.DMA((N_BUF,)),
            pltpu.VMEM((D_MODEL, 2 * D_FF), BF16),
            pltpu.VMEM((D_FF, D_MODEL), BF16),
            pltpu.VMEM((D_MODEL, 2 * D_FF), F32),
            pltpu.VMEM((D_FF, D_MODEL), F32),
            pltpu.SemaphoreType.DMA((2,)),
        ],
    )
    return pl.pallas_call(
        _experts_kernel,
        grid_spec=grid_spec,
        out_shape=jax.ShapeDtypeStruct((n_tiles * ROW_TILE * CHUNKS, LANES), F32),
        compiler_params=pltpu.CompilerParams(
            dimension_semantics=("arbitrary",), vmem_limit_bytes=VMEM_LIMIT),
        name="experts",
    )(tile_e, n_used, next_e, pair_of_row, h_rows, w_gu, b_gu.reshape(N_EXPERTS, 1, 2 * D_FF),
      w_down, b_down.reshape(N_EXPERTS, 1, D_MODEL))


def _gather_pairs(pos_ref, y_hbm, gb, sem, tile, unrolled):
    base = tile * (TOK_TILE * TOP_K)

    def copy(n):
        row = pos_ref[base + n]
        dst = ((n % TOP_K) * TOK_TILE + n // TOP_K) * CHUNKS
        if not isinstance(dst, int):
            dst = pl.multiple_of(dst, CHUNKS)
        return pltpu.make_async_copy(y_hbm.at[pl.ds(pl.multiple_of(row * CHUNKS, CHUNKS), CHUNKS), :],
                                     gb.at[pl.ds(dst, CHUNKS), :], sem)

    if unrolled:
        for n in range(TOK_TILE * TOP_K):
            copy(n).start(priority=n % N_DMA_QUEUES)
    else:
        def issue(n, c):
            copy(n).start()
            return c

        lax.fori_loop(0, TOK_TILE * TOP_K, issue, 0)


def _combine(gb, x1_ref, gates_ref, fn_ref, out_ref):
    gates = gates_ref[...]
    cols = []
    for c in range(CHUNKS):
        acc = gb[pl.ds(c, TOK_TILE, stride=CHUNKS), :] * gates[:, 0:1]
        for kk in range(1, TOP_K):
            acc = acc + gb[pl.ds(kk * TOK_TILE * CHUNKS + c, TOK_TILE, stride=CHUNKS), :] * gates[:, kk:kk + 1]
        cols.append(acc)
    out_ref[...] = _rms(x1_ref[...] + jnp.concatenate(cols, axis=1), fn_ref[...])


def _final_kernel(tile0, pos_ref, y_hbm, x1_ref, gates_ref, fn_ref, out_ref, g0, g1, g2, sem):
    s = pl.program_id(0)
    n = pl.num_programs(0) - COMPUTE_LAG
    par = s % N_BUF
    gs = (g0, g1, g2)
    steady = (s >= COMPUTE_LAG) & (s < n)
    for p in range(N_BUF):
        c = (p + N_BUF - COMPUTE_LAG) % N_BUF
        mine = par == p

        @pl.when(mine & steady)
        def _(p=p, c=c):
            _wait_all_rows(gs[c], sem.at[c])
            _gather_pairs(pos_ref, y_hbm, gs[p], sem.at[p], tile0 + s, True)
            _combine(gs[c], x1_ref, gates_ref, fn_ref, out_ref)

        @pl.when(mine & (s < COMPUTE_LAG))
        def _(p=p):
            _gather_pairs(pos_ref, y_hbm, gs[p], sem.at[p], tile0 + s, False)

        @pl.when(mine & (s >= n))
        def _(c=c):
            _wait_all_rows(gs[c], sem.at[c])
            _combine(gs[c], x1_ref, gates_ref, fn_ref, out_ref)


def _final_call(tok0, n_tok, pos_flat, y_rows, x1_all, gates_all, final_norm):
    assert tok0 % TOK_TILE == 0 and n_tok % TOK_TILE == 0 and n_tok // TOK_TILE >= COMPUTE_LAG
    tile0 = tok0 // TOK_TILE

    def tok_block(s, pos):
        return (tile0 + jnp.maximum(s - COMPUTE_LAG, 0), 0)

    pair_buffer = pltpu.VMEM((TOK_TILE * TOP_K * CHUNKS, LANES), F32)
    grid_spec = pltpu.PrefetchScalarGridSpec(
        num_scalar_prefetch=1,
        grid=(n_tok // TOK_TILE + COMPUTE_LAG,),
        in_specs=[
            pl.BlockSpec(memory_space=pl.ANY),
            pl.BlockSpec((TOK_TILE, D_MODEL), tok_block),
            pl.BlockSpec((TOK_TILE, TOP_K), tok_block),
            pl.BlockSpec((1, D_MODEL), lambda s, pos: (0, 0)),
        ],
        out_specs=pl.BlockSpec((TOK_TILE, D_MODEL), lambda s, pos: (jnp.maximum(s - COMPUTE_LAG, 0), 0)),
        scratch_shapes=[pair_buffer] * N_BUF + [pltpu.SemaphoreType.DMA((N_BUF,))],
    )
    return pl.pallas_call(
        functools.partial(_final_kernel, tile0),
        grid_spec=grid_spec,
        out_shape=jax.ShapeDtypeStruct((n_tok, D_MODEL), F32),
        compiler_params=pltpu.CompilerParams(
            dimension_semantics=("arbitrary",), vmem_limit_bytes=VMEM_LIMIT),
        name="final",
    )(pos_flat, y_rows, x1_all, gates_all, final_norm)


def _pick_tile(n, target):
    t = min(n, target)
    while n % t:
        t -= 1
    return t


def kernel(x_prompt, x_sample, state_gla, state_pool, norm1, w_in, w_gk2, b_gk, gla_norm, w_pool,
           pool_scale, w_o, norm2, w_router, b_router, w_gate_up, b_gate_up, w_down, b_down, final_norm):
    depth = w_in.shape[0]
    assert depth == 1
    B, L, _ = x_prompt.shape
    BS, LS, _ = x_sample.shape
    n_p, n_s = B * L, BS * LS
    n_all = n_p + n_s

    wi = w_in[0]
    o_glr = 2 * KEY_W + 2 * VAL_W
    wmain = jnp.concatenate(
        [wi[:, 0:o_glr], wi[:, o_glr + GATE_RANK:], wi[:, o_glr:o_glr + GATE_RANK],
         jnp.zeros((D_MODEL, LANES - GATE_RANK), F32)], axis=1).astype(BF16)
    wgk = jnp.concatenate([w_gk2[0], jnp.zeros((LANES - GATE_RANK, KEY_W), F32)], axis=0)
    wgkh, wgkl = _split2(wgk)
    wr = jnp.concatenate([w_router[0], jnp.zeros((D_MODEL, LANES - N_EXPERTS), F32)], axis=1)
    wrh, wrl = _split2(wr)
    br = jnp.concatenate([b_router[0], jnp.zeros((LANES - N_EXPERTS,), F32)]).reshape(1, LANES)
    weights = (norm1[0].reshape(1, D_MODEL), wmain, wgkh, wgkl, b_gk[0].reshape(1, KEY_W),
               gla_norm[0].reshape(1, DV), w_pool[0].astype(BF16), pool_scale[0].reshape(1, POOL_W),
               w_o[0].astype(BF16), norm2[0].reshape(1, D_MODEL), wrh, wrl, br)

    assert L % GLA_CHUNK == 0 and LS in (SUBLANES, BF16_ROWS)
    cfg_p = MixerCfg(ns=1, tl=_pick_tile(L, PROMPT_TILE), chunk=GLA_CHUNK, start_pos=0, n_alias=0)
    s0_p = jnp.zeros((B, VAL_W, KEY_W), F32)
    buf0_p = jnp.zeros((B, POOL_BUF, POOL_W), F32)
    x1_all, h_all, topi_all, gates_all, st_p, buf_p = _mixer_call(
        cfg_p, n_all, 0, x_prompt, s0_p, buf0_p, weights, ())

    cfg_s = MixerCfg(ns=_pick_tile(BS, SAMPLE_TILE // LS), tl=LS, chunk=LS, start_pos=PAST_LEN, n_alias=4)
    r_s = cfg_s.ns * cfg_s.tl
    assert n_p % r_s == 0
    x1_all, h_all, topi_all, gates_all, st_s, buf_s = _mixer_call(
        cfg_s, n_all, n_p // r_s, x_sample, state_gla[0], state_pool[0], weights,
        (x1_all, h_all, topi_all, gates_all))

    pos, counts = _route_call(topi_all)
    n_tiles = (n_all * TOP_K + N_EXPERTS * (ROW_TILE - 1)) // ROW_TILE
    tiles_per_e = (counts[0, :N_EXPERTS] + (ROW_TILE - 1)) // ROW_TILE
    ends = jnp.cumsum(tiles_per_e)
    n_used = ends[-1].astype(jnp.int32)
    tile_ids = jnp.minimum(jnp.arange(n_tiles, dtype=jnp.int32), n_used - 1)
    tile_e = jnp.sum(tile_ids[:, None] >= ends[None, :], axis=1).astype(jnp.int32)
    e_ids = jnp.arange(N_EXPERTS, dtype=jnp.int32)
    later = (tiles_per_e[None, :] > 0) & (e_ids[None, :] > e_ids[:, None])
    next_e = jnp.min(jnp.where(later, e_ids[None, :], N_EXPERTS), axis=1)
    next_e = jnp.where(next_e == N_EXPERTS, -1, next_e).astype(jnp.int32)
    pos_flat = pos.reshape(n_all * TOP_K)
    pad_hi = (ends * ROW_TILE).astype(jnp.int32)
    pad_lo = pad_hi - (tiles_per_e * ROW_TILE - counts[0, :N_EXPERTS]).astype(jnp.int32)
    pair_of_row = _invperm_call(pad_lo, pad_hi, pos_flat, n_tiles * ROW_TILE)

    y_rows = _experts_call(tile_e, n_used.reshape(1), next_e, pair_of_row, h_all,
                           w_gate_up[0], b_gate_up[0], w_down[0], b_down[0])

    fn = final_norm.reshape(1, D_MODEL)
    y_p = _final_call(0, n_p, pos_flat, y_rows, x1_all, gates_all, fn)
    y_s = _final_call(n_p, n_s, pos_flat, y_rows, x1_all, gates_all, fn)

    st_p = jnp.stack([st_p[:, h * DV:(h + 1) * DV, h * DK:(h + 1) * DK] for h in range(N_HEADS)], axis=1)
    return (y_p.reshape(B, L, D_MODEL), y_s.reshape(BS, LS, D_MODEL),
            jnp.swapaxes(st_p, -1, -2)[None], buf_p[None],
            st_s[None], buf_s[None])
```

```python
import functools
from typing import NamedTuple

import jax
import jax.numpy as jnp
from jax import lax
from jax.experimental import pallas as pl
from jax.experimental.pallas import tpu as pltpu

F32 = jnp.float32
BF16 = jnp.bfloat16

D_MODEL = 1024
N_HEADS = 4
DK = 64
DV = 128
KEY_W = N_HEADS * DK
VAL_W = N_HEADS * DV
GATE_RANK = 16
GATE_NORMALIZER = 16.0
GLA_CHUNK = 64
POOL_WINDOWS = (2, 4, 8, 16)
POOL_W = 512
POOL_GROUP = 128
POOL_BUF = 15
N_EXPERTS = 32
TOP_K = 4
D_FF = 1024
SWIGLU_LIMIT = 7.0
SWIGLU_ALPHA = 1.702
EPS = 1e-5
PAST_LEN = 16384

LANES = 128
SUBLANES = 8
BF16_ROWS = 16
CHUNKS = D_MODEL // LANES
HALO = 16
DIAG = 16
MAIN_COLS = 2 * KEY_W + 2 * VAL_W + POOL_W + LANES
PROMPT_TILE = 512
ROW_TILE = 256
SAMPLE_TILE = 256
TOK_TILE = 128
ROUTE_TILE = 1024
INV_BLOCK = 4096
N_DMA_QUEUES = 2
WRITE_DMA_QUEUE = 1
N_BUF = 3
COMPUTE_LAG = 2
VMEM_LIMIT = 56 * 1024 * 1024
MASKED_EXPONENT = -1e30


def _dot(a, b):
    return jnp.dot(a, b, preferred_element_type=F32)


def _dot_nt(a, b):
    return lax.dot_general(a, b, (((1,), (1,)), ((), ())), preferred_element_type=F32)


def _dot_tn(a, b):
    return lax.dot_general(a, b, (((0,), (0,)), ((), ())), preferred_element_type=F32)


def _split2(a):
    hi = a.astype(BF16)
    lo = (a - hi.astype(F32)).astype(BF16)
    return hi, lo


def _dot3(a, b_hi, b_lo):
    a_hi, a_lo = _split2(a)
    return _dot(a_hi, b_hi) + _dot(a_lo, b_hi) + _dot(a_hi, b_lo)


def _rms(x, w):
    return x * lax.rsqrt(jnp.mean(x * x, axis=-1, keepdims=True) + EPS) * w


class MixerCfg(NamedTuple):
    ns: int
    tl: int
    chunk: int
    start_pos: int
    n_alias: int


def _mixer_dims(cfg):
    rows = cfg.ns * cfg.tl
    diag = min(cfg.chunk, DIAG)
    n_off = cfg.chunk // diag - 1
    width = cfg.chunk if n_off else rows
    return rows, diag, n_off, width


def _scratch_spec(cfg):
    R, S, n_off, W = _mixer_dims(cfg)
    C, ns, tl = cfg.chunk, cfg.ns, cfg.tl
    spec = [
        ("ext", (ns, tl + HALO, POOL_W), F32),
        ("kh", (R + HALO, KEY_W), F32),
        ("gh", (R + HALO, KEY_W), F32),
        ("egl", (R, KEY_W), F32),
        ("a", (R, N_HEADS * W), F32),
        ("o", (R, VAL_W), F32),
    ]
    if n_off:
        nc = R // C
        spec += [
            ("st", (VAL_W, KEY_W), F32),
            ("qg", (R, KEY_W), BF16),
            ("kd", (R, KEY_W), BF16),
            ("v", (R, VAL_W), BF16),
            ("qcat", (R, n_off * KEY_W), BF16),
            ("kbd", (nc, N_HEADS * C, n_off * KEY_W), BF16),
            ("vbd", (nc, N_HEADS * C, VAL_W), BF16),
        ]
    else:
        spec += [
            ("qg", (R, KEY_W), F32),
            ("kd", (R, KEY_W), F32),
            ("v", (R, VAL_W), F32),
        ]
    return spec


def _gla_chunked(cfg, l, q, k, v, G, s0_ref, st_out_ref, bdm_ref, sc):
    R, S, n_off, W = _mixer_dims(cfg)
    C = cfg.chunk
    nc = R // C
    n_l = pl.num_programs(1)
    G3 = G.reshape(nc, C, KEY_W)
    glast = jnp.broadcast_to(G3[:, C - 1:C, :], (nc, C, KEY_W)).reshape(R, KEY_W)
    sc["qg"][...] = (q * jnp.exp(G)).astype(BF16)
    sc["kd"][...] = (k * jnp.exp(glast - G)).astype(BF16)
    sc["egl"][...] = jnp.exp(glast)
    v_bf = v.astype(BF16)
    sc["v"][...] = v_bf

    row_c = lax.broadcasted_iota(jnp.int32, (R, 1), 0) % C
    q_parts, k_parts = [], []
    for a in range(1, n_off + 1):
        ra = jnp.broadcast_to(G3[:, a * S - 1:a * S, :], (nc, C, KEY_W)).reshape(R, KEY_W)
        in_block = (row_c >= a * S) & (row_c < (a + 1) * S)
        q_parts.append(jnp.where(in_block, q * jnp.exp(jnp.minimum(G - ra, 0.0)), 0.0))
        k_parts.append(jnp.where(row_c < a * S, k * jnp.exp(jnp.minimum(ra - G, 0.0)), 0.0))
    sc["qcat"][...] = jnp.concatenate(q_parts, axis=1).astype(BF16)
    kcat = jnp.concatenate(k_parts, axis=1).astype(BF16)
    head_of_k = (lax.broadcasted_iota(jnp.int32, (1, n_off * KEY_W), 1) % KEY_W) // DK
    head_of_v = lax.broadcasted_iota(jnp.int32, (1, VAL_W), 1) // DV
    for h in range(N_HEADS):
        sc["kbd"][:, h * C:(h + 1) * C, :] = jnp.where(head_of_k == h, kcat, 0.0).reshape(nc, C, n_off * KEY_W)
        sc["vbd"][:, h * C:(h + 1) * C, :] = jnp.where(head_of_v == h, v_bf, 0.0).reshape(nc, C, VAL_W)

    @pl.when(l == 0)
    def _():
        sc["st"][...] = s0_ref[0]

    def chunk_body(c, carry):
        rows = pl.ds(pl.multiple_of(c * C, C), C)
        st = sc["st"][...]
        a_all = sc["a"][rows, :] + _dot_nt(sc["qcat"][rows, :], sc["kbd"][c])
        sc["o"][rows, :] = (_dot(a_all.astype(BF16), sc["vbd"][c])
                            + _dot_nt(sc["qg"][rows, :], st.astype(BF16)))
        upd = _dot_tn(sc["v"][rows, :], sc["kd"][rows, :])
        sc["st"][...] = st * sc["egl"][pl.ds(c * C, 1), :] + upd * bdm_ref[...]
        return carry

    lax.fori_loop(0, nc, chunk_body, 0)

    @pl.when(l == n_l - 1)
    def _():
        st_out_ref[0] = sc["st"][...]


def _gla_single_chunk(cfg, q, k, v, G, s0_ref, st_out_ref, sc):
    R, S, n_off, W = _mixer_dims(cfg)
    C = cfg.chunk
    G3 = G.reshape(R // C, C, KEY_W)
    glast = jnp.broadcast_to(G3[:, C - 1:C, :], (R // C, C, KEY_W)).reshape(R, KEY_W)
    sc["qg"][...] = q * jnp.exp(G)
    sc["kd"][...] = k * jnp.exp(glast - G)
    sc["egl"][...] = jnp.exp(glast)
    sc["v"][...] = v

    a_bf = sc["a"][...].astype(BF16)
    v_bf = v.astype(BF16)
    for h in range(N_HEADS):
        sc["o"][:, h * DV:(h + 1) * DV] = _dot(a_bf[:, h * W:(h + 1) * W], v_bf[:, h * DV:(h + 1) * DV])

    kpad = max(C, BF16_ROWS)

    def seq_body(c, carry):
        cs = pl.multiple_of(c * C, C)
        rows = pl.ds(cs, C)
        for h in range(N_HEADS):
            kc = slice(h * DK, (h + 1) * DK)
            vc = slice(h * DV, (h + 1) * DV)
            st = s0_ref[c, h]
            sc["o"][rows, vc] = sc["o"][rows, vc] + _dot(sc["qg"][rows, kc].astype(BF16), st.astype(BF16))
            v_h = sc["v"][rows, vc]
            kd_h = sc["kd"][rows, kc]
            if kpad > C:
                v_h = jnp.concatenate([v_h, jnp.zeros((kpad - C, DV), F32)], axis=0)
                kd_h = jnp.concatenate([kd_h, jnp.zeros((kpad - C, DK), F32)], axis=0)
            decay = sc["egl"][pl.ds(cs, SUBLANES), kc].T[:, 0:1]
            st_out_ref[c, h] = st * decay + _dot_tn(kd_h.astype(BF16), v_h.astype(BF16))
        return carry

    lax.fori_loop(0, R // C, seq_body, 0)


def _mixer_kernel(cfg, x_ref, s0_ref, buf0_ref, n1_ref, wmain_ref, wgkh_ref, wgkl_ref, bgk_ref,
                  gn_ref, wpool_ref, pscale_ref, wo_ref, n2_ref, wrh_ref, wrl_ref, br_ref,
                  tri_ref, hb_ref, bdm_ref, *rest):
    rest = rest[cfg.n_alias:]
    x1_ref, h_ref, topi_ref, gates_ref, st_out_ref, bufo_ref = rest[:6]
    sc = dict(zip([name for name, _, _ in _scratch_spec(cfg)], rest[6:]))

    ns, tl, C = cfg.ns, cfg.tl, cfg.chunk
    R, S, n_off, W = _mixer_dims(cfg)
    l = pl.program_id(1)
    n_l = pl.num_programs(1)
    ext_s = sc["ext"]

    x = x_ref[...].reshape(R, D_MODEL)
    xn = _rms(x, n1_ref[...]).astype(BF16)
    p = _dot(xn, wmain_ref[...])
    q = p[:, 0:KEY_W] * (DK ** -0.5)
    k = p[:, KEY_W:2 * KEY_W]
    v = p[:, 2 * KEY_W:2 * KEY_W + VAL_W]
    og = p[:, 2 * KEY_W + VAL_W:2 * KEY_W + 2 * VAL_W]
    u = p[:, 2 * KEY_W + 2 * VAL_W:2 * KEY_W + 2 * VAL_W + POOL_W]
    glr = p[:, MAIN_COLS - LANES:MAIN_COLS]
    ext_s[:, HALO:HALO + tl, :] = u.reshape(ns, tl, POOL_W)

    z = _dot3(glr, wgkh_ref[...], wgkl_ref[...]) + bgk_ref[...]
    g = -(jnp.maximum(-z, 0.0) + jnp.log(1.0 + jnp.exp(-jnp.abs(z)))) / GATE_NORMALIZER
    tri = tri_ref[...]
    g_hi = g.astype(BF16)
    g_r = g - g_hi.astype(F32)
    g_mid = g_r.astype(BF16)
    g_lo = (g_r - g_mid.astype(F32)).astype(BF16)
    G = _dot(tri, g_hi) + _dot(tri, g_mid) + _dot(tri, g_lo)

    sc["kh"][0:HALO, :] = jnp.zeros((HALO, KEY_W), F32)
    sc["gh"][0:HALO, :] = jnp.zeros((HALO, KEY_W), F32)
    sc["kh"][HALO:HALO + R, :] = k
    sc["gh"][HALO:HALO + R, :] = G
    row = lax.broadcasted_iota(jnp.int32, (R, 1), 0)
    row_s = row % S
    row_w = row % W
    col_w = lax.broadcasted_iota(jnp.int32, (1, N_HEADS * W), 1) % W
    head_bcast = hb_ref[...]
    a_all = jnp.zeros((R, N_HEADS * W), F32)
    for d in range(S):
        k_sh = sc["kh"][HALO - d:HALO - d + R, :]
        g_sh = sc["gh"][HALO - d:HALO - d + R, :]
        e = jnp.exp(jnp.where(row_s >= d, G - g_sh, MASKED_EXPONENT))
        term = (q * k_sh * e).astype(BF16)
        spread = _dot(term, head_bcast)
        a_all = jnp.where(col_w == row_w - d, spread, a_all)
    sc["a"][...] = a_all

    if n_off:
        _gla_chunked(cfg, l, q, k, v, G, s0_ref, st_out_ref, bdm_ref, sc)
    else:
        _gla_single_chunk(cfg, q, k, v, G, s0_ref, st_out_ref, sc)

    o = sc["o"][...]
    gn = gn_ref[...]
    o_heads = []
    for h in range(N_HEADS):
        vc = slice(h * DV, (h + 1) * DV)
        og_h = og[:, vc]
        o_heads.append(_rms(o[:, vc], gn) * (og_h * jax.nn.sigmoid(og_h)))

    @pl.when(l == 0)
    def _():
        ext_s[:, 0:HALO - POOL_BUF, :] = jnp.zeros((ns, HALO - POOL_BUF, POOL_W), F32)
        ext_s[:, HALO - POOL_BUF:HALO, :] = buf0_ref[...]

    pos = cfg.start_pos + l * tl + lax.broadcasted_iota(jnp.int32, (1, tl, 1), 1)
    z_groups = []
    for gi, w in enumerate(POOL_WINDOWS):
        gc = slice(gi * POOL_GROUP, (gi + 1) * POOL_GROUP)
        s = ext_s[:, HALO:HALO + tl, gc]
        for dd in range(1, w):
            s = s + ext_s[:, HALO - dd:HALO - dd + tl, gc]
        cnt = jnp.minimum(w, pos + 1).astype(F32)
        dmean = (s / cnt - ext_s[:, HALO:HALO + tl, gc]).reshape(R, POOL_GROUP)
        z_groups.append(_dot(dmean.astype(BF16), wpool_ref[gi]))
    zp = jnp.concatenate(z_groups, axis=1) * pscale_ref[...]

    @pl.when(l == n_l - 1)
    def _():
        bufo_ref[...] = ext_s[:, tl + HALO - POOL_BUF:tl + HALO, :]

    @pl.when(l < n_l - 1)
    def _():
        ext_s[:, 0:HALO, :] = ext_s[:, tl:tl + HALO, :]

    cat = jnp.concatenate(o_heads + [zp], axis=1).astype(BF16)
    x1 = x + _dot(cat, wo_ref[...])
    x1_ref[...] = x1
    hn = _rms(x1, n2_ref[...])
    for c in range(CHUNKS):
        h_ref[pl.ds(c, R, stride=CHUNKS), :] = hn[:, c * LANES:(c + 1) * LANES]
    logits = _dot3(hn, wrh_ref[...], wrl_ref[...]) + br_ref[...]
    lane = lax.broadcasted_iota(jnp.int32, (R, LANES), 1)
    lg = jnp.where(lane < N_EXPERTS, logits, -jnp.inf)
    vals, idxs = [], []
    for _ in range(TOP_K):
        m = jnp.max(lg, axis=1, keepdims=True)
        idx = jnp.min(jnp.where(lg == m, lane, LANES), axis=1, keepdims=True)
        vals.append(m)
        idxs.append(idx)
        lg = jnp.where(lane == idx, -jnp.inf, lg)
    exps = [jnp.exp(vv - vals[0]) for vv in vals]
    den = exps[0] + exps[1] + exps[2] + exps[3]
    ti = jnp.zeros((R, LANES), jnp.int32)
    gt = jnp.zeros((R, LANES), F32)
    for kk in range(TOP_K):
        ti = jnp.where(lane == kk, idxs[kk], ti)
        gt = jnp.where(lane == kk, exps[kk] / den, gt)
    topi_ref[...] = ti[:, 0:TOP_K]
    gates_ref[...] = gt[:, 0:TOP_K]


def _mixer_constants(cfg):
    R, S, n_off, W = _mixer_dims(cfg)
    C = cfg.chunk
    r = jnp.arange(R)
    tri = ((r[:, None] // C == r[None, :] // C) & (r[None, :] <= r[:, None])).astype(BF16)
    head_bcast = (jnp.arange(KEY_W)[:, None] // DK == jnp.arange(N_HEADS * W)[None, :] // W).astype(BF16)
    block_diag = (jnp.arange(VAL_W)[:, None] // DV == jnp.arange(KEY_W)[None, :] // DK).astype(F32)
    return tri, head_bcast, block_diag


def _mixer_call(cfg, n_tok_all, row_block0, x, s0, buf0, weights, aliased):
    B, L, _ = x.shape
    ns, tl, C = cfg.ns, cfg.tl, cfg.chunk
    R, S, n_off, W = _mixer_dims(cfg)
    n_b, n_l = B // ns, L // tl
    assert B % ns == 0 and L % tl == 0 and tl % C == 0 and R % SUBLANES == 0
    assert (n_l == 1 or tl >= HALO) and (n_off == 0 or ns == 1) and (n_off > 0 or tl == C)

    def const(shape):
        return pl.BlockSpec(shape, lambda b, l: (0,) * len(shape))

    state_block = (1, VAL_W, KEY_W) if n_off else (ns, N_HEADS, DK, DV)
    state_spec = pl.BlockSpec(state_block, lambda b, l: (b,) + (0,) * (len(state_block) - 1))
    consts = _mixer_constants(cfg)
    operands = (x, s0, buf0) + tuple(weights) + consts
    in_specs = [
        pl.BlockSpec((ns, tl, D_MODEL), lambda b, l: (b, l, 0)),
        state_spec,
        pl.BlockSpec((ns, POOL_BUF, POOL_W), lambda b, l: (b, 0, 0)),
    ] + [const(w.shape) for w in tuple(weights) + consts] + [pl.BlockSpec(memory_space=pl.ANY)] * len(aliased)

    def tok_block(rows, width):
        return pl.BlockSpec((rows, width), lambda b, l: (row_block0 + b * n_l + l, 0))

    out_specs = [
        tok_block(R, D_MODEL), tok_block(R * CHUNKS, LANES), tok_block(R, TOP_K), tok_block(R, TOP_K),
        state_spec,
        pl.BlockSpec((ns, POOL_BUF, POOL_W), lambda b, l: (b, 0, 0)),
    ]
    out_shape = [
        jax.ShapeDtypeStruct((n_tok_all, D_MODEL), F32),
        jax.ShapeDtypeStruct((n_tok_all * CHUNKS, LANES), F32),
        jax.ShapeDtypeStruct((n_tok_all, TOP_K), jnp.int32),
        jax.ShapeDtypeStruct((n_tok_all, TOP_K), F32),
        jax.ShapeDtypeStruct((B,) + state_block[1:], F32),
        jax.ShapeDtypeStruct((B, POOL_BUF, POOL_W), F32),
    ]
    aliases = {len(operands) + i: i for i in range(len(aliased))}
    return pl.pallas_call(
        functools.partial(_mixer_kernel, cfg),
        grid=(n_b, n_l),
        in_specs=in_specs,
        out_specs=out_specs,
        out_shape=out_shape,
        scratch_shapes=[pltpu.VMEM(shape, dtype) for _, shape, dtype in _scratch_spec(cfg)],
        input_output_aliases=aliases,
        compiler_params=pltpu.CompilerParams(
            dimension_semantics=("arbitrary", "arbitrary"), vmem_limit_bytes=VMEM_LIMIT),
        name="mixer",
    )(*operands, *aliased)


def _route_kernel(topi_ref, lower_ref, pos_ref, counts_ref, cnt_s, carry_s, gstart_s):
    ph = pl.program_id(0)
    i = pl.program_id(1)
    TT = topi_ref.shape[0]
    topi = topi_ref[...]
    lane = lax.broadcasted_iota(jnp.int32, (TT, LANES), 1)
    hot = jnp.zeros((TT, LANES), F32)
    for kk in range(TOP_K):
        hot = hot + (lane == topi[:, kk:kk + 1]).astype(F32)
    colsum = jnp.sum(hot, axis=0, keepdims=True)

    @pl.when((ph == 0) & (i == 0))
    def _():
        cnt_s[...] = jnp.zeros_like(cnt_s)

    @pl.when(ph == 0)
    def _():
        cnt_s[...] = cnt_s[...] + colsum

    @pl.when((ph == 1) & (i == 0))
    def _():
        cnt = cnt_s[...]
        counts_ref[...] = cnt.astype(jnp.int32)
        tiles = jnp.floor((cnt + (ROW_TILE - 1)) * (1.0 / ROW_TILE))
        tiles8 = jnp.broadcast_to(tiles, (SUBLANES, LANES))
        ur = lax.broadcasted_iota(jnp.int32, (LANES, LANES), 0)
        uc = lax.broadcasted_iota(jnp.int32, (LANES, LANES), 1)
        upper = (ur < uc).astype(BF16)
        t_hi, t_lo = _split2(tiles8)
        excl = _dot(t_hi, upper) + _dot(t_lo, upper)
        gstart_s[...] = excl[0:1, :] * float(ROW_TILE)
        carry_s[...] = jnp.zeros_like(carry_s)

    @pl.when(ph == 1)
    def _():
        rank = _dot(lower_ref[...], hot.astype(BF16)) + carry_s[...] + gstart_s[...]
        out = jnp.zeros((TT, LANES), F32)
        for kk in range(TOP_K):
            pk = jnp.sum(jnp.where(lane == topi[:, kk:kk + 1], rank, 0.0), axis=1, keepdims=True)
            out = jnp.where(lane == kk, pk, out)
        pos_ref[...] = out[:, 0:TOP_K].astype(jnp.int32)
        carry_s[...] = carry_s[...] + colsum


def _route_call(topi):
    T = topi.shape[0]
    assert T % ROUTE_TILE == 0
    r = jnp.arange(ROUTE_TILE)
    lower = (r[None, :] < r[:, None]).astype(BF16)
    return pl.pallas_call(
        _route_kernel,
        grid=(2, T // ROUTE_TILE),
        in_specs=[pl.BlockSpec((ROUTE_TILE, TOP_K), lambda ph, i: (i, 0)),
                  pl.BlockSpec((ROUTE_TILE, ROUTE_TILE), lambda ph, i: (0, 0))],
        out_specs=[pl.BlockSpec((ROUTE_TILE, TOP_K), lambda ph, i: (i * ph, 0)),
                   pl.BlockSpec((1, LANES), lambda ph, i: (0, 0))],
        out_shape=[jax.ShapeDtypeStruct((T, TOP_K), jnp.int32),
                   jax.ShapeDtypeStruct((1, LANES), jnp.int32)],
        scratch_shapes=[pltpu.VMEM((1, LANES), F32)] * 3,
        compiler_params=pltpu.CompilerParams(dimension_semantics=("arbitrary", "arbitrary")),
        name="route",
    )(topi, lower)


def _invperm_kernel(pad_lo_ref, pad_hi_ref, pos_ref, pair_ref):
    i = pl.program_id(0)

    @pl.when(i == 0)
    def _():
        def fill(r, c):
            pair_ref[r] = 0
            return c

        for e in range(N_EXPERTS):
            lax.fori_loop(pad_lo_ref[e], pad_hi_ref[e], fill, 0)

    @pl.when(i > 0)
    def _():
        base = (i - 1) * INV_BLOCK

        def put(n, c):
            pair_ref[pos_ref[n]] = base + n
            return c

        lax.fori_loop(0, INV_BLOCK, put, 0, unroll=32)


def _invperm_call(pad_lo, pad_hi, pos_flat, n_rows):
    n_pairs = pos_flat.shape[0]
    assert n_pairs % INV_BLOCK == 0
    grid_spec = pltpu.PrefetchScalarGridSpec(
        num_scalar_prefetch=2,
        grid=(1 + n_pairs // INV_BLOCK,),
        in_specs=[pl.BlockSpec((INV_BLOCK,), lambda i, lo, hi: (jnp.maximum(i - 1, 0),),
                               memory_space=pltpu.SMEM)],
        out_specs=pl.BlockSpec(memory_space=pltpu.SMEM),
    )
    return pl.pallas_call(
        _invperm_kernel,
        grid_spec=grid_spec,
        out_shape=jax.ShapeDtypeStruct((n_rows,), jnp.int32),
        compiler_params=pltpu.CompilerParams(dimension_semantics=("arbitrary",)),
        name="invperm",
    )(pad_lo, pad_hi, pos_flat)


def _row_copy_in(h_hbm, xb, sem, r, pair):
    tok = pair >> 2
    return pltpu.make_async_copy(h_hbm.at[pl.ds(pl.multiple_of(tok * CHUNKS, CHUNKS), CHUNKS), :],
                                 xb.at[pl.ds(r * CHUNKS, CHUNKS), :], sem)


def _gather_tile(tab_ref, h_hbm, xb, sem, tile, unrolled):
    base = tile * ROW_TILE
    if unrolled:
        for r in range(ROW_TILE):
            _row_copy_in(h_hbm, xb, sem, r, tab_ref[base + r]).start(priority=r % N_DMA_QUEUES)
    else:
        def issue(r, c):
            _row_copy_in(h_hbm, xb, sem, r, tab_ref[base + r]).start()
            return c

        lax.fori_loop(0, ROW_TILE, issue, 0)


def _wait_all_rows(buf, sem):
    pltpu.make_async_copy(buf, buf, sem).wait()


def _expert_mlp(xb, yb, wgu_s, wd_s, bgu_ref, bd_ref):
    xs = jnp.concatenate([xb[pl.ds(c, ROW_TILE, stride=CHUNKS), :] for c in range(CHUNKS)],
                         axis=1).astype(BF16)
    gu = _dot(xs, wgu_s[...]) + bgu_ref[0]
    gate = jnp.minimum(gu[:, 0:D_FF], SWIGLU_LIMIT)
    up = jnp.clip(gu[:, D_FF:2 * D_FF], -SWIGLU_LIMIT, SWIGLU_LIMIT)
    act = ((up + 1.0) * gate * jax.nn.sigmoid(SWIGLU_ALPHA * gate)).astype(BF16)
    y = _dot(act, wd_s[...]) + bd_ref[0]
    for c in range(CHUNKS):
        yb[pl.ds(c, ROW_TILE, stride=CHUNKS), :] = y[:, c * LANES:(c + 1) * LANES]


def _experts_kernel(te_ref, nt_ref, nxt_ref, tab_ref, h_hbm, wgu_hbm, bgu_ref, wd_hbm, bd_ref, y_hbm,
                    x0, x1, x2, y0, y1, y2, gsem, ssem, wgu_s, wd_s, wgu_f, wd_f, wsem):
    s = pl.program_id(0)
    n_used = nt_ref[0]
    par = s % N_BUF
    tc = jnp.clip(s - COMPUTE_LAG, 0, te_ref.shape[0] - 1)
    xs, ys = (x0, x1, x2), (y0, y1, y2)
    computing = (s >= COMPUTE_LAG) & (s < n_used + COMPUTE_LAG)

    def weight_copies(e):
        return (pltpu.make_async_copy(wgu_hbm.at[e], wgu_f, wsem.at[0]),
                pltpu.make_async_copy(wd_hbm.at[e], wd_f, wsem.at[1]))

    @pl.when(s == 0)
    def _():
        for cp in weight_copies(te_ref[0]):
            cp.start()

    @pl.when(computing & ((s == COMPUTE_LAG) | (te_ref[tc] != te_ref[jnp.maximum(tc - 1, 0)])))
    def _():
        for cp in weight_copies(te_ref[tc]):
            cp.wait()
        wgu_s[...] = wgu_f[...].astype(BF16)
        wd_s[...] = wd_f[...].astype(BF16)
        nxt = nxt_ref[te_ref[tc]]

        @pl.when(nxt >= 0)
        def _():
            for cp in weight_copies(nxt):
                cp.start()

    def write_out(c):
        rows = pl.ds(pl.multiple_of(tc * (ROW_TILE * CHUNKS), ROW_TILE * CHUNKS), ROW_TILE * CHUNKS)
        pltpu.make_async_copy(ys[c], y_hbm.at[rows, :], ssem.at[c]).start(priority=WRITE_DMA_QUEUE)

    last = n_used + COMPUTE_LAG - 1
    steady = (s >= COMPUTE_LAG) & (s < n_used)
    edge = jnp.logical_not(steady) & (s <= last)
    for p in range(N_BUF):
        c = (p + N_BUF - COMPUTE_LAG) % N_BUF
        mine = par == p

        @pl.when(mine & (s >= COMPUTE_LAG + N_BUF) & (s <= last))
        def _(c=c):
            _wait_all_rows(ys[c], ssem.at[c])

        @pl.when(mine & steady)
        def _(p=p, c=c):
            _wait_all_rows(xs[c], gsem.at[c])
            _gather_tile(tab_ref, h_hbm, xs[p], gsem.at[p], s, True)
            _expert_mlp(xs[c], ys[c], wgu_s, wd_s, bgu_ref, bd_ref)
            write_out(c)

        @pl.when(mine & edge & computing)
        def _(c=c):
            _wait_all_rows(xs[c], gsem.at[c])

        @pl.when(mine & edge & (s < n_used))
        def _(p=p):
            _gather_tile(tab_ref, h_hbm, xs[p], gsem.at[p], s, False)

        @pl.when(mine & edge & computing)
        def _(c=c):
            _expert_mlp(xs[c], ys[c], wgu_s, wd_s, bgu_ref, bd_ref)
            write_out(c)

    @pl.when(s == last)
    def _():
        for c in range(N_BUF):
            _wait_all_rows(ys[c], ssem.at[c])


def _experts_call(tile_e, n_used, next_e, pair_of_row, h_rows, w_gu, b_gu, w_down, b_down):
    n_tiles = tile_e.shape[0]

    def expert_block(s, te, nt, nxt, tab):
        return (te[jnp.clip(s - COMPUTE_LAG, 0, n_tiles - 1)], 0, 0)

    row_buffer = pltpu.VMEM((ROW_TILE * CHUNKS, LANES), F32)
    grid_spec = pltpu.PrefetchScalarGridSpec(
        num_scalar_prefetch=4,
        grid=(n_tiles + COMPUTE_LAG,),
        in_specs=[
            pl.BlockSpec(memory_space=pl.ANY),
            pl.BlockSpec(memory_space=pl.ANY),
            pl.BlockSpec((1, 1, 2 * D_FF), expert_block),
            pl.BlockSpec(memory_space=pl.ANY),
            pl.BlockSpec((1, 1, D_MODEL), expert_block),
        ],
        out_specs=pl.BlockSpec(memory_space=pl.ANY),
        scratch_shapes=[row_buffer] * N_BUF + [row_buffer] * N_BUF + [
            pltpu.SemaphoreType.DMA((N_BUF,)),
            pltpu.SemaphoreType.DMA((N_BUF,)),
            pltpu.VMEM((D_MODEL, 2 * D_FF), BF16),
            pltpu.VMEM((D_FF, D_MODEL), BF16),
            pltpu.VMEM((D_MODEL, 2 * D_FF), F32),
            pltpu.VMEM((D_FF, D_MODEL), F32),
            pltpu.SemaphoreType.DMA((2,)),
        ],
    )
    return pl.pallas_call(
        _experts_kernel,
        grid_spec=grid_spec,
        out_shape=jax.ShapeDtypeStruct((n_tiles * ROW_TILE * CHUNKS, LANES), F32),
        compiler_params=pltpu.CompilerParams(
            dimension_semantics=("arbitrary",), vmem_limit_bytes=VMEM_LIMIT),
        name="experts",
    )(tile_e, n_used, next_e, pair_of_row, h_rows, w_gu, b_gu.reshape(N_EXPERTS, 1, 2 * D_FF),
      w_down, b_down.reshape(N_EXPERTS, 1, D_MODEL))


def _gather_pairs(pos_ref, y_hbm, gb, sem, tile, unrolled):
    base = tile * (TOK_TILE * TOP_K)

    def copy(n):
        row = pos_ref[base + n]
        dst = ((n % TOP_K) * TOK_TILE + n // TOP_K) * CHUNKS
        if not isinstance(dst, int):
            dst = pl.multiple_of(dst, CHUNKS)
        return pltpu.make_async_copy(y_hbm.at[pl.ds(pl.multiple_of(row * CHUNKS, CHUNKS), CHUNKS), :],
                                     gb.at[pl.ds(dst, CHUNKS), :], sem)

    if unrolled:
        for n in range(TOK_TILE * TOP_K):
            copy(n).start(priority=n % N_DMA_QUEUES)
    else:
        def issue(n, c):
            copy(n).start()
            return c

        lax.fori_loop(0, TOK_TILE * TOP_K, issue, 0)


def _combine(gb, x1_ref, gates_ref, fn_ref, out_ref):
    gates = gates_ref[...]
    cols = []
    for c in range(CHUNKS):
        acc = gb[pl.ds(c, TOK_TILE, stride=CHUNKS), :] * gates[:, 0:1]
        for kk in range(1, TOP_K):
            acc = acc + gb[pl.ds(kk * TOK_TILE * CHUNKS + c, TOK_TILE, stride=CHUNKS), :] * gates[:, kk:kk + 1]
        cols.append(acc)
    out_ref[...] = _rms(x1_ref[...] + jnp.concatenate(cols, axis=1), fn_ref[...])


def _final_kernel(tile0, pos_ref, y_hbm, x1_ref, gates_ref, fn_ref, out_ref, g0, g1, g2, sem):
    s = pl.program_id(0)
    n = pl.num_programs(0) - COMPUTE_LAG
    par = s % N_BUF
    gs = (g0, g1, g2)
    steady = (s >= COMPUTE_LAG) & (s < n)
    for p in range(N_BUF):
        c = (p + N_BUF - COMPUTE_LAG) % N_BUF
        mine = par == p

        @pl.when(mine & steady)
        def _(p=p, c=c):
            _wait_all_rows(gs[c], sem.at[c])
            _gather_pairs(pos_ref, y_hbm, gs[p], sem.at[p], tile0 + s, True)
            _combine(gs[c], x1_ref, gates_ref, fn_ref, out_ref)

        @pl.when(mine & (s < COMPUTE_LAG))
        def _(p=p):
            _gather_pairs(pos_ref, y_hbm, gs[p], sem.at[p], tile0 + s, False)

        @pl.when(mine & (s >= n))
        def _(c=c):
            _wait_all_rows(gs[c], sem.at[c])
            _combine(gs[c], x1_ref, gates_ref, fn_ref, out_ref)


def _final_call(tok0, n_tok, pos_flat, y_rows, x1_all, gates_all, final_norm):
    assert tok0 % TOK_TILE == 0 and n_tok % TOK_TILE == 0 and n_tok // TOK_TILE >= COMPUTE_LAG
    tile0 = tok0 // TOK_TILE

    def tok_block(s, pos):
        return (tile0 + jnp.maximum(s - COMPUTE_LAG, 0), 0)

    pair_buffer = pltpu.VMEM((TOK_TILE * TOP_K * CHUNKS, LANES), F32)
    grid_spec = pltpu.PrefetchScalarGridSpec(
        num_scalar_prefetch=1,
        grid=(n_tok // TOK_TILE + COMPUTE_LAG,),
        in_specs=[
            pl.BlockSpec(memory_space=pl.ANY),
            pl.BlockSpec((TOK_TILE, D_MODEL), tok_block),
            pl.BlockSpec((TOK_TILE, TOP_K), tok_block),
            pl.BlockSpec((1, D_MODEL), lambda s, pos: (0, 0)),
        ],
        out_specs=pl.BlockSpec((TOK_TILE, D_MODEL), lambda s, pos: (jnp.maximum(s - COMPUTE_LAG, 0), 0)),
        scratch_shapes=[pair_buffer] * N_BUF + [pltpu.SemaphoreType.DMA((N_BUF,))],
    )
    return pl.pallas_call(
        functools.partial(_final_kernel, tile0),
        grid_spec=grid_spec,
        out_shape=jax.ShapeDtypeStruct((n_tok, D_MODEL), F32),
        compiler_params=pltpu.CompilerParams(
            dimension_semantics=("arbitrary",), vmem_limit_bytes=VMEM_LIMIT),
        name="final",
    )(pos_flat, y_rows, x1_all, gates_all, final_norm)


def _pick_tile(n, target):
    t = min(n, target)
    while n % t:
        t -= 1
    return t


def kernel(x_prompt, x_sample, state_gla, state_pool, norm1, w_in, w_gk2, b_gk, gla_norm, w_pool,
           pool_scale, w_o, norm2, w_router, b_router, w_gate_up, b_gate_up, w_down, b_down, final_norm):
    depth = w_in.shape[0]
    assert depth == 1
    B, L, _ = x_prompt.shape
    BS, LS, _ = x_sample.shape
    n_p, n_s = B * L, BS * LS
    n_all = n_p + n_s

    wi = w_in[0]
    o_glr = 2 * KEY_W + 2 * VAL_W
    wmain = jnp.concatenate(
        [wi[:, 0:o_glr], wi[:, o_glr + GATE_RANK:], wi[:, o_glr:o_glr + GATE_RANK],
         jnp.zeros((D_MODEL, LANES - GATE_RANK), F32)], axis=1).astype(BF16)
    wgk = jnp.concatenate([w_gk2[0], jnp.zeros((LANES - GATE_RANK, KEY_W), F32)], axis=0)
    wgkh, wgkl = _split2(wgk)
    wr = jnp.concatenate([w_router[0], jnp.zeros((D_MODEL, LANES - N_EXPERTS), F32)], axis=1)
    wrh, wrl = _split2(wr)
    br = jnp.concatenate([b_router[0], jnp.zeros((LANES - N_EXPERTS,), F32)]).reshape(1, LANES)
    weights = (norm1[0].reshape(1, D_MODEL), wmain, wgkh, wgkl, b_gk[0].reshape(1, KEY_W),
               gla_norm[0].reshape(1, DV), w_pool[0].astype(BF16), pool_scale[0].reshape(1, POOL_W),
               w_o[0].astype(BF16), norm2[0].reshape(1, D_MODEL), wrh, wrl, br)

    assert L % GLA_CHUNK == 0 and LS in (SUBLANES, BF16_ROWS)
    cfg_p = MixerCfg(ns=1, tl=_pick_tile(L, PROMPT_TILE), chunk=GLA_CHUNK, start_pos=0, n_alias=0)
    s0_p = jnp.zeros((B, VAL_W, KEY_W), F32)
    buf0_p = jnp.zeros((B, POOL_BUF, POOL_W), F32)
    x1_all, h_all, topi_all, gates_all, st_p, buf_p = _mixer_call(
        cfg_p, n_all, 0, x_prompt, s0_p, buf0_p, weights, ())

    cfg_s = MixerCfg(ns=_pick_tile(BS, SAMPLE_TILE // LS), tl=LS, chunk=LS, start_pos=PAST_LEN, n_alias=4)
    r_s = cfg_s.ns * cfg_s.tl
    assert n_p % r_s == 0
    x1_all, h_all, topi_all, gates_all, st_s, buf_s = _mixer_call(
        cfg_s, n_all, n_p // r_s, x_sample, state_gla[0], state_pool[0], weights,
        (x1_all, h_all, topi_all, gates_all))

    pos, counts = _route_call(topi_all)
    n_tiles = (n_all * TOP_K + N_EXPERTS * (ROW_TILE - 1)) // ROW_TILE
    tiles_per_e = (counts[0, :N_EXPERTS] + (ROW_TILE - 1)) // ROW_TILE
    ends = jnp.cumsum(tiles_per_e)
    n_used = ends[-1].astype(jnp.int32)
    tile_ids = jnp.minimum(jnp.arange(n_tiles, dtype=jnp.int32), n_used - 1)
    tile_e = jnp.sum(tile_ids[:, None] >= ends[None, :], axis=1).astype(jnp.int32)
    e_ids = jnp.arange(N_EXPERTS, dtype=jnp.int32)
    later = (tiles_per_e[None, :] > 0) & (e_ids[None, :] > e_ids[:, None])
    next_e = jnp.min(jnp.where(later, e_ids[None, :], N_EXPERTS), axis=1)
    next_e = jnp.where(next_e == N_EXPERTS, -1, next_e).astype(jnp.int32)
    pos_flat = pos.reshape(n_all * TOP_K)
    pad_hi = (ends * ROW_TILE).astype(jnp.int32)
    pad_lo = pad_hi - (tiles_per_e * ROW_TILE - counts[0, :N_EXPERTS]).astype(jnp.int32)
    pair_of_row = _invperm_call(pad_lo, pad_hi, pos_flat, n_tiles * ROW_TILE)

    y_rows = _experts_call(tile_e, n_used.reshape(1), next_e, pair_of_row, h_all,
                           w_gate_up[0], b_gate_up[0], w_down[0], b_down[0])

    fn = final_norm.reshape(1, D_MODEL)
    y_p = _final_call(0, n_p, pos_flat, y_rows, x1_all, gates_all, fn)
    y_s = _final_call(n_p, n_s, pos_flat, y_rows, x1_all, gates_all, fn)

    st_p = jnp.stack([st_p[:, h * DV:(h + 1) * DV, h * DK:(h + 1) * DK] for h in range(N_HEADS)], axis=1)
    return (y_p.reshape(B, L, D_MODEL), y_s.reshape(BS, LS, D_MODEL),
            jnp.swapaxes(st_p, -1, -2)[None], buf_p[None],
            st_s[None], buf_s[None])
```

```python
import functools
from typing import NamedTuple

import jax
import jax.numpy as jnp
from jax import lax
from jax.experimental import pallas as pl
from jax.experimental.pallas import tpu as pltpu

F32 = jnp.float32
BF16 = jnp.bfloat16

D_MODEL = 1024
N_HEADS = 4
DK = 64
DV = 128
KEY_W = N_HEADS * DK
VAL_W = N_HEADS * DV
GATE_RANK = 16
GATE_NORMALIZER = 16.0
GLA_CHUNK = 64
POOL_WINDOWS = (2, 4, 8, 16)
POOL_W = 512
POOL_GROUP = 128
POOL_BUF = 15
N_EXPERTS = 32
TOP_K = 4
D_FF = 1024
SWIGLU_LIMIT = 7.0
SWIGLU_ALPHA = 1.702
EPS = 1e-5
PAST_LEN = 16384

LANES = 128
SUBLANES = 8
BF16_ROWS = 16
CHUNKS = D_MODEL // LANES
HALO = 16
DIAG = 16
MAIN_COLS = 2 * KEY_W + 2 * VAL_W + POOL_W + LANES
PROMPT_TILE = 512
ROW_TILE = 256
SAMPLE_TILE = 256
TOK_TILE = 128
ROUTE_TILE = 1024
INV_BLOCK = 4096
N_DMA_QUEUES = 2
WRITE_DMA_QUEUE = 1
N_BUF = 3
COMPUTE_LAG = 2
VMEM_LIMIT = 56 * 1024 * 1024
MASKED_EXPONENT = -1e30


def _dot(a, b):
    return jnp.dot(a, b, preferred_element_type=F32)


def _dot_nt(a, b):
    return lax.dot_general(a, b, (((1,), (1,)), ((), ())), preferred_element_type=F32)


def _dot_tn(a, b):
    return lax.dot_general(a, b, (((0,), (0,)), ((), ())), preferred_element_type=F32)


def _split2(a):
    hi = a.astype(BF16)
    lo = (a - hi.astype(F32)).astype(BF16)
    return hi, lo


def _dot3(a, b_hi, b_lo):
    a_hi, a_lo = _split2(a)
    return _dot(a_hi, b_hi) + _dot(a_lo, b_hi) + _dot(a_hi, b_lo)


def _rms(x, w):
    return x * lax.rsqrt(jnp.mean(x * x, axis=-1, keepdims=True) + EPS) * w


class MixerCfg(NamedTuple):
    ns: int
    tl: int
    chunk: int
    start_pos: int
    n_alias: int


def _mixer_dims(cfg):
    rows = cfg.ns * cfg.tl
    diag = min(cfg.chunk, DIAG)
    n_off = cfg.chunk // diag - 1
    width = cfg.chunk if n_off else rows
    return rows, diag, n_off, width


def _scratch_spec(cfg):
    R, S, n_off, W = _mixer_dims(cfg)
    C, ns, tl = cfg.chunk, cfg.ns, cfg.tl
    spec = [
        ("ext", (ns, tl + HALO, POOL_W), F32),
        ("kh", (R + HALO, KEY_W), F32),
        ("gh", (R + HALO, KEY_W), F32),
        ("egl", (R, KEY_W), F32),
        ("a", (R, N_HEADS * W), F32),
        ("o", (R, VAL_W), F32),
    ]
    if n_off:
        nc = R // C
        spec += [
            ("st", (VAL_W, KEY_W), F32),
            ("qg", (R, KEY_W), BF16),
            ("kd", (R, KEY_W), BF16),
            ("v", (R, VAL_W), BF16),
            ("qcat", (R, n_off * KEY_W), BF16),
            ("kbd", (nc, N_HEADS * C, n_off * KEY_W), BF16),
            ("vbd", (nc, N_HEADS * C, VAL_W), BF16),
        ]
    else:
        spec += [
            ("qg", (R, KEY_W), F32),
            ("kd", (R, KEY_W), F32),
            ("v", (R, VAL_W), F32),
        ]
    return spec


def _gla_chunked(cfg, l, q, k, v, G, s0_ref, st_out_ref, bdm_ref, sc):
    R, S, n_off, W = _mixer_dims(cfg)
    C = cfg.chunk
    nc = R // C
    n_l = pl.num_programs(1)
    G3 = G.reshape(nc, C, KEY_W)
    glast = jnp.broadcast_to(G3[:, C - 1:C, :], (nc, C, KEY_W)).reshape(R, KEY_W)
    sc["qg"][...] = (q * jnp.exp(G)).astype(BF16)
    sc["kd"][...] = (k * jnp.exp(glast - G)).astype(BF16)
    sc["egl"][...] = jnp.exp(glast)
    v_bf = v.astype(BF16)
    sc["v"][...] = v_bf

    row_c = lax.broadcasted_iota(jnp.int32, (R, 1), 0) % C
    q_parts, k_parts = [], []
    for a in range(1, n_off + 1):
        ra = jnp.broadcast_to(G3[:, a * S - 1:a * S, :], (nc, C, KEY_W)).reshape(R, KEY_W)
        in_block = (row_c >= a * S) & (row_c < (a + 1) * S)
        q_parts.append(jnp.where(in_block, q * jnp.exp(jnp.minimum(G - ra, 0.0)), 0.0))
        k_parts.append(jnp.where(row_c < a * S, k * jnp.exp(jnp.minimum(ra - G, 0.0)), 0.0))
    sc["qcat"][...] = jnp.concatenate(q_parts, axis=1).astype(BF16)
    kcat = jnp.concatenate(k_parts, axis=1).astype(BF16)
    head_of_k = (lax.broadcasted_iota(jnp.int32, (1, n_off * KEY_W), 1) % KEY_W) // DK
    head_of_v = lax.broadcasted_iota(jnp.int32, (1, VAL_W), 1) // DV
    for h in range(N_HEADS):
        sc["kbd"][:, h * C:(h + 1) * C, :] = jnp.where(head_of_k == h, kcat, 0.0).reshape(nc, C, n_off * KEY_W)
        sc["vbd"][:, h * C:(h + 1) * C, :] = jnp.where(head_of_v == h, v_bf, 0.0).reshape(nc, C, VAL_W)

    @pl.when(l == 0)
    def _():
        sc["st"][...] = s0_ref[0]

    def chunk_body(c, carry):
        rows = pl.ds(pl.multiple_of(c * C, C), C)
        st = sc["st"][...]
        a_all = sc["a"][rows, :] + _dot_nt(sc["qcat"][rows, :], sc["kbd"][c])
        sc["o"][rows, :] = (_dot(a_all.astype(BF16), sc["vbd"][c])
                            + _dot_nt(sc["qg"][rows, :], st.astype(BF16)))
        upd = _dot_tn(sc["v"][rows, :], sc["kd"][rows, :])
        sc["st"][...] = st * sc["egl"][pl.ds(c * C, 1), :] + upd * bdm_ref[...]
        return carry

    lax.fori_loop(0, nc, chunk_body, 0)

    @pl.when(l == n_l - 1)
    def _():
        st_out_ref[0] = sc["st"][...]


def _gla_single_chunk(cfg, q, k, v, G, s0_ref, st_out_ref, sc):
    R, S, n_off, W = _mixer_dims(cfg)
    C = cfg.chunk
    G3 = G.reshape(R // C, C, KEY_W)
    glast = jnp.broadcast_to(G3[:, C - 1:C, :], (R // C, C, KEY_W)).reshape(R, KEY_W)
    sc["qg"][...] = q * jnp.exp(G)
    sc["kd"][...] = k * jnp.exp(glast - G)
    sc["egl"][...] = jnp.exp(glast)
    sc["v"][...] = v

    a_bf = sc["a"][...].astype(BF16)
    v_bf = v.astype(BF16)
    for h in range(N_HEADS):
        sc["o"][:, h * DV:(h + 1) * DV] = _dot(a_bf[:, h * W:(h + 1) * W], v_bf[:, h * DV:(h + 1) * DV])

    kpad = max(C, BF16_ROWS)

    def seq_body(c, carry):
        cs = pl.multiple_of(c * C, C)
        rows = pl.ds(cs, C)
        for h in range(N_HEADS):
            kc = slice(h * DK, (h + 1) * DK)
            vc = slice(h * DV, (h + 1) * DV)
            st = s0_ref[c, h]
            sc["o"][rows, vc] = sc["o"][rows, vc] + _dot(sc["qg"][rows, kc].astype(BF16), st.astype(BF16))
            v_h = sc["v"][rows, vc]
            kd_h = sc["kd"][rows, kc]
            if kpad > C:
                v_h = jnp.concatenate([v_h, jnp.zeros((kpad - C, DV), F32)], axis=0)
                kd_h = jnp.concatenate([kd_h, jnp.zeros((kpad - C, DK), F32)], axis=0)
            decay = sc["egl"][pl.ds(cs, SUBLANES), kc].T[:, 0:1]
            st_out_ref[c, h] = st * decay + _dot_tn(kd_h.astype(BF16), v_h.astype(BF16))
        return carry

    lax.fori_loop(0, R // C, seq_body, 0)


def _mixer_kernel(cfg, x_ref, s0_ref, buf0_ref, n1_ref, wmain_ref, wgkh_ref, wgkl_ref, bgk_ref,
                  gn_ref, wpool_ref, pscale_ref, wo_ref, n2_ref, wrh_ref, wrl_ref, br_ref,
                  tri_ref, hb_ref, bdm_ref, *rest):
    rest = rest[cfg.n_alias:]
    x1_ref, h_ref, topi_ref, gates_ref, st_out_ref, bufo_ref = rest[:6]
    sc = dict(zip([name for name, _, _ in _scratch_spec(cfg)], rest[6:]))

    ns, tl, C = cfg.ns, cfg.tl, cfg.chunk
    R, S, n_off, W = _mixer_dims(cfg)
    l = pl.program_id(1)
    n_l = pl.num_programs(1)
    ext_s = sc["ext"]

    x = x_ref[...].reshape(R, D_MODEL)
    xn = _rms(x, n1_ref[...]).astype(BF16)
    p = _dot(xn, wmain_ref[...])
    q = p[:, 0:KEY_W] * (DK ** -0.5)
    k = p[:, KEY_W:2 * KEY_W]
    v = p[:, 2 * KEY_W:2 * KEY_W + VAL_W]
    og = p[:, 2 * KEY_W + VAL_W:2 * KEY_W + 2 * VAL_W]
    u = p[:, 2 * KEY_W + 2 * VAL_W:2 * KEY_W + 2 * VAL_W + POOL_W]
    glr = p[:, MAIN_COLS - LANES:MAIN_COLS]
    ext_s[:, HALO:HALO + tl, :] = u.reshape(ns, tl, POOL_W)

    z = _dot3(glr, wgkh_ref[...], wgkl_ref[...]) + bgk_ref[...]
    g = -(jnp.maximum(-z, 0.0) + jnp.log(1.0 + jnp.exp(-jnp.abs(z)))) / GATE_NORMALIZER
    tri = tri_ref[...]
    g_hi = g.astype(BF16)
    g_r = g - g_hi.astype(F32)
    g_mid = g_r.astype(BF16)
    g_lo = (g_r - g_mid.astype(F32)).astype(BF16)
    G = _dot(tri, g_hi) + _dot(tri, g_mid) + _dot(tri, g_lo)

    sc["kh"][0:HALO, :] = jnp.zeros((HALO, KEY_W), F32)
    sc["gh"][0:HALO, :] = jnp.zeros((HALO, KEY_W), F32)
    sc["kh"][HALO:HALO + R, :] = k
    sc["gh"][HALO:HALO + R, :] = G
    row = lax.broadcasted_iota(jnp.int32, (R, 1), 0)
    row_s = row % S
    row_w = row % W
    col_w = lax.broadcasted_iota(jnp.int32, (1, N_HEADS * W), 1) % W
    head_bcast = hb_ref[...]
    a_all = jnp.zeros((R, N_HEADS * W), F32)
    for d in range(S):
        k_sh = sc["kh"][HALO - d:HALO - d + R, :]
        g_sh = sc["gh"][HALO - d:HALO - d + R, :]
        e = jnp.exp(jnp.where(row_s >= d, G - g_sh, MASKED_EXPONENT))
        term = (q * k_sh * e).astype(BF16)
        spread = _dot(term, head_bcast)
        a_all = jnp.where(col_w == row_w - d, spread, a_all)
    sc["a"][...] = a_all

    if n_off:
        _gla_chunked(cfg, l, q, k, v, G, s0_ref, st_out_ref, bdm_ref, sc)
    else:
        _gla_single_chunk(cfg, q, k, v, G, s0_ref, st_out_ref, sc)

    o = sc["o"][...]
    gn = gn_ref[...]
    o_heads = []
    for h in range(N_HEADS):
        vc = slice(h * DV, (h + 1) * DV)
        og_h = og[:, vc]
        o_heads.append(_rms(o[:, vc], gn) * (og_h * jax.nn.sigmoid(og_h)))

    @pl.when(l == 0)
    def _():
        ext_s[:, 0:HALO - POOL_BUF, :] = jnp.zeros((ns, HALO - POOL_BUF, POOL_W), F32)
        ext_s[:, HALO - POOL_BUF:HALO, :] = buf0_ref[...]

    pos = cfg.start_pos + l * tl + lax.broadcasted_iota(jnp.int32, (1, tl, 1), 1)
    z_groups = []
    for gi, w in enumerate(POOL_WINDOWS):
        gc = slice(gi * POOL_GROUP, (gi + 1) * POOL_GROUP)
        s = ext_s[:, HALO:HALO + tl, gc]
        for dd in range(1, w):
            s = s + ext_s[:, HALO - dd:HALO - dd + tl, gc]
        cnt = jnp.minimum(w, pos + 1).astype(F32)
        dmean = (s / cnt - ext_s[:, HALO:HALO + tl, gc]).reshape(R, POOL_GROUP)
        z_groups.append(_dot(dmean.astype(BF16), wpool_ref[gi]))
    zp = jnp.concatenate(z_groups, axis=1) * pscale_ref[...]

    @pl.when(l == n_l - 1)
    def _():
        bufo_ref[...] = ext_s[:, tl + HALO - POOL_BUF:tl + HALO, :]

    @pl.when(l < n_l - 1)
    def _():
        ext_s[:, 0:HALO, :] = ext_s[:, tl:tl + HALO, :]

    cat = jnp.concatenate(o_heads + [zp], axis=1).astype(BF16)
    x1 = x + _dot(cat, wo_ref[...])
    x1_ref[...] = x1
    hn = _rms(x1, n2_ref[...])
    for c in range(CHUNKS):
        h_ref[pl.ds(c, R, stride=CHUNKS), :] = hn[:, c * LANES:(c + 1) * LANES]
    logits = _dot3(hn, wrh_ref[...], wrl_ref[...]) + br_ref[...]
    lane = lax.broadcasted_iota(jnp.int32, (R, LANES), 1)
    lg = jnp.where(lane < N_EXPERTS, logits, -jnp.inf)
    vals, idxs = [], []
    for _ in range(TOP_K):
        m = jnp.max(lg, axis=1, keepdims=True)
        idx = jnp.min(jnp.where(lg == m, lane, LANES), axis=1, keepdims=True)
        vals.append(m)
        idxs.append(idx)
        lg = jnp.where(lane == idx, -jnp.inf, lg)
    exps = [jnp.exp(vv - vals[0]) for vv in vals]
    den = exps[0] + exps[1] + exps[2] + exps[3]
    ti = jnp.zeros((R, LANES), jnp.int32)
    gt = jnp.zeros((R, LANES), F32)
    for kk in range(TOP_K):
        ti = jnp.where(lane == kk, idxs[kk], ti)
        gt = jnp.where(lane == kk, exps[kk] / den, gt)
    topi_ref[...] = ti[:, 0:TOP_K]
    gates_ref[...] = gt[:, 0:TOP_K]


def _mixer_constants(cfg):
    R, S, n_off, W = _mixer_dims(cfg)
    C = cfg.chunk
    r = jnp.arange(R)
    tri = ((r[:, None] // C == r[None, :] // C) & (r[None, :] <= r[:, None])).astype(BF16)
    head_bcast = (jnp.arange(KEY_W)[:, None] // DK == jnp.arange(N_HEADS * W)[None, :] // W).astype(BF16)
    block_diag = (jnp.arange(VAL_W)[:, None] // DV == jnp.arange(KEY_W)[None, :] // DK).astype(F32)
    return tri, head_bcast, block_diag


def _mixer_call(cfg, n_tok_all, row_block0, x, s0, buf0, weights, aliased):
    B, L, _ = x.shape
    ns, tl, C = cfg.ns, cfg.tl, cfg.chunk
    R, S, n_off, W = _mixer_dims(cfg)
    n_b, n_l = B // ns, L // tl
    assert B % ns == 0 and L % tl == 0 and tl % C == 0 and R % SUBLANES == 0
    assert (n_l == 1 or tl >= HALO) and (n_off == 0 or ns == 1) and (n_off > 0 or tl == C)

    def const(shape):
        return pl.BlockSpec(shape, lambda b, l: (0,) * len(shape))

    state_block = (1, VAL_W, KEY_W) if n_off else (ns, N_HEADS, DK, DV)
    state_spec = pl.BlockSpec(state_block, lambda b, l: (b,) + (0,) * (len(state_block) - 1))
    consts = _mixer_constants(cfg)
    operands = (x, s0, buf0) + tuple(weights) + consts
    in_specs = [
        pl.BlockSpec((ns, tl, D_MODEL), lambda b, l: (b, l, 0)),
        state_spec,
        pl.BlockSpec((ns, POOL_BUF, POOL_W), lambda b, l: (b, 0, 0)),
    ] + [const(w.shape) for w in tuple(weights) + consts] + [pl.BlockSpec(memory_space=pl.ANY)] * len(aliased)

    def tok_block(rows, width):
        return pl.BlockSpec((rows, width), lambda b, l: (row_block0 + b * n_l + l, 0))

    out_specs = [
        tok_block(R, D_MODEL), tok_block(R * CHUNKS, LANES), tok_block(R, TOP_K), tok_block(R, TOP_K),
        state_spec,
        pl.BlockSpec((ns, POOL_BUF, POOL_W), lambda b, l: (b, 0, 0)),
    ]
    out_shape = [
        jax.ShapeDtypeStruct((n_tok_all, D_MODEL), F32),
        jax.ShapeDtypeStruct((n_tok_all * CHUNKS, LANES), F32),
        jax.ShapeDtypeStruct((n_tok_all, TOP_K), jnp.int32),
        jax.ShapeDtypeStruct((n_tok_all, TOP_K), F32),
        jax.ShapeDtypeStruct((B,) + state_block[1:], F32),
        jax.ShapeDtypeStruct((B, POOL_BUF, POOL_W), F32),
    ]
    aliases = {len(operands) + i: i for i in range(len(aliased))}
    return pl.pallas_call(
        functools.partial(_mixer_kernel, cfg),
        grid=(n_b, n_l),
        in_specs=in_specs,
        out_specs=out_specs,
        out_shape=out_shape,
        scratch_shapes=[pltpu.VMEM(shape, dtype) for _, shape, dtype in _scratch_spec(cfg)],
        input_output_aliases=aliases,
        compiler_params=pltpu.CompilerParams(
            dimension_semantics=("arbitrary", "arbitrary"), vmem_limit_bytes=VMEM_LIMIT),
        name="mixer",
    )(*operands, *aliased)


def _route_kernel(topi_ref, lower_ref, pos_ref, counts_ref, cnt_s, carry_s, gstart_s):
    ph = pl.program_id(0)
    i = pl.program_id(1)
    TT = topi_ref.shape[0]
    topi = topi_ref[...]
    lane = lax.broadcasted_iota(jnp.int32, (TT, LANES), 1)
    hot = jnp.zeros((TT, LANES), F32)
    for kk in range(TOP_K):
        hot = hot + (lane == topi[:, kk:kk + 1]).astype(F32)
    colsum = jnp.sum(hot, axis=0, keepdims=True)

    @pl.when((ph == 0) & (i == 0))
    def _():
        cnt_s[...] = jnp.zeros_like(cnt_s)

    @pl.when(ph == 0)
    def _():
        cnt_s[...] = cnt_s[...] + colsum

    @pl.when((ph == 1) & (i == 0))
    def _():
        cnt = cnt_s[...]
        counts_ref[...] = cnt.astype(jnp.int32)
        tiles = jnp.floor((cnt + (ROW_TILE - 1)) * (1.0 / ROW_TILE))
        tiles8 = jnp.broadcast_to(tiles, (SUBLANES, LANES))
        ur = lax.broadcasted_iota(jnp.int32, (LANES, LANES), 0)
        uc = lax.broadcasted_iota(jnp.int32, (LANES, LANES), 1)
        upper = (ur < uc).astype(BF16)
        t_hi, t_lo = _split2(tiles8)
        excl = _dot(t_hi, upper) + _dot(t_lo, upper)
        gstart_s[...] = excl[0:1, :] * float(ROW_TILE)
        carry_s[...] = jnp.zeros_like(carry_s)

    @pl.when(ph == 1)
    def _():
        rank = _dot(lower_ref[...], hot.astype(BF16)) + carry_s[...] + gstart_s[...]
        out = jnp.zeros((TT, LANES), F32)
        for kk in range(TOP_K):
            pk = jnp.sum(jnp.where(lane == topi[:, kk:kk + 1], rank, 0.0), axis=1, keepdims=True)
            out = jnp.where(lane == kk, pk, out)
        pos_ref[...] = out[:, 0:TOP_K].astype(jnp.int32)
        carry_s[...] = carry_s[...] + colsum


def _route_call(topi):
    T = topi.shape[0]
    assert T % ROUTE_TILE == 0
    r = jnp.arange(ROUTE_TILE)
    lower = (r[None, :] < r[:, None]).astype(BF16)
    return pl.pallas_call(
        _route_kernel,
        grid=(2, T // ROUTE_TILE),
        in_specs=[pl.BlockSpec((ROUTE_TILE, TOP_K), lambda ph, i: (i, 0)),
                  pl.BlockSpec((ROUTE_TILE, ROUTE_TILE), lambda ph, i: (0, 0))],
        out_specs=[pl.BlockSpec((ROUTE_TILE, TOP_K), lambda ph, i: (i * ph, 0)),
                   pl.BlockSpec((1, LANES), lambda ph, i: (0, 0))],
        out_shape=[jax.ShapeDtypeStruct((T, TOP_K), jnp.int32),
                   jax.ShapeDtypeStruct((1, LANES), jnp.int32)],
        scratch_shapes=[pltpu.VMEM((1, LANES), F32)] * 3,
        compiler_params=pltpu.CompilerParams(dimension_semantics=("arbitrary", "arbitrary")),
        name="route",
    )(topi, lower)


def _invperm_kernel(pad_lo_ref, pad_hi_ref, pos_ref, pair_ref):
    i = pl.program_id(0)

    @pl.when(i == 0)
    def _():
        def fill(r, c):
            pair_ref[r] = 0
            return c

        for e in range(N_EXPERTS):
            lax.fori_loop(pad_lo_ref[e], pad_hi_ref[e], fill, 0)

    @pl.when(i > 0)
    def _():
        base = (i - 1) * INV_BLOCK

        def put(n, c):
            pair_ref[pos_ref[n]] = base + n
            return c

        lax.fori_loop(0, INV_BLOCK, put, 0, unroll=32)


def _invperm_call(pad_lo, pad_hi, pos_flat, n_rows):
    n_pairs = pos_flat.shape[0]
    assert n_pairs % INV_BLOCK == 0
    grid_spec = pltpu.PrefetchScalarGridSpec(
        num_scalar_prefetch=2,
        grid=(1 + n_pairs // INV_BLOCK,),
        in_specs=[pl.BlockSpec((INV_BLOCK,), lambda i, lo, hi: (jnp.maximum(i - 1, 0),),
                               memory_space=pltpu.SMEM)],
        out_specs=pl.BlockSpec(memory_space=pltpu.SMEM),
    )
    return pl.pallas_call(
        _invperm_kernel,
        grid_spec=grid_spec,
        out_shape=jax.ShapeDtypeStruct((n_rows,), jnp.int32),
        compiler_params=pltpu.CompilerParams(dimension_semantics=("arbitrary",)),
        name="invperm",
    )(pad_lo, pad_hi, pos_flat)


def _row_copy_in(h_hbm, xb, sem, r, pair):
    tok = pair >> 2
    return pltpu.make_async_copy(h_hbm.at[pl.ds(pl.multiple_of(tok * CHUNKS, CHUNKS), CHUNKS), :],
                                 xb.at[pl.ds(r * CHUNKS, CHUNKS), :], sem)


def _gather_tile(tab_ref, h_hbm, xb, sem, tile, unrolled):
    base = tile * ROW_TILE
    if unrolled:
        for r in range(ROW_TILE):
            _row_copy_in(h_hbm, xb, sem, r, tab_ref[base + r]).start(priority=r % N_DMA_QUEUES)
    else:
        def issue(r, c):
            _row_copy_in(h_hbm, xb, sem, r, tab_ref[base + r]).start()
            return c

        lax.fori_loop(0, ROW_TILE, issue, 0)


def _wait_all_rows(buf, sem):
    pltpu.make_async_copy(buf, buf, sem).wait()


def _expert_mlp(xb, yb, wgu_s, wd_s, bgu_ref, bd_ref):
    xs = jnp.concatenate([xb[pl.ds(c, ROW_TILE, stride=CHUNKS), :] for c in range(CHUNKS)],
                         axis=1).astype(BF16)
    gu = _dot(xs, wgu_s[...]) + bgu_ref[0]
    gate = jnp.minimum(gu[:, 0:D_FF], SWIGLU_LIMIT)
    up = jnp.clip(gu[:, D_FF:2 * D_FF], -SWIGLU_LIMIT, SWIGLU_LIMIT)
    act = ((up + 1.0) * gate * jax.nn.sigmoid(SWIGLU_ALPHA * gate)).astype(BF16)
    y = _dot(act, wd_s[...]) + bd_ref[0]
    for c in range(CHUNKS):
        yb[pl.ds(c, ROW_TILE, stride=CHUNKS), :] = y[:, c * LANES:(c + 1) * LANES]


def _experts_kernel(te_ref, nt_ref, nxt_ref, tab_ref, h_hbm, wgu_hbm, bgu_ref, wd_hbm, bd_ref, y_hbm,
                    x0, x1, x2, y0, y1, y2, gsem, ssem, wgu_s, wd_s, wgu_f, wd_f, wsem):
    s = pl.program_id(0)
    n_used = nt_ref[0]
    par = s % N_BUF
    tc = jnp.clip(s - COMPUTE_LAG, 0, te_ref.shape[0] - 1)
    xs, ys = (x0, x1, x2), (y0, y1, y2)
    computing = (s >= COMPUTE_LAG) & (s < n_used + COMPUTE_LAG)

    def weight_copies(e):
        return (pltpu.make_async_copy(wgu_hbm.at[e], wgu_f, wsem.at[0]),
                pltpu.make_async_copy(wd_hbm.at[e], wd_f, wsem.at[1]))

    @pl.when(s == 0)
    def _():
        for cp in weight_copies(te_ref[0]):
            cp.start()

    @pl.when(computing & ((s == COMPUTE_LAG) | (te_ref[tc] != te_ref[jnp.maximum(tc - 1, 0)])))
    def _():
        for cp in weight_copies(te_ref[tc]):
            cp.wait()
        wgu_s[...] = wgu_f[...].astype(BF16)
        wd_s[...] = wd_f[...].astype(BF16)
        nxt = nxt_ref[te_ref[tc]]

        @pl.when(nxt >= 0)
        def _():
            for cp in weight_copies(nxt):
                cp.start()

    def write_out(c):
        rows = pl.ds(pl.multiple_of(tc * (ROW_TILE * CHUNKS), ROW_TILE * CHUNKS), ROW_TILE * CHUNKS)
        pltpu.make_async_copy(ys[c], y_hbm.at[rows, :], ssem.at[c]).start(priority=WRITE_DMA_QUEUE)

    last = n_used + COMPUTE_LAG - 1
    steady = (s >= COMPUTE_LAG) & (s < n_used)
    edge = jnp.logical_not(steady) & (s <= last)
    for p in range(N_BUF):
        c = (p + N_BUF - COMPUTE_LAG) % N_BUF
        mine = par == p

        @pl.when(mine & (s >= COMPUTE_LAG + N_BUF) & (s <= last))
        def _(c=c):
            _wait_all_rows(ys[c], ssem.at[c])

        @pl.when(mine & steady)
        def _(p=p, c=c):
            _wait_all_rows(xs[c], gsem.at[c])
            _gather_tile(tab_ref, h_hbm, xs[p], gsem.at[p], s, True)
            _expert_mlp(xs[c], ys[c], wgu_s, wd_s, bgu_ref, bd_ref)
            write_out(c)

        @pl.when(mine & edge & computing)
        def _(c=c):
            _wait_all_rows(xs[c], gsem.at[c])

        @pl.when(mine & edge & (s < n_used))
        def _(p=p):
            _gather_tile(tab_ref, h_hbm, xs[p], gsem.at[p], s, False)

        @pl.when(mine & edge & computing)
        def _(c=c):
            _expert_mlp(xs[c], ys[c], wgu_s, wd_s, bgu_ref, bd_ref)
            write_out(c)

    @pl.when(s == last)
    def _():
        for c in range(N_BUF):
            _wait_all_rows(ys[c], ssem.at[c])


def _experts_call(tile_e, n_used, next_e, pair_of_row, h_rows, w_gu, b_gu, w_down, b_down):
    n_tiles = tile_e.shape[0]

    def expert_block(s, te, nt, nxt, tab):
        return (te[jnp.clip(s - COMPUTE_LAG, 0, n_tiles - 1)], 0, 0)

    row_buffer = pltpu.VMEM((ROW_TILE * CHUNKS, LANES), F32)
    grid_spec = pltpu.PrefetchScalarGridSpec(
        num_scalar_prefetch=4,
        grid=(n_tiles + COMPUTE_LAG,),
        in_specs=[
            pl.BlockSpec(memory_space=pl.ANY),
            pl.BlockSpec(memory_space=pl.ANY),
            pl.BlockSpec((1, 1, 2 * D_FF), expert_block),
            pl.BlockSpec(memory_space=pl.ANY),
            pl.BlockSpec((1, 1, D_MODEL), expert_block),
        ],
        out_specs=pl.BlockSpec(memory_space=pl.ANY),
        scratch_shapes=[row_buffer] * N_BUF + [row_buffer] * N_BUF + [
            pltpu.SemaphoreType.DMA((N_BUF,)),
            pltpu.SemaphoreType.DMA((N_BUF,)),
            pltpu.VMEM((D_MODEL, 2 * D_FF), BF16),
            pltpu.VMEM((D_FF, D_MODEL), BF16),
            pltpu.VMEM((D_MODEL, 2 * D_FF), F32),
            pltpu.VMEM((D_FF, D_MODEL), F32),
            pltpu.SemaphoreType.DMA((2,)),
        ],
    )
    return pl.pallas_call(
        _experts_kernel,
        grid_spec=grid_spec,
        out_shape=jax.ShapeDtypeStruct((n_tiles * ROW_TILE * CHUNKS, LANES), F32),
        compiler_params=pltpu.CompilerParams(
            dimension_semantics=("arbitrary",), vmem_limit_bytes=VMEM_LIMIT),
        name="experts",
    )(tile_e, n_used, next_e, pair_of_row, h_rows, w_gu, b_gu.reshape(N_EXPERTS, 1, 2 * D_FF),
      w_down, b_down.reshape(N_EXPERTS, 1, D_MODEL))


def _gather_pairs(pos_ref, y_hbm, gb, sem, tile, unrolled):
    base = tile * (TOK_TILE * TOP_K)

    def copy(n):
        row = pos_ref[base + n]
        dst = ((n % TOP_K) * TOK_TILE + n // TOP_K) * CHUNKS
        if not isinstance(dst, int):
            dst = pl.multiple_of(dst, CHUNKS)
        return pltpu.make_async_copy(y_hbm.at[pl.ds(pl.multiple_of(row * CHUNKS, CHUNKS), CHUNKS), :],
                                     gb.at[pl.ds(dst, CHUNKS), :], sem)

    if unrolled:
        for n in range(TOK_TILE * TOP_K):
            copy(n).start(priority=n % N_DMA_QUEUES)
    else:
        def issue(n, c):
            copy(n).start()
            return c

        lax.fori_loop(0, TOK_TILE * TOP_K, issue, 0)


def _combine(gb, x1_ref, gates_ref, fn_ref, res):
    gates = gates_ref[...]
    cols = []
    for c in range(CHUNKS):
        acc = gb[pl.ds(c, TOK_TILE, stride=CHUNKS), :] * gates[:, 0:1]
        for kk in range(1, TOP_K):
            acc = acc + gb[pl.ds(kk * TOK_TILE * CHUNKS + c, TOK_TILE, stride=CHUNKS), :] * gates[:, kk:kk + 1]
        cols.append(acc)
    res[...] = _rms(x1_ref[...] + jnp.concatenate(cols, axis=1), fn_ref[...])


def _final_kernel(n_first, pos_ref, y_hbm, x1_ref, gates_ref, fn_ref, out_a_ref, out_b_ref, g0, g1, g2, res, sem):
    s = pl.program_id(0)
    n = pl.num_programs(0) - COMPUTE_LAG
    par = s % N_BUF
    gs = (g0, g1, g2)
    steady = (s >= COMPUTE_LAG) & (s < n)
    for p in range(N_BUF):
        c = (p + N_BUF - COMPUTE_LAG) % N_BUF
        mine = par == p

        @pl.when(mine & steady)
        def _(p=p, c=c):
            _wait_all_rows(gs[c], sem.at[c])
            _gather_pairs(pos_ref, y_hbm, gs[p], sem.at[p], s, True)
            _combine(gs[c], x1_ref, gates_ref, fn_ref, res)

        @pl.when(mine & (s < COMPUTE_LAG))
        def _(p=p):
            _gather_pairs(pos_ref, y_hbm, gs[p], sem.at[p], s, False)

        @pl.when(mine & (s >= n))
        def _(c=c):
            _wait_all_rows(gs[c], sem.at[c])
            _combine(gs[c], x1_ref, gates_ref, fn_ref, res)

    tile = s - COMPUTE_LAG

    @pl.when((tile >= 0) & (tile < n_first))
    def _():
        out_a_ref[...] = res[...]

    @pl.when(tile >= n_first)
    def _():
        out_b_ref[...] = res[...]


def _final_call(n_first_tok, pos_flat, y_rows, x1_all, gates_all, final_norm):
    n_tok = x1_all.shape[0]
    assert n_first_tok % TOK_TILE == 0 and n_tok % TOK_TILE == 0 and n_tok // TOK_TILE >= COMPUTE_LAG
    n_first = n_first_tok // TOK_TILE
    n_tiles = n_tok // TOK_TILE
    assert 0 < n_first < n_tiles

    def tok_block(s, pos):
        return (jnp.maximum(s - COMPUTE_LAG, 0), 0)

    pair_buffer = pltpu.VMEM((TOK_TILE * TOP_K * CHUNKS, LANES), F32)
    grid_spec = pltpu.PrefetchScalarGridSpec(
        num_scalar_prefetch=1,
        grid=(n_tiles + COMPUTE_LAG,),
        in_specs=[
            pl.BlockSpec(memory_space=pl.ANY),
            pl.BlockSpec((TOK_TILE, D_MODEL), tok_block),
            pl.BlockSpec((TOK_TILE, TOP_K), tok_block),
            pl.BlockSpec((1, D_MODEL), lambda s, pos: (0, 0)),
        ],
        out_specs=[
            pl.BlockSpec((TOK_TILE, D_MODEL), lambda s, pos: (jnp.clip(s - COMPUTE_LAG, 0, n_first - 1), 0)),
            pl.BlockSpec((TOK_TILE, D_MODEL), lambda s, pos: (jnp.maximum(s - COMPUTE_LAG - n_first, 0), 0)),
        ],
        scratch_shapes=[pair_buffer] * N_BUF + [pltpu.VMEM((TOK_TILE, D_MODEL), F32),
                                                pltpu.SemaphoreType.DMA((N_BUF,))],
    )
    return pl.pallas_call(
        functools.partial(_final_kernel, n_first),
        grid_spec=grid_spec,
        out_shape=[jax.ShapeDtypeStruct((n_first_tok, D_MODEL), F32),
                   jax.ShapeDtypeStruct((n_tok - n_first_tok, D_MODEL), F32)],
        compiler_params=pltpu.CompilerParams(
            dimension_semantics=("arbitrary",), vmem_limit_bytes=VMEM_LIMIT),
        name="final",
    )(pos_flat, y_rows, x1_all, gates_all, final_norm)


def _pick_tile(n, target):
    t = min(n, target)
    while n % t:
        t -= 1
    return t


def kernel(x_prompt, x_sample, state_gla, state_pool, norm1, w_in, w_gk2, b_gk, gla_norm, w_pool,
           pool_scale, w_o, norm2, w_router, b_router, w_gate_up, b_gate_up, w_down, b_down, final_norm):
    depth = w_in.shape[0]
    assert depth == 1
    B, L, _ = x_prompt.shape
    BS, LS, _ = x_sample.shape
    n_p, n_s = B * L, BS * LS
    n_all = n_p + n_s

    wi = w_in[0]
    o_glr = 2 * KEY_W + 2 * VAL_W
    wmain = jnp.concatenate(
        [wi[:, 0:o_glr], wi[:, o_glr + GATE_RANK:], wi[:, o_glr:o_glr + GATE_RANK],
         jnp.zeros((D_MODEL, LANES - GATE_RANK), F32)], axis=1).astype(BF16)
    wgk = jnp.concatenate([w_gk2[0], jnp.zeros((LANES - GATE_RANK, KEY_W), F32)], axis=0)
    wgkh, wgkl = _split2(wgk)
    wr = jnp.concatenate([w_router[0], jnp.zeros((D_MODEL, LANES - N_EXPERTS), F32)], axis=1)
    wrh, wrl = _split2(wr)
    br = jnp.concatenate([b_router[0], jnp.zeros((LANES - N_EXPERTS,), F32)]).reshape(1, LANES)
    weights = (norm1[0].reshape(1, D_MODEL), wmain, wgkh, wgkl, b_gk[0].reshape(1, KEY_W),
               gla_norm[0].reshape(1, DV), w_pool[0].astype(BF16), pool_scale[0].reshape(1, POOL_W),
               w_o[0].astype(BF16), norm2[0].reshape(1, D_MODEL), wrh, wrl, br)

    assert L % GLA_CHUNK == 0 and LS in (SUBLANES, BF16_ROWS)
    cfg_p = MixerCfg(ns=1, tl=_pick_tile(L, PROMPT_TILE), chunk=GLA_CHUNK, start_pos=0, n_alias=0)
    s0_p = jnp.zeros((B, VAL_W, KEY_W), F32)
    buf0_p = jnp.zeros((B, POOL_BUF, POOL_W), F32)
    x1_all, h_all, topi_all, gates_all, st_p, buf_p = _mixer_call(
        cfg_p, n_all, 0, x_prompt, s0_p, buf0_p, weights, ())

    cfg_s = MixerCfg(ns=_pick_tile(BS, SAMPLE_TILE // LS), tl=LS, chunk=LS, start_pos=PAST_LEN, n_alias=4)
    r_s = cfg_s.ns * cfg_s.tl
    assert n_p % r_s == 0
    x1_all, h_all, topi_all, gates_all, st_s, buf_s = _mixer_call(
        cfg_s, n_all, n_p // r_s, x_sample, state_gla[0], state_pool[0], weights,
        (x1_all, h_all, topi_all, gates_all))

    pos, counts = _route_call(topi_all)
    n_tiles = (n_all * TOP_K + N_EXPERTS * (ROW_TILE - 1)) // ROW_TILE
    tiles_per_e = (counts[0, :N_EXPERTS] + (ROW_TILE - 1)) // ROW_TILE
    ends = jnp.cumsum(tiles_per_e)
    n_used = ends[-1].astype(jnp.int32)
    tile_ids = jnp.minimum(jnp.arange(n_tiles, dtype=jnp.int32), n_used - 1)
    tile_e = jnp.sum(tile_ids[:, None] >= ends[None, :], axis=1).astype(jnp.int32)
    e_ids = jnp.arange(N_EXPERTS, dtype=jnp.int32)
    later = (tiles_per_e[None, :] > 0) & (e_ids[None, :] > e_ids[:, None])
    next_e = jnp.min(jnp.where(later, e_ids[None, :], N_EXPERTS), axis=1)
    next_e = jnp.where(next_e == N_EXPERTS, -1, next_e).astype(jnp.int32)
    pos_flat = pos.reshape(n_all * TOP_K)
    pad_hi = (ends * ROW_TILE).astype(jnp.int32)
    pad_lo = pad_hi - (tiles_per_e * ROW_TILE - counts[0, :N_EXPERTS]).astype(jnp.int32)
    pair_of_row = _invperm_call(pad_lo, pad_hi, pos_flat, n_tiles * ROW_TILE)

    y_rows = _experts_call(tile_e, n_used.reshape(1), next_e, pair_of_row, h_all,
                           w_gate_up[0], b_gate_up[0], w_down[0], b_down[0])

    fn = final_norm.reshape(1, D_MODEL)
    y_p, y_s = _final_call(n_p, pos_flat, y_rows, x1_all, gates_all, fn)

    st_p = jnp.stack([st_p[:, h * DV:(h + 1) * DV, h * DK:(h + 1) * DK] for h in range(N_HEADS)], axis=1)
    return (y_p.reshape(B, L, D_MODEL), y_s.reshape(BS, LS, D_MODEL),
            jnp.swapaxes(st_p, -1, -2)[None], buf_p[None],
            st_s[None], buf_s[None])
```
